```python
import jax, jax.numpy as jnp
from jax import lax
import numpy as np

D_MODEL = 1024
BATCH = 16
SEQ = 256
DEPTH = 2
DEC_BATCH = 2
DEC_SEQ = 4096
PAST_LEN = 256

GRID_W = 64
N_MIXERS = 2
N_ATTN_LAYERS = (DEPTH + 1) // 2
N_POOL_LAYERS = DEPTH // 2
HEAD_DIM = 128
N_HEADS = D_MODEL // HEAD_DIM
N_KV_HEADS = N_HEADS // 4
QKV_DIM = (N_HEADS + 2 * N_KV_HEADS) * HEAD_DIM
ROPE_THETA = 10000.0
Q_BLOCK = 128
POOL_WINDOWS = (2, 4, 8, 16)
N_POOL_GROUPS = len(POOL_WINDOWS)
POOL_GROUP_DIM = D_MODEL // N_POOL_GROUPS
N_EXPERTS = 16
CAPACITY_FACTOR = 2
D_FF_EXPERT = D_MODEL
N_MOD = 6
EPS = 1e-6

kernel_name = "hybrid_dit_attn_pool_ecmoe_step"


def rms_norm(x, gain):
    x32 = x.astype(jnp.float32)
    y = x32 * lax.rsqrt(jnp.mean(x32 * x32, axis=-1, keepdims=True) + EPS)
    return (y * gain.astype(jnp.float32)).astype(x.dtype)


def modulate(h, shift, scale):
    return h * (1 + scale) + shift


def rope_1d(x, pos):
    n = x.shape[-1] // 2
    inv_freq = ROPE_THETA ** (-jnp.arange(n, dtype=jnp.float32) / n)
    ang = pos.astype(jnp.float32)[:, None] * inv_freq[None, :]
    cos = jnp.cos(ang)[:, None, :]
    sin = jnp.sin(ang)[:, None, :]
    x32 = x.astype(jnp.float32)
    x1, x2 = x32[..., :n], x32[..., n:]
    return jnp.concatenate([x1 * cos - x2 * sin, x1 * sin + x2 * cos], axis=-1).astype(x.dtype)


def axial_rope(x):
    L = x.shape[1]
    rows = L // GRID_W
    row = jnp.repeat(jnp.arange(rows), GRID_W)
    col = jnp.tile(jnp.arange(GRID_W), rows)
    half = HEAD_DIM // 2
    return jnp.concatenate([rope_1d(x[..., :half], row), rope_1d(x[..., half:], col)], axis=-1)


def qkv_proj(h, w_qkv, q_gain, k_gain):
    B, L, _ = h.shape
    qkv = h @ w_qkv
    q, k, v = jnp.split(qkv, [N_HEADS * HEAD_DIM, (N_HEADS + N_KV_HEADS) * HEAD_DIM], axis=-1)
    q = rms_norm(q.reshape(B, L, N_HEADS, HEAD_DIM), q_gain)
    k = rms_norm(k.reshape(B, L, N_KV_HEADS, HEAD_DIM), k_gain)
    v = v.reshape(B, L, N_KV_HEADS, HEAD_DIM)
    return q, k, v


def blocked_attention(q, k, v):
    B, L, H, Dh = q.shape
    KV = k.shape[2]
    G = H // KV
    nb = L // Q_BLOCK
    qb = jnp.moveaxis(q.reshape(B, nb, Q_BLOCK, KV, G, Dh), 1, 0)
    k32 = k.astype(jnp.float32)
    scale = Dh ** -0.5

    def one_block(qi):
        s = jnp.einsum("bqkgd,bskd->bkgqs", qi.astype(jnp.float32), k32) * scale
        p = jax.nn.softmax(s, axis=-1)
        return jnp.einsum("bkgqs,bskd->bqkgd", p.astype(v.dtype), v)

    o = lax.map(one_block, qb)
    return jnp.moveaxis(o, 0, 1).reshape(B, L, H * Dh)


def pool_mix(h, w_pool, pool_scale):
    B, L, D = h.shape
    h32 = h.astype(jnp.float32)
    cs = jnp.concatenate([jnp.zeros((B, 1, D), jnp.float32), lax.cumsum(h32, axis=1)], axis=1)
    t = jnp.arange(L)
    outs = []
    for g, w in enumerate(POOL_WINDOWS):
        lo = jnp.clip(t - w // 2, 0, L)
        hi = jnp.clip(t + w - w // 2, 0, L)
        sl = slice(g * POOL_GROUP_DIM, (g + 1) * POOL_GROUP_DIM)
        csg = cs[..., sl]
        mean = (csg[:, hi] - csg[:, lo]) / (hi - lo).astype(jnp.float32)[None, :, None]
        outs.append(mean - h32[..., sl])
    d = jnp.stack(outs, axis=2).astype(h.dtype)
    y = jnp.einsum("blgc,gcd->blgd", d, w_pool).reshape(B, L, D)
    return y * pool_scale


def expert_choice_ffn(h, w_router, w_gate, w_up, w_down):
    B, L, D = h.shape
    n_tok = B * L
    cap = CAPACITY_FACTOR * n_tok // N_EXPERTS
    xf = h.reshape(n_tok, D)
    aff = jax.nn.softmax((xf @ w_router).astype(jnp.float32), axis=-1)
    gates, idx = lax.top_k(aff.T, cap)
    xe = xf[idx]
    a = jnp.einsum("ecd,edf->ecf", xe, w_gate)
    u = jnp.einsum("ecd,edf->ecf", xe, w_up)
    ye = jnp.einsum("ecf,efd->ecd", jax.nn.silu(a) * u, w_down) * gates[..., None].astype(h.dtype)
    out = jnp.zeros_like(xf).at[idx.reshape(-1)].add(ye.reshape(-1, D))
    return out.reshape(B, L, D)


def setup_inputs(seed: int = 0) -> dict:
    key = jax.random.key(seed)
    ks = jax.random.split(key, 20)
    f32 = jnp.float32
    nrm = lambda k, shape, s: jax.random.normal(k, shape, f32) * s
    return {
        "x_prompt": nrm(ks[0], (BATCH, SEQ, D_MODEL), 1.0),
        "x_sample": nrm(ks[1], (DEC_BATCH, DEC_SEQ, D_MODEL), 1.0),
        "cache_k": nrm(ks[2], (DEC_BATCH, N_ATTN_LAYERS, PAST_LEN, N_KV_HEADS, HEAD_DIM), 1.0),
        "cache_v": nrm(ks[3], (DEC_BATCH, N_ATTN_LAYERS, PAST_LEN, N_KV_HEADS, HEAD_DIM), 1.0),
        "c": nrm(ks[4], (DEC_BATCH, D_MODEL), 1.0),
        "c_ctx": nrm(ks[5], (D_MODEL,), 1.0),
        "norm1": 1.0 + nrm(ks[6], (DEPTH, D_MODEL), 0.01),
        "norm2": 1.0 + nrm(ks[7], (DEPTH, D_MODEL), 0.01),
        "w_ada": nrm(ks[8], (DEPTH, D_MODEL, N_MOD * D_MODEL), D_MODEL ** -0.5),
        "b_ada": nrm(ks[9], (DEPTH, N_MOD * D_MODEL), 0.01),
        "w_qkv": nrm(ks[10], (N_ATTN_LAYERS, D_MODEL, QKV_DIM), D_MODEL ** -0.5),
        "q_norm": 1.0 + nrm(ks[11], (N_ATTN_LAYERS, HEAD_DIM), 0.01),
        "k_norm": 1.0 + nrm(ks[12], (N_ATTN_LAYERS, HEAD_DIM), 0.01),
        "w_o": nrm(ks[13], (N_ATTN_LAYERS, N_HEADS * HEAD_DIM, D_MODEL), (N_HEADS * HEAD_DIM) ** -0.5),
        "w_pool": nrm(ks[14], (N_POOL_LAYERS, N_POOL_GROUPS, POOL_GROUP_DIM, POOL_GROUP_DIM), POOL_GROUP_DIM ** -0.5),
        "pool_scale": 1.0 + nrm(ks[15], (N_POOL_LAYERS, D_MODEL), 0.1),
        "w_router": nrm(ks[16], (DEPTH, D_MODEL, N_EXPERTS), D_MODEL ** -0.5),
        "w_e_gate": nrm(ks[17], (DEPTH, N_EXPERTS, D_MODEL, D_FF_EXPERT), D_MODEL ** -0.5),
        "w_e_up": nrm(ks[18], (DEPTH, N_EXPERTS, D_MODEL, D_FF_EXPERT), D_MODEL ** -0.5),
        "w_e_down": nrm(ks[19], (DEPTH, N_EXPERTS, D_FF_EXPERT, D_MODEL), D_FF_EXPERT ** -0.5),
    }


def reference(x_prompt, x_sample, cache_k, cache_v, c, c_ctx, norm1, norm2, w_ada, b_ada,
              w_qkv, q_norm, k_norm, w_o, w_pool, pool_scale, w_router, w_e_gate, w_e_up, w_e_down):
    ctx, lat = x_prompt, x_sample
    new_k, new_v = [], []
    for i in range(DEPTH):
        j = i // N_MIXERS
        mod_ctx = (jax.nn.silu(c_ctx) @ w_ada[i] + b_ada[i])[None, None, :]
        mod_lat = (jax.nn.silu(c) @ w_ada[i] + b_ada[i])[:, None, :]
        sh1_c, sc1_c, g1_c, sh2_c, sc2_c, g2_c = jnp.split(mod_ctx, N_MOD, axis=-1)
        sh1_l, sc1_l, g1_l, sh2_l, sc2_l, g2_l = jnp.split(mod_lat, N_MOD, axis=-1)

        hc = modulate(rms_norm(ctx, norm1[i]), sh1_c, sc1_c)
        hl = modulate(rms_norm(lat, norm1[i]), sh1_l, sc1_l)
        if i % N_MIXERS == 0:
            qc, kc, vc = qkv_proj(hc, w_qkv[j], q_norm[j], k_norm[j])
            new_k.append(kc)
            new_v.append(vc)
            oc = blocked_attention(qc, kc, vc) @ w_o[j]
            ql, kl, vl = qkv_proj(hl, w_qkv[j], q_norm[j], k_norm[j])
            ql = axial_rope(ql)
            kl = axial_rope(kl)
            keys = jnp.concatenate([kl, cache_k[:, j].astype(kl.dtype)], axis=1)
            vals = jnp.concatenate([vl, cache_v[:, j].astype(vl.dtype)], axis=1)
            ol = blocked_attention(ql, keys, vals) @ w_o[j]
        else:
            oc = pool_mix(hc, w_pool[j], pool_scale[j])
            ol = pool_mix(hl, w_pool[j], pool_scale[j])
        ctx = ctx + g1_c * oc
        lat = lat + g1_l * ol

        hc = modulate(rms_norm(ctx, norm2[i]), sh2_c, sc2_c)
        hl = modulate(rms_norm(lat, norm2[i]), sh2_l, sc2_l)
        ctx = ctx + g2_c * expert_choice_ffn(hc, w_router[i], w_e_gate[i], w_e_up[i], w_e_down[i])
        lat = lat + g2_l * expert_choice_ffn(hl, w_router[i], w_e_gate[i], w_e_up[i], w_e_down[i])

    new_cache_k = jnp.stack(new_k, axis=1)
    new_cache_v = jnp.stack(new_v, axis=1)
    return (ctx, lat, new_cache_k, new_cache_v)
```

```python
import functools

import jax
import jax.numpy as jnp
from jax import lax
from jax.experimental import pallas as pl
from jax.experimental.pallas import tpu as pltpu

F32 = jnp.float32
BF16 = jnp.bfloat16
I32 = jnp.int32
U32 = jnp.uint32

D_MODEL = 1024
HEAD_DIM = 128
N_HEADS = 8
N_KV = 2
Q_PER_KV = N_HEADS // N_KV
QKV_DIM = (N_HEADS + 2 * N_KV) * HEAD_DIM
GRID_W = 64
ROPE_THETA = 10000.0
POOL_WINDOWS = (2, 4, 8, 16)
POOL_GROUP = D_MODEL // len(POOL_WINDOWS)
POOL_HALO = 8
N_EXPERTS = 16
CAPACITY_FACTOR = 2
N_MOD = 6
EPS = 1e-6

LANES = 128
SUBLANES = 8
HALF = D_MODEL // 2
HP_W = HALF + LANES
VMEM_LIMIT = 56 * 1024 * 1024


def _cparams(n_axes, vmem=VMEM_LIMIT):
    return pltpu.CompilerParams(dimension_semantics=("arbitrary",) * n_axes, vmem_limit_bytes=vmem)


def _silu(x):
    return x / (1.0 + jnp.exp(-x))


def _rms(x, gain):
    return x * lax.rsqrt(jnp.mean(x * x, axis=-1, keepdims=True) + EPS) * gain


def _dot(a, b):
    return jnp.dot(a, b, preferred_element_type=F32)


def _dot_nt(a, b):
    return lax.dot_general(a, b, (((1,), (1,)), ((), ())), preferred_element_type=F32)


def _ada_kernel(c_ref, w_ref, b_ref, o_ref):
    s = _silu(c_ref[...]).astype(BF16)
    o_ref[0] = _dot(s, w_ref[0].astype(BF16)) + b_ref[0]


def _ada(cvec, w_ada, b_ada):
    depth = w_ada.shape[0]
    out = pl.pallas_call(
        _ada_kernel,
        out_shape=jax.ShapeDtypeStruct((depth, SUBLANES, N_MOD * D_MODEL), F32),
        grid=(depth, N_MOD),
        in_specs=[
            pl.BlockSpec((SUBLANES, D_MODEL), lambda i, j: (0, 0)),
            pl.BlockSpec((1, D_MODEL, D_MODEL), lambda i, j: (i, 0, j)),
            pl.BlockSpec((1, 1, D_MODEL), lambda i, j: (i, 0, j)),
        ],
        out_specs=pl.BlockSpec((1, SUBLANES, D_MODEL), lambda i, j: (i, 0, j)),
        compiler_params=_cparams(2),
        name="ada",
    )(cvec, w_ada, b_ada.reshape(depth, 1, N_MOD * D_MODEL))
    return out.reshape(depth, SUBLANES, N_MOD, D_MODEL)


def _mod_spec(layer, row_of_tile):
    return pl.BlockSpec((1, 1, N_MOD, D_MODEL), lambda i, *_: (layer, row_of_tile(i), 0, 0))


def _qkv_kernel(*refs, rope, cache_out):
    x_ref, mod_ref, n1_ref, w_ref, qn_ref, kn_ref = refs[:6]
    refs = refs[6:]
    if rope:
        cos_ref, sin_ref = refs[:2]
        refs = refs[2:]
    q_ref, k_ref, v_ref = refs[:3]
    refs = refs[3:]
    if cache_out:
        kc_ref, vc_ref = refs[:2]
        refs = refs[2:]
    (wb_ref,) = refs

    @pl.when(pl.program_id(0) == 0)
    def _():
        wb_ref[...] = w_ref[...].astype(BF16)

    m = mod_ref[0, 0]
    h = _rms(x_ref[...], n1_ref[...]) * (1.0 + m[1:2]) + m[0:1]
    qkv = _dot(h.astype(BF16), wb_ref[...])
    scale = HEAD_DIM ** -0.5
    if rope:
        lane = lax.broadcasted_iota(I32, (x_ref.shape[0], HEAD_DIM), 1)
        first = (lane & (HEAD_DIM // 4)) == 0
        cos = cos_ref[...]
        sin = sin_ref[...]
    for hh in range(N_HEADS + N_KV):
        sl = slice(hh * HEAD_DIM, (hh + 1) * HEAD_DIM)
        xh = _rms(qkv[:, sl], qn_ref[...] if hh < N_HEADS else kn_ref[...])
        if cache_out and hh >= N_HEADS:
            kc_ref[:, (hh - N_HEADS) * HEAD_DIM:(hh - N_HEADS + 1) * HEAD_DIM] = xh
        if rope:
            rot = jnp.where(first, pltpu.roll(xh, HEAD_DIM - HEAD_DIM // 4, 1), pltpu.roll(xh, HEAD_DIM // 4, 1))
            xh = xh * cos + rot * sin
        if hh < N_HEADS:
            q_ref[:, sl] = (xh * scale).astype(BF16)
        else:
            k_ref[:, (hh - N_HEADS) * HEAD_DIM:(hh - N_HEADS + 1) * HEAD_DIM] = xh.astype(BF16)
    v = qkv[:, (N_HEADS + N_KV) * HEAD_DIM:]
    v_ref[...] = v.astype(BF16)
    if cache_out:
        vc_ref[...] = v


def _qkv(x, mods, layer, row_of_tile, n1, w_qkv, qn, kn, rope_tabs, cache_out, tm):
    n_tok = x.shape[0]
    kvw = N_KV * HEAD_DIM
    rope = rope_tabs is not None
    in_specs = [
        pl.BlockSpec((tm, D_MODEL), lambda i: (i, 0)),
        _mod_spec(layer, row_of_tile),
        pl.BlockSpec((1, D_MODEL), lambda i: (0, 0)),
        pl.BlockSpec((D_MODEL, QKV_DIM), lambda i: (0, 0)),
        pl.BlockSpec((1, HEAD_DIM), lambda i: (0, 0)),
        pl.BlockSpec((1, HEAD_DIM), lambda i: (0, 0)),
    ]
    args = [x, mods, n1, w_qkv, qn, kn]
    if rope:
        seq_tiles = rope_tabs[0].shape[0] // tm
        in_specs += [pl.BlockSpec((tm, HEAD_DIM), lambda i: (i % seq_tiles, 0))] * 2
        args += list(rope_tabs)
    out_shape = [jax.ShapeDtypeStruct((n_tok, D_MODEL), BF16),
                 jax.ShapeDtypeStruct((n_tok, kvw), BF16),
                 jax.ShapeDtypeStruct((n_tok, kvw), BF16)]
    out_specs = [pl.BlockSpec((tm, D_MODEL), lambda i: (i, 0)),
                 pl.BlockSpec((tm, kvw), lambda i: (i, 0)),
                 pl.BlockSpec((tm, kvw), lambda i: (i, 0))]
    if cache_out:
        out_shape += [jax.ShapeDtypeStruct((n_tok, kvw), F32)] * 2
        out_specs += [pl.BlockSpec((tm, kvw), lambda i: (i, 0))] * 2
    return pl.pallas_call(
        functools.partial(_qkv_kernel, rope=rope, cache_out=cache_out),
        out_shape=out_shape,
        grid=(n_tok // tm,),
        in_specs=in_specs,
        out_specs=out_specs,
        scratch_shapes=[pltpu.VMEM((D_MODEL, QKV_DIM), BF16)],
        compiler_params=_cparams(1),
        name="qkv_rope" if rope else "qkv",
    )(*args)


def _rope_tables(seq_len):
    half = HEAD_DIM // 2
    n = half // 2
    inv_freq = ROPE_THETA ** (-jnp.arange(n, dtype=F32) / n)
    rows = seq_len // GRID_W
    row = jnp.repeat(jnp.arange(rows), GRID_W).astype(F32)
    col = jnp.tile(jnp.arange(GRID_W), rows).astype(F32)
    ang_r = row[:, None] * inv_freq[None, :]
    ang_c = col[:, None] * inv_freq[None, :]
    cos = jnp.concatenate([jnp.cos(ang_r)] * 2 + [jnp.cos(ang_c)] * 2, axis=-1)
    sin = jnp.concatenate([-jnp.sin(ang_r), jnp.sin(ang_r), -jnp.sin(ang_c), jnp.sin(ang_c)], axis=-1)
    return cos, sin


def _softmax_pv(scores, values):
    m = functools.reduce(jnp.maximum, [jnp.max(s, axis=-1, keepdims=True) for s in scores])
    ps = [jnp.exp(s - m) for s in scores]
    l = functools.reduce(jnp.add, [jnp.sum(p, axis=-1, keepdims=True) for p in ps])
    o = functools.reduce(jnp.add, [_dot(p.astype(BF16), v) for p, v in zip(ps, values)])
    return o / l


def _attn_ctx_kernel(q_ref, k_ref, v_ref, o_ref):
    for g in range(N_KV):
        kg = k_ref[:, g * HEAD_DIM:(g + 1) * HEAD_DIM]
        vg = v_ref[:, g * HEAD_DIM:(g + 1) * HEAD_DIM]
        for h in range(Q_PER_KV):
            sl = slice((g * Q_PER_KV + h) * HEAD_DIM, (g * Q_PER_KV + h + 1) * HEAD_DIM)
            o_ref[:, sl] = _softmax_pv([_dot_nt(q_ref[:, sl], kg)], [vg]).astype(BF16)


def _attn_ctx(q, k, v, seq):
    n_tok = q.shape[0]
    kvw = N_KV * HEAD_DIM
    return pl.pallas_call(
        _attn_ctx_kernel,
        out_shape=jax.ShapeDtypeStruct((n_tok, D_MODEL), BF16),
        grid=(n_tok // seq,),
        in_specs=[pl.BlockSpec((seq, D_MODEL), lambda b: (b, 0)),
                  pl.BlockSpec((seq, kvw), lambda b: (b, 0)),
                  pl.BlockSpec((seq, kvw), lambda b: (b, 0))],
        out_specs=pl.BlockSpec((seq, D_MODEL), lambda b: (b, 0)),
        compiler_params=_cparams(1),
        name="attn_ctx",
    )(q, k, v)


def _attn_lat_kernel(q_ref, k_ref, v_ref, kc_ref, vc_ref, o_ref):
    k = k_ref[...]
    v = v_ref[...]
    kc = kc_ref[0]
    vc = vc_ref[0]
    for h in range(Q_PER_KV):
        sl = slice(h * HEAD_DIM, (h + 1) * HEAD_DIM)
        qh = q_ref[:, sl]
        o_ref[:, sl] = _softmax_pv([_dot_nt(qh, k), _dot_nt(qh, kc)], [v, vc]).astype(BF16)


def _attn_lat(q, k, v, kc, vc, seq, tq):
    n_tok = q.shape[0]
    batch = n_tok // seq
    past = kc.shape[1]
    qt = seq // tq
    gw = Q_PER_KV * HEAD_DIM
    return pl.pallas_call(
        _attn_lat_kernel,
        out_shape=jax.ShapeDtypeStruct((n_tok, D_MODEL), BF16),
        grid=(batch, N_KV, qt),
        in_specs=[pl.BlockSpec((tq, gw), lambda b, g, i: (b * qt + i, g)),
                  pl.BlockSpec((seq, HEAD_DIM), lambda b, g, i: (b, g)),
                  pl.BlockSpec((seq, HEAD_DIM), lambda b, g, i: (b, g)),
                  pl.BlockSpec((1, past, HEAD_DIM), lambda b, g, i: (b, 0, g)),
                  pl.BlockSpec((1, past, HEAD_DIM), lambda b, g, i: (b, 0, g))],
        out_specs=pl.BlockSpec((tq, gw), lambda b, g, i: (b * qt + i, g)),
        compiler_params=_cparams(3),
        name="attn_lat",
    )(q, k, v, kc, vc)


def _moe_front(x1, m, n2, wr, x1_ref, hp_ref, aff_ref):
    tm = x1.shape[0]
    x1_ref[...] = x1
    hb = (_rms(x1, n2) * (1.0 + m[4:5]) + m[3:4]).astype(BF16)
    logits = _dot(hb, wr)
    lane = lax.broadcasted_iota(I32, logits.shape, 1)
    logits = jnp.where(lane < N_EXPERTS, logits, -1e30)
    ex = jnp.exp(logits - jnp.max(logits, axis=-1, keepdims=True))
    aff = ex / jnp.sum(ex, axis=-1, keepdims=True)
    u = pltpu.bitcast(hb.astype(F32), U32)
    hp_ref[:, 0:HALF] = (u[:, HALF:] & jnp.uint32(0xFFFF0000)) | (u[:, :HALF] >> 16)
    hp_ref[:, HALF:HP_W] = pltpu.bitcast(aff, U32)
    aff_t = aff.T
    for b in range(tm // LANES):
        aff_ref[b * N_EXPERTS:(b + 1) * N_EXPERTS, :] = aff_t[0:N_EXPERTS, b * LANES:(b + 1) * LANES]


def _front_out(n_tok, tm):
    shapes = [jax.ShapeDtypeStruct((n_tok, D_MODEL), F32),
              jax.ShapeDtypeStruct((n_tok, HP_W), U32),
              jax.ShapeDtypeStruct((n_tok // LANES * N_EXPERTS, LANES), F32)]
    specs = [pl.BlockSpec((tm, D_MODEL), lambda i: (i, 0)),
             pl.BlockSpec((tm, HP_W), lambda i: (i, 0)),
             pl.BlockSpec((tm // LANES * N_EXPERTS, LANES), lambda i: (i, 0))]
    return shapes, specs


def _post_kernel(o_ref, x_ref, mod_ref, n2_ref, wo_ref, wr_ref, x1_ref, hp_ref, aff_ref, wb_ref):
    @pl.when(pl.program_id(0) == 0)
    def _():
        wb_ref[...] = wo_ref[...].astype(BF16)

    m = mod_ref[0, 0]
    x1 = x_ref[...] + m[2:3] * _dot(o_ref[...], wb_ref[...])
    _moe_front(x1, m, n2_ref[...], wr_ref[...], x1_ref, hp_ref, aff_ref)


def _post(o, x, mods, layer, row_of_tile, n2, w_o, wr, tm):
    n_tok = x.shape[0]
    shapes, specs = _front_out(n_tok, tm)
    return pl.pallas_call(
        _post_kernel,
        out_shape=shapes,
        grid=(n_tok // tm,),
        in_specs=[pl.BlockSpec((tm, D_MODEL), lambda i: (i, 0)),
                  pl.BlockSpec((tm, D_MODEL), lambda i: (i, 0)),
                  _mod_spec(layer, row_of_tile),
                  pl.BlockSpec((1, D_MODEL), lambda i: (0, 0)),
                  pl.BlockSpec((D_MODEL, D_MODEL), lambda i: (0, 0)),
                  pl.BlockSpec((D_MODEL, LANES), lambda i: (0, 0))],
        out_specs=specs,
        scratch_shapes=[pltpu.VMEM((D_MODEL, D_MODEL), BF16)],
        compiler_params=_cparams(1),
        name="post",
    )(o, x, mods, n2, w_o, wr)


def _pool_kernel(x_ref, xp_ref, xn_ref, mod_ref, n1_ref, n2_ref, wp_ref, ps_ref, wr_ref,
                 x1_ref, hp_ref, aff_ref, *, seq):
    tm = x_ref.shape[0]
    i = pl.program_id(0)
    m = mod_ref[0, 0]
    x = x_ref[...]

    def norm_mod(v):
        return _rms(v, n1_ref[...]) * (1.0 + m[1:2]) + m[0:1]

    prev_ok = ((i * tm) % seq != 0).astype(F32)
    next_ok = (((i + 1) * tm) % seq != 0).astype(F32)
    h = norm_mod(x)
    hz = jnp.concatenate([norm_mod(xp_ref[...]) * prev_ok, h, norm_mod(xn_ref[...]) * next_ok], axis=0)
    rows = tm + 2 * POOL_HALO
    t = (i * tm) % seq + lax.broadcasted_iota(I32, (tm, 1), 0)
    ys = []
    for g, w in enumerate(POOL_WINDOWS):
        sl = slice(g * POOL_GROUP, (g + 1) * POOL_GROUP)
        f = hz[:, sl]
        step = 1
        while step < w:
            f = f + pltpu.roll(f, rows - step, 0)
            step *= 2
        win = pltpu.roll(f, w // 2, 0)[POOL_HALO:POOL_HALO + tm]
        cnt = (jnp.minimum(t + w // 2, seq) - jnp.maximum(t - w // 2, 0)).astype(F32)
        d = (win / cnt - h[:, sl]).astype(BF16)
        ys.append(_dot(d, wp_ref[g].astype(BF16)))
    y = jnp.concatenate(ys, axis=-1) * ps_ref[...]
    x1 = x + m[2:3] * y
    _moe_front(x1, m, n2_ref[...], wr_ref[...], x1_ref, hp_ref, aff_ref)


def _pool(x, mods, layer, row_of_tile, n1, n2, w_pool, pool_scale, wr, seq, tm):
    n_tok = x.shape[0]
    hb = tm // POOL_HALO
    last = n_tok // POOL_HALO - 1
    shapes, specs = _front_out(n_tok, tm)
    return pl.pallas_call(
        functools.partial(_pool_kernel, seq=seq),
        out_shape=shapes,
        grid=(n_tok // tm,),
        in_specs=[pl.BlockSpec((tm, D_MODEL), lambda i: (i, 0)),
                  pl.BlockSpec((POOL_HALO, D_MODEL), lambda i: (jnp.maximum(i * hb - 1, 0), 0)),
                  pl.BlockSpec((POOL_HALO, D_MODEL), lambda i: (jnp.minimum((i + 1) * hb, last), 0)),
                  _mod_spec(layer, row_of_tile),
                  pl.BlockSpec((1, D_MODEL), lambda i: (0, 0)),
                  pl.BlockSpec((1, D_MODEL), lambda i: (0, 0)),
                  pl.BlockSpec(w_pool.shape, lambda i: (0, 0, 0)),
                  pl.BlockSpec((1, D_MODEL), lambda i: (0, 0)),
                  pl.BlockSpec((D_MODEL, LANES), lambda i: (0, 0))],
        out_specs=specs,
        compiler_params=_cparams(1),
        name="pool",
    )(x, x, x, mods, n1, n2, w_pool, pool_scale, wr)


def _select_kernel(aff_ref, idx_ref, bnd_ref, *, cap, n_quarters):
    nb = aff_ref.shape[0] // N_EXPERTS
    rows = nb * N_EXPERTS
    shape3 = (nb, N_EXPERTS, LANES)
    bits = pltpu.bitcast(aff_ref[...], I32).reshape(shape3)

    def count(mask):
        return jnp.sum(jnp.sum(mask.astype(F32), axis=0), axis=-1, keepdims=True)

    def search(i, t):
        cand = t | (jnp.int32(1) << (30 - i))
        return jnp.where(count(bits >= cand[None]) >= cap, cand, t)

    thr = lax.fori_loop(0, 31, search, jnp.zeros((N_EXPERTS, LANES), I32))
    gt = bits > thr[None]
    eq = bits == thr[None]
    need = cap - count(gt)

    kk = lax.broadcasted_iota(I32, (LANES, LANES), 0)
    nn = lax.broadcasted_iota(I32, (LANES, LANES), 1)
    upper = (kk <= nn).astype(BF16)
    ones = jnp.ones((LANES, LANES), BF16)

    def prefix(mask):
        m2 = mask.astype(F32).astype(BF16).reshape(rows, LANES)
        p = _dot(m2, upper).reshape(shape3)
        s = _dot(m2, ones).reshape(shape3)
        offs = []
        run = jnp.zeros((N_EXPERTS, LANES), F32)
        for b in range(nb):
            offs.append(run)
            run = run + s[b]
        return p, jnp.stack(offs, axis=0), s

    pe, oe, _ = prefix(eq)
    sel = gt | (eq & ((pe + oe) <= need[None]))
    ps, os_, ss = prefix(sel)

    lane = lax.broadcasted_iota(I32, (rows, LANES), 1)
    sel2 = sel.reshape(rows, LANES)
    dist = jnp.where(sel2, lane - (ps.reshape(rows, LANES).astype(I32) - 1), 0)
    alive = sel2.astype(I32)
    val = lane
    for k in range(7):
        s = 1 << k
        move = alive * ((dist >> k) & 1)
        inc = pltpu.roll(move, LANES - s, 1) * (lane < LANES - s).astype(I32)
        val = jnp.where(inc == 1, pltpu.roll(val, LANES - s, 1), val)
        dist = jnp.where(inc == 1, pltpu.roll(dist, LANES - s, 1), dist)
        alive = alive - move + inc
    local = val.reshape(shape3)

    top = os_ + ss
    lane_e = lax.broadcasted_iota(I32, (N_EXPERTS, LANES), 1)
    for jc in range(cap // LANES):
        j = (lane_e + jc * LANES).astype(F32)
        acc = jnp.zeros((N_EXPERTS, LANES), I32)
        for b in range(nb):
            inside = (os_[b] <= j) & (j < top[b])
            jl = (j - os_[b]).astype(I32) & (LANES - 1)
            acc = jnp.where(inside, jnp.take_along_axis(local[b], jl, axis=1) + b * LANES, acc)
        idx_ref[:, jc * LANES:(jc + 1) * LANES] = acc

    bnd = jnp.full((N_EXPERTS, LANES), cap, I32)
    for q in range(n_quarters):
        bnd = jnp.where(lane_e == q, os_[q * (nb // n_quarters)].astype(I32), bnd)
    bnd_ref[...] = bnd


def _select(aff, cap, n_quarters):
    rows = aff.shape[0]
    return pl.pallas_call(
        functools.partial(_select_kernel, cap=cap, n_quarters=n_quarters),
        out_shape=[jax.ShapeDtypeStruct((N_EXPERTS, cap), I32),
                   jax.ShapeDtypeStruct((N_EXPERTS, LANES), I32)],
        grid=(1,),
        in_specs=[pl.BlockSpec((rows, LANES), lambda i: (0, 0))],
        out_specs=[pl.BlockSpec((N_EXPERTS, cap), lambda i: (0, 0)),
                   pl.BlockSpec((N_EXPERTS, LANES), lambda i: (0, 0))],
        compiler_params=_cparams(1),
        name="select",
    )(aff)


GATHER_UNROLL = 8


def _ffn_kernel(idx_ref, hp_ref, wg_ref, wu_ref, wd_ref, y_ref, xu_ref, xb_ref, gate_ref):
    e = pl.program_id(0)
    f = pl.program_id(1)
    cap = xu_ref.shape[0]

    @pl.when(f == 0)
    def _():
        def gather(g, carry):
            for r in range(GATHER_UNROLL):
                j = g * GATHER_UNROLL + r
                xu_ref[pl.ds(j, 1), :] = hp_ref[pl.ds(idx_ref[e, j], 1), :]
            return carry

        lax.fori_loop(0, cap // GATHER_UNROLL, gather, 0)
        w = xu_ref[:, 0:HALF]
        xb_ref[:, 0:HALF] = pltpu.bitcast(w << 16, F32).astype(BF16)
        xb_ref[:, HALF:] = pltpu.bitcast(w & jnp.uint32(0xFFFF0000), F32).astype(BF16)
        aff = pltpu.bitcast(xu_ref[:, HALF:HP_W], F32)
        lane = lax.broadcasted_iota(I32, aff.shape, 1)
        gate_ref[...] = jnp.sum(jnp.where(lane == e, aff, 0.0), axis=1, keepdims=True)

    xe = xb_ref[...]
    a = _dot(xe, wg_ref[0, 0].astype(BF16))
    u = _dot(xe, wu_ref[0, 0].astype(BF16))
    y = _dot((_silu(a) * u).astype(BF16), wd_ref[0, 0].astype(BF16)) * gate_ref[...]

    @pl.when(f == 0)
    def _():
        y_ref[0] = y

    @pl.when(f > 0)
    def _():
        y_ref[0] += y


def _ffn(idx, hp, layer, w_gate, w_up, w_down, ft):
    cap = idx.shape[1]
    n_tok = hp.shape[0]
    d_ff = w_gate.shape[3]
    return pl.pallas_call(
        _ffn_kernel,
        out_shape=jax.ShapeDtypeStruct((N_EXPERTS, cap, D_MODEL), F32),
        grid_spec=pltpu.PrefetchScalarGridSpec(
            num_scalar_prefetch=1,
            grid=(N_EXPERTS, d_ff // ft),
            in_specs=[pl.BlockSpec((n_tok, HP_W), lambda e, f, idx: (0, 0), pipeline_mode=pl.Buffered(1)),
                      pl.BlockSpec((1, 1, D_MODEL, ft), lambda e, f, idx: (layer, e, 0, f)),
                      pl.BlockSpec((1, 1, D_MODEL, ft), lambda e, f, idx: (layer, e, 0, f)),
                      pl.BlockSpec((1, 1, ft, D_MODEL), lambda e, f, idx: (layer, e, f, 0))],
            out_specs=pl.BlockSpec((1, cap, D_MODEL), lambda e, f, idx: (e, 0, 0)),
            scratch_shapes=[pltpu.VMEM((cap, HP_W), U32),
                            pltpu.VMEM((cap, D_MODEL), BF16),
                            pltpu.VMEM((cap, 1), F32)]),
        compiler_params=_cparams(2),
        name="ffn",
    )(idx, hp, w_gate, w_up, w_down)


SCATTER_UNROLL = 4


def _combine_kernel(idx_ref, bnd_ref, y_ref, x1_ref, mod_ref, o_ref, acc_ref):
    q = pl.program_id(0)
    e = pl.program_id(1)
    tb = o_ref.shape[0]
    cap = y_ref.shape[1]

    @pl.when(e == 0)
    def _():
        acc_ref[...] = jnp.zeros_like(acc_ref)

    lo = bnd_ref[e, q]
    hi = bnd_ref[e, q + 1]

    def scatter(g, carry):
        rows, sums = [], []
        for r in range(SCATTER_UNROLL):
            j = g * SCATTER_UNROLL + r
            n = jnp.where((j >= lo) & (j < hi), idx_ref[e, j] - q * tb, tb + r)
            rows.append(n)
            sums.append(acc_ref[pl.ds(n, 1), :] + y_ref[0, pl.ds(j, 1), :])
        for n, s in zip(rows, sums):
            acc_ref[pl.ds(n, 1), :] = s
        return carry

    lax.fori_loop(lo // SCATTER_UNROLL, (hi + SCATTER_UNROLL - 1) // SCATTER_UNROLL, scatter, 0)

    @pl.when(e == N_EXPERTS - 1)
    def _():
        o_ref[...] = x1_ref[...] + mod_ref[0, 0][5:6] * acc_ref[0:tb, :]


def _combine(idx, bnd, y, x1, mods, layer, row_of_tile, tb):
    n_tok = x1.shape[0]
    cap = idx.shape[1]
    return pl.pallas_call(
        _combine_kernel,
        out_shape=jax.ShapeDtypeStruct((n_tok, D_MODEL), F32),
        grid_spec=pltpu.PrefetchScalarGridSpec(
            num_scalar_prefetch=2,
            grid=(n_tok // tb, N_EXPERTS),
            in_specs=[pl.BlockSpec((1, cap, D_MODEL), lambda q, e, *_: (e, 0, 0)),
                      pl.BlockSpec((tb, D_MODEL), lambda q, e, *_: (q, 0), pipeline_mode=pl.Buffered(1)),
                      pl.BlockSpec((1, 1, N_MOD, D_MODEL), lambda q, e, *_: (layer, row_of_tile(q), 0, 0))],
            out_specs=pl.BlockSpec((tb, D_MODEL), lambda q, e, *_: (q, 0)),
            scratch_shapes=[pltpu.VMEM((tb + SUBLANES, D_MODEL), F32)]),
        compiler_params=_cparams(2),
        name="combine",
    )(idx, bnd, y, x1, mods)


TOKEN_TILE = 512
Q_TILE = 256
FF_TILE = 512
COMBINE_TILE = 2048


def _moe(x1, hp, aff, mods, layer, row_of_tile_combine, w_gate, w_up, w_down):
    n_tok = x1.shape[0]
    cap = CAPACITY_FACTOR * n_tok // N_EXPERTS
    idx, bnd = _select(aff, cap, n_tok // COMBINE_TILE)
    y = _ffn(idx, hp, layer, w_gate, w_up, w_down, FF_TILE)
    return _combine(idx, bnd, y, x1, mods, layer, row_of_tile_combine, COMBINE_TILE)


def kernel(x_prompt, x_sample, cache_k, cache_v, c, c_ctx, norm1, norm2, w_ada, b_ada, w_qkv, q_norm, k_norm,
           w_o, w_pool, pool_scale, w_router, w_e_gate, w_e_up, w_e_down):
    batch, seq, _ = x_prompt.shape
    dec_batch, dec_seq, _ = x_sample.shape
    depth = w_ada.shape[0]
    kvw = N_KV * HEAD_DIM

    cvec = jnp.zeros((SUBLANES, D_MODEL), F32).at[0].set(c_ctx).at[1:1 + dec_batch].set(c)
    mods = _ada(cvec, w_ada, b_ada)
    ctx = x_prompt.reshape(batch * seq, D_MODEL)
    lat = x_sample.reshape(dec_batch * dec_seq, D_MODEL)
    ctx_row = lambda i: 0
    lat_row = lambda tile: (lambda i: 1 + (i * tile) // dec_seq)
    ctx_tile = min(TOKEN_TILE, seq)
    rope_tabs = _rope_tables(dec_seq)
    new_k = new_v = None

    for layer in range(depth):
        j = layer // 2
        n1 = norm1[layer][None]
        n2 = norm2[layer][None]
        wr = jnp.pad(w_router[layer], ((0, 0), (0, LANES - N_EXPERTS))).astype(BF16)
        if layer % 2 == 0:
            qn = q_norm[j][None]
            kn = k_norm[j][None]
            qc, kc, vc, new_k, new_v = _qkv(ctx, mods, layer, ctx_row, n1, w_qkv[j], qn, kn, None, True, TOKEN_TILE)
            oc = _attn_ctx(qc, kc, vc, seq)
            ql, kl, vl = _qkv(lat, mods, layer, lat_row(TOKEN_TILE), n1, w_qkv[j], qn, kn, rope_tabs, False,
                              TOKEN_TILE)
            past_k = cache_k[:, j].reshape(dec_batch, -1, kvw).astype(BF16)
            past_v = cache_v[:, j].reshape(dec_batch, -1, kvw).astype(BF16)
            ol = _attn_lat(ql, kl, vl, past_k, past_v, dec_seq, Q_TILE)
            ctx1 = _post(oc, ctx, mods, layer, ctx_row, n2, w_o[j], wr, TOKEN_TILE)
            lat1 = _post(ol, lat, mods, layer, lat_row(TOKEN_TILE), n2, w_o[j], wr, TOKEN_TILE)
        else:
            ps = pool_scale[j][None]
            ctx1 = _pool(ctx, mods, layer, ctx_row, n1, n2, w_pool[j], ps, wr, seq, ctx_tile)
            lat1 = _pool(lat, mods, layer, lat_row(TOKEN_TILE), n1, n2, w_pool[j], ps, wr, dec_seq, TOKEN_TILE)
        ctx = _moe(*ctx1, mods, layer, ctx_row, w_e_gate, w_e_up, w_e_down)
        lat = _moe(*lat1, mods, layer, lat_row(COMBINE_TILE), w_e_gate, w_e_up, w_e_down)

    new_cache_k = new_k.reshape(batch, 1, seq, N_KV, HEAD_DIM)
    new_cache_v = new_v.reshape(batch, 1, seq, N_KV, HEAD_DIM)
    return (ctx.reshape(batch, seq, D_MODEL), lat.reshape(dec_batch, dec_seq, D_MODEL), new_cache_k, new_cache_v)
```

```python
import functools

import jax
import jax.numpy as jnp
from jax import lax
from jax.experimental import pallas as pl
from jax.experimental.pallas import tpu as pltpu

F32 = jnp.float32
BF16 = jnp.bfloat16
I32 = jnp.int32
U32 = jnp.uint32

D_MODEL = 1024
HEAD_DIM = 128
N_HEADS = 8
N_KV = 2
Q_PER_KV = N_HEADS // N_KV
QKV_DIM = (N_HEADS + 2 * N_KV) * HEAD_DIM
GRID_W = 64
ROPE_THETA = 10000.0
POOL_WINDOWS = (2, 4, 8, 16)
POOL_GROUP = D_MODEL // len(POOL_WINDOWS)
POOL_HALO = 8
N_EXPERTS = 16
CAPACITY_FACTOR = 2
N_MOD = 6
EPS = 1e-6
LOG2_E = 1.4426950408889634

LANES = 128
SUBLANES = 8
HALF = D_MODEL // 2
HP_ROWS = HALF // LANES
Y_ROWS = D_MODEL // LANES
VMEM_LIMIT = 56 * 1024 * 1024


def _cparams(n_axes, vmem=VMEM_LIMIT):
    return pltpu.CompilerParams(dimension_semantics=("arbitrary",) * n_axes, vmem_limit_bytes=vmem)


def _silu(x):
    return x / (1.0 + jnp.exp(-x))


def _rms(x, gain):
    return x * lax.rsqrt(jnp.mean(x * x, axis=-1, keepdims=True) + EPS) * gain


def _dot(a, b):
    return jnp.dot(a, b, preferred_element_type=F32)


def _dot_nt(a, b):
    return lax.dot_general(a, b, (((1,), (1,)), ((), ())), preferred_element_type=F32)


def _ada_kernel(c_ref, w_ref, b_ref, o_ref):
    s = _silu(c_ref[...]).astype(BF16)
    o_ref[0] = _dot(s, w_ref[0].astype(BF16)) + b_ref[0]


def _ada(cvec, w_ada, b_ada):
    depth = w_ada.shape[0]
    out = pl.pallas_call(
        _ada_kernel,
        out_shape=jax.ShapeDtypeStruct((depth, SUBLANES, N_MOD * D_MODEL), F32),
        grid=(depth, N_MOD),
        in_specs=[
            pl.BlockSpec((SUBLANES, D_MODEL), lambda i, j: (0, 0)),
            pl.BlockSpec((1, D_MODEL, D_MODEL), lambda i, j: (i, 0, j)),
            pl.BlockSpec((1, 1, D_MODEL), lambda i, j: (i, 0, j)),
        ],
        out_specs=pl.BlockSpec((1, SUBLANES, D_MODEL), lambda i, j: (i, 0, j)),
        compiler_params=_cparams(2),
        name="ada",
    )(cvec, w_ada, b_ada.reshape(depth, 1, N_MOD * D_MODEL))
    return out.reshape(depth, SUBLANES, N_MOD, D_MODEL)


def _mod_spec(layer, row_of_tile):
    return pl.BlockSpec((1, 1, N_MOD, D_MODEL), lambda i, *_: (layer, row_of_tile(i), 0, 0))


def _qkv_kernel(*refs, rope, cache_out):
    x_ref, mod_ref, n1_ref, w_ref, qn_ref, kn_ref = refs[:6]
    refs = refs[6:]
    if rope:
        cos_ref, sin_ref = refs[:2]
        refs = refs[2:]
    q_ref, k_ref, v_ref = refs[:3]
    refs = refs[3:]
    if cache_out:
        kc_ref, vc_ref = refs[:2]
        refs = refs[2:]
    (wb_ref,) = refs

    @pl.when(pl.program_id(0) == 0)
    def _():
        wb_ref[...] = w_ref[...].astype(BF16)

    m = mod_ref[0, 0]
    h = _rms(x_ref[...], n1_ref[...]) * (1.0 + m[1:2]) + m[0:1]
    qkv = _dot(h.astype(BF16), wb_ref[...])
    scale = HEAD_DIM ** -0.5 * LOG2_E
    if rope:
        lane = lax.broadcasted_iota(I32, (x_ref.shape[0], HEAD_DIM), 1)
        first = (lane & (HEAD_DIM // 4)) == 0
        cos = cos_ref[...]
        sin = sin_ref[...]
    for hh in range(N_HEADS + N_KV):
        sl = slice(hh * HEAD_DIM, (hh + 1) * HEAD_DIM)
        xh = _rms(qkv[:, sl], qn_ref[...] if hh < N_HEADS else kn_ref[...])
        if cache_out and hh >= N_HEADS:
            kc_ref[:, (hh - N_HEADS) * HEAD_DIM:(hh - N_HEADS + 1) * HEAD_DIM] = xh
        if rope:
            rot = jnp.where(first, pltpu.roll(xh, HEAD_DIM - HEAD_DIM // 4, 1), pltpu.roll(xh, HEAD_DIM // 4, 1))
            xh = xh * cos + rot * sin
        if hh < N_HEADS:
            q_ref[:, sl] = (xh * scale).astype(BF16)
        else:
            k_ref[:, (hh - N_HEADS) * HEAD_DIM:(hh - N_HEADS + 1) * HEAD_DIM] = xh.astype(BF16)
    v = qkv[:, (N_HEADS + N_KV) * HEAD_DIM:]
    v_ref[...] = v.astype(BF16)
    if cache_out:
        vc_ref[...] = v


def _qkv(x, mods, layer, row_of_tile, n1, w_qkv, qn, kn, rope_tabs, cache_out, tm):
    n_tok = x.shape[0]
    kvw = N_KV * HEAD_DIM
    rope = rope_tabs is not None
    in_specs = [
        pl.BlockSpec((tm, D_MODEL), lambda i: (i, 0)),
        _mod_spec(layer, row_of_tile),
        pl.BlockSpec((1, D_MODEL), lambda i: (0, 0)),
        pl.BlockSpec((D_MODEL, QKV_DIM), lambda i: (0, 0)),
        pl.BlockSpec((1, HEAD_DIM), lambda i: (0, 0)),
        pl.BlockSpec((1, HEAD_DIM), lambda i: (0, 0)),
    ]
    args = [x, mods, n1, w_qkv, qn, kn]
    if rope:
        seq_tiles = rope_tabs[0].shape[0] // tm
        in_specs += [pl.BlockSpec((tm, HEAD_DIM), lambda i: (i % seq_tiles, 0))] * 2
        args += list(rope_tabs)
    out_shape = [jax.ShapeDtypeStruct((n_tok, D_MODEL), BF16),
                 jax.ShapeDtypeStruct((n_tok, kvw), BF16),
                 jax.ShapeDtypeStruct((n_tok, kvw), BF16)]
    out_specs = [pl.BlockSpec((tm, D_MODEL), lambda i: (i, 0)),
                 pl.BlockSpec((tm, kvw), lambda i: (i, 0)),
                 pl.BlockSpec((tm, kvw), lambda i: (i, 0))]
    if cache_out:
        out_shape += [jax.ShapeDtypeStruct((n_tok, kvw), F32)] * 2
        out_specs += [pl.BlockSpec((tm, kvw), lambda i: (i, 0))] * 2
    return pl.pallas_call(
        functools.partial(_qkv_kernel, rope=rope, cache_out=cache_out),
        out_shape=out_shape,
        grid=(n_tok // tm,),
        in_specs=in_specs,
        out_specs=out_specs,
        scratch_shapes=[pltpu.VMEM((D_MODEL, QKV_DIM), BF16)],
        compiler_params=_cparams(1),
        name="qkv_rope" if rope else "qkv",
    )(*args)


def _rope_tables(seq_len):
    half = HEAD_DIM // 2
    n = half // 2
    inv_freq = ROPE_THETA ** (-jnp.arange(n, dtype=F32) / n)
    rows = seq_len // GRID_W
    row = jnp.repeat(jnp.arange(rows), GRID_W).astype(F32)
    col = jnp.tile(jnp.arange(GRID_W), rows).astype(F32)
    ang_r = row[:, None] * inv_freq[None, :]
    ang_c = col[:, None] * inv_freq[None, :]
    cos = jnp.concatenate([jnp.cos(ang_r)] * 2 + [jnp.cos(ang_c)] * 2, axis=-1)
    sin = jnp.concatenate([-jnp.sin(ang_r), jnp.sin(ang_r), -jnp.sin(ang_c), jnp.sin(ang_c)], axis=-1)
    return cos, sin


def _with_ones(v):
    return jnp.concatenate([v, jnp.ones_like(v)], axis=1)


def _gqa_attention(q, segments):
    rows = q.shape[0]
    qs = jnp.concatenate([q[:, h * HEAD_DIM:(h + 1) * HEAD_DIM] for h in range(Q_PER_KV)], axis=0)
    m = jnp.full((rows * Q_PER_KV, 1), -1e30, F32)
    acc = jnp.zeros((rows * Q_PER_KV, 2 * HEAD_DIM), F32)
    for k, v in segments:
        s = _dot_nt(qs, k)
        m_new = jnp.maximum(m, jnp.max(s, axis=-1, keepdims=True))
        acc = acc * jnp.exp2(m - m_new) + _dot(jnp.exp2(s - m_new).astype(BF16), v)
        m = m_new
    o = (acc[:, :HEAD_DIM] / acc[:, HEAD_DIM:]).astype(BF16)
    return jnp.concatenate([o[h * rows:(h + 1) * rows] for h in range(Q_PER_KV)], axis=1)


def _attn_ctx_kernel(q_ref, k_ref, v_ref, o_ref):
    gw = Q_PER_KV * HEAD_DIM
    for g in range(N_KV):
        kg = k_ref[:, g * HEAD_DIM:(g + 1) * HEAD_DIM]
        vg = _with_ones(v_ref[:, g * HEAD_DIM:(g + 1) * HEAD_DIM])
        o_ref[:, g * gw:(g + 1) * gw] = _gqa_attention(q_ref[:, g * gw:(g + 1) * gw], [(kg, vg)])


def _attn_ctx(q, k, v, seq):
    n_tok = q.shape[0]
    kvw = N_KV * HEAD_DIM
    return pl.pallas_call(
        _attn_ctx_kernel,
        out_shape=jax.ShapeDtypeStruct((n_tok, D_MODEL), BF16),
        grid=(n_tok // seq,),
        in_specs=[pl.BlockSpec((seq, D_MODEL), lambda b: (b, 0)),
                  pl.BlockSpec((seq, kvw), lambda b: (b, 0)),
                  pl.BlockSpec((seq, kvw), lambda b: (b, 0))],
        out_specs=pl.BlockSpec((seq, D_MODEL), lambda b: (b, 0)),
        compiler_params=_cparams(1),
        name="attn_ctx",
    )(q, k, v)


KEY_CHUNK = 1024


def _attn_lat_kernel(q_ref, k_ref, v_ref, kc_ref, vc_ref, o_ref):
    segments = [(k_ref[c:c + KEY_CHUNK], _with_ones(v_ref[c:c + KEY_CHUNK]))
                for c in range(0, k_ref.shape[0], KEY_CHUNK)]
    segments.append((kc_ref[0], _with_ones(vc_ref[0])))
    o_ref[...] = _gqa_attention(q_ref[...], segments)


def _attn_lat(q, k, v, kc, vc, seq, tq):
    n_tok = q.shape[0]
    batch = n_tok // seq
    past = kc.shape[1]
    qt = seq // tq
    gw = Q_PER_KV * HEAD_DIM
    return pl.pallas_call(
        _attn_lat_kernel,
        out_shape=jax.ShapeDtypeStruct((n_tok, D_MODEL), BF16),
        grid=(batch, N_KV, qt),
        in_specs=[pl.BlockSpec((tq, gw), lambda b, g, i: (b * qt + i, g)),
                  pl.BlockSpec((seq, HEAD_DIM), lambda b, g, i: (b, g)),
                  pl.BlockSpec((seq, HEAD_DIM), lambda b, g, i: (b, g)),
                  pl.BlockSpec((1, past, HEAD_DIM), lambda b, g, i: (b, 0, g)),
                  pl.BlockSpec((1, past, HEAD_DIM), lambda b, g, i: (b, 0, g))],
        out_specs=pl.BlockSpec((tq, gw), lambda b, g, i: (b * qt + i, g)),
        compiler_params=_cparams(3),
        name="attn_lat",
    )(q, k, v, kc, vc)


def _moe_front(x1, m, n2, wr, x1_ref, hp_ref, aff_ref):
    tm = x1.shape[0]
    x1_ref[...] = x1
    h2 = _rms(x1, n2) * (1.0 + m[4:5]) + m[3:4]
    logits = _dot(h2.astype(BF16), wr)
    lane = lax.broadcasted_iota(I32, logits.shape, 1)
    logits = jnp.where(lane < N_EXPERTS, logits, -1e30)
    ex = jnp.exp(logits - jnp.max(logits, axis=-1, keepdims=True))
    aff = ex / jnp.sum(ex, axis=-1, keepdims=True)
    packed = pltpu.pack_elementwise([h2[:, :HALF], h2[:, HALF:]], packed_dtype=BF16)
    for s in range(HP_ROWS):
        hp_ref[pl.ds(s, tm, stride=HP_ROWS), :] = packed[:, s * LANES:(s + 1) * LANES]
    aff_t = aff.T
    for b in range(tm // LANES):
        aff_ref[b * N_EXPERTS:(b + 1) * N_EXPERTS, :] = aff_t[0:N_EXPERTS, b * LANES:(b + 1) * LANES]


def _front_out(n_tok, tm):
    shapes = [jax.ShapeDtypeStruct((n_tok, D_MODEL), F32),
              jax.ShapeDtypeStruct((n_tok * HP_ROWS, LANES), U32),
              jax.ShapeDtypeStruct((n_tok // LANES * N_EXPERTS, LANES), F32)]
    specs = [pl.BlockSpec((tm, D_MODEL), lambda i: (i, 0)),
             pl.BlockSpec((tm * HP_ROWS, LANES), lambda i: (i, 0)),
             pl.BlockSpec((tm // LANES * N_EXPERTS, LANES), lambda i: (i, 0))]
    return shapes, specs


def _post_kernel(o_ref, x_ref, mod_ref, n2_ref, wo_ref, wr_ref, x1_ref, hp_ref, aff_ref, wb_ref):
    @pl.when(pl.program_id(0) == 0)
    def _():
        wb_ref[...] = wo_ref[...].astype(BF16)

    m = mod_ref[0, 0]
    x1 = x_ref[...] + m[2:3] * _dot(o_ref[...], wb_ref[...])
    _moe_front(x1, m, n2_ref[...], wr_ref[...], x1_ref, hp_ref, aff_ref)


def _post(o, x, mods, layer, row_of_tile, n2, w_o, wr, tm):
    n_tok = x.shape[0]
    shapes, specs = _front_out(n_tok, tm)
    return pl.pallas_call(
        _post_kernel,
        out_shape=shapes,
        grid=(n_tok // tm,),
        in_specs=[pl.BlockSpec((tm, D_MODEL), lambda i: (i, 0)),
                  pl.BlockSpec((tm, D_MODEL), lambda i: (i, 0)),
                  _mod_spec(layer, row_of_tile),
                  pl.BlockSpec((1, D_MODEL), lambda i: (0, 0)),
                  pl.BlockSpec((D_MODEL, D_MODEL), lambda i: (0, 0)),
                  pl.BlockSpec((D_MODEL, LANES), lambda i: (0, 0))],
        out_specs=specs,
        scratch_shapes=[pltpu.VMEM((D_MODEL, D_MODEL), BF16)],
        compiler_params=_cparams(1),
        name="post",
    )(o, x, mods, n2, w_o, wr)


def _pool_kernel(x_ref, xp_ref, xn_ref, mod_ref, n1_ref, n2_ref, wp_ref, ps_ref, wr_ref,
                 x1_ref, hp_ref, aff_ref, *, seq):
    tm = x_ref.shape[0]
    i = pl.program_id(0)
    m = mod_ref[0, 0]
    x = x_ref[...]

    def norm_mod(v):
        return _rms(v, n1_ref[...]) * (1.0 + m[1:2]) + m[0:1]

    prev_ok = ((i * tm) % seq != 0).astype(F32)
    next_ok = (((i + 1) * tm) % seq != 0).astype(F32)
    h = norm_mod(x)
    hz = jnp.concatenate([norm_mod(xp_ref[...]) * prev_ok, h, norm_mod(xn_ref[...]) * next_ok], axis=0)
    rows = tm + 2 * POOL_HALO
    t = (i * tm) % seq + lax.broadcasted_iota(I32, (tm, 1), 0)
    ys = []
    for g, w in enumerate(POOL_WINDOWS):
        sl = slice(g * POOL_GROUP, (g + 1) * POOL_GROUP)
        f = hz[:, sl]
        step = 1
        while step < w:
            f = f + pltpu.roll(f, rows - step, 0)
            step *= 2
        win = pltpu.roll(f, w // 2, 0)[POOL_HALO:POOL_HALO + tm]
        cnt = (jnp.minimum(t + w // 2, seq) - jnp.maximum(t - w // 2, 0)).astype(F32)
        d = (win / cnt - h[:, sl]).astype(BF16)
        ys.append(_dot(d, wp_ref[g].astype(BF16)))
    y = jnp.concatenate(ys, axis=-1) * ps_ref[...]
    x1 = x + m[2:3] * y
    _moe_front(x1, m, n2_ref[...], wr_ref[...], x1_ref, hp_ref, aff_ref)


def _pool(x, mods, layer, row_of_tile, n1, n2, w_pool, pool_scale, wr, seq, tm):
    n_tok = x.shape[0]
    hb = tm // POOL_HALO
    last = n_tok // POOL_HALO - 1
    shapes, specs = _front_out(n_tok, tm)
    return pl.pallas_call(
        functools.partial(_pool_kernel, seq=seq),
        out_shape=shapes,
        grid=(n_tok // tm,),
        in_specs=[pl.BlockSpec((tm, D_MODEL), lambda i: (i, 0)),
                  pl.BlockSpec((POOL_HALO, D_MODEL), lambda i: (jnp.maximum(i * hb - 1, 0), 0)),
                  pl.BlockSpec((POOL_HALO, D_MODEL), lambda i: (jnp.minimum((i + 1) * hb, last), 0)),
                  _mod_spec(layer, row_of_tile),
                  pl.BlockSpec((1, D_MODEL), lambda i: (0, 0)),
                  pl.BlockSpec((1, D_MODEL), lambda i: (0, 0)),
                  pl.BlockSpec(w_pool.shape, lambda i: (0, 0, 0)),
                  pl.BlockSpec((1, D_MODEL), lambda i: (0, 0)),
                  pl.BlockSpec((D_MODEL, LANES), lambda i: (0, 0))],
        out_specs=specs,
        compiler_params=_cparams(1),
        name="pool",
    )(x, x, x, mods, n1, n2, w_pool, pool_scale, wr)


def _select_kernel(aff_ref, idx_ref, gate_ref, bnd_ref, *, cap, n_quarters):
    nb = aff_ref.shape[0] // N_EXPERTS
    rows = nb * N_EXPERTS
    shape3 = (nb, N_EXPERTS, LANES)
    aff = aff_ref[...].reshape(shape3)

    def count(mask):
        return jnp.sum(jnp.sum(mask.astype(F32), axis=0), axis=-1, keepdims=True)

    def search(i, t):
        cand = t | (jnp.int32(1) << (29 - i))
        return jnp.where(count(aff >= pltpu.bitcast(cand, F32)[None]) >= cap, cand, t)

    thr = pltpu.bitcast(lax.fori_loop(0, 30, search, jnp.zeros((N_EXPERTS, LANES), I32)), F32)
    gt = aff > thr[None]
    eq = aff == thr[None]
    need = cap - count(gt)

    kk = lax.broadcasted_iota(I32, (LANES, LANES), 0)
    nn = lax.broadcasted_iota(I32, (LANES, LANES), 1)
    upper = (kk <= nn).astype(BF16)
    ones = jnp.ones((LANES, LANES), BF16)

    def prefix(mask):
        m2 = mask.astype(F32).astype(BF16).reshape(rows, LANES)
        p = _dot(m2, upper).reshape(shape3)
        s = _dot(m2, ones).reshape(shape3)
        offs = []
        run = jnp.zeros((N_EXPERTS, LANES), F32)
        for b in range(nb):
            offs.append(run)
            run = run + s[b]
        return p, jnp.stack(offs, axis=0), s

    pe, oe, _ = prefix(eq)
    sel = gt | (eq & ((pe + oe) <= need[None]))
    ps, os_, ss = prefix(sel)

    lane = lax.broadcasted_iota(I32, (rows, LANES), 1)
    sel2 = sel.reshape(rows, LANES)
    dist = jnp.where(sel2, lane - (ps.reshape(rows, LANES).astype(I32) - 1), 0)
    alive = sel2.astype(I32)
    val = lane
    gval = aff.reshape(rows, LANES)
    for k in range(7):
        s = 1 << k
        move = alive * ((dist >> k) & 1)
        inc = pltpu.roll(move, LANES - s, 1) * (lane < LANES - s).astype(I32) == 1
        val = jnp.where(inc, pltpu.roll(val, LANES - s, 1), val)
        gval = jnp.where(inc, pltpu.roll(gval, LANES - s, 1), gval)
        dist = jnp.where(inc, pltpu.roll(dist, LANES - s, 1), dist)
        alive = alive - move + inc.astype(I32)
    local = val.reshape(shape3)
    local_gate = gval.reshape(shape3)

    top = os_ + ss
    lane_e = lax.broadcasted_iota(I32, (N_EXPERTS, LANES), 1)
    for jc in range(cap // LANES):
        j = (lane_e + jc * LANES).astype(F32)
        acc = jnp.zeros((N_EXPERTS, LANES), I32)
        gacc = jnp.zeros((N_EXPERTS, LANES), F32)
        for b in range(nb):
            inside = (os_[b] <= j) & (j < top[b])
            jl = (j - os_[b]).astype(I32) & (LANES - 1)
            acc = jnp.where(inside, jnp.take_along_axis(local[b], jl, axis=1) + b * LANES, acc)
            gacc = jnp.where(inside, jnp.take_along_axis(local_gate[b], jl, axis=1), gacc)
        idx_ref[:, jc * LANES:(jc + 1) * LANES] = acc
        gate_ref[:, jc * LANES:(jc + 1) * LANES] = gacc

    bnd = jnp.full((N_EXPERTS, LANES), cap, I32)
    for q in range(n_quarters):
        bnd = jnp.where(lane_e == q, os_[q * (nb // n_quarters)].astype(I32), bnd)
    bnd_ref[...] = bnd


def _select(aff, cap, n_quarters):
    rows = aff.shape[0]
    return pl.pallas_call(
        functools.partial(_select_kernel, cap=cap, n_quarters=n_quarters),
        out_shape=[jax.ShapeDtypeStruct((N_EXPERTS, cap), I32),
                   jax.ShapeDtypeStruct((N_EXPERTS, cap), F32),
                   jax.ShapeDtypeStruct((N_EXPERTS, LANES), I32)],
        grid=(1,),
        in_specs=[pl.BlockSpec((rows, LANES), lambda i: (0, 0))],
        out_specs=[pl.BlockSpec((N_EXPERTS, cap), lambda i: (0, 0)),
                   pl.BlockSpec((N_EXPERTS, cap), lambda i: (0, 0)),
                   pl.BlockSpec((N_EXPERTS, LANES), lambda i: (0, 0))],
        compiler_params=_cparams(1),
        name="select",
    )(aff)


GATHER_UNROLL = 8


def _ffn_kernel(idx_ref, hp_ref, wg_ref, wu_ref, wd_ref, y_ref, slab_ref, xb_ref, yacc_ref):
    e = pl.program_id(0)
    f = pl.program_id(1)
    nf = pl.num_programs(1)
    cap = xb_ref.shape[0]

    @pl.when(f == 0)
    def _():
        def gather(g, carry):
            for r in range(GATHER_UNROLL):
                j = g * GATHER_UNROLL + r
                src = pl.multiple_of(idx_ref[e * cap + j] * HP_ROWS, HP_ROWS)
                slab_ref[pl.ds(pl.multiple_of(j * HP_ROWS, HP_ROWS), HP_ROWS), :] = hp_ref[pl.ds(src, HP_ROWS), :]
            return carry

        lax.fori_loop(0, cap // GATHER_UNROLL, gather, 0)
        for s in range(HP_ROWS):
            w = slab_ref[pl.ds(s, cap, stride=HP_ROWS), :]
            for half in range(2):
                v = pltpu.unpack_elementwise(w, index=half, packed_dtype=BF16, unpacked_dtype=F32)
                xb_ref[:, half * HALF + s * LANES:half * HALF + (s + 1) * LANES] = v.astype(BF16)

    xe = xb_ref[...]
    a = _dot(xe, wg_ref[0, 0].astype(BF16))
    u = _dot(xe, wu_ref[0, 0].astype(BF16))
    y = _dot((_silu(a) * u).astype(BF16), wd_ref[0, 0].astype(BF16))

    @pl.when(f == 0)
    def _():
        yacc_ref[...] = y

    @pl.when((f > 0) & (f < nf - 1))
    def _():
        yacc_ref[...] += y

    @pl.when(f == nf - 1)
    def _():
        tot = yacc_ref[...] + y
        for s in range(Y_ROWS):
            y_ref[0, pl.ds(s, cap, stride=Y_ROWS), :] = tot[:, s * LANES:(s + 1) * LANES]


def _ffn(idx, hp, layer, w_gate, w_up, w_down, ft):
    cap = idx.shape[0] // N_EXPERTS
    d_ff = w_gate.shape[3]
    assert d_ff // ft >= 2
    return pl.pallas_call(
        _ffn_kernel,
        out_shape=jax.ShapeDtypeStruct((N_EXPERTS, cap * Y_ROWS, LANES), F32),
        grid_spec=pltpu.PrefetchScalarGridSpec(
            num_scalar_prefetch=1,
            grid=(N_EXPERTS, d_ff // ft),
            in_specs=[pl.BlockSpec(hp.shape, lambda e, f, idx: (0, 0), pipeline_mode=pl.Buffered(1)),
                      pl.BlockSpec((1, 1, D_MODEL, ft), lambda e, f, idx: (layer, e, 0, f)),
                      pl.BlockSpec((1, 1, D_MODEL, ft), lambda e, f, idx: (layer, e, 0, f)),
                      pl.BlockSpec((1, 1, ft, D_MODEL), lambda e, f, idx: (layer, e, f, 0))],
            out_specs=pl.BlockSpec((1, cap * Y_ROWS, LANES), lambda e, f, idx: (e, 0, 0)),
            scratch_shapes=[pltpu.VMEM((cap * HP_ROWS, LANES), U32),
                            pltpu.VMEM((cap, D_MODEL), BF16),
                            pltpu.VMEM((cap, D_MODEL), F32)]),
        compiler_params=_cparams(2),
        name="ffn",
    )(idx, hp, w_gate, w_up, w_down)


SCATTER_UNROLL = 8


def _combine_kernel(idx_ref, bnd_ref, gate_ref, y_ref, x1_ref, mod_ref, o_ref, acc_ref):
    q = pl.program_id(0)
    e = pl.program_id(1)
    tb = o_ref.shape[0]
    cap = y_ref.shape[1] // Y_ROWS

    @pl.when(e == 0)
    def _():
        acc_ref[...] = jnp.zeros_like(acc_ref)

    lo = bnd_ref[e, q]
    hi = bnd_ref[e, q + 1]

    def scatter(g, carry):
        rows, sums = [], []
        for r in range(SCATTER_UNROLL):
            j = g * SCATTER_UNROLL + r
            n = jnp.where((j >= lo) & (j < hi), idx_ref[e * cap + j] - q * tb, tb + r)
            dst = pl.ds(pl.multiple_of(n * Y_ROWS, Y_ROWS), Y_ROWS)
            src = pl.ds(pl.multiple_of(j * Y_ROWS, Y_ROWS), Y_ROWS)
            rows.append(dst)
            sums.append(acc_ref[dst, :] + gate_ref[e * cap + j] * y_ref[0, src, :])
        for dst, s in zip(rows, sums):
            acc_ref[dst, :] = s
        return carry

    lax.fori_loop(lo // SCATTER_UNROLL, (hi + SCATTER_UNROLL - 1) // SCATTER_UNROLL, scatter, 0)

    @pl.when(e == N_EXPERTS - 1)
    def _():
        g2 = mod_ref[0, 0][5:6]
        for s in range(Y_ROWS):
            sl = slice(s * LANES, (s + 1) * LANES)
            o_ref[:, sl] = x1_ref[:, sl] + g2[:, sl] * acc_ref[pl.ds(s, tb, stride=Y_ROWS), :]


def _combine(idx, bnd, gate, y, x1, mods, layer, row_of_tile, tb):
    n_tok = x1.shape[0]
    cap = idx.shape[0] // N_EXPERTS
    return pl.pallas_call(
        _combine_kernel,
        out_shape=jax.ShapeDtypeStruct((n_tok, D_MODEL), F32),
        grid_spec=pltpu.PrefetchScalarGridSpec(
            num_scalar_prefetch=3,
            grid=(n_tok // tb, N_EXPERTS),
            in_specs=[pl.BlockSpec((1, cap * Y_ROWS, LANES), lambda q, e, *_: (e, 0, 0)),
                      pl.BlockSpec((tb, D_MODEL), lambda q, e, *_: (q, 0), pipeline_mode=pl.Buffered(1)),
                      pl.BlockSpec((1, 1, N_MOD, D_MODEL), lambda q, e, *_: (layer, row_of_tile(q), 0, 0))],
            out_specs=pl.BlockSpec((tb, D_MODEL), lambda q, e, *_: (q, 0)),
            scratch_shapes=[pltpu.VMEM(((tb + SCATTER_UNROLL) * Y_ROWS, LANES), F32)]),
        compiler_params=_cparams(2),
        name="combine",
    )(idx, bnd, gate, y, x1, mods)


TOKEN_TILE = 512
Q_TILE = 256
FF_TILE = 512
COMBINE_TILE = 2048


def _moe(x1, hp, aff, mods, layer, row_of_tile_combine, w_gate, w_up, w_down):
    n_tok = x1.shape[0]
    cap = CAPACITY_FACTOR * n_tok // N_EXPERTS
    idx, gate, bnd = _select(aff, cap, n_tok // COMBINE_TILE)
    idx = idx.reshape(-1)
    gate = gate.reshape(-1)
    y = _ffn(idx, hp, layer, w_gate, w_up, w_down, FF_TILE)
    return _combine(idx, bnd, gate, y, x1, mods, layer, row_of_tile_combine, COMBINE_TILE)


def kernel(x_prompt, x_sample, cache_k, cache_v, c, c_ctx, norm1, norm2, w_ada, b_ada, w_qkv, q_norm, k_norm,
           w_o, w_pool, pool_scale, w_router, w_e_gate, w_e_up, w_e_down):
    batch, seq, _ = x_prompt.shape
    dec_batch, dec_seq, _ = x_sample.shape
    depth = w_ada.shape[0]
    kvw = N_KV * HEAD_DIM

    cvec = jnp.zeros((SUBLANES, D_MODEL), F32).at[0].set(c_ctx).at[1:1 + dec_batch].set(c)
    mods = _ada(cvec, w_ada, b_ada)
    ctx = x_prompt.reshape(batch * seq, D_MODEL)
    lat = x_sample.reshape(dec_batch * dec_seq, D_MODEL)
    ctx_row = lambda i: 0
    lat_row = lambda tile: (lambda i: 1 + (i * tile) // dec_seq)
    ctx_tile = min(TOKEN_TILE, seq)
    rope_tabs = _rope_tables(dec_seq)
    new_k = new_v = None

    for layer in range(depth):
        j = layer // 2
        n1 = norm1[layer][None]
        n2 = norm2[layer][None]
        wr = jnp.pad(w_router[layer], ((0, 0), (0, LANES - N_EXPERTS))).astype(BF16)
        if layer % 2 == 0:
            qn = q_norm[j][None]
            kn = k_norm[j][None]
            qc, kc, vc, new_k, new_v = _qkv(ctx, mods, layer, ctx_row, n1, w_qkv[j], qn, kn, None, True, TOKEN_TILE)
            oc = _attn_ctx(qc, kc, vc, seq)
            ql, kl, vl = _qkv(lat, mods, layer, lat_row(TOKEN_TILE), n1, w_qkv[j], qn, kn, rope_tabs, False,
                              TOKEN_TILE)
            past_k = cache_k[:, j].reshape(dec_batch, -1, kvw).astype(BF16)
            past_v = cache_v[:, j].reshape(dec_batch, -1, kvw).astype(BF16)
            ol = _attn_lat(ql, kl, vl, past_k, past_v, dec_seq, Q_TILE)
            ctx1 = _post(oc, ctx, mods, layer, ctx_row, n2, w_o[j], wr, TOKEN_TILE)
            lat1 = _post(ol, lat, mods, layer, lat_row(TOKEN_TILE), n2, w_o[j], wr, TOKEN_TILE)
        else:
            ps = pool_scale[j][None]
            ctx1 = _pool(ctx, mods, layer, ctx_row, n1, n2, w_pool[j], ps, wr, seq, ctx_tile)
            lat1 = _pool(lat, mods, layer, lat_row(TOKEN_TILE), n1, n2, w_pool[j], ps, wr, dec_seq, TOKEN_TILE)
        ctx = _moe(*ctx1, mods, layer, ctx_row, w_e_gate, w_e_up, w_e_down)
        lat = _moe(*lat1, mods, layer, lat_row(COMBINE_TILE), w_e_gate, w_e_up, w_e_down)

    new_cache_k = new_k.reshape(batch, 1, seq, N_KV, HEAD_DIM)
    new_cache_v = new_v.reshape(batch, 1, seq, N_KV, HEAD_DIM)
    return (ctx.reshape(batch, seq, D_MODEL), lat.reshape(dec_batch, dec_seq, D_MODEL), new_cache_k, new_cache_v)
```

```python
import functools

import jax
import jax.numpy as jnp
from jax import lax
from jax.experimental import pallas as pl
from jax.experimental.pallas import tpu as pltpu

F32 = jnp.float32
BF16 = jnp.bfloat16
I32 = jnp.int32
U32 = jnp.uint32

D_MODEL = 1024
HEAD_DIM = 128
N_HEADS = 8
N_KV = 2
Q_PER_KV = N_HEADS // N_KV
QKV_DIM = (N_HEADS + 2 * N_KV) * HEAD_DIM
GRID_W = 64
ROPE_THETA = 10000.0
POOL_WINDOWS = (2, 4, 8, 16)
POOL_GROUP = D_MODEL // len(POOL_WINDOWS)
POOL_HALO = 8
N_EXPERTS = 16
CAPACITY_FACTOR = 2
N_MOD = 6
EPS = 1e-6
LOG2_E = 1.4426950408889634

LANES = 128
SUBLANES = 8
HALF = D_MODEL // 2
HP_ROWS = HALF // LANES
Y_ROWS = D_MODEL // LANES
VMEM_LIMIT = 56 * 1024 * 1024


def _cparams(n_axes, vmem=VMEM_LIMIT):
    return pltpu.CompilerParams(dimension_semantics=("arbitrary",) * n_axes, vmem_limit_bytes=vmem)


def _silu(x):
    return x / (1.0 + jnp.exp(-x))


def _rms(x, gain):
    return x * lax.rsqrt(jnp.mean(x * x, axis=-1, keepdims=True) + EPS) * gain


def _dot(a, b):
    return jnp.dot(a, b, preferred_element_type=F32)


def _dot_nt(a, b):
    return lax.dot_general(a, b, (((1,), (1,)), ((), ())), preferred_element_type=F32)


def _ada_kernel(c_ref, w_ref, b_ref, o_ref):
    s = _silu(c_ref[...]).astype(BF16)
    o_ref[0] = _dot(s, w_ref[0].astype(BF16)) + b_ref[0]


def _ada(cvec, w_ada, b_ada):
    depth = w_ada.shape[0]
    out = pl.pallas_call(
        _ada_kernel,
        out_shape=jax.ShapeDtypeStruct((depth, SUBLANES, N_MOD * D_MODEL), F32),
        grid=(depth, N_MOD),
        in_specs=[
            pl.BlockSpec((SUBLANES, D_MODEL), lambda i, j: (0, 0)),
            pl.BlockSpec((1, D_MODEL, D_MODEL), lambda i, j: (i, 0, j)),
            pl.BlockSpec((1, 1, D_MODEL), lambda i, j: (i, 0, j)),
        ],
        out_specs=pl.BlockSpec((1, SUBLANES, D_MODEL), lambda i, j: (i, 0, j)),
        compiler_params=_cparams(2),
        name="ada",
    )(cvec, w_ada, b_ada.reshape(depth, 1, N_MOD * D_MODEL))
    return out.reshape(depth, SUBLANES, N_MOD, D_MODEL)


def _mod_spec(layer, row_of_tile):
    return pl.BlockSpec((1, 1, N_MOD, D_MODEL), lambda i, *_: (layer, row_of_tile(i), 0, 0))


def _qkv_kernel(*refs, rope, cache_out):
    x_ref, mod_ref, n1_ref, w_ref, qn_ref, kn_ref = refs[:6]
    refs = refs[6:]
    if rope:
        cos_ref, sin_ref = refs[:2]
        refs = refs[2:]
    q_ref, k_ref, v_ref = refs[:3]
    refs = refs[3:]
    if cache_out:
        kc_ref, vc_ref = refs[:2]
        refs = refs[2:]
    (wb_ref,) = refs

    @pl.when(pl.program_id(0) == 0)
    def _():
        wb_ref[...] = w_ref[...].astype(BF16)

    m = mod_ref[0, 0]
    h = _rms(x_ref[...], n1_ref[...]) * (1.0 + m[1:2]) + m[0:1]
    qkv = _dot(h.astype(BF16), wb_ref[...])
    scale = HEAD_DIM ** -0.5 * LOG2_E
    if rope:
        lane = lax.broadcasted_iota(I32, (x_ref.shape[0], HEAD_DIM), 1)
        first = (lane & (HEAD_DIM // 4)) == 0
        cos = cos_ref[...]
        sin = sin_ref[...]
    for hh in range(N_HEADS + N_KV):
        sl = slice(hh * HEAD_DIM, (hh + 1) * HEAD_DIM)
        xh = _rms(qkv[:, sl], qn_ref[...] if hh < N_HEADS else kn_ref[...])
        if cache_out and hh >= N_HEADS:
            kc_ref[:, (hh - N_HEADS) * HEAD_DIM:(hh - N_HEADS + 1) * HEAD_DIM] = xh
        if rope:
            rot = jnp.where(first, pltpu.roll(xh, HEAD_DIM - HEAD_DIM // 4, 1), pltpu.roll(xh, HEAD_DIM // 4, 1))
            xh = xh * cos + rot * sin
        if hh < N_HEADS:
            q_ref[:, sl] = (xh * scale).astype(BF16)
        else:
            k_ref[:, (hh - N_HEADS) * HEAD_DIM:(hh - N_HEADS + 1) * HEAD_DIM] = xh.astype(BF16)
    v = qkv[:, (N_HEADS + N_KV) * HEAD_DIM:]
    v_ref[...] = v.astype(BF16)
    if cache_out:
        vc_ref[...] = v


def _qkv(x, mods, layer, row_of_tile, n1, w_qkv, qn, kn, rope_tabs, cache_out, tm):
    n_tok = x.shape[0]
    kvw = N_KV * HEAD_DIM
    rope = rope_tabs is not None
    in_specs = [
        pl.BlockSpec((tm, D_MODEL), lambda i: (i, 0)),
        _mod_spec(layer, row_of_tile),
        pl.BlockSpec((1, D_MODEL), lambda i: (0, 0)),
        pl.BlockSpec((D_MODEL, QKV_DIM), lambda i: (0, 0)),
        pl.BlockSpec((1, HEAD_DIM), lambda i: (0, 0)),
        pl.BlockSpec((1, HEAD_DIM), lambda i: (0, 0)),
    ]
    args = [x, mods, n1, w_qkv, qn, kn]
    if rope:
        seq_tiles = rope_tabs[0].shape[0] // tm
        in_specs += [pl.BlockSpec((tm, HEAD_DIM), lambda i: (i % seq_tiles, 0))] * 2
        args += list(rope_tabs)
    out_shape = [jax.ShapeDtypeStruct((n_tok, D_MODEL), BF16),
                 jax.ShapeDtypeStruct((n_tok, kvw), BF16),
                 jax.ShapeDtypeStruct((n_tok, kvw), BF16)]
    out_specs = [pl.BlockSpec((tm, D_MODEL), lambda i: (i, 0)),
                 pl.BlockSpec((tm, kvw), lambda i: (i, 0)),
                 pl.BlockSpec((tm, kvw), lambda i: (i, 0))]
    if cache_out:
        out_shape += [jax.ShapeDtypeStruct((n_tok, kvw), F32)] * 2
        out_specs += [pl.BlockSpec((tm, kvw), lambda i: (i, 0))] * 2
    return pl.pallas_call(
        functools.partial(_qkv_kernel, rope=rope, cache_out=cache_out),
        out_shape=out_shape,
        grid=(n_tok // tm,),
        in_specs=in_specs,
        out_specs=out_specs,
        scratch_shapes=[pltpu.VMEM((D_MODEL, QKV_DIM), BF16)],
        compiler_params=_cparams(1),
        name="qkv_rope" if rope else "qkv",
    )(*args)


def _rope_tables(seq_len):
    half = HEAD_DIM // 2
    n = half // 2
    inv_freq = ROPE_THETA ** (-jnp.arange(n, dtype=F32) / n)
    rows = seq_len // GRID_W
    row = jnp.repeat(jnp.arange(rows), GRID_W).astype(F32)
    col = jnp.tile(jnp.arange(GRID_W), rows).astype(F32)
    ang_r = row[:, None] * inv_freq[None, :]
    ang_c = col[:, None] * inv_freq[None, :]
    cos = jnp.concatenate([jnp.cos(ang_r)] * 2 + [jnp.cos(ang_c)] * 2, axis=-1)
    sin = jnp.concatenate([-jnp.sin(ang_r), jnp.sin(ang_r), -jnp.sin(ang_c), jnp.sin(ang_c)], axis=-1)
    return cos, sin


def _with_ones(v):
    return jnp.concatenate([v, jnp.ones_like(v)], axis=1)


def _gqa_attention(q, segments):
    rows = q.shape[0]
    qs = jnp.concatenate([q[:, h * HEAD_DIM:(h + 1) * HEAD_DIM] for h in range(Q_PER_KV)], axis=0)
    m = jnp.full((rows * Q_PER_KV, 1), -1e30, F32)
    acc = jnp.zeros((rows * Q_PER_KV, 2 * HEAD_DIM), F32)
    for k, v in segments:
        s = _dot_nt(qs, k)
        m_new = jnp.maximum(m, jnp.max(s, axis=-1, keepdims=True))
        acc = acc * jnp.exp2(m - m_new) + _dot(jnp.exp2(s - m_new).astype(BF16), v)
        m = m_new
    o = (acc[:, :HEAD_DIM] / acc[:, HEAD_DIM:]).astype(BF16)
    return jnp.concatenate([o[h * rows:(h + 1) * rows] for h in range(Q_PER_KV)], axis=1)


def _attn_ctx_kernel(q_ref, k_ref, v_ref, o_ref):
    gw = Q_PER_KV * HEAD_DIM
    for g in range(N_KV):
        kg = k_ref[:, g * HEAD_DIM:(g + 1) * HEAD_DIM]
        vg = _with_ones(v_ref[:, g * HEAD_DIM:(g + 1) * HEAD_DIM])
        o_ref[:, g * gw:(g + 1) * gw] = _gqa_attention(q_ref[:, g * gw:(g + 1) * gw], [(kg, vg)])


def _attn_ctx(q, k, v, seq):
    n_tok = q.shape[0]
    kvw = N_KV * HEAD_DIM
    return pl.pallas_call(
        _attn_ctx_kernel,
        out_shape=jax.ShapeDtypeStruct((n_tok, D_MODEL), BF16),
        grid=(n_tok // seq,),
        in_specs=[pl.BlockSpec((seq, D_MODEL), lambda b: (b, 0)),
                  pl.BlockSpec((seq, kvw), lambda b: (b, 0)),
                  pl.BlockSpec((seq, kvw), lambda b: (b, 0))],
        out_specs=pl.BlockSpec((seq, D_MODEL), lambda b: (b, 0)),
        compiler_params=_cparams(1),
        name="attn_ctx",
    )(q, k, v)


KEY_CHUNK = 1024


def _attn_lat_kernel(q_ref, k_ref, v_ref, kc_ref, vc_ref, o_ref):
    segments = [(k_ref[c:c + KEY_CHUNK], _with_ones(v_ref[c:c + KEY_CHUNK]))
                for c in range(0, k_ref.shape[0], KEY_CHUNK)]
    segments.append((kc_ref[0], _with_ones(vc_ref[0])))
    o_ref[...] = _gqa_attention(q_ref[...], segments)


def _attn_lat(q, k, v, kc, vc, seq, tq):
    n_tok = q.shape[0]
    batch = n_tok // seq
    past = kc.shape[1]
    qt = seq // tq
    gw = Q_PER_KV * HEAD_DIM
    return pl.pallas_call(
        _attn_lat_kernel,
        out_shape=jax.ShapeDtypeStruct((n_tok, D_MODEL), BF16),
        grid=(batch, N_KV, qt),
        in_specs=[pl.BlockSpec((tq, gw), lambda b, g, i: (b * qt + i, g)),
                  pl.BlockSpec((seq, HEAD_DIM), lambda b, g, i: (b, g)),
                  pl.BlockSpec((seq, HEAD_DIM), lambda b, g, i: (b, g)),
                  pl.BlockSpec((1, past, HEAD_DIM), lambda b, g, i: (b, 0, g)),
                  pl.BlockSpec((1, past, HEAD_DIM), lambda b, g, i: (b, 0, g))],
        out_specs=pl.BlockSpec((tq, gw), lambda b, g, i: (b * qt + i, g)),
        compiler_params=_cparams(3),
        name="attn_lat",
    )(q, k, v, kc, vc)


def _moe_front(x1, m, n2, wr, x1_ref, hp_ref, aff_ref):
    tm = x1.shape[0]
    x1_ref[...] = x1
    h2 = _rms(x1, n2) * (1.0 + m[4:5]) + m[3:4]
    logits = _dot(h2.astype(BF16), wr)
    lane = lax.broadcasted_iota(I32, logits.shape, 1)
    logits = jnp.where(lane < N_EXPERTS, logits, -1e30)
    ex = jnp.exp(logits - jnp.max(logits, axis=-1, keepdims=True))
    aff = ex / jnp.sum(ex, axis=-1, keepdims=True)
    packed = pltpu.pack_elementwise([h2[:, :HALF], h2[:, HALF:]], packed_dtype=BF16)
    for s in range(HP_ROWS):
        hp_ref[pl.ds(s, tm, stride=HP_ROWS), :] = packed[:, s * LANES:(s + 1) * LANES]
    aff_t = aff.T
    for b in range(tm // LANES):
        aff_ref[b * N_EXPERTS:(b + 1) * N_EXPERTS, :] = aff_t[0:N_EXPERTS, b * LANES:(b + 1) * LANES]


def _front_out(n_tok, tm):
    shapes = [jax.ShapeDtypeStruct((n_tok, D_MODEL), F32),
              jax.ShapeDtypeStruct((n_tok * HP_ROWS, LANES), U32),
              jax.ShapeDtypeStruct((n_tok // LANES * N_EXPERTS, LANES), F32)]
    specs = [pl.BlockSpec((tm, D_MODEL), lambda i: (i, 0)),
             pl.BlockSpec((tm * HP_ROWS, LANES), lambda i: (i, 0)),
             pl.BlockSpec((tm // LANES * N_EXPERTS, LANES), lambda i: (i, 0))]
    return shapes, specs


def _post_kernel(o_ref, x_ref, mod_ref, n2_ref, wo_ref, wr_ref, x1_ref, hp_ref, aff_ref, wb_ref):
    @pl.when(pl.program_id(0) == 0)
    def _():
        wb_ref[...] = wo_ref[...].astype(BF16)

    m = mod_ref[0, 0]
    x1 = x_ref[...] + m[2:3] * _dot(o_ref[...], wb_ref[...])
    _moe_front(x1, m, n2_ref[...], wr_ref[...], x1_ref, hp_ref, aff_ref)


def _post(o, x, mods, layer, row_of_tile, n2, w_o, wr, tm):
    n_tok = x.shape[0]
    shapes, specs = _front_out(n_tok, tm)
    return pl.pallas_call(
        _post_kernel,
        out_shape=shapes,
        grid=(n_tok // tm,),
        in_specs=[pl.BlockSpec((tm, D_MODEL), lambda i: (i, 0)),
                  pl.BlockSpec((tm, D_MODEL), lambda i: (i, 0)),
                  _mod_spec(layer, row_of_tile),
                  pl.BlockSpec((1, D_MODEL), lambda i: (0, 0)),
                  pl.BlockSpec((D_MODEL, D_MODEL), lambda i: (0, 0)),
                  pl.BlockSpec((D_MODEL, LANES), lambda i: (0, 0))],
        out_specs=specs,
        scratch_shapes=[pltpu.VMEM((D_MODEL, D_MODEL), BF16)],
        compiler_params=_cparams(1),
        name="post",
    )(o, x, mods, n2, w_o, wr)


def _pool_kernel(x_ref, xp_ref, xn_ref, mod_ref, n1_ref, n2_ref, wp_ref, ps_ref, wr_ref,
                 x1_ref, hp_ref, aff_ref, *, seq):
    tm = x_ref.shape[0]
    i = pl.program_id(0)
    m = mod_ref[0, 0]
    x = x_ref[...]

    def norm_mod(v):
        return _rms(v, n1_ref[...]) * (1.0 + m[1:2]) + m[0:1]

    prev_ok = ((i * tm) % seq != 0).astype(F32)
    next_ok = (((i + 1) * tm) % seq != 0).astype(F32)
    h = norm_mod(x)
    hz = jnp.concatenate([norm_mod(xp_ref[...]) * prev_ok, h, norm_mod(xn_ref[...]) * next_ok], axis=0)
    rows = tm + 2 * POOL_HALO
    t = (i * tm) % seq + lax.broadcasted_iota(I32, (tm, 1), 0)
    ys = []
    for g, w in enumerate(POOL_WINDOWS):
        sl = slice(g * POOL_GROUP, (g + 1) * POOL_GROUP)
        f = hz[:, sl]
        step = 1
        while step < w:
            f = f + pltpu.roll(f, rows - step, 0)
            step *= 2
        win = pltpu.roll(f, w // 2, 0)[POOL_HALO:POOL_HALO + tm]
        cnt = (jnp.minimum(t + w // 2, seq) - jnp.maximum(t - w // 2, 0)).astype(F32)
        d = (win / cnt - h[:, sl]).astype(BF16)
        ys.append(_dot(d, wp_ref[g].astype(BF16)))
    y = jnp.concatenate(ys, axis=-1) * ps_ref[...]
    x1 = x + m[2:3] * y
    _moe_front(x1, m, n2_ref[...], wr_ref[...], x1_ref, hp_ref, aff_ref)


def _pool(x, mods, layer, row_of_tile, n1, n2, w_pool, pool_scale, wr, seq, tm):
    n_tok = x.shape[0]
    hb = tm // POOL_HALO
    last = n_tok // POOL_HALO - 1
    shapes, specs = _front_out(n_tok, tm)
    return pl.pallas_call(
        functools.partial(_pool_kernel, seq=seq),
        out_shape=shapes,
        grid=(n_tok // tm,),
        in_specs=[pl.BlockSpec((tm, D_MODEL), lambda i: (i, 0)),
                  pl.BlockSpec((POOL_HALO, D_MODEL), lambda i: (jnp.maximum(i * hb - 1, 0), 0)),
                  pl.BlockSpec((POOL_HALO, D_MODEL), lambda i: (jnp.minimum((i + 1) * hb, last), 0)),
                  _mod_spec(layer, row_of_tile),
                  pl.BlockSpec((1, D_MODEL), lambda i: (0, 0)),
                  pl.BlockSpec((1, D_MODEL), lambda i: (0, 0)),
                  pl.BlockSpec(w_pool.shape, lambda i: (0, 0, 0)),
                  pl.BlockSpec((1, D_MODEL), lambda i: (0, 0)),
                  pl.BlockSpec((D_MODEL, LANES), lambda i: (0, 0))],
        out_specs=specs,
        compiler_params=_cparams(1),
        name="pool",
    )(x, x, x, mods, n1, n2, w_pool, pool_scale, wr)


def _select_kernel(aff_ref, idx_ref, gate_ref, *, cap):
    nb = aff_ref.shape[0] // N_EXPERTS
    rows = nb * N_EXPERTS
    shape3 = (nb, N_EXPERTS, LANES)
    aff = aff_ref[...].reshape(shape3)

    def count(mask):
        return jnp.sum(jnp.sum(mask.astype(F32), axis=0), axis=-1, keepdims=True)

    def search(i, t):
        cand = t | (jnp.int32(1) << (29 - i))
        return jnp.where(count(aff >= pltpu.bitcast(cand, F32)[None]) >= cap, cand, t)

    thr = pltpu.bitcast(lax.fori_loop(0, 30, search, jnp.zeros((N_EXPERTS, LANES), I32)), F32)
    gt = aff > thr[None]
    eq = aff == thr[None]
    need = cap - count(gt)

    kk = lax.broadcasted_iota(I32, (LANES, LANES), 0)
    nn = lax.broadcasted_iota(I32, (LANES, LANES), 1)
    upper = (kk <= nn).astype(BF16)
    ones = jnp.ones((LANES, LANES), BF16)

    def prefix(mask):
        m2 = mask.astype(F32).astype(BF16).reshape(rows, LANES)
        p = _dot(m2, upper).reshape(shape3)
        s = _dot(m2, ones).reshape(shape3)
        offs = []
        run = jnp.zeros((N_EXPERTS, LANES), F32)
        for b in range(nb):
            offs.append(run)
            run = run + s[b]
        return p, jnp.stack(offs, axis=0), s

    pe, oe, _ = prefix(eq)
    sel = gt | (eq & ((pe + oe) <= need[None]))
    ps, os_, ss = prefix(sel)

    lane = lax.broadcasted_iota(I32, (rows, LANES), 1)
    sel2 = sel.reshape(rows, LANES)
    dist = jnp.where(sel2, lane - (ps.reshape(rows, LANES).astype(I32) - 1), 0)
    alive = sel2.astype(I32)
    val = lane
    gval = aff.reshape(rows, LANES)
    for k in range(7):
        s = 1 << k
        move = alive * ((dist >> k) & 1)
        inc = pltpu.roll(move, LANES - s, 1) * (lane < LANES - s).astype(I32) == 1
        val = jnp.where(inc, pltpu.roll(val, LANES - s, 1), val)
        gval = jnp.where(inc, pltpu.roll(gval, LANES - s, 1), gval)
        dist = jnp.where(inc, pltpu.roll(dist, LANES - s, 1), dist)
        alive = alive - move + inc.astype(I32)
    local = val.reshape(shape3)
    local_gate = gval.reshape(shape3)

    top = os_ + ss
    lane_e = lax.broadcasted_iota(I32, (N_EXPERTS, LANES), 1)
    for jc in range(cap // LANES):
        j = (lane_e + jc * LANES).astype(F32)
        acc = jnp.zeros((N_EXPERTS, LANES), I32)
        gacc = jnp.zeros((N_EXPERTS, LANES), F32)
        for b in range(nb):
            inside = (os_[b] <= j) & (j < top[b])
            jl = (j - os_[b]).astype(I32) & (LANES - 1)
            acc = jnp.where(inside, jnp.take_along_axis(local[b], jl, axis=1) + b * LANES, acc)
            gacc = jnp.where(inside, jnp.take_along_axis(local_gate[b], jl, axis=1), gacc)
        idx_ref[:, jc * LANES:(jc + 1) * LANES] = acc
        gate_ref[:, jc * LANES:(jc + 1) * LANES] = gacc


def _select(aff, cap):
    rows = aff.shape[0]
    return pl.pallas_call(
        functools.partial(_select_kernel, cap=cap),
        out_shape=[jax.ShapeDtypeStruct((N_EXPERTS, cap), I32),
                   jax.ShapeDtypeStruct((N_EXPERTS, cap), F32)],
        grid=(1,),
        in_specs=[pl.BlockSpec((rows, LANES), lambda i: (0, 0))],
        out_specs=[pl.BlockSpec((N_EXPERTS, cap), lambda i: (0, 0)),
                   pl.BlockSpec((N_EXPERTS, cap), lambda i: (0, 0))],
        compiler_params=_cparams(1),
        name="select",
    )(aff)


GATHER_UNROLL = 8


FF_CHUNK = 256


def _ffn_kernel(idx_ref, hp_ref, wg_ref, wu_ref, wd_ref, y_ref, slab_ref, xb_ref, yacc_ref, *, nf):
    e = pl.program_id(0)
    f = pl.program_id(1)
    cap = xb_ref.shape[0]
    ft = wg_ref.shape[3]

    def copy_row(table_base, j):
        src = pl.multiple_of(idx_ref[table_base + j] * HP_ROWS, HP_ROWS)
        slab_ref[pl.ds(pl.multiple_of(j * HP_ROWS, HP_ROWS), HP_ROWS), :] = hp_ref[pl.ds(src, HP_ROWS), :]

    @pl.when((e == 0) & (f == 0))
    def _():
        def gather(g, carry):
            for r in range(GATHER_UNROLL):
                copy_row(0, g * GATHER_UNROLL + r)
            return carry

        lax.fori_loop(0, cap // GATHER_UNROLL, gather, 0)

    @pl.when(f == 0)
    def _():
        for s in range(HP_ROWS):
            w = slab_ref[pl.ds(s, cap, stride=HP_ROWS), :]
            for half in range(2):
                v = pltpu.unpack_elementwise(w, index=half, packed_dtype=BF16, unpacked_dtype=F32)
                xb_ref[:, half * HALF + s * LANES:half * HALF + (s + 1) * LANES] = v.astype(BF16)

    part = cap // nf
    nxt = jnp.minimum(e + 1, N_EXPERTS - 1) * cap
    for r in range(part):
        copy_row(nxt, f * part + r)
    xe = xb_ref[...]
    y = None
    for c in range(0, ft, FF_CHUNK):
        a = _dot(xe, wg_ref[0, 0, :, c:c + FF_CHUNK].astype(BF16))
        u = _dot(xe, wu_ref[0, 0, :, c:c + FF_CHUNK].astype(BF16))
        yc = _dot((_silu(a) * u).astype(BF16), wd_ref[0, 0, c:c + FF_CHUNK, :].astype(BF16))
        y = yc if y is None else y + yc

    @pl.when(f == 0)
    def _():
        yacc_ref[...] = y

    @pl.when((f > 0) & (f < nf - 1))
    def _():
        yacc_ref[...] += y

    @pl.when(f == nf - 1)
    def _():
        tot = yacc_ref[...] + y
        for s in range(Y_ROWS):
            y_ref[0, pl.ds(s, cap, stride=Y_ROWS), :] = tot[:, s * LANES:(s + 1) * LANES]


def _ffn(idx, hp, layer, w_gate, w_up, w_down, ft):
    cap = idx.shape[0] // N_EXPERTS
    d_ff = w_gate.shape[3]
    nf = d_ff // ft
    assert nf >= 2 and cap % nf == 0 and ft % FF_CHUNK == 0
    return pl.pallas_call(
        functools.partial(_ffn_kernel, nf=nf),
        out_shape=jax.ShapeDtypeStruct((N_EXPERTS, cap * Y_ROWS, LANES), F32),
        grid_spec=pltpu.PrefetchScalarGridSpec(
            num_scalar_prefetch=1,
            grid=(N_EXPERTS, d_ff // ft),
            in_specs=[pl.BlockSpec(hp.shape, lambda e, f, idx: (0, 0), pipeline_mode=pl.Buffered(1)),
                      pl.BlockSpec((1, 1, D_MODEL, ft), lambda e, f, idx: (layer, e, 0, f)),
                      pl.BlockSpec((1, 1, D_MODEL, ft), lambda e, f, idx: (layer, e, 0, f)),
                      pl.BlockSpec((1, 1, ft, D_MODEL), lambda e, f, idx: (layer, e, f, 0))],
            out_specs=pl.BlockSpec((1, cap * Y_ROWS, LANES), lambda e, f, idx: (e, 0, 0)),
            scratch_shapes=[pltpu.VMEM((cap * HP_ROWS, LANES), U32),
                            pltpu.VMEM((cap, D_MODEL), BF16),
                            pltpu.VMEM((cap, D_MODEL), F32)]),
        compiler_params=_cparams(2),
        name="ffn",
    )(idx, hp, w_gate, w_up, w_down)


SCATTER_UNROLL = 8


def _combine_kernel(idx_ref, gate_ref, y_ref, x1_ref, mod_ref, o_ref, acc_ref):
    s = pl.program_id(0)
    tf = o_ref.shape[0]
    cap = y_ref.shape[1] // Y_ROWS

    @pl.when(s == 0)
    def _():
        acc_ref[...] = jnp.zeros_like(acc_ref)

    @pl.when(s < N_EXPERTS)
    def _():
        def scatter(g, carry):
            rows, sums = [], []
            for r in range(SCATTER_UNROLL):
                j = g * SCATTER_UNROLL + r
                dst = pl.ds(pl.multiple_of(idx_ref[s * cap + j] * Y_ROWS, Y_ROWS), Y_ROWS)
                src = pl.ds(pl.multiple_of(j * Y_ROWS, Y_ROWS), Y_ROWS)
                rows.append(dst)
                sums.append(acc_ref[dst, :] + gate_ref[s * cap + j] * y_ref[0, src, :])
            for dst, v in zip(rows, sums):
                acc_ref[dst, :] = v
            return carry

        lax.fori_loop(0, cap // SCATTER_UNROLL, scatter, 0)

    @pl.when(s >= N_EXPERTS)
    def _():
        g2 = mod_ref[0, 0][5:6]
        tile = acc_ref.at[pl.ds(pl.multiple_of((s - N_EXPERTS) * tf * Y_ROWS, tf * Y_ROWS), tf * Y_ROWS), :]
        for c in range(Y_ROWS):
            sl = slice(c * LANES, (c + 1) * LANES)
            o_ref[:, sl] = x1_ref[:, sl] + g2[:, sl] * tile[pl.ds(c, tf, stride=Y_ROWS), :]


def _combine(idx, gate, y, x1, mods, layer, row_of_tile, tf):
    n_tok = x1.shape[0]
    cap = idx.shape[0] // N_EXPERTS
    tile_of = lambda s: jnp.maximum(s - N_EXPERTS, 0)
    return pl.pallas_call(
        _combine_kernel,
        out_shape=jax.ShapeDtypeStruct((n_tok, D_MODEL), F32),
        grid_spec=pltpu.PrefetchScalarGridSpec(
            num_scalar_prefetch=2,
            grid=(N_EXPERTS + n_tok // tf,),
            in_specs=[pl.BlockSpec((1, cap * Y_ROWS, LANES), lambda s, *_: (jnp.minimum(s, N_EXPERTS - 1), 0, 0)),
                      pl.BlockSpec((tf, D_MODEL), lambda s, *_: (tile_of(s), 0)),
                      pl.BlockSpec((1, 1, N_MOD, D_MODEL), lambda s, *_: (layer, row_of_tile(tile_of(s)), 0, 0))],
            out_specs=pl.BlockSpec((tf, D_MODEL), lambda s, *_: (tile_of(s), 0)),
            scratch_shapes=[pltpu.VMEM((n_tok * Y_ROWS, LANES), F32)]),
        compiler_params=_cparams(1),
        name="combine",
    )(idx, gate, y, x1, mods)


TOKEN_TILE = 512
Q_TILE = 256
FF_TILE = 512


def _moe(x1, hp, aff, mods, layer, row_of_tile, w_gate, w_up, w_down):
    n_tok = x1.shape[0]
    cap = CAPACITY_FACTOR * n_tok // N_EXPERTS
    idx, gate = _select(aff, cap)
    idx = idx.reshape(-1)
    gate = gate.reshape(-1)
    y = _ffn(idx, hp, layer, w_gate, w_up, w_down, FF_TILE)
    return _combine(idx, gate, y, x1, mods, layer, row_of_tile, TOKEN_TILE)


def kernel(x_prompt, x_sample, cache_k, cache_v, c, c_ctx, norm1, norm2, w_ada, b_ada, w_qkv, q_norm, k_norm,
           w_o, w_pool, pool_scale, w_router, w_e_gate, w_e_up, w_e_down):
    batch, seq, _ = x_prompt.shape
    dec_batch, dec_seq, _ = x_sample.shape
    depth = w_ada.shape[0]
    kvw = N_KV * HEAD_DIM

    cvec = jnp.zeros((SUBLANES, D_MODEL), F32).at[0].set(c_ctx).at[1:1 + dec_batch].set(c)
    mods = _ada(cvec, w_ada, b_ada)
    ctx = x_prompt.reshape(batch * seq, D_MODEL)
    lat = x_sample.reshape(dec_batch * dec_seq, D_MODEL)
    ctx_row = lambda i: 0
    lat_row = lambda tile: (lambda i: 1 + (i * tile) // dec_seq)
    ctx_tile = min(TOKEN_TILE, seq)
    rope_tabs = _rope_tables(dec_seq)
    new_k = new_v = None

    for layer in range(depth):
        j = layer // 2
        n1 = norm1[layer][None]
        n2 = norm2[layer][None]
        wr = jnp.pad(w_router[layer], ((0, 0), (0, LANES - N_EXPERTS))).astype(BF16)
        if layer % 2 == 0:
            qn = q_norm[j][None]
            kn = k_norm[j][None]
            qc, kc, vc, new_k, new_v = _qkv(ctx, mods, layer, ctx_row, n1, w_qkv[j], qn, kn, None, True, TOKEN_TILE)
            oc = _attn_ctx(qc, kc, vc, seq)
            ql, kl, vl = _qkv(lat, mods, layer, lat_row(TOKEN_TILE), n1, w_qkv[j], qn, kn, rope_tabs, False,
                              TOKEN_TILE)
            past_k = cache_k[:, j].reshape(dec_batch, -1, kvw).astype(BF16)
            past_v = cache_v[:, j].reshape(dec_batch, -1, kvw).astype(BF16)
            ol = _attn_lat(ql, kl, vl, past_k, past_v, dec_seq, Q_TILE)
            ctx1 = _post(oc, ctx, mods, layer, ctx_row, n2, w_o[j], wr, TOKEN_TILE)
            lat1 = _post(ol, lat, mods, layer, lat_row(TOKEN_TILE), n2, w_o[j], wr, TOKEN_TILE)
        else:
            ps = pool_scale[j][None]
            ctx1 = _pool(ctx, mods, layer, ctx_row, n1, n2, w_pool[j], ps, wr, seq, ctx_tile)
            lat1 = _pool(lat, mods, layer, lat_row(TOKEN_TILE), n1, n2, w_pool[j], ps, wr, dec_seq, TOKEN_TILE)
        ctx = _moe(*ctx1, mods, layer, ctx_row, w_e_gate, w_e_up, w_e_down)
        lat = _moe(*lat1, mods, layer, lat_row(TOKEN_TILE), w_e_gate, w_e_up, w_e_down)

    new_cache_k = new_k.reshape(batch, 1, seq, N_KV, HEAD_DIM)
    new_cache_v = new_v.reshape(batch, 1, seq, N_KV, HEAD_DIM)
    return (ctx.reshape(batch, seq, D_MODEL), lat.reshape(dec_batch, dec_seq, D_MODEL), new_cache_k, new_cache_v)
```

```python
import functools

import jax
import jax.numpy as jnp
import numpy as np
from jax import lax
from jax.experimental import pallas as pl
from jax.experimental.pallas import tpu as pltpu

F32 = jnp.float32
BF16 = jnp.bfloat16
I32 = jnp.int32
U32 = jnp.uint32

D_MODEL = 1024
HEAD_DIM = 128
N_HEADS = 8
N_KV = 2
Q_PER_KV = N_HEADS // N_KV
QKV_DIM = (N_HEADS + 2 * N_KV) * HEAD_DIM
GRID_W = 64
ROPE_THETA = 10000.0
POOL_WINDOWS = (2, 4, 8, 16)
POOL_GROUP = D_MODEL // len(POOL_WINDOWS)
POOL_HALO = 8
N_EXPERTS = 16
CAPACITY_FACTOR = 2
N_MOD = 6
EPS = 1e-6
LOG2_E = 1.4426950408889634

LANES = 128
SUBLANES = 8
HALF = D_MODEL // 2
HP_ROWS = HALF // LANES
Y_ROWS = D_MODEL // LANES
VMEM_LIMIT = 56 * 1024 * 1024


def _cparams(n_axes, vmem=VMEM_LIMIT):
    return pltpu.CompilerParams(dimension_semantics=("arbitrary",) * n_axes, vmem_limit_bytes=vmem)


def _silu(x):
    return x / (1.0 + jnp.exp(-x))


def _rms(x, gain):
    return x * lax.rsqrt(jnp.mean(x * x, axis=-1, keepdims=True) + EPS) * gain


def _dot(a, b):
    return jnp.dot(a, b, preferred_element_type=F32)


def _dot_nt(a, b):
    return lax.dot_general(a, b, (((1,), (1,)), ((), ())), preferred_element_type=F32)


def _ada_kernel(c_ref, w_ref, b_ref, o_ref):
    s = _silu(c_ref[...]).astype(BF16)
    o_ref[0] = _dot(s, w_ref[0].astype(BF16)) + b_ref[0]


def _ada(cvec, w_ada, b_ada):
    depth = w_ada.shape[0]
    out = pl.pallas_call(
        _ada_kernel,
        out_shape=jax.ShapeDtypeStruct((depth, SUBLANES, N_MOD * D_MODEL), F32),
        grid=(depth, N_MOD),
        in_specs=[
            pl.BlockSpec((SUBLANES, D_MODEL), lambda i, j: (0, 0)),
            pl.BlockSpec((1, D_MODEL, D_MODEL), lambda i, j: (i, 0, j)),
            pl.BlockSpec((1, 1, D_MODEL), lambda i, j: (i, 0, j)),
        ],
        out_specs=pl.BlockSpec((1, SUBLANES, D_MODEL), lambda i, j: (i, 0, j)),
        compiler_params=_cparams(2),
        name="ada",
    )(cvec, w_ada, b_ada.reshape(depth, 1, N_MOD * D_MODEL))
    return out.reshape(depth, SUBLANES, N_MOD, D_MODEL)


def _mod_spec(layer, row_of_tile):
    return pl.BlockSpec((1, 1, N_MOD, D_MODEL), lambda i, *_: (layer, row_of_tile(i), 0, 0))


def _qkv_kernel(*refs, rope, cache_out):
    x_ref, mod_ref, n1_ref, w_ref, qn_ref, kn_ref = refs[:6]
    refs = refs[6:]
    if rope:
        cos_ref, sin_ref = refs[:2]
        refs = refs[2:]
    q_ref, k_ref, v_ref = refs[:3]
    refs = refs[3:]
    if cache_out:
        kc_ref, vc_ref = refs[:2]
        refs = refs[2:]
    (wb_ref,) = refs

    qk_w = (N_HEADS + N_KV) * HEAD_DIM
    quarter = HEAD_DIM // 4

    def partner(a):
        width = a.shape[1]
        first = (lax.broadcasted_iota(I32, a.shape, 1) & quarter) == 0
        return jnp.where(first, pltpu.roll(a, width - quarter, 1), pltpu.roll(a, quarter, 1))

    @pl.when(pl.program_id(0) == 0)
    def _():
        w = w_ref[...]
        wb_ref[:, 0:QKV_DIM] = w.astype(BF16)
        if rope:
            wb_ref[:, QKV_DIM:] = partner(w[:, 0:qk_w]).astype(BF16)

    m = mod_ref[0, 0]
    h = _rms(x_ref[...], n1_ref[...]) * (1.0 + m[1:2]) + m[0:1]
    qkv = _dot(h.astype(BF16), wb_ref[...])
    scale = HEAD_DIM ** -0.5 * LOG2_E
    if rope:
        gains = {True: qn_ref[...], False: kn_ref[...]}
        cos_g = {key: cos_ref[...] * g for key, g in gains.items()}
        sin_g = {key: sin_ref[...] * partner(g) for key, g in gains.items()}
    for hh in range(N_HEADS + N_KV):
        sl = slice(hh * HEAD_DIM, (hh + 1) * HEAD_DIM)
        is_q = hh < N_HEADS
        if rope:
            raw = qkv[:, sl]
            norm = lax.rsqrt(jnp.mean(raw * raw, axis=-1, keepdims=True) + EPS)
            xh = (raw * cos_g[is_q] + qkv[:, QKV_DIM + hh * HEAD_DIM:QKV_DIM + (hh + 1) * HEAD_DIM] * sin_g[is_q]) * norm
        else:
            xh = _rms(qkv[:, sl], qn_ref[...] if is_q else kn_ref[...])
        if cache_out and hh >= N_HEADS:
            kc_ref[:, (hh - N_HEADS) * HEAD_DIM:(hh - N_HEADS + 1) * HEAD_DIM] = xh
        if hh < N_HEADS:
            q_ref[:, sl] = (xh * scale).astype(BF16)
        else:
            k_ref[:, (hh - N_HEADS) * HEAD_DIM:(hh - N_HEADS + 1) * HEAD_DIM] = xh.astype(BF16)
    v = qkv[:, qk_w:QKV_DIM]
    v_ref[...] = v.astype(BF16)
    if cache_out:
        vc_ref[...] = v


def _qkv(x, mods, layer, row_of_tile, n1, w_qkv, qn, kn, rope_tabs, cache_out, tm):
    n_tok = x.shape[0]
    kvw = N_KV * HEAD_DIM
    rope = rope_tabs is not None
    in_specs = [
        pl.BlockSpec((tm, D_MODEL), lambda i: (i, 0)),
        _mod_spec(layer, row_of_tile),
        pl.BlockSpec((1, D_MODEL), lambda i: (0, 0)),
        pl.BlockSpec((D_MODEL, QKV_DIM), lambda i: (0, 0)),
        pl.BlockSpec((1, HEAD_DIM), lambda i: (0, 0)),
        pl.BlockSpec((1, HEAD_DIM), lambda i: (0, 0)),
    ]
    args = [x, mods, n1, w_qkv, qn, kn]
    if rope:
        seq_tiles = rope_tabs[0].shape[0] // tm
        in_specs += [pl.BlockSpec((tm, HEAD_DIM), lambda i: (i % seq_tiles, 0))] * 2
        args += list(rope_tabs)
    out_shape = [jax.ShapeDtypeStruct((n_tok, D_MODEL), BF16),
                 jax.ShapeDtypeStruct((n_tok, kvw), BF16),
                 jax.ShapeDtypeStruct((n_tok, kvw), BF16)]
    out_specs = [pl.BlockSpec((tm, D_MODEL), lambda i: (i, 0)),
                 pl.BlockSpec((tm, kvw), lambda i: (i, 0)),
                 pl.BlockSpec((tm, kvw), lambda i: (i, 0))]
    if cache_out:
        out_shape += [jax.ShapeDtypeStruct((n_tok, kvw), F32)] * 2
        out_specs += [pl.BlockSpec((tm, kvw), lambda i: (i, 0))] * 2
    return pl.pallas_call(
        functools.partial(_qkv_kernel, rope=rope, cache_out=cache_out),
        out_shape=out_shape,
        grid=(n_tok // tm,),
        in_specs=in_specs,
        out_specs=out_specs,
        scratch_shapes=[pltpu.VMEM((D_MODEL, QKV_DIM + ((N_HEADS + N_KV) * HEAD_DIM if rope else 0)), BF16)],
        compiler_params=_cparams(1),
        name="qkv_rope" if rope else "qkv",
    )(*args)


def _rope_tables(seq_len):
    half = HEAD_DIM // 2
    n = half // 2
    inv_freq = ROPE_THETA ** (-np.arange(n, dtype=np.float64) / n)
    rows = seq_len // GRID_W
    row = np.repeat(np.arange(rows), GRID_W).astype(np.float64)
    col = np.tile(np.arange(GRID_W), rows).astype(np.float64)
    ang_r = row[:, None] * inv_freq[None, :]
    ang_c = col[:, None] * inv_freq[None, :]
    cos = np.concatenate([np.cos(ang_r)] * 2 + [np.cos(ang_c)] * 2, axis=-1)
    sin = np.concatenate([-np.sin(ang_r), np.sin(ang_r), -np.sin(ang_c), np.sin(ang_c)], axis=-1)
    return jnp.asarray(cos, F32), jnp.asarray(sin, F32)


def _with_ones(v):
    return jnp.concatenate([v, jnp.ones_like(v)], axis=1)


def _gqa_attention(q, segments):
    rows = q.shape[0]
    qs = jnp.concatenate([q[:, h * HEAD_DIM:(h + 1) * HEAD_DIM] for h in range(Q_PER_KV)], axis=0)
    m = jnp.full((rows * Q_PER_KV, 1), -1e30, F32)
    acc = jnp.zeros((rows * Q_PER_KV, 2 * HEAD_DIM), F32)
    for k, v in segments:
        s = _dot_nt(qs, k)
        m_new = jnp.maximum(m, jnp.max(s, axis=-1, keepdims=True))
        acc = acc * jnp.exp2(m - m_new) + _dot(jnp.exp2(s - m_new).astype(BF16), v)
        m = m_new
    o = (acc[:, :HEAD_DIM] / acc[:, HEAD_DIM:]).astype(BF16)
    return jnp.concatenate([o[h * rows:(h + 1) * rows] for h in range(Q_PER_KV)], axis=1)


def _attn_ctx_kernel(q_ref, k_ref, v_ref, o_ref):
    gw = Q_PER_KV * HEAD_DIM
    for g in range(N_KV):
        kg = k_ref[:, g * HEAD_DIM:(g + 1) * HEAD_DIM]
        vg = _with_ones(v_ref[:, g * HEAD_DIM:(g + 1) * HEAD_DIM])
        o_ref[:, g * gw:(g + 1) * gw] = _gqa_attention(q_ref[:, g * gw:(g + 1) * gw], [(kg, vg)])


def _attn_ctx(q, k, v, seq):
    n_tok = q.shape[0]
    kvw = N_KV * HEAD_DIM
    return pl.pallas_call(
        _attn_ctx_kernel,
        out_shape=jax.ShapeDtypeStruct((n_tok, D_MODEL), BF16),
        grid=(n_tok // seq,),
        in_specs=[pl.BlockSpec((seq, D_MODEL), lambda b: (b, 0)),
                  pl.BlockSpec((seq, kvw), lambda b: (b, 0)),
                  pl.BlockSpec((seq, kvw), lambda b: (b, 0))],
        out_specs=pl.BlockSpec((seq, D_MODEL), lambda b: (b, 0)),
        compiler_params=_cparams(1),
        name="attn_ctx",
    )(q, k, v)


KEY_CHUNK = 1024


def _attn_lat_kernel(q_ref, k_ref, v_ref, kc_ref, vc_ref, o_ref):
    segments = [(k_ref[c:c + KEY_CHUNK], _with_ones(v_ref[c:c + KEY_CHUNK]))
                for c in range(0, k_ref.shape[0], KEY_CHUNK)]
    segments.append((kc_ref[0], _with_ones(vc_ref[0])))
    o_ref[...] = _gqa_attention(q_ref[...], segments)


def _attn_lat(q, k, v, kc, vc, seq, tq):
    n_tok = q.shape[0]
    batch = n_tok // seq
    past = kc.shape[1]
    qt = seq // tq
    gw = Q_PER_KV * HEAD_DIM
    return pl.pallas_call(
        _attn_lat_kernel,
        out_shape=jax.ShapeDtypeStruct((n_tok, D_MODEL), BF16),
        grid=(batch, N_KV, qt),
        in_specs=[pl.BlockSpec((tq, gw), lambda b, g, i: (b * qt + i, g)),
                  pl.BlockSpec((seq, HEAD_DIM), lambda b, g, i: (b, g)),
                  pl.BlockSpec((seq, HEAD_DIM), lambda b, g, i: (b, g)),
                  pl.BlockSpec((1, past, HEAD_DIM), lambda b, g, i: (b, 0, g)),
                  pl.BlockSpec((1, past, HEAD_DIM), lambda b, g, i: (b, 0, g))],
        out_specs=pl.BlockSpec((tq, gw), lambda b, g, i: (b * qt + i, g)),
        compiler_params=_cparams(3),
        name="attn_lat",
    )(q, k, v, kc, vc)


def _moe_front(x1, m, n2, wr, x1_ref, hp_ref, aff_ref):
    tm = x1.shape[0]
    x1_ref[...] = x1
    h2 = _rms(x1, n2) * (1.0 + m[4:5]) + m[3:4]
    aff = _router_softmax(h2.astype(BF16), wr)
    packed = pltpu.pack_elementwise([h2[:, :HALF], h2[:, HALF:]], packed_dtype=BF16)
    for s in range(HP_ROWS):
        hp_ref[pl.ds(s, tm, stride=HP_ROWS), :] = packed[:, s * LANES:(s + 1) * LANES]
    aff_t = aff.T
    for b in range(tm // LANES):
        aff_ref[b * N_EXPERTS:(b + 1) * N_EXPERTS, :] = aff_t[0:N_EXPERTS, b * LANES:(b + 1) * LANES]


def _front_out(n_tok, tm):
    shapes = [jax.ShapeDtypeStruct((n_tok, D_MODEL), F32),
              jax.ShapeDtypeStruct((n_tok * HP_ROWS, LANES), U32),
              jax.ShapeDtypeStruct((n_tok // LANES * N_EXPERTS, LANES), F32)]
    specs = [pl.BlockSpec((tm, D_MODEL), lambda i: (i, 0)),
             pl.BlockSpec((tm * HP_ROWS, LANES), lambda i: (i, 0)),
             pl.BlockSpec((tm // LANES * N_EXPERTS, LANES), lambda i: (i, 0))]
    return shapes, specs


def _post_kernel(o_ref, x_ref, mod_ref, n2_ref, wo_ref, wr_ref, x1_ref, hp_ref, aff_ref, wb_ref):
    @pl.when(pl.program_id(0) == 0)
    def _():
        wb_ref[...] = wo_ref[...].astype(BF16)

    m = mod_ref[0, 0]
    x1 = x_ref[...] + m[2:3] * _dot(o_ref[...], wb_ref[...])
    _moe_front(x1, m, n2_ref[...], wr_ref[...], x1_ref, hp_ref, aff_ref)


def _post(o, x, mods, layer, row_of_tile, n2, w_o, wr, tm):
    n_tok = x.shape[0]
    shapes, specs = _front_out(n_tok, tm)
    return pl.pallas_call(
        _post_kernel,
        out_shape=shapes,
        grid=(n_tok // tm,),
        in_specs=[pl.BlockSpec((tm, D_MODEL), lambda i: (i, 0)),
                  pl.BlockSpec((tm, D_MODEL), lambda i: (i, 0)),
                  _mod_spec(layer, row_of_tile),
                  pl.BlockSpec((1, D_MODEL), lambda i: (0, 0)),
                  pl.BlockSpec((D_MODEL, D_MODEL), lambda i: (0, 0)),
                  pl.BlockSpec((D_MODEL, LANES), lambda i: (0, 0))],
        out_specs=specs,
        scratch_shapes=[pltpu.VMEM((D_MODEL, D_MODEL), BF16)],
        compiler_params=_cparams(1),
        name="post",
    )(o, x, mods, n2, w_o, wr)


def _pool_kernel(x_ref, xp_ref, xn_ref, mod_ref, n1_ref, n2_ref, wp_ref, ps_ref, wr_ref,
                 x1_ref, hp_ref, aff_ref, *, seq):
    tm = x_ref.shape[0]
    i = pl.program_id(0)
    m = mod_ref[0, 0]
    x = x_ref[...]

    def norm_mod(v):
        return _rms(v, n1_ref[...]) * (1.0 + m[1:2]) + m[0:1]

    prev_ok = ((i * tm) % seq != 0).astype(F32)
    next_ok = (((i + 1) * tm) % seq != 0).astype(F32)
    h = norm_mod(x)
    hz = jnp.concatenate([norm_mod(xp_ref[...]) * prev_ok, h, norm_mod(xn_ref[...]) * next_ok], axis=0)
    rows = tm + 2 * POOL_HALO
    t = (i * tm) % seq + lax.broadcasted_iota(I32, (tm, 1), 0)
    ys = []
    for g, w in enumerate(POOL_WINDOWS):
        sl = slice(g * POOL_GROUP, (g + 1) * POOL_GROUP)
        f = hz[:, sl]
        step = 1
        while step < w:
            f = f + pltpu.roll(f, rows - step, 0)
            step *= 2
        win = pltpu.roll(f, w // 2, 0)[POOL_HALO:POOL_HALO + tm]
        cnt = (jnp.minimum(t + w // 2, seq) - jnp.maximum(t - w // 2, 0)).astype(F32)
        d = (win / cnt - h[:, sl]).astype(BF16)
        ys.append(_dot(d, wp_ref[g].astype(BF16)))
    y = jnp.concatenate(ys, axis=-1) * ps_ref[...]
    x1 = x + m[2:3] * y
    _moe_front(x1, m, n2_ref[...], wr_ref[...], x1_ref, hp_ref, aff_ref)


def _pool(x, mods, layer, row_of_tile, n1, n2, w_pool, pool_scale, wr, seq, tm):
    n_tok = x.shape[0]
    hb = tm // POOL_HALO
    last = n_tok // POOL_HALO - 1
    shapes, specs = _front_out(n_tok, tm)
    return pl.pallas_call(
        functools.partial(_pool_kernel, seq=seq),
        out_shape=shapes,
        grid=(n_tok // tm,),
        in_specs=[pl.BlockSpec((tm, D_MODEL), lambda i: (i, 0)),
                  pl.BlockSpec((POOL_HALO, D_MODEL), lambda i: (jnp.maximum(i * hb - 1, 0), 0)),
                  pl.BlockSpec((POOL_HALO, D_MODEL), lambda i: (jnp.minimum((i + 1) * hb, last), 0)),
                  _mod_spec(layer, row_of_tile),
                  pl.BlockSpec((1, D_MODEL), lambda i: (0, 0)),
                  pl.BlockSpec((1, D_MODEL), lambda i: (0, 0)),
                  pl.BlockSpec(w_pool.shape, lambda i: (0, 0, 0)),
                  pl.BlockSpec((1, D_MODEL), lambda i: (0, 0)),
                  pl.BlockSpec((D_MODEL, LANES), lambda i: (0, 0))],
        out_specs=specs,
        compiler_params=_cparams(1),
        name="pool",
    )(x, x, x, mods, n1, n2, w_pool, pool_scale, wr)


def _select_kernel(aff_ref, src_ref, dst_ref, *, cap):
    nb = aff_ref.shape[0] // N_EXPERTS
    rows = nb * N_EXPERTS
    shape3 = (nb, N_EXPERTS, LANES)
    aff = aff_ref[...].reshape(shape3)

    def count(mask):
        return jnp.sum(jnp.sum(mask.astype(F32), axis=0), axis=-1, keepdims=True)

    def search(i, t):
        cand = t | (jnp.int32(1) << (29 - i))
        return jnp.where(count(aff >= pltpu.bitcast(cand, F32)[None]) >= cap, cand, t)

    thr = pltpu.bitcast(lax.fori_loop(0, 30, search, jnp.zeros((N_EXPERTS, LANES), I32)), F32)
    gt = aff > thr[None]
    eq = aff == thr[None]
    need = cap - count(gt)

    kk = lax.broadcasted_iota(I32, (LANES, LANES), 0)
    nn = lax.broadcasted_iota(I32, (LANES, LANES), 1)
    upper = (kk <= nn).astype(BF16)
    ones = jnp.ones((LANES, LANES), BF16)

    def prefix(mask):
        m2 = mask.astype(F32).astype(BF16).reshape(rows, LANES)
        p = _dot(m2, upper).reshape(shape3)
        s = _dot(m2, ones).reshape(shape3)
        offs = []
        run = jnp.zeros((N_EXPERTS, LANES), F32)
        for b in range(nb):
            offs.append(run)
            run = run + s[b]
        return p, jnp.stack(offs, axis=0), s

    pe, oe, _ = prefix(eq)
    sel = gt | (eq & ((pe + oe) <= need[None]))
    ps, os_, ss = prefix(sel)

    lane = lax.broadcasted_iota(I32, (rows, LANES), 1)
    sel2 = sel.reshape(rows, LANES)
    dist = jnp.where(sel2, lane - (ps.reshape(rows, LANES).astype(I32) - 1), 0)
    alive = sel2.astype(I32)
    val = lane
    for k in range(7):
        s = 1 << k
        move = alive * ((dist >> k) & 1)
        inc = pltpu.roll(move, LANES - s, 1) * (lane < LANES - s).astype(I32) == 1
        val = jnp.where(inc, pltpu.roll(val, LANES - s, 1), val)
        dist = jnp.where(inc, pltpu.roll(dist, LANES - s, 1), dist)
        alive = alive - move + inc.astype(I32)
    local = val.reshape(shape3)

    top = os_ + ss
    lane_e = lax.broadcasted_iota(I32, (N_EXPERTS, LANES), 1)
    for jc in range(cap // LANES):
        j = (lane_e + jc * LANES).astype(F32)
        acc = jnp.zeros((N_EXPERTS, LANES), I32)
        for b in range(nb):
            inside = (os_[b] <= j) & (j < top[b])
            jl = (j - os_[b]).astype(I32) & (LANES - 1)
            acc = jnp.where(inside, jnp.take_along_axis(local[b], jl, axis=1) + b * LANES, acc)
        src_ref[pl.ds(jc, N_EXPERTS, stride=cap // LANES), :] = acc * HP_ROWS
        dst_ref[pl.ds(jc, N_EXPERTS, stride=cap // LANES), :] = acc * Y_ROWS


def _select(aff, cap):
    rows = aff.shape[0]
    out_rows = N_EXPERTS * cap // LANES
    src, dst = pl.pallas_call(
        functools.partial(_select_kernel, cap=cap),
        out_shape=[jax.ShapeDtypeStruct((out_rows, LANES), I32),
                   jax.ShapeDtypeStruct((out_rows, LANES), I32)],
        grid=(1,),
        in_specs=[pl.BlockSpec((rows, LANES), lambda i: (0, 0))],
        out_specs=[pl.BlockSpec((out_rows, LANES), lambda i: (0, 0)),
                   pl.BlockSpec((out_rows, LANES), lambda i: (0, 0))],
        compiler_params=_cparams(1),
        name="select",
    )(aff)
    return src.reshape(-1), dst.reshape(-1)


GATHER_UNROLL = 8


FF_CHUNK = 256


def _router_softmax(hb, wr):
    logits = _dot(hb, wr)
    lane = lax.broadcasted_iota(I32, logits.shape, 1)
    logits = jnp.where(lane < N_EXPERTS, logits, -1e30)
    ex = jnp.exp(logits - jnp.max(logits, axis=-1, keepdims=True))
    return ex / jnp.sum(ex, axis=-1, keepdims=True)


def _ffn_kernel(src_ref, hp_ref, wr_ref, wg_ref, wu_ref, wd_ref, y_ref, slab_ref, xb_ref, yacc_ref, gate_ref, *, nf):
    e = pl.program_id(0)
    f = pl.program_id(1)
    cap = xb_ref.shape[0]
    ft = wg_ref.shape[3]

    def copy_row(table_base, j):
        src = pl.multiple_of(src_ref[table_base + j], HP_ROWS)
        slab_ref[pl.ds(pl.multiple_of(j * HP_ROWS, HP_ROWS), HP_ROWS), :] = hp_ref[pl.ds(src, HP_ROWS), :]

    @pl.when((e == 0) & (f == 0))
    def _():
        def gather(g, carry):
            for r in range(GATHER_UNROLL):
                copy_row(0, g * GATHER_UNROLL + r)
            return carry

        lax.fori_loop(0, cap // GATHER_UNROLL, gather, 0)

    @pl.when(f == 0)
    def _():
        for s in range(HP_ROWS):
            w = slab_ref[pl.ds(s, cap, stride=HP_ROWS), :]
            for half in range(2):
                v = pltpu.unpack_elementwise(w, index=half, packed_dtype=BF16, unpacked_dtype=F32)
                xb_ref[:, half * HALF + s * LANES:half * HALF + (s + 1) * LANES] = v.astype(BF16)
        aff = _router_softmax(xb_ref[...], wr_ref[...])
        lane = lax.broadcasted_iota(I32, aff.shape, 1)
        gate_ref[...] = jnp.sum(jnp.where(lane == e, aff, 0.0), axis=1, keepdims=True)

    part = cap // nf
    nxt = jnp.minimum(e + 1, N_EXPERTS - 1) * cap
    for r in range(part):
        copy_row(nxt, f * part + r)
    xe = xb_ref[...]
    y = None
    for c in range(0, ft, FF_CHUNK):
        a = _dot(xe, wg_ref[0, 0, :, c:c + FF_CHUNK].astype(BF16))
        u = _dot(xe, wu_ref[0, 0, :, c:c + FF_CHUNK].astype(BF16))
        yc = _dot((_silu(a) * u).astype(BF16), wd_ref[0, 0, c:c + FF_CHUNK, :].astype(BF16))
        y = yc if y is None else y + yc

    @pl.when(f == 0)
    def _():
        yacc_ref[...] = y

    @pl.when((f > 0) & (f < nf - 1))
    def _():
        yacc_ref[...] += y

    @pl.when(f == nf - 1)
    def _():
        tot = (yacc_ref[...] + y) * gate_ref[...]
        for s in range(Y_ROWS):
            y_ref[0, pl.ds(s, cap, stride=Y_ROWS), :] = tot[:, s * LANES:(s + 1) * LANES]


def _ffn(src, hp, wr, layer, w_gate, w_up, w_down, ft):
    cap = src.shape[0] // N_EXPERTS
    d_ff = w_gate.shape[3]
    nf = d_ff // ft
    assert nf >= 2 and cap % nf == 0 and ft % FF_CHUNK == 0
    return pl.pallas_call(
        functools.partial(_ffn_kernel, nf=nf),
        out_shape=jax.ShapeDtypeStruct((N_EXPERTS, cap * Y_ROWS, LANES), F32),
        grid_spec=pltpu.PrefetchScalarGridSpec(
            num_scalar_prefetch=1,
            grid=(N_EXPERTS, d_ff // ft),
            in_specs=[pl.BlockSpec(hp.shape, lambda e, f, idx: (0, 0), pipeline_mode=pl.Buffered(1)),
                      pl.BlockSpec((D_MODEL, LANES), lambda e, f, idx: (0, 0)),
                      pl.BlockSpec((1, 1, D_MODEL, ft), lambda e, f, idx: (layer, e, 0, f)),
                      pl.BlockSpec((1, 1, D_MODEL, ft), lambda e, f, idx: (layer, e, 0, f)),
                      pl.BlockSpec((1, 1, ft, D_MODEL), lambda e, f, idx: (layer, e, f, 0))],
            out_specs=pl.BlockSpec((1, cap * Y_ROWS, LANES), lambda e, f, idx: (e, 0, 0)),
            scratch_shapes=[pltpu.VMEM((cap * HP_ROWS, LANES), U32),
                            pltpu.VMEM((cap, D_MODEL), BF16),
                            pltpu.VMEM((cap, D_MODEL), F32),
                            pltpu.VMEM((cap, 1), F32)]),
        compiler_params=_cparams(2),
        name="ffn",
    )(src, hp, wr, w_gate, w_up, w_down)


SCATTER_UNROLL = 8


def _combine_kernel(dst_ref, y_ref, x1_ref, mod_ref, o_ref, acc_ref):
    s = pl.program_id(0)
    tf = o_ref.shape[0]
    cap = y_ref.shape[1] // Y_ROWS

    @pl.when(s == 0)
    def _():
        acc_ref[...] = jnp.zeros_like(acc_ref)

    @pl.when(s < N_EXPERTS)
    def _():
        def scatter(g, carry):
            rows, sums = [], []
            for r in range(SCATTER_UNROLL):
                j = g * SCATTER_UNROLL + r
                dst = pl.ds(pl.multiple_of(dst_ref[s * cap + j], Y_ROWS), Y_ROWS)
                src = pl.ds(pl.multiple_of(j * Y_ROWS, Y_ROWS), Y_ROWS)
                rows.append(dst)
                sums.append(acc_ref[dst, :] + y_ref[0, src, :])
            for dst, v in zip(rows, sums):
                acc_ref[dst, :] = v
            return carry

        lax.fori_loop(0, cap // SCATTER_UNROLL, scatter, 0)

    @pl.when(s >= N_EXPERTS)
    def _():
        g2 = mod_ref[0, 0][5:6]
        tile = acc_ref.at[pl.ds(pl.multiple_of((s - N_EXPERTS) * tf * Y_ROWS, tf * Y_ROWS), tf * Y_ROWS), :]
        for c in range(Y_ROWS):
            sl = slice(c * LANES, (c + 1) * LANES)
            o_ref[:, sl] = x1_ref[:, sl] + g2[:, sl] * tile[pl.ds(c, tf, stride=Y_ROWS), :]


def _combine(dst, y, x1, mods, layer, row_of_tile, tf):
    n_tok = x1.shape[0]
    cap = dst.shape[0] // N_EXPERTS
    tile_of = lambda s: jnp.maximum(s - N_EXPERTS, 0)
    return pl.pallas_call(
        _combine_kernel,
        out_shape=jax.ShapeDtypeStruct((n_tok, D_MODEL), F32),
        grid_spec=pltpu.PrefetchScalarGridSpec(
            num_scalar_prefetch=1,
            grid=(N_EXPERTS + n_tok // tf,),
            in_specs=[pl.BlockSpec((1, cap * Y_ROWS, LANES), lambda s, *_: (jnp.minimum(s, N_EXPERTS - 1), 0, 0)),
                      pl.BlockSpec((tf, D_MODEL), lambda s, *_: (tile_of(s), 0)),
                      pl.BlockSpec((1, 1, N_MOD, D_MODEL), lambda s, *_: (layer, row_of_tile(tile_of(s)), 0, 0))],
            out_specs=pl.BlockSpec((tf, D_MODEL), lambda s, *_: (tile_of(s), 0)),
            scratch_shapes=[pltpu.VMEM((n_tok * Y_ROWS, LANES), F32)]),
        compiler_params=_cparams(1),
        name="combine",
    )(dst, y, x1, mods)


TOKEN_TILE = 512
Q_TILE = 256
FF_TILE = 512


def _moe(x1, hp, aff, wr, mods, layer, row_of_tile, w_gate, w_up, w_down):
    n_tok = x1.shape[0]
    cap = CAPACITY_FACTOR * n_tok // N_EXPERTS
    src, dst = _select(aff, cap)
    y = _ffn(src, hp, wr, layer, w_gate, w_up, w_down, FF_TILE)
    return _combine(dst, y, x1, mods, layer, row_of_tile, TOKEN_TILE)


def kernel(x_prompt, x_sample, cache_k, cache_v, c, c_ctx, norm1, norm2, w_ada, b_ada, w_qkv, q_norm, k_norm,
           w_o, w_pool, pool_scale, w_router, w_e_gate, w_e_up, w_e_down):
    batch, seq, _ = x_prompt.shape
    dec_batch, dec_seq, _ = x_sample.shape
    depth = w_ada.shape[0]
    kvw = N_KV * HEAD_DIM

    cvec = jnp.zeros((SUBLANES, D_MODEL), F32).at[0].set(c_ctx).at[1:1 + dec_batch].set(c)
    mods = _ada(cvec, w_ada, b_ada)
    ctx = x_prompt.reshape(batch * seq, D_MODEL)
    lat = x_sample.reshape(dec_batch * dec_seq, D_MODEL)
    ctx_row = lambda i: 0
    lat_row = lambda tile: (lambda i: 1 + (i * tile) // dec_seq)
    ctx_tile = min(TOKEN_TILE, seq)
    rope_tabs = _rope_tables(dec_seq)
    new_k = new_v = None

    for layer in range(depth):
        j = layer // 2
        n1 = norm1[layer][None]
        n2 = norm2[layer][None]
        wr = jnp.pad(w_router[layer], ((0, 0), (0, LANES - N_EXPERTS))).astype(BF16)
        if layer % 2 == 0:
            qn = q_norm[j][None]
            kn = k_norm[j][None]
            qc, kc, vc, new_k, new_v = _qkv(ctx, mods, layer, ctx_row, n1, w_qkv[j], qn, kn, None, True, TOKEN_TILE)
            oc = _attn_ctx(qc, kc, vc, seq)
            ql, kl, vl = _qkv(lat, mods, layer, lat_row(TOKEN_TILE), n1, w_qkv[j], qn, kn, rope_tabs, False,
                              TOKEN_TILE)
            past_k = cache_k[:, j].reshape(dec_batch, -1, kvw).astype(BF16)
            past_v = cache_v[:, j].reshape(dec_batch, -1, kvw).astype(BF16)
            ol = _attn_lat(ql, kl, vl, past_k, past_v, dec_seq, Q_TILE)
            ctx1 = _post(oc, ctx, mods, layer, ctx_row, n2, w_o[j], wr, TOKEN_TILE)
            lat1 = _post(ol, lat, mods, layer, lat_row(TOKEN_TILE), n2, w_o[j], wr, TOKEN_TILE)
        else:
            ps = pool_scale[j][None]
            ctx1 = _pool(ctx, mods, layer, ctx_row, n1, n2, w_pool[j], ps, wr, seq, ctx_tile)
            lat1 = _pool(lat, mods, layer, lat_row(TOKEN_TILE), n1, n2, w_pool[j], ps, wr, dec_seq, TOKEN_TILE)
        ctx = _moe(*ctx1, wr, mods, layer, ctx_row, w_e_gate, w_e_up, w_e_down)
        lat = _moe(*lat1, wr, mods, layer, lat_row(TOKEN_TILE), w_e_gate, w_e_up, w_e_down)

    new_cache_k = new_k.reshape(batch, 1, seq, N_KV, HEAD_DIM)
    new_cache_v = new_v.reshape(batch, 1, seq, N_KV, HEAD_DIM)
    return (ctx.reshape(batch, seq, D_MODEL), lat.reshape(dec_batch, dec_seq, D_MODEL), new_cache_k, new_cache_v)
```

```python
import functools

import jax
import jax.numpy as jnp
import numpy as np
from jax import lax
from jax.experimental import pallas as pl
from jax.experimental.pallas import tpu as pltpu

F32 = jnp.float32
BF16 = jnp.bfloat16
I32 = jnp.int32
U32 = jnp.uint32

D_MODEL = 1024
HEAD_DIM = 128
N_HEADS = 8
N_KV = 2
Q_PER_KV = N_HEADS // N_KV
QKV_DIM = (N_HEADS + 2 * N_KV) * HEAD_DIM
GRID_W = 64
ROPE_THETA = 10000.0
POOL_WINDOWS = (2, 4, 8, 16)
POOL_GROUP = D_MODEL // len(POOL_WINDOWS)
POOL_HALO = 8
N_EXPERTS = 16
CAPACITY_FACTOR = 2
N_MOD = 6
EPS = 1e-6
LOG2_E = 1.4426950408889634

LANES = 128
SUBLANES = 8
HALF = D_MODEL // 2
HP_ROWS = HALF // LANES
Y_ROWS = D_MODEL // LANES
VMEM_LIMIT = 56 * 1024 * 1024


def _cparams(n_axes, vmem=VMEM_LIMIT):
    return pltpu.CompilerParams(dimension_semantics=("arbitrary",) * n_axes, vmem_limit_bytes=vmem)


def _silu(x):
    return x / (1.0 + jnp.exp(-x))


def _rms(x, gain):
    return x * lax.rsqrt(jnp.mean(x * x, axis=-1, keepdims=True) + EPS) * gain


def _dot(a, b):
    return jnp.dot(a, b, preferred_element_type=F32)


def _dot_nt(a, b):
    return lax.dot_general(a, b, (((1,), (1,)), ((), ())), preferred_element_type=F32)


def _ada_kernel(c_ref, w_ref, b_ref, o_ref):
    s = _silu(c_ref[...]).astype(BF16)
    o_ref[0] = _dot(s, w_ref[0].astype(BF16)) + b_ref[0]


def _ada(cvec, w_ada, b_ada):
    depth = w_ada.shape[0]
    out = pl.pallas_call(
        _ada_kernel,
        out_shape=jax.ShapeDtypeStruct((depth, SUBLANES, N_MOD * D_MODEL), F32),
        grid=(depth, N_MOD),
        in_specs=[
            pl.BlockSpec((SUBLANES, D_MODEL), lambda i, j: (0, 0)),
            pl.BlockSpec((1, D_MODEL, D_MODEL), lambda i, j: (i, 0, j)),
            pl.BlockSpec((1, 1, D_MODEL), lambda i, j: (i, 0, j)),
        ],
        out_specs=pl.BlockSpec((1, SUBLANES, D_MODEL), lambda i, j: (i, 0, j)),
        compiler_params=_cparams(2),
        name="ada",
    )(cvec, w_ada, b_ada.reshape(depth, 1, N_MOD * D_MODEL))
    return out.reshape(depth, SUBLANES, N_MOD, D_MODEL)


def _mod_spec(layer, row_of_tile):
    return pl.BlockSpec((1, 1, N_MOD, D_MODEL), lambda i, *_: (layer, row_of_tile(i), 0, 0))


def _qkv_kernel(*refs, rope, cache_out):
    x_ref, mod_ref, n1_ref, w_ref, qn_ref, kn_ref = refs[:6]
    refs = refs[6:]
    if rope:
        cos_ref, sin_ref = refs[:2]
        refs = refs[2:]
    q_ref, k_ref, v_ref = refs[:3]
    refs = refs[3:]
    if cache_out:
        kc_ref, vc_ref = refs[:2]
        refs = refs[2:]
    (wb_ref,) = refs

    qk_w = (N_HEADS + N_KV) * HEAD_DIM
    quarter = HEAD_DIM // 4

    def partner(a):
        width = a.shape[1]
        first = (lax.broadcasted_iota(I32, a.shape, 1) & quarter) == 0
        return jnp.where(first, pltpu.roll(a, width - quarter, 1), pltpu.roll(a, quarter, 1))

    @pl.when(pl.program_id(0) == 0)
    def _():
        w = w_ref[...]
        wb_ref[:, 0:QKV_DIM] = w.astype(BF16)
        if rope:
            wb_ref[:, QKV_DIM:] = partner(w[:, 0:qk_w]).astype(BF16)

    m = mod_ref[0, 0]
    h = _rms(x_ref[...], n1_ref[...]) * (1.0 + m[1:2]) + m[0:1]
    qkv = _dot(h.astype(BF16), wb_ref[...])
    scale = HEAD_DIM ** -0.5 * LOG2_E
    if rope:
        gains = {True: qn_ref[...], False: kn_ref[...]}
        cos_g = {key: cos_ref[...] * g for key, g in gains.items()}
        sin_g = {key: sin_ref[...] * partner(g) for key, g in gains.items()}
    for hh in range(N_HEADS + N_KV):
        sl = slice(hh * HEAD_DIM, (hh + 1) * HEAD_DIM)
        is_q = hh < N_HEADS
        if rope:
            raw = qkv[:, sl]
            norm = lax.rsqrt(jnp.mean(raw * raw, axis=-1, keepdims=True) + EPS)
            xh = (raw * cos_g[is_q] + qkv[:, QKV_DIM + hh * HEAD_DIM:QKV_DIM + (hh + 1) * HEAD_DIM] * sin_g[is_q]) * norm
        else:
            xh = _rms(qkv[:, sl], qn_ref[...] if is_q else kn_ref[...])
        if cache_out and hh >= N_HEADS:
            kc_ref[:, (hh - N_HEADS) * HEAD_DIM:(hh - N_HEADS + 1) * HEAD_DIM] = xh
        if hh < N_HEADS:
            q_ref[:, sl] = (xh * scale).astype(BF16)
        else:
            k_ref[:, (hh - N_HEADS) * HEAD_DIM:(hh - N_HEADS + 1) * HEAD_DIM] = xh.astype(BF16)
    v = qkv[:, qk_w:QKV_DIM]
    v_ref[...] = v.astype(BF16)
    if cache_out:
        vc_ref[...] = v


def _qkv(x, mods, layer, row_of_tile, n1, w_qkv, qn, kn, rope_tabs, cache_out, tm):
    n_tok = x.shape[0]
    kvw = N_KV * HEAD_DIM
    rope = rope_tabs is not None
    in_specs = [
        pl.BlockSpec((tm, D_MODEL), lambda i: (i, 0)),
        _mod_spec(layer, row_of_tile),
        pl.BlockSpec((1, D_MODEL), lambda i: (0, 0)),
        pl.BlockSpec((D_MODEL, QKV_DIM), lambda i: (0, 0)),
        pl.BlockSpec((1, HEAD_DIM), lambda i: (0, 0)),
        pl.BlockSpec((1, HEAD_DIM), lambda i: (0, 0)),
    ]
    args = [x, mods, n1, w_qkv, qn, kn]
    if rope:
        seq_tiles = rope_tabs[0].shape[0] // tm
        in_specs += [pl.BlockSpec((tm, HEAD_DIM), lambda i: (i % seq_tiles, 0))] * 2
        args += list(rope_tabs)
    out_shape = [jax.ShapeDtypeStruct((n_tok, D_MODEL), BF16),
                 jax.ShapeDtypeStruct((n_tok, kvw), BF16),
                 jax.ShapeDtypeStruct((n_tok, kvw), BF16)]
    out_specs = [pl.BlockSpec((tm, D_MODEL), lambda i: (i, 0)),
                 pl.BlockSpec((tm, kvw), lambda i: (i, 0)),
                 pl.BlockSpec((tm, kvw), lambda i: (i, 0))]
    if cache_out:
        out_shape += [jax.ShapeDtypeStruct((n_tok, kvw), F32)] * 2
        out_specs += [pl.BlockSpec((tm, kvw), lambda i: (i, 0))] * 2
    return pl.pallas_call(
        functools.partial(_qkv_kernel, rope=rope, cache_out=cache_out),
        out_shape=out_shape,
        grid=(n_tok // tm,),
        in_specs=in_specs,
        out_specs=out_specs,
        scratch_shapes=[pltpu.VMEM((D_MODEL, QKV_DIM + ((N_HEADS + N_KV) * HEAD_DIM if rope else 0)), BF16)],
        compiler_params=_cparams(1),
        name="qkv_rope" if rope else "qkv",
    )(*args)


def _rope_tables(seq_len):
    half = HEAD_DIM // 2
    n = half // 2
    inv_freq = ROPE_THETA ** (-np.arange(n, dtype=np.float64) / n)
    rows = seq_len // GRID_W
    row = np.repeat(np.arange(rows), GRID_W).astype(np.float64)
    col = np.tile(np.arange(GRID_W), rows).astype(np.float64)
    ang_r = row[:, None] * inv_freq[None, :]
    ang_c = col[:, None] * inv_freq[None, :]
    cos = np.concatenate([np.cos(ang_r)] * 2 + [np.cos(ang_c)] * 2, axis=-1)
    sin = np.concatenate([-np.sin(ang_r), np.sin(ang_r), -np.sin(ang_c), np.sin(ang_c)], axis=-1)
    return jnp.asarray(cos, F32), jnp.asarray(sin, F32)


def _with_ones(v):
    return jnp.concatenate([v, jnp.ones_like(v)], axis=1)


def _gqa_attention(q, segments):
    rows = q.shape[0]
    qs = jnp.concatenate([q[:, h * HEAD_DIM:(h + 1) * HEAD_DIM] for h in range(Q_PER_KV)], axis=0)
    m = jnp.full((rows * Q_PER_KV, 1), -1e30, F32)
    acc = jnp.zeros((rows * Q_PER_KV, 2 * HEAD_DIM), F32)
    for k, v in segments:
        s = _dot_nt(qs, k)
        m_new = jnp.maximum(m, jnp.max(s, axis=-1, keepdims=True))
        acc = acc * jnp.exp2(m - m_new) + _dot(jnp.exp2(s - m_new).astype(BF16), v)
        m = m_new
    o = (acc[:, :HEAD_DIM] / acc[:, HEAD_DIM:]).astype(BF16)
    return jnp.concatenate([o[h * rows:(h + 1) * rows] for h in range(Q_PER_KV)], axis=1)


def _attn_ctx_kernel(q_ref, k_ref, v_ref, o_ref):
    gw = Q_PER_KV * HEAD_DIM
    for g in range(N_KV):
        kg = k_ref[:, g * HEAD_DIM:(g + 1) * HEAD_DIM]
        vg = _with_ones(v_ref[:, g * HEAD_DIM:(g + 1) * HEAD_DIM])
        o_ref[:, g * gw:(g + 1) * gw] = _gqa_attention(q_ref[:, g * gw:(g + 1) * gw], [(kg, vg)])


def _attn_ctx(q, k, v, seq):
    n_tok = q.shape[0]
    kvw = N_KV * HEAD_DIM
    return pl.pallas_call(
        _attn_ctx_kernel,
        out_shape=jax.ShapeDtypeStruct((n_tok, D_MODEL), BF16),
        grid=(n_tok // seq,),
        in_specs=[pl.BlockSpec((seq, D_MODEL), lambda b: (b, 0)),
                  pl.BlockSpec((seq, kvw), lambda b: (b, 0)),
                  pl.BlockSpec((seq, kvw), lambda b: (b, 0))],
        out_specs=pl.BlockSpec((seq, D_MODEL), lambda b: (b, 0)),
        compiler_params=_cparams(1),
        name="attn_ctx",
    )(q, k, v)


KEY_CHUNK = 1024


def _attn_lat_kernel(q_ref, k_ref, v_ref, kc_ref, vc_ref, o_ref):
    segments = [(k_ref[c:c + KEY_CHUNK], _with_ones(v_ref[c:c + KEY_CHUNK]))
                for c in range(0, k_ref.shape[0], KEY_CHUNK)]
    segments.append((kc_ref[0], _with_ones(vc_ref[0])))
    o_ref[...] = _gqa_attention(q_ref[...], segments)


def _attn_lat(q, k, v, kc, vc, seq, tq):
    n_tok = q.shape[0]
    batch = n_tok // seq
    past = kc.shape[1]
    qt = seq // tq
    gw = Q_PER_KV * HEAD_DIM
    return pl.pallas_call(
        _attn_lat_kernel,
        out_shape=jax.ShapeDtypeStruct((n_tok, D_MODEL), BF16),
        grid=(batch, N_KV, qt),
        in_specs=[pl.BlockSpec((tq, gw), lambda b, g, i: (b * qt + i, g)),
                  pl.BlockSpec((seq, HEAD_DIM), lambda b, g, i: (b, g)),
                  pl.BlockSpec((seq, HEAD_DIM), lambda b, g, i: (b, g)),
                  pl.BlockSpec((1, past, HEAD_DIM), lambda b, g, i: (b, 0, g)),
                  pl.BlockSpec((1, past, HEAD_DIM), lambda b, g, i: (b, 0, g))],
        out_specs=pl.BlockSpec((tq, gw), lambda b, g, i: (b * qt + i, g)),
        compiler_params=_cparams(3),
        name="attn_lat",
    )(q, k, v, kc, vc)


def _router_softmax(hb, wr):
    logits = _dot(hb, wr)
    lane = lax.broadcasted_iota(I32, logits.shape, 1)
    logits = jnp.where(lane < N_EXPERTS, logits, -1e30)
    ex = jnp.exp(logits - jnp.max(logits, axis=-1, keepdims=True))
    return ex / jnp.sum(ex, axis=-1, keepdims=True)


def _moe_front(x1, m, n2, wr, x1_ref, hp_ref, aff_ref):
    tm = x1.shape[0]
    x1_ref[...] = x1
    h2 = _rms(x1, n2) * (1.0 + m[4:5]) + m[3:4]
    aff = _router_softmax(h2.astype(BF16), wr)
    packed = pltpu.pack_elementwise([h2[:, :HALF], h2[:, HALF:]], packed_dtype=BF16)
    for s in range(HP_ROWS):
        hp_ref[pl.ds(s, tm, stride=HP_ROWS), :] = packed[:, s * LANES:(s + 1) * LANES]
    aff_t = aff.T
    for b in range(tm // LANES):
        aff_ref[b * N_EXPERTS:(b + 1) * N_EXPERTS, :] = aff_t[0:N_EXPERTS, b * LANES:(b + 1) * LANES]


def _front_out(n_tok, tm):
    shapes = [jax.ShapeDtypeStruct((n_tok, D_MODEL), F32),
              jax.ShapeDtypeStruct((n_tok * HP_ROWS, LANES), U32),
              jax.ShapeDtypeStruct((n_tok // LANES * N_EXPERTS, LANES), F32)]
    specs = [pl.BlockSpec((tm, D_MODEL), lambda i: (i, 0)),
             pl.BlockSpec((tm * HP_ROWS, LANES), lambda i: (i, 0)),
             pl.BlockSpec((tm // LANES * N_EXPERTS, LANES), lambda i: (i, 0))]
    return shapes, specs


def _post_kernel(o_ref, x_ref, mod_ref, n2_ref, wo_ref, wr_ref, x1_ref, hp_ref, aff_ref, wb_ref):
    @pl.when(pl.program_id(0) == 0)
    def _():
        wb_ref[...] = wo_ref[...].astype(BF16)

    m = mod_ref[0, 0]
    x1 = x_ref[...] + m[2:3] * _dot(o_ref[...], wb_ref[...])
    _moe_front(x1, m, n2_ref[...], wr_ref[...], x1_ref, hp_ref, aff_ref)


def _post(o, x, mods, layer, row_of_tile, n2, w_o, wr, tm):
    n_tok = x.shape[0]
    shapes, specs = _front_out(n_tok, tm)
    return pl.pallas_call(
        _post_kernel,
        out_shape=shapes,
        grid=(n_tok // tm,),
        in_specs=[pl.BlockSpec((tm, D_MODEL), lambda i: (i, 0)),
                  pl.BlockSpec((tm, D_MODEL), lambda i: (i, 0)),
                  _mod_spec(layer, row_of_tile),
                  pl.BlockSpec((1, D_MODEL), lambda i: (0, 0)),
                  pl.BlockSpec((D_MODEL, D_MODEL), lambda i: (0, 0)),
                  pl.BlockSpec((D_MODEL, LANES), lambda i: (0, 0))],
        out_specs=specs,
        scratch_shapes=[pltpu.VMEM((D_MODEL, D_MODEL), BF16)],
        compiler_params=_cparams(1),
        name="post",
    )(o, x, mods, n2, w_o, wr)


def _pool_kernel(x_ref, xp_ref, xn_ref, mod_ref, n1_ref, n2_ref, wp_ref, ps_ref, wr_ref,
                 x1_ref, hp_ref, aff_ref, *, seq):
    tm = x_ref.shape[0]
    i = pl.program_id(0)
    m = mod_ref[0, 0]
    x = x_ref[...]

    def norm_mod(v):
        return _rms(v, n1_ref[...]) * (1.0 + m[1:2]) + m[0:1]

    prev_ok = ((i * tm) % seq != 0).astype(F32)
    next_ok = (((i + 1) * tm) % seq != 0).astype(F32)
    h = norm_mod(x)
    hz = jnp.concatenate([norm_mod(xp_ref[...]) * prev_ok, h, norm_mod(xn_ref[...]) * next_ok], axis=0)
    rows = tm + 2 * POOL_HALO
    t = (i * tm) % seq + lax.broadcasted_iota(I32, (tm, 1), 0)
    ys = []
    for g, w in enumerate(POOL_WINDOWS):
        sl = slice(g * POOL_GROUP, (g + 1) * POOL_GROUP)
        f = hz[:, sl]
        step = 1
        while step < w:
            f = f + pltpu.roll(f, rows - step, 0)
            step *= 2
        win = pltpu.roll(f, w // 2, 0)[POOL_HALO:POOL_HALO + tm]
        cnt = (jnp.minimum(t + w // 2, seq) - jnp.maximum(t - w // 2, 0)).astype(F32)
        d = (win / cnt - h[:, sl]).astype(BF16)
        ys.append(_dot(d, wp_ref[g].astype(BF16)))
    y = jnp.concatenate(ys, axis=-1) * ps_ref[...]
    x1 = x + m[2:3] * y
    _moe_front(x1, m, n2_ref[...], wr_ref[...], x1_ref, hp_ref, aff_ref)


def _pool(x, mods, layer, row_of_tile, n1, n2, w_pool, pool_scale, wr, seq, tm):
    n_tok = x.shape[0]
    hb = tm // POOL_HALO
    last = n_tok // POOL_HALO - 1
    shapes, specs = _front_out(n_tok, tm)
    return pl.pallas_call(
        functools.partial(_pool_kernel, seq=seq),
        out_shape=shapes,
        grid=(n_tok // tm,),
        in_specs=[pl.BlockSpec((tm, D_MODEL), lambda i: (i, 0)),
                  pl.BlockSpec((POOL_HALO, D_MODEL), lambda i: (jnp.maximum(i * hb - 1, 0), 0)),
                  pl.BlockSpec((POOL_HALO, D_MODEL), lambda i: (jnp.minimum((i + 1) * hb, last), 0)),
                  _mod_spec(layer, row_of_tile),
                  pl.BlockSpec((1, D_MODEL), lambda i: (0, 0)),
                  pl.BlockSpec((1, D_MODEL), lambda i: (0, 0)),
                  pl.BlockSpec(w_pool.shape, lambda i: (0, 0, 0)),
                  pl.BlockSpec((1, D_MODEL), lambda i: (0, 0)),
                  pl.BlockSpec((D_MODEL, LANES), lambda i: (0, 0))],
        out_specs=specs,
        compiler_params=_cparams(1),
        name="pool",
    )(x, x, x, mods, n1, n2, w_pool, pool_scale, wr)


def _select_kernel(aff_ref, src_ref, dst_ref, gate_ref, *, cap):
    nb = aff_ref.shape[0] // N_EXPERTS
    rows = nb * N_EXPERTS
    shape3 = (nb, N_EXPERTS, LANES)
    aff = aff_ref[...].reshape(shape3)

    def count(mask):
        return jnp.sum(jnp.sum(mask.astype(F32), axis=0), axis=-1, keepdims=True)

    def search(i, t):
        cand = t | (jnp.int32(1) << (29 - i))
        return jnp.where(count(aff >= pltpu.bitcast(cand, F32)[None]) >= cap, cand, t)

    thr = pltpu.bitcast(lax.fori_loop(0, 30, search, jnp.zeros((N_EXPERTS, LANES), I32)), F32)
    gt = aff > thr[None]
    eq = aff == thr[None]
    need = cap - count(gt)

    kk = lax.broadcasted_iota(I32, (LANES, LANES), 0)
    nn = lax.broadcasted_iota(I32, (LANES, LANES), 1)
    upper = (kk <= nn).astype(BF16)
    ones = jnp.ones((LANES, LANES), BF16)

    def prefix(mask):
        m2 = mask.astype(F32).astype(BF16).reshape(rows, LANES)
        p = _dot(m2, upper).reshape(shape3)
        s = _dot(m2, ones).reshape(shape3)
        offs = []
        run = jnp.zeros((N_EXPERTS, LANES), F32)
        for b in range(nb):
            offs.append(run)
            run = run + s[b]
        return p, jnp.stack(offs, axis=0), s

    pe, oe, _ = prefix(eq)
    sel = gt | (eq & ((pe + oe) <= need[None]))
    ps, os_, ss = prefix(sel)

    lane = lax.broadcasted_iota(I32, (rows, LANES), 1)
    sel2 = sel.reshape(rows, LANES)
    dist = jnp.where(sel2, lane - (ps.reshape(rows, LANES).astype(I32) - 1), 0)
    alive = sel2.astype(I32)
    val = lane
    gval = aff.reshape(rows, LANES)
    for k in range(7):
        s = 1 << k
        move = alive * ((dist >> k) & 1)
        inc = pltpu.roll(move, LANES - s, 1) * (lane < LANES - s).astype(I32) == 1
        val = jnp.where(inc, pltpu.roll(val, LANES - s, 1), val)
        gval = jnp.where(inc, pltpu.roll(gval, LANES - s, 1), gval)
        dist = jnp.where(inc, pltpu.roll(dist, LANES - s, 1), dist)
        alive = alive - move + inc.astype(I32)
    local = val.reshape(shape3)
    local_gate = gval.reshape(shape3)

    top = os_ + ss
    lane_e = lax.broadcasted_iota(I32, (N_EXPERTS, LANES), 1)
    for jc in range(cap // LANES):
        j = (lane_e + jc * LANES).astype(F32)
        acc = jnp.zeros((N_EXPERTS, LANES), I32)
        gacc = jnp.zeros((N_EXPERTS, LANES), F32)
        for b in range(nb):
            inside = (os_[b] <= j) & (j < top[b])
            jl = (j - os_[b]).astype(I32) & (LANES - 1)
            acc = jnp.where(inside, jnp.take_along_axis(local[b], jl, axis=1) + b * LANES, acc)
            gacc = jnp.where(inside, jnp.take_along_axis(local_gate[b], jl, axis=1), gacc)
        out_rows = pl.ds(jc, N_EXPERTS, stride=cap // LANES)
        src_ref[out_rows, :] = acc * HP_ROWS
        dst_ref[out_rows, :] = acc * Y_ROWS
        gate_ref[out_rows, :] = gacc


def _select(aff, cap):
    rows = aff.shape[0]
    out_rows = N_EXPERTS * cap // LANES
    out_spec = pl.BlockSpec((out_rows, LANES), lambda i: (0, 0))
    src, dst, gate = pl.pallas_call(
        functools.partial(_select_kernel, cap=cap),
        out_shape=[jax.ShapeDtypeStruct((out_rows, LANES), I32),
                   jax.ShapeDtypeStruct((out_rows, LANES), I32),
                   jax.ShapeDtypeStruct((out_rows, LANES), F32)],
        grid=(1,),
        in_specs=[pl.BlockSpec((rows, LANES), lambda i: (0, 0))],
        out_specs=[out_spec, out_spec, out_spec],
        compiler_params=_cparams(1),
        name="select",
    )(aff)
    return src.reshape(-1), dst.reshape(-1), gate


GATHER_UNROLL = 8


FF_CHUNK = 256


def _ffn_kernel(src_ref, hp_ref, gate_ref, wg_ref, wu_ref, wd_ref, y_ref, slab_ref, xb_ref, yacc_ref, *, nf):
    e = pl.program_id(0)
    f = pl.program_id(1)
    cap = xb_ref.shape[1]
    ft = wg_ref.shape[3]
    part = cap // nf

    def copy_row(table_base, j):
        src = pl.multiple_of(src_ref[table_base + j], HP_ROWS)
        slab_ref[pl.ds(pl.multiple_of(j * HP_ROWS, HP_ROWS), HP_ROWS), :] = hp_ref[pl.ds(src, HP_ROWS), :]

    def unpack(slot, row0, n):
        packed = slab_ref.at[pl.ds(pl.multiple_of(row0 * HP_ROWS, n * HP_ROWS), n * HP_ROWS), :]
        for s in range(HP_ROWS):
            w = packed[pl.ds(s, n, stride=HP_ROWS), :]
            for half in range(2):
                v = pltpu.unpack_elementwise(w, index=half, packed_dtype=BF16, unpacked_dtype=F32)
                xb_ref[slot, pl.ds(row0, n), half * HALF + s * LANES:half * HALF + (s + 1) * LANES] = v.astype(BF16)

    @pl.when((e == 0) & (f == 0))
    def _():
        def gather(g, carry):
            for r in range(GATHER_UNROLL):
                copy_row(0, g * GATHER_UNROLL + r)
            return carry

        lax.fori_loop(0, cap // GATHER_UNROLL, gather, 0)
        unpack(0, 0, cap)
        yacc_ref[...] = jnp.zeros_like(yacc_ref)

    cur = e % 2
    nxt = jnp.minimum(e + 1, N_EXPERTS - 1)
    row0 = pl.multiple_of(f * part, part)
    for r in range(part):
        copy_row(nxt * cap, row0 + r)
    unpack(1 - cur, row0, part)

    xe = xb_ref[cur]
    y = jnp.where(f > 0, yacc_ref[...], 0.0)
    for c in range(0, ft, FF_CHUNK):
        a = _dot(xe, wg_ref[0, 0, :, c:c + FF_CHUNK].astype(BF16))
        u = _dot(xe, wu_ref[0, 0, :, c:c + FF_CHUNK].astype(BF16))
        y = y + _dot((_silu(a) * u).astype(BF16), wd_ref[0, 0, c:c + FF_CHUNK, :].astype(BF16))
    yacc_ref[...] = y

    @pl.when(f == nf - 1)
    def _():
        eye = (lax.broadcasted_iota(I32, (LANES, LANES), 0) == lax.broadcasted_iota(I32, (LANES, LANES), 1))
        for c in range(cap // LANES):
            g_row = gate_ref[pl.ds(e * (cap // LANES) + c, 1), :]
            g_col = jnp.sum(jnp.where(eye, g_row, 0.0), axis=1, keepdims=True)
            tot = yacc_ref[c * LANES:(c + 1) * LANES, :] * g_col
            for s in range(Y_ROWS):
                y_ref[0, pl.ds(c * LANES * Y_ROWS + s, LANES, stride=Y_ROWS), :] = tot[:, s * LANES:(s + 1) * LANES]


def _ffn(src, gate, hp, layer, w_gate, w_up, w_down, ft):
    cap = src.shape[0] // N_EXPERTS
    d_ff = w_gate.shape[3]
    nf = d_ff // ft
    assert nf >= 2 and cap % nf == 0 and ft % FF_CHUNK == 0
    return pl.pallas_call(
        functools.partial(_ffn_kernel, nf=nf),
        out_shape=jax.ShapeDtypeStruct((N_EXPERTS, cap * Y_ROWS, LANES), F32),
        grid_spec=pltpu.PrefetchScalarGridSpec(
            num_scalar_prefetch=1,
            grid=(N_EXPERTS, d_ff // ft),
            in_specs=[pl.BlockSpec(hp.shape, lambda e, f, idx: (0, 0), pipeline_mode=pl.Buffered(1)),
                      pl.BlockSpec(gate.shape, lambda e, f, idx: (0, 0)),
                      pl.BlockSpec((1, 1, D_MODEL, ft), lambda e, f, idx: (layer, e, 0, f)),
                      pl.BlockSpec((1, 1, D_MODEL, ft), lambda e, f, idx: (layer, e, 0, f)),
                      pl.BlockSpec((1, 1, ft, D_MODEL), lambda e, f, idx: (layer, e, f, 0))],
            out_specs=pl.BlockSpec((1, cap * Y_ROWS, LANES), lambda e, f, idx: (e, 0, 0)),
            scratch_shapes=[pltpu.VMEM((cap * HP_ROWS, LANES), U32),
                            pltpu.VMEM((2, cap, D_MODEL), BF16),
                            pltpu.VMEM((cap, D_MODEL), F32)]),
        compiler_params=_cparams(2),
        name="ffn",
    )(src, hp, gate, w_gate, w_up, w_down)


SCATTER_UNROLL = 16


def _combine_kernel(dst_ref, y_ref, x1_ref, mod_ref, o_ref, acc_ref):
    s = pl.program_id(0)
    tf = o_ref.shape[0]
    cap = y_ref.shape[1] // Y_ROWS

    @pl.when(s == 0)
    def _():
        acc_ref[...] = jnp.zeros_like(acc_ref)

    @pl.when(s < N_EXPERTS)
    def _():
        def scatter(g, carry):
            base = s * cap + g * SCATTER_UNROLL
            rows, sums = [], []
            for r in range(SCATTER_UNROLL):
                dst = pl.ds(pl.multiple_of(dst_ref[base + r], Y_ROWS), Y_ROWS)
                src = pl.ds(pl.multiple_of((g * SCATTER_UNROLL + r) * Y_ROWS, Y_ROWS), Y_ROWS)
                rows.append(dst)
                sums.append(acc_ref[dst, :] + y_ref[0, src, :])
            for dst, v in zip(rows, sums):
                acc_ref[dst, :] = v
            return carry

        lax.fori_loop(0, cap // SCATTER_UNROLL, scatter, 0)

    @pl.when(s >= N_EXPERTS)
    def _():
        g2 = mod_ref[0, 0][5:6]
        tile = acc_ref.at[pl.ds(pl.multiple_of((s - N_EXPERTS) * tf * Y_ROWS, tf * Y_ROWS), tf * Y_ROWS), :]
        for c in range(Y_ROWS):
            sl = slice(c * LANES, (c + 1) * LANES)
            o_ref[:, sl] = x1_ref[:, sl] + g2[:, sl] * tile[pl.ds(c, tf, stride=Y_ROWS), :]


def _combine(dst, y, x1, mods, layer, row_of_tile, tf):
    n_tok = x1.shape[0]
    cap = dst.shape[0] // N_EXPERTS
    tile_of = lambda s: jnp.maximum(s - N_EXPERTS, 0)
    return pl.pallas_call(
        _combine_kernel,
        out_shape=jax.ShapeDtypeStruct((n_tok, D_MODEL), F32),
        grid_spec=pltpu.PrefetchScalarGridSpec(
            num_scalar_prefetch=1,
            grid=(N_EXPERTS + n_tok // tf,),
            in_specs=[pl.BlockSpec((1, cap * Y_ROWS, LANES), lambda s, *_: (jnp.minimum(s, N_EXPERTS - 1), 0, 0)),
                      pl.BlockSpec((tf, D_MODEL), lambda s, *_: (tile_of(s), 0)),
                      pl.BlockSpec((1, 1, N_MOD, D_MODEL), lambda s, *_: (layer, row_of_tile(tile_of(s)), 0, 0))],
            out_specs=pl.BlockSpec((tf, D_MODEL), lambda s, *_: (tile_of(s), 0)),
            scratch_shapes=[pltpu.VMEM((n_tok * Y_ROWS, LANES), F32)]),
        compiler_params=_cparams(1),
        name="combine",
    )(dst, y, x1, mods)


TOKEN_TILE = 512
Q_TILE = 256
FF_TILE = 512


def _moe(x1, hp, aff, mods, layer, row_of_tile, w_gate, w_up, w_down):
    n_tok = x1.shape[0]
    cap = CAPACITY_FACTOR * n_tok // N_EXPERTS
    src, dst, gate = _select(aff, cap)
    y = _ffn(src, gate, hp, layer, w_gate, w_up, w_down, FF_TILE)
    return _combine(dst, y, x1, mods, layer, row_of_tile, TOKEN_TILE)


def kernel(x_prompt, x_sample, cache_k, cache_v, c, c_ctx, norm1, norm2, w_ada, b_ada, w_qkv, q_norm, k_norm,
           w_o, w_pool, pool_scale, w_router, w_e_gate, w_e_up, w_e_down):
    batch, seq, _ = x_prompt.shape
    dec_batch, dec_seq, _ = x_sample.shape
    depth = w_ada.shape[0]
    kvw = N_KV * HEAD_DIM

    cvec = jnp.zeros((SUBLANES, D_MODEL), F32).at[0].set(c_ctx).at[1:1 + dec_batch].set(c)
    mods = _ada(cvec, w_ada, b_ada)
    ctx = x_prompt.reshape(batch * seq, D_MODEL)
    lat = x_sample.reshape(dec_batch * dec_seq, D_MODEL)
    ctx_row = lambda i: 0
    lat_row = lambda tile: (lambda i: 1 + (i * tile) // dec_seq)
    ctx_tile = min(TOKEN_TILE, seq)
    rope_tabs = _rope_tables(dec_seq)
    new_k = new_v = None

    for layer in range(depth):
        j = layer // 2
        n1 = norm1[layer][None]
        n2 = norm2[layer][None]
        wr = jnp.pad(w_router[layer], ((0, 0), (0, LANES - N_EXPERTS))).astype(BF16)
        if layer % 2 == 0:
            qn = q_norm[j][None]
            kn = k_norm[j][None]
            qc, kc, vc, new_k, new_v = _qkv(ctx, mods, layer, ctx_row, n1, w_qkv[j], qn, kn, None, True, TOKEN_TILE)
            oc = _attn_ctx(qc, kc, vc, seq)
            ql, kl, vl = _qkv(lat, mods, layer, lat_row(TOKEN_TILE), n1, w_qkv[j], qn, kn, rope_tabs, False,
                              TOKEN_TILE)
            past_k = cache_k[:, j].reshape(dec_batch, -1, kvw).astype(BF16)
            past_v = cache_v[:, j].reshape(dec_batch, -1, kvw).astype(BF16)
            ol = _attn_lat(ql, kl, vl, past_k, past_v, dec_seq, Q_TILE)
            ctx1 = _post(oc, ctx, mods, layer, ctx_row, n2, w_o[j], wr, TOKEN_TILE)
            lat1 = _post(ol, lat, mods, layer, lat_row(TOKEN_TILE), n2, w_o[j], wr, TOKEN_TILE)
        else:
            ps = pool_scale[j][None]
            ctx1 = _pool(ctx, mods, layer, ctx_row, n1, n2, w_pool[j], ps, wr, seq, ctx_tile)
            lat1 = _pool(lat, mods, layer, lat_row(TOKEN_TILE), n1, n2, w_pool[j], ps, wr, dec_seq, TOKEN_TILE)
        ctx = _moe(*ctx1, mods, layer, ctx_row, w_e_gate, w_e_up, w_e_down)
        lat = _moe(*lat1, mods, layer, lat_row(TOKEN_TILE), w_e_gate, w_e_up, w_e_down)

    new_cache_k = new_k.reshape(batch, 1, seq, N_KV, HEAD_DIM)
    new_cache_v = new_v.reshape(batch, 1, seq, N_KV, HEAD_DIM)
    return (ctx.reshape(batch, seq, D_MODEL), lat.reshape(dec_batch, dec_seq, D_MODEL), new_cache_k, new_cache_v)
```

```python
import functools

import jax
import jax.numpy as jnp
import numpy as np
from jax import lax
from jax.experimental import pallas as pl
from jax.experimental.pallas import tpu as pltpu

F32 = jnp.float32
BF16 = jnp.bfloat16
I32 = jnp.int32
U32 = jnp.uint32

D_MODEL = 1024
HEAD_DIM = 128
N_HEADS = 8
N_KV = 2
Q_PER_KV = N_HEADS // N_KV
QKV_DIM = (N_HEADS + 2 * N_KV) * HEAD_DIM
GRID_W = 64
ROPE_THETA = 10000.0
POOL_WINDOWS = (2, 4, 8, 16)
POOL_GROUP = D_MODEL // len(POOL_WINDOWS)
POOL_HALO = 8
N_EXPERTS = 16
CAPACITY_FACTOR = 2
N_MOD = 6
EPS = 1e-6
LOG2_E = 1.4426950408889634

LANES = 128
SUBLANES = 8
HALF = D_MODEL // 2
HP_ROWS = HALF // LANES
Y_ROWS = D_MODEL // LANES
VMEM_LIMIT = 56 * 1024 * 1024


def _cparams(n_axes, vmem=VMEM_LIMIT):
    return pltpu.CompilerParams(dimension_semantics=("arbitrary",) * n_axes, vmem_limit_bytes=vmem)


def _silu(x):
    return x / (1.0 + jnp.exp(-x))


def _rms(x, gain):
    return x * lax.rsqrt(jnp.mean(x * x, axis=-1, keepdims=True) + EPS) * gain


def _dot(a, b):
    return jnp.dot(a, b, preferred_element_type=F32)


def _dot_nt(a, b):
    return lax.dot_general(a, b, (((1,), (1,)), ((), ())), preferred_element_type=F32)


def _ada_kernel(c_ref, w_ref, b_ref, o_ref):
    s = _silu(c_ref[...]).astype(BF16)
    o_ref[0] = _dot(s, w_ref[0].astype(BF16)) + b_ref[0]


def _ada(cvec, w_ada, b_ada):
    depth = w_ada.shape[0]
    out = pl.pallas_call(
        _ada_kernel,
        out_shape=jax.ShapeDtypeStruct((depth, SUBLANES, N_MOD * D_MODEL), F32),
        grid=(depth, N_MOD),
        in_specs=[
            pl.BlockSpec((SUBLANES, D_MODEL), lambda i, j: (0, 0)),
            pl.BlockSpec((1, D_MODEL, D_MODEL), lambda i, j: (i, 0, j)),
            pl.BlockSpec((1, 1, D_MODEL), lambda i, j: (i, 0, j)),
        ],
        out_specs=pl.BlockSpec((1, SUBLANES, D_MODEL), lambda i, j: (i, 0, j)),
        compiler_params=_cparams(2),
        name="ada",
    )(cvec, w_ada, b_ada.reshape(depth, 1, N_MOD * D_MODEL))
    return out.reshape(depth, SUBLANES, N_MOD, D_MODEL)


def _mod_spec(layer, row_of_tile):
    return pl.BlockSpec((1, 1, N_MOD, D_MODEL), lambda i, *_: (layer, row_of_tile(i), 0, 0))


def _qkv_kernel(*refs, rope, cache_out):
    x_ref, mod_ref, n1_ref, w_ref, qn_ref, kn_ref = refs[:6]
    refs = refs[6:]
    if rope:
        cos_ref, sin_ref = refs[:2]
        refs = refs[2:]
    q_ref, k_ref, v_ref = refs[:3]
    refs = refs[3:]
    if cache_out:
        kc_ref, vc_ref = refs[:2]
        refs = refs[2:]
    (wb_ref,) = refs

    qk_w = (N_HEADS + N_KV) * HEAD_DIM
    quarter = HEAD_DIM // 4

    def partner(a):
        width = a.shape[1]
        first = (lax.broadcasted_iota(I32, a.shape, 1) & quarter) == 0
        return jnp.where(first, pltpu.roll(a, width - quarter, 1), pltpu.roll(a, quarter, 1))

    @pl.when(pl.program_id(0) == 0)
    def _():
        w = w_ref[...]
        wb_ref[:, 0:QKV_DIM] = w.astype(BF16)
        if rope:
            wb_ref[:, QKV_DIM:] = partner(w[:, 0:qk_w]).astype(BF16)

    m = mod_ref[0, 0]
    h = _rms(x_ref[...], n1_ref[...]) * (1.0 + m[1:2]) + m[0:1]
    qkv = _dot(h.astype(BF16), wb_ref[...])
    scale = HEAD_DIM ** -0.5 * LOG2_E
    if rope:
        gains = {True: qn_ref[...], False: kn_ref[...]}
        cos_g = {key: cos_ref[...] * g for key, g in gains.items()}
        sin_g = {key: sin_ref[...] * partner(g) for key, g in gains.items()}
    for hh in range(N_HEADS + N_KV):
        sl = slice(hh * HEAD_DIM, (hh + 1) * HEAD_DIM)
        is_q = hh < N_HEADS
        if rope:
            raw = qkv[:, sl]
            norm = lax.rsqrt(jnp.mean(raw * raw, axis=-1, keepdims=True) + EPS)
            xh = (raw * cos_g[is_q] + qkv[:, QKV_DIM + hh * HEAD_DIM:QKV_DIM + (hh + 1) * HEAD_DIM] * sin_g[is_q]) * norm
        else:
            xh = _rms(qkv[:, sl], qn_ref[...] if is_q else kn_ref[...])
        if cache_out and hh >= N_HEADS:
            kc_ref[:, (hh - N_HEADS) * HEAD_DIM:(hh - N_HEADS + 1) * HEAD_DIM] = xh
        if hh < N_HEADS:
            q_ref[:, sl] = (xh * scale).astype(BF16)
        else:
            k_ref[:, (hh - N_HEADS) * HEAD_DIM:(hh - N_HEADS + 1) * HEAD_DIM] = xh.astype(BF16)
    v = qkv[:, qk_w:QKV_DIM]
    v_ref[...] = v.astype(BF16)
    if cache_out:
        vc_ref[...] = v


def _qkv(x, mods, layer, row_of_tile, n1, w_qkv, qn, kn, rope_tabs, cache_out, tm):
    n_tok = x.shape[0]
    kvw = N_KV * HEAD_DIM
    rope = rope_tabs is not None
    in_specs = [
        pl.BlockSpec((tm, D_MODEL), lambda i: (i, 0)),
        _mod_spec(layer, row_of_tile),
        pl.BlockSpec((1, D_MODEL), lambda i: (0, 0)),
        pl.BlockSpec((D_MODEL, QKV_DIM), lambda i: (0, 0)),
        pl.BlockSpec((1, HEAD_DIM), lambda i: (0, 0)),
        pl.BlockSpec((1, HEAD_DIM), lambda i: (0, 0)),
    ]
    args = [x, mods, n1, w_qkv, qn, kn]
    if rope:
        seq_tiles = rope_tabs[0].shape[0] // tm
        in_specs += [pl.BlockSpec((tm, HEAD_DIM), lambda i: (i % seq_tiles, 0))] * 2
        args += list(rope_tabs)
    out_shape = [jax.ShapeDtypeStruct((n_tok, D_MODEL), BF16),
                 jax.ShapeDtypeStruct((n_tok, kvw), BF16),
                 jax.ShapeDtypeStruct((n_tok, kvw), BF16)]
    out_specs = [pl.BlockSpec((tm, D_MODEL), lambda i: (i, 0)),
                 pl.BlockSpec((tm, kvw), lambda i: (i, 0)),
                 pl.BlockSpec((tm, kvw), lambda i: (i, 0))]
    if cache_out:
        out_shape += [jax.ShapeDtypeStruct((n_tok, kvw), F32)] * 2
        out_specs += [pl.BlockSpec((tm, kvw), lambda i: (i, 0))] * 2
    return pl.pallas_call(
        functools.partial(_qkv_kernel, rope=rope, cache_out=cache_out),
        out_shape=out_shape,
        grid=(n_tok // tm,),
        in_specs=in_specs,
        out_specs=out_specs,
        scratch_shapes=[pltpu.VMEM((D_MODEL, QKV_DIM + ((N_HEADS + N_KV) * HEAD_DIM if rope else 0)), BF16)],
        compiler_params=_cparams(1),
        name="qkv_rope" if rope else "qkv",
    )(*args)


def _rope_tables(seq_len):
    half = HEAD_DIM // 2
    n = half // 2
    inv_freq = ROPE_THETA ** (-np.arange(n, dtype=np.float64) / n)
    rows = seq_len // GRID_W
    row = np.repeat(np.arange(rows), GRID_W).astype(np.float64)
    col = np.tile(np.arange(GRID_W), rows).astype(np.float64)
    ang_r = row[:, None] * inv_freq[None, :]
    ang_c = col[:, None] * inv_freq[None, :]
    cos = np.concatenate([np.cos(ang_r)] * 2 + [np.cos(ang_c)] * 2, axis=-1)
    sin = np.concatenate([-np.sin(ang_r), np.sin(ang_r), -np.sin(ang_c), np.sin(ang_c)], axis=-1)
    return jnp.asarray(cos, F32), jnp.asarray(sin, F32)


def _with_ones(v):
    return jnp.concatenate([v, jnp.ones_like(v)], axis=1)


def _gqa_attention(q, segments):
    rows = q.shape[0]
    qs = jnp.concatenate([q[:, h * HEAD_DIM:(h + 1) * HEAD_DIM] for h in range(Q_PER_KV)], axis=0)
    m = jnp.full((rows * Q_PER_KV, 1), -1e30, F32)
    acc = jnp.zeros((rows * Q_PER_KV, 2 * HEAD_DIM), F32)
    for k, v in segments:
        s = _dot_nt(qs, k)
        m_new = jnp.maximum(m, jnp.max(s, axis=-1, keepdims=True))
        acc = acc * jnp.exp2(m - m_new) + _dot(jnp.exp2(s - m_new).astype(BF16), v)
        m = m_new
    o = (acc[:, :HEAD_DIM] / acc[:, HEAD_DIM:]).astype(BF16)
    return jnp.concatenate([o[h * rows:(h + 1) * rows] for h in range(Q_PER_KV)], axis=1)


def _attn_ctx_kernel(q_ref, k_ref, v_ref, o_ref):
    gw = Q_PER_KV * HEAD_DIM
    for g in range(N_KV):
        kg = k_ref[:, g * HEAD_DIM:(g + 1) * HEAD_DIM]
        vg = _with_ones(v_ref[:, g * HEAD_DIM:(g + 1) * HEAD_DIM])
        o_ref[:, g * gw:(g + 1) * gw] = _gqa_attention(q_ref[:, g * gw:(g + 1) * gw], [(kg, vg)])


def _attn_ctx(q, k, v, seq):
    n_tok = q.shape[0]
    kvw = N_KV * HEAD_DIM
    return pl.pallas_call(
        _attn_ctx_kernel,
        out_shape=jax.ShapeDtypeStruct((n_tok, D_MODEL), BF16),
        grid=(n_tok // seq,),
        in_specs=[pl.BlockSpec((seq, D_MODEL), lambda b: (b, 0)),
                  pl.BlockSpec((seq, kvw), lambda b: (b, 0)),
                  pl.BlockSpec((seq, kvw), lambda b: (b, 0))],
        out_specs=pl.BlockSpec((seq, D_MODEL), lambda b: (b, 0)),
        compiler_params=_cparams(1),
        name="attn_ctx",
    )(q, k, v)


KEY_CHUNK = 1024


def _attn_lat_kernel(q_ref, k_ref, v_ref, kc_ref, vc_ref, o_ref):
    segments = [(k_ref[c:c + KEY_CHUNK], _with_ones(v_ref[c:c + KEY_CHUNK]))
                for c in range(0, k_ref.shape[0], KEY_CHUNK)]
    segments.append((kc_ref[0], _with_ones(vc_ref[0])))
    o_ref[...] = _gqa_attention(q_ref[...], segments)


def _attn_lat(q, k, v, kc, vc, seq, tq):
    n_tok = q.shape[0]
    batch = n_tok // seq
    past = kc.shape[1]
    qt = seq // tq
    gw = Q_PER_KV * HEAD_DIM
    return pl.pallas_call(
        _attn_lat_kernel,
        out_shape=jax.ShapeDtypeStruct((n_tok, D_MODEL), BF16),
        grid=(batch, N_KV, qt),
        in_specs=[pl.BlockSpec((tq, gw), lambda b, g, i: (b * qt + i, g)),
                  pl.BlockSpec((seq, HEAD_DIM), lambda b, g, i: (b, g)),
                  pl.BlockSpec((seq, HEAD_DIM), lambda b, g, i: (b, g)),
                  pl.BlockSpec((1, past, HEAD_DIM), lambda b, g, i: (b, 0, g)),
                  pl.BlockSpec((1, past, HEAD_DIM), lambda b, g, i: (b, 0, g))],
        out_specs=pl.BlockSpec((tq, gw), lambda b, g, i: (b * qt + i, g)),
        compiler_params=_cparams(3),
        name="attn_lat",
    )(q, k, v, kc, vc)


def _router_softmax(hb, wr):
    logits = _dot(hb, wr)
    lane = lax.broadcasted_iota(I32, logits.shape, 1)
    logits = jnp.where(lane < N_EXPERTS, logits, -1e30)
    ex = jnp.exp(logits - jnp.max(logits, axis=-1, keepdims=True))
    return ex / jnp.sum(ex, axis=-1, keepdims=True)


def _moe_front(x1, m, n2, wr, x1_ref, hp_ref, aff_ref):
    tm = x1.shape[0]
    x1_ref[...] = x1
    h2 = _rms(x1, n2) * (1.0 + m[4:5]) + m[3:4]
    aff = _router_softmax(h2.astype(BF16), wr)
    packed = pltpu.pack_elementwise([h2[:, :HALF], h2[:, HALF:]], packed_dtype=BF16)
    for s in range(HP_ROWS):
        hp_ref[pl.ds(s, tm, stride=HP_ROWS), :] = packed[:, s * LANES:(s + 1) * LANES]
    aff_t = aff.T
    for b in range(tm // LANES):
        aff_ref[b * N_EXPERTS:(b + 1) * N_EXPERTS, :] = aff_t[0:N_EXPERTS, b * LANES:(b + 1) * LANES]


def _front_out(n_tok, tm):
    shapes = [jax.ShapeDtypeStruct((n_tok, D_MODEL), F32),
              jax.ShapeDtypeStruct((n_tok * HP_ROWS, LANES), U32),
              jax.ShapeDtypeStruct((n_tok // LANES * N_EXPERTS, LANES), F32)]
    specs = [pl.BlockSpec((tm, D_MODEL), lambda i: (i, 0)),
             pl.BlockSpec((tm * HP_ROWS, LANES), lambda i: (i, 0)),
             pl.BlockSpec((tm // LANES * N_EXPERTS, LANES), lambda i: (i, 0))]
    return shapes, specs


def _post_kernel(o_ref, x_ref, mod_ref, n2_ref, wo_ref, wr_ref, x1_ref, hp_ref, aff_ref, wb_ref):
    @pl.when(pl.program_id(0) == 0)
    def _():
        wb_ref[...] = wo_ref[...].astype(BF16)

    m = mod_ref[0, 0]
    x1 = x_ref[...] + m[2:3] * _dot(o_ref[...], wb_ref[...])
    _moe_front(x1, m, n2_ref[...], wr_ref[...], x1_ref, hp_ref, aff_ref)


def _post(o, x, mods, layer, row_of_tile, n2, w_o, wr, tm):
    n_tok = x.shape[0]
    shapes, specs = _front_out(n_tok, tm)
    return pl.pallas_call(
        _post_kernel,
        out_shape=shapes,
        grid=(n_tok // tm,),
        in_specs=[pl.BlockSpec((tm, D_MODEL), lambda i: (i, 0)),
                  pl.BlockSpec((tm, D_MODEL), lambda i: (i, 0)),
                  _mod_spec(layer, row_of_tile),
                  pl.BlockSpec((1, D_MODEL), lambda i: (0, 0)),
                  pl.BlockSpec((D_MODEL, D_MODEL), lambda i: (0, 0)),
                  pl.BlockSpec((D_MODEL, LANES), lambda i: (0, 0))],
        out_specs=specs,
        scratch_shapes=[pltpu.VMEM((D_MODEL, D_MODEL), BF16)],
        compiler_params=_cparams(1),
        name="post",
    )(o, x, mods, n2, w_o, wr)


def _pool_kernel(x_ref, xp_ref, xn_ref, mod_ref, n1_ref, n2_ref, wp_ref, ps_ref, wr_ref,
                 x1_ref, hp_ref, aff_ref, *, seq):
    tm = x_ref.shape[0]
    i = pl.program_id(0)
    m = mod_ref[0, 0]
    x = x_ref[...]

    def norm_mod(v):
        return _rms(v, n1_ref[...]) * (1.0 + m[1:2]) + m[0:1]

    prev_ok = ((i * tm) % seq != 0).astype(F32)
    next_ok = (((i + 1) * tm) % seq != 0).astype(F32)
    h = norm_mod(x)
    hz = jnp.concatenate([norm_mod(xp_ref[...]) * prev_ok, h, norm_mod(xn_ref[...]) * next_ok], axis=0)
    rows = tm + 2 * POOL_HALO
    t = (i * tm) % seq + lax.broadcasted_iota(I32, (tm, 1), 0)
    ys = []
    for g, w in enumerate(POOL_WINDOWS):
        sl = slice(g * POOL_GROUP, (g + 1) * POOL_GROUP)
        f = hz[:, sl]
        step = 1
        while step < w:
            f = f + pltpu.roll(f, rows - step, 0)
            step *= 2
        win = pltpu.roll(f, w // 2, 0)[POOL_HALO:POOL_HALO + tm]
        cnt = (jnp.minimum(t + w // 2, seq) - jnp.maximum(t - w // 2, 0)).astype(F32)
        d = (win / cnt - h[:, sl]).astype(BF16)
        ys.append(_dot(d, wp_ref[g].astype(BF16)))
    y = jnp.concatenate(ys, axis=-1) * ps_ref[...]
    x1 = x + m[2:3] * y
    _moe_front(x1, m, n2_ref[...], wr_ref[...], x1_ref, hp_ref, aff_ref)


def _pool(x, mods, layer, row_of_tile, n1, n2, w_pool, pool_scale, wr, seq, tm):
    n_tok = x.shape[0]
    hb = tm // POOL_HALO
    last = n_tok // POOL_HALO - 1
    shapes, specs = _front_out(n_tok, tm)
    return pl.pallas_call(
        functools.partial(_pool_kernel, seq=seq),
        out_shape=shapes,
        grid=(n_tok // tm,),
        in_specs=[pl.BlockSpec((tm, D_MODEL), lambda i: (i, 0)),
                  pl.BlockSpec((POOL_HALO, D_MODEL), lambda i: (jnp.maximum(i * hb - 1, 0), 0)),
                  pl.BlockSpec((POOL_HALO, D_MODEL), lambda i: (jnp.minimum((i + 1) * hb, last), 0)),
                  _mod_spec(layer, row_of_tile),
                  pl.BlockSpec((1, D_MODEL), lambda i: (0, 0)),
                  pl.BlockSpec((1, D_MODEL), lambda i: (0, 0)),
                  pl.BlockSpec(w_pool.shape, lambda i: (0, 0, 0)),
                  pl.BlockSpec((1, D_MODEL), lambda i: (0, 0)),
                  pl.BlockSpec((D_MODEL, LANES), lambda i: (0, 0))],
        out_specs=specs,
        compiler_params=_cparams(1),
        name="pool",
    )(x, x, x, mods, n1, n2, w_pool, pool_scale, wr)


def _select_kernel(aff_ref, src_ref, dst_ref, gate_ref, *, cap):
    nb = aff_ref.shape[0] // N_EXPERTS
    rows = nb * N_EXPERTS
    shape3 = (nb, N_EXPERTS, LANES)
    aff = aff_ref[...].reshape(shape3)

    def count(mask):
        return jnp.sum(jnp.sum(mask.astype(F32), axis=0), axis=-1, keepdims=True)

    def search(i, t):
        cand = t | (jnp.int32(1) << (29 - i))
        return jnp.where(count(aff >= pltpu.bitcast(cand, F32)[None]) >= cap, cand, t)

    thr = pltpu.bitcast(lax.fori_loop(0, 30, search, jnp.zeros((N_EXPERTS, LANES), I32)), F32)
    gt = aff > thr[None]
    eq = aff == thr[None]
    need = cap - count(gt)

    kk = lax.broadcasted_iota(I32, (LANES, LANES), 0)
    nn = lax.broadcasted_iota(I32, (LANES, LANES), 1)
    upper = (kk <= nn).astype(BF16)
    ones = jnp.ones((LANES, LANES), BF16)

    def prefix(mask):
        m2 = mask.astype(F32).astype(BF16).reshape(rows, LANES)
        p = _dot(m2, upper).reshape(shape3)
        s = _dot(m2, ones).reshape(shape3)
        offs = []
        run = jnp.zeros((N_EXPERTS, LANES), F32)
        for b in range(nb):
            offs.append(run)
            run = run + s[b]
        return p, jnp.stack(offs, axis=0), s

    pe, oe, _ = prefix(eq)
    sel = gt | (eq & ((pe + oe) <= need[None]))
    ps, os_, ss = prefix(sel)

    lane = lax.broadcasted_iota(I32, (rows, LANES), 1)
    sel2 = sel.reshape(rows, LANES)
    dist = jnp.where(sel2, lane - (ps.reshape(rows, LANES).astype(I32) - 1), 0)
    alive = sel2.astype(I32)
    val = lane
    gval = aff.reshape(rows, LANES)
    for k in range(7):
        s = 1 << k
        move = alive * ((dist >> k) & 1)
        inc = pltpu.roll(move, LANES - s, 1) * (lane < LANES - s).astype(I32) == 1
        val = jnp.where(inc, pltpu.roll(val, LANES - s, 1), val)
        gval = jnp.where(inc, pltpu.roll(gval, LANES - s, 1), gval)
        dist = jnp.where(inc, pltpu.roll(dist, LANES - s, 1), dist)
        alive = alive - move + inc.astype(I32)
    local = val.reshape(shape3)
    local_gate = gval.reshape(shape3)

    top = os_ + ss
    lane_e = lax.broadcasted_iota(I32, (N_EXPERTS, LANES), 1)
    for jc in range(cap // LANES):
        j = (lane_e + jc * LANES).astype(F32)
        acc = jnp.zeros((N_EXPERTS, LANES), I32)
        gacc = jnp.zeros((N_EXPERTS, LANES), F32)
        for b in range(nb):
            inside = (os_[b] <= j) & (j < top[b])
            jl = (j - os_[b]).astype(I32) & (LANES - 1)
            acc = jnp.where(inside, jnp.take_along_axis(local[b], jl, axis=1) + b * LANES, acc)
            gacc = jnp.where(inside, jnp.take_along_axis(local_gate[b], jl, axis=1), gacc)
        out_rows = pl.ds(jc, N_EXPERTS, stride=cap // LANES)
        src_ref[out_rows, :] = acc * HP_ROWS
        dst_ref[out_rows, :] = acc * Y_ROWS
        gate_ref[out_rows, :] = gacc


def _select(aff, cap):
    rows = aff.shape[0]
    out_rows = N_EXPERTS * cap // LANES
    out_spec = pl.BlockSpec((out_rows, LANES), lambda i: (0, 0))
    src, dst, gate = pl.pallas_call(
        functools.partial(_select_kernel, cap=cap),
        out_shape=[jax.ShapeDtypeStruct((out_rows, LANES), I32),
                   jax.ShapeDtypeStruct((out_rows, LANES), I32),
                   jax.ShapeDtypeStruct((out_rows, LANES), F32)],
        grid=(1,),
        in_specs=[pl.BlockSpec((rows, LANES), lambda i: (0, 0))],
        out_specs=[out_spec, out_spec, out_spec],
        compiler_params=_cparams(1),
        name="select",
    )(aff)
    return src.reshape(-1), dst.reshape(-1), gate


GATHER_UNROLL = 8


def _gather_kernel(src_ref, hp_ref, o_ref):
    e = pl.program_id(0)
    cap = o_ref.shape[0] // HP_ROWS

    def gather(g, carry):
        base = e * cap + g * GATHER_UNROLL
        for r in range(GATHER_UNROLL):
            src = pl.multiple_of(src_ref[base + r], HP_ROWS)
            dst = pl.multiple_of((g * GATHER_UNROLL + r) * HP_ROWS, HP_ROWS)
            o_ref[pl.ds(dst, HP_ROWS), :] = hp_ref[pl.ds(src, HP_ROWS), :]
        return carry

    lax.fori_loop(0, cap // GATHER_UNROLL, gather, 0)


def _gather(src, hp):
    cap = src.shape[0] // N_EXPERTS
    return pl.pallas_call(
        _gather_kernel,
        out_shape=jax.ShapeDtypeStruct((N_EXPERTS * cap * HP_ROWS, LANES), U32),
        grid_spec=pltpu.PrefetchScalarGridSpec(
            num_scalar_prefetch=1,
            grid=(N_EXPERTS,),
            in_specs=[pl.BlockSpec(hp.shape, lambda e, src: (0, 0), pipeline_mode=pl.Buffered(1))],
            out_specs=pl.BlockSpec((cap * HP_ROWS, LANES), lambda e, src: (e, 0))),
        compiler_params=_cparams(1),
        name="gather",
    )(src, hp)


FF_CHUNK = 256


def _ffn_kernel(*refs, n_streams):
    e = pl.program_id(0)
    ins = refs[:2 * n_streams]
    wg_ref, wu_ref, wd_ref = refs[2 * n_streams:2 * n_streams + 3]
    outs = refs[2 * n_streams + 3:]
    d_ff = wg_ref.shape[3]
    eye = lax.broadcasted_iota(I32, (LANES, LANES), 0) == lax.broadcasted_iota(I32, (LANES, LANES), 1)
    for i in range(n_streams):
        xp_ref, gate_ref, y_ref = ins[2 * i], ins[2 * i + 1], outs[i]
        cap = xp_ref.shape[0] // HP_ROWS
        halves = [[], []]
        for s in range(HP_ROWS):
            w = xp_ref[pl.ds(s, cap, stride=HP_ROWS), :]
            for half in range(2):
                v = pltpu.unpack_elementwise(w, index=half, packed_dtype=BF16, unpacked_dtype=F32)
                halves[half].append(v.astype(BF16))
        xe = jnp.concatenate(halves[0] + halves[1], axis=1)
        y = None
        for c in range(0, d_ff, FF_CHUNK):
            a = _dot(xe, wg_ref[0, 0, :, c:c + FF_CHUNK].astype(BF16))
            u = _dot(xe, wu_ref[0, 0, :, c:c + FF_CHUNK].astype(BF16))
            yc = _dot((_silu(a) * u).astype(BF16), wd_ref[0, 0, c:c + FF_CHUNK, :].astype(BF16))
            y = yc if y is None else y + yc
        for c in range(cap // LANES):
            g_row = gate_ref[pl.ds(e * (cap // LANES) + c, 1), :]
            g_col = jnp.sum(jnp.where(eye, g_row, 0.0), axis=1, keepdims=True)
            tot = y[c * LANES:(c + 1) * LANES, :] * g_col
            for s in range(Y_ROWS):
                y_ref[0, pl.ds(c * LANES * Y_ROWS + s, LANES, stride=Y_ROWS), :] = tot[:, s * LANES:(s + 1) * LANES]


def _ffn(streams, layer, w_gate, w_up, w_down):
    caps = [xp.shape[0] // (N_EXPERTS * HP_ROWS) for xp, _ in streams]
    in_specs, args = [], []
    for (xp, gate), cap in zip(streams, caps):
        in_specs += [pl.BlockSpec((cap * HP_ROWS, LANES), lambda e: (e, 0)),
                     pl.BlockSpec(gate.shape, lambda e: (0, 0))]
        args += [xp, gate]
    w_spec = pl.BlockSpec((1, 1) + w_gate.shape[2:], lambda e: (layer, e, 0, 0))
    return pl.pallas_call(
        functools.partial(_ffn_kernel, n_streams=len(streams)),
        out_shape=[jax.ShapeDtypeStruct((N_EXPERTS, cap * Y_ROWS, LANES), F32) for cap in caps],
        grid=(N_EXPERTS,),
        in_specs=in_specs + [w_spec, w_spec, w_spec],
        out_specs=[pl.BlockSpec((1, cap * Y_ROWS, LANES), lambda e: (e, 0, 0)) for cap in caps],
        compiler_params=_cparams(1),
        name="ffn",
    )(*args, w_gate, w_up, w_down)


SCATTER_UNROLL = 16


def _combine_kernel(dst_ref, y_ref, x1_ref, mod_ref, o_ref, acc_ref):
    s = pl.program_id(0)
    tf = o_ref.shape[0]
    cap = y_ref.shape[1] // Y_ROWS

    @pl.when(s == 0)
    def _():
        acc_ref[...] = jnp.zeros_like(acc_ref)

    @pl.when(s < N_EXPERTS)
    def _():
        def scatter(g, carry):
            base = s * cap + g * SCATTER_UNROLL
            rows, sums = [], []
            for r in range(SCATTER_UNROLL):
                dst = pl.ds(pl.multiple_of(dst_ref[base + r], Y_ROWS), Y_ROWS)
                src = pl.ds(pl.multiple_of((g * SCATTER_UNROLL + r) * Y_ROWS, Y_ROWS), Y_ROWS)
                rows.append(dst)
                sums.append(acc_ref[dst, :] + y_ref[0, src, :])
            for dst, v in zip(rows, sums):
                acc_ref[dst, :] = v
            return carry

        lax.fori_loop(0, cap // SCATTER_UNROLL, scatter, 0)

    @pl.when(s >= N_EXPERTS)
    def _():
        g2 = mod_ref[0, 0][5:6]
        tile = acc_ref.at[pl.ds(pl.multiple_of((s - N_EXPERTS) * tf * Y_ROWS, tf * Y_ROWS), tf * Y_ROWS), :]
        for c in range(Y_ROWS):
            sl = slice(c * LANES, (c + 1) * LANES)
            o_ref[:, sl] = x1_ref[:, sl] + g2[:, sl] * tile[pl.ds(c, tf, stride=Y_ROWS), :]


def _combine(dst, y, x1, mods, layer, row_of_tile, tf):
    n_tok = x1.shape[0]
    cap = dst.shape[0] // N_EXPERTS
    tile_of = lambda s: jnp.maximum(s - N_EXPERTS, 0)
    return pl.pallas_call(
        _combine_kernel,
        out_shape=jax.ShapeDtypeStruct((n_tok, D_MODEL), F32),
        grid_spec=pltpu.PrefetchScalarGridSpec(
            num_scalar_prefetch=1,
            grid=(N_EXPERTS + n_tok // tf,),
            in_specs=[pl.BlockSpec((1, cap * Y_ROWS, LANES), lambda s, *_: (jnp.minimum(s, N_EXPERTS - 1), 0, 0)),
                      pl.BlockSpec((tf, D_MODEL), lambda s, *_: (tile_of(s), 0)),
                      pl.BlockSpec((1, 1, N_MOD, D_MODEL), lambda s, *_: (layer, row_of_tile(tile_of(s)), 0, 0))],
            out_specs=pl.BlockSpec((tf, D_MODEL), lambda s, *_: (tile_of(s), 0)),
            scratch_shapes=[pltpu.VMEM((n_tok * Y_ROWS, LANES), F32)]),
        compiler_params=_cparams(1),
        name="combine",
    )(dst, y, x1, mods)


TOKEN_TILE = 512
Q_TILE = 256


def _moe(fronts, rows_of_tile, mods, layer, w_gate, w_up, w_down):
    routed, dsts = [], []
    for x1, hp, aff in fronts:
        cap = CAPACITY_FACTOR * x1.shape[0] // N_EXPERTS
        src, dst, gate = _select(aff, cap)
        routed.append((_gather(src, hp), gate))
        dsts.append(dst)
    ys = _ffn(routed, layer, w_gate, w_up, w_down)
    return [_combine(dst, y, x1, mods, layer, row_of_tile, TOKEN_TILE)
            for dst, y, (x1, _, _), row_of_tile in zip(dsts, ys, fronts, rows_of_tile)]


def kernel(x_prompt, x_sample, cache_k, cache_v, c, c_ctx, norm1, norm2, w_ada, b_ada, w_qkv, q_norm, k_norm,
           w_o, w_pool, pool_scale, w_router, w_e_gate, w_e_up, w_e_down):
    batch, seq, _ = x_prompt.shape
    dec_batch, dec_seq, _ = x_sample.shape
    depth = w_ada.shape[0]
    kvw = N_KV * HEAD_DIM

    cvec = jnp.zeros((SUBLANES, D_MODEL), F32).at[0].set(c_ctx).at[1:1 + dec_batch].set(c)
    mods = _ada(cvec, w_ada, b_ada)
    ctx = x_prompt.reshape(batch * seq, D_MODEL)
    lat = x_sample.reshape(dec_batch * dec_seq, D_MODEL)
    ctx_row = lambda i: 0
    lat_row = lambda tile: (lambda i: 1 + (i * tile) // dec_seq)
    ctx_tile = min(TOKEN_TILE, seq)
    rope_tabs = _rope_tables(dec_seq)
    new_k = new_v = None

    for layer in range(depth):
        j = layer // 2
        n1 = norm1[layer][None]
        n2 = norm2[layer][None]
        wr = jnp.pad(w_router[layer], ((0, 0), (0, LANES - N_EXPERTS))).astype(BF16)
        if layer % 2 == 0:
            qn = q_norm[j][None]
            kn = k_norm[j][None]
            qc, kc, vc, new_k, new_v = _qkv(ctx, mods, layer, ctx_row, n1, w_qkv[j], qn, kn, None, True, TOKEN_TILE)
            oc = _attn_ctx(qc, kc, vc, seq)
            ql, kl, vl = _qkv(lat, mods, layer, lat_row(TOKEN_TILE), n1, w_qkv[j], qn, kn, rope_tabs, False,
                              TOKEN_TILE)
            past_k = cache_k[:, j].reshape(dec_batch, -1, kvw).astype(BF16)
            past_v = cache_v[:, j].reshape(dec_batch, -1, kvw).astype(BF16)
            ol = _attn_lat(ql, kl, vl, past_k, past_v, dec_seq, Q_TILE)
            ctx1 = _post(oc, ctx, mods, layer, ctx_row, n2, w_o[j], wr, TOKEN_TILE)
            lat1 = _post(ol, lat, mods, layer, lat_row(TOKEN_TILE), n2, w_o[j], wr, TOKEN_TILE)
        else:
            ps = pool_scale[j][None]
            ctx1 = _pool(ctx, mods, layer, ctx_row, n1, n2, w_pool[j], ps, wr, seq, ctx_tile)
            lat1 = _pool(lat, mods, layer, lat_row(TOKEN_TILE), n1, n2, w_pool[j], ps, wr, dec_seq, TOKEN_TILE)
        ctx, lat = _moe([ctx1, lat1], [ctx_row, lat_row(TOKEN_TILE)], mods, layer, w_e_gate, w_e_up, w_e_down)

    new_cache_k = new_k.reshape(batch, 1, seq, N_KV, HEAD_DIM)
    new_cache_v = new_v.reshape(batch, 1, seq, N_KV, HEAD_DIM)
    return (ctx.reshape(batch, seq, D_MODEL), lat.reshape(dec_batch, dec_seq, D_MODEL), new_cache_k, new_cache_v)
```

```python
import functools

import jax
import jax.numpy as jnp
import numpy as np
from jax import lax
from jax.experimental import pallas as pl
from jax.experimental.pallas import tpu as pltpu

F32 = jnp.float32
BF16 = jnp.bfloat16
I32 = jnp.int32
U32 = jnp.uint32

D_MODEL = 1024
HEAD_DIM = 128
N_HEADS = 8
N_KV = 2
Q_PER_KV = N_HEADS // N_KV
QKV_DIM = (N_HEADS + 2 * N_KV) * HEAD_DIM
GRID_W = 64
ROPE_THETA = 10000.0
POOL_WINDOWS = (2, 4, 8, 16)
POOL_GROUP = D_MODEL // len(POOL_WINDOWS)
POOL_HALO = 8
N_EXPERTS = 16
CAPACITY_FACTOR = 2
N_MOD = 6
EPS = 1e-6
LOG2_E = 1.4426950408889634

LANES = 128
SUBLANES = 8
HALF = D_MODEL // 2
HP_ROWS = HALF // LANES
Y_ROWS = D_MODEL // LANES
VMEM_LIMIT = 56 * 1024 * 1024


def _cparams(n_axes, vmem=VMEM_LIMIT):
    return pltpu.CompilerParams(dimension_semantics=("arbitrary",) * n_axes, vmem_limit_bytes=vmem)


def _silu(x):
    return x / (1.0 + jnp.exp(-x))


def _rms(x, gain):
    return x * lax.rsqrt(jnp.mean(x * x, axis=-1, keepdims=True) + EPS) * gain


def _norm_mod(x, gain, shift, scale):
    return _rms(x, gain * (1.0 + scale)) + shift


def _dot(a, b):
    return jnp.dot(a, b, preferred_element_type=F32)


def _dot_nt(a, b):
    return lax.dot_general(a, b, (((1,), (1,)), ((), ())), preferred_element_type=F32)


def _ada_kernel(c_ref, w_ref, b_ref, o_ref):
    s = _silu(c_ref[...]).astype(BF16)
    o_ref[0] = _dot(s, w_ref[0].astype(BF16)) + b_ref[0]


def _ada(cvec, w_ada, b_ada):
    depth = w_ada.shape[0]
    out = pl.pallas_call(
        _ada_kernel,
        out_shape=jax.ShapeDtypeStruct((depth, SUBLANES, N_MOD * D_MODEL), F32),
        grid=(depth, N_MOD),
        in_specs=[
            pl.BlockSpec((SUBLANES, D_MODEL), lambda i, j: (0, 0)),
            pl.BlockSpec((1, D_MODEL, D_MODEL), lambda i, j: (i, 0, j)),
            pl.BlockSpec((1, 1, D_MODEL), lambda i, j: (i, 0, j)),
        ],
        out_specs=pl.BlockSpec((1, SUBLANES, D_MODEL), lambda i, j: (i, 0, j)),
        compiler_params=_cparams(2),
        name="ada",
    )(cvec, w_ada, b_ada.reshape(depth, 1, N_MOD * D_MODEL))
    return out.reshape(depth, SUBLANES, N_MOD, D_MODEL)


def _mod_spec(layer, row_of_tile):
    return pl.BlockSpec((1, 1, N_MOD, D_MODEL), lambda i, *_: (layer, row_of_tile(i), 0, 0))


def _qkv_kernel(*refs, rope, cache_out):
    x_ref, mod_ref, n1_ref, w_ref, qn_ref, kn_ref = refs[:6]
    refs = refs[6:]
    if rope:
        cos_ref, sin_ref = refs[:2]
        refs = refs[2:]
    q_ref, k_ref, v_ref = refs[:3]
    refs = refs[3:]
    if cache_out:
        kc_ref, vc_ref = refs[:2]
        refs = refs[2:]
    (wb_ref,) = refs

    qk_w = (N_HEADS + N_KV) * HEAD_DIM
    quarter = HEAD_DIM // 4

    def partner(a):
        width = a.shape[1]
        first = (lax.broadcasted_iota(I32, a.shape, 1) & quarter) == 0
        return jnp.where(first, pltpu.roll(a, width - quarter, 1), pltpu.roll(a, quarter, 1))

    @pl.when(pl.program_id(0) == 0)
    def _():
        w = w_ref[...]
        wb_ref[:, 0:QKV_DIM] = w.astype(BF16)
        if rope:
            wb_ref[:, QKV_DIM:] = partner(w[:, 0:qk_w]).astype(BF16)

    m = mod_ref[0, 0]
    h = _norm_mod(x_ref[...], n1_ref[...], m[0:1], m[1:2])
    qkv = _dot(h.astype(BF16), wb_ref[...])
    scale = HEAD_DIM ** -0.5 * LOG2_E
    if rope:
        gains = {True: qn_ref[...], False: kn_ref[...]}
        cos_g = {key: cos_ref[...] * g for key, g in gains.items()}
        sin_g = {key: sin_ref[...] * partner(g) for key, g in gains.items()}
    for hh in range(N_HEADS + N_KV):
        sl = slice(hh * HEAD_DIM, (hh + 1) * HEAD_DIM)
        is_q = hh < N_HEADS
        if rope:
            raw = qkv[:, sl]
            norm = lax.rsqrt(jnp.mean(raw * raw, axis=-1, keepdims=True) + EPS)
            xh = (raw * cos_g[is_q] + qkv[:, QKV_DIM + hh * HEAD_DIM:QKV_DIM + (hh + 1) * HEAD_DIM] * sin_g[is_q]) * norm
        else:
            xh = _rms(qkv[:, sl], qn_ref[...] if is_q else kn_ref[...])
        if cache_out and hh >= N_HEADS:
            kc_ref[:, (hh - N_HEADS) * HEAD_DIM:(hh - N_HEADS + 1) * HEAD_DIM] = xh
        if hh < N_HEADS:
            q_ref[:, sl] = (xh * scale).astype(BF16)
        else:
            k_ref[:, (hh - N_HEADS) * HEAD_DIM:(hh - N_HEADS + 1) * HEAD_DIM] = xh.astype(BF16)
    v = qkv[:, qk_w:QKV_DIM]
    v_ref[...] = v.astype(BF16)
    if cache_out:
        vc_ref[...] = v


def _qkv(x, mods, layer, row_of_tile, n1, w_qkv, qn, kn, rope_tabs, cache_out, tm):
    n_tok = x.shape[0]
    kvw = N_KV * HEAD_DIM
    rope = rope_tabs is not None
    in_specs = [
        pl.BlockSpec((tm, D_MODEL), lambda i: (i, 0)),
        _mod_spec(layer, row_of_tile),
        pl.BlockSpec((1, D_MODEL), lambda i: (0, 0)),
        pl.BlockSpec((D_MODEL, QKV_DIM), lambda i: (0, 0)),
        pl.BlockSpec((1, HEAD_DIM), lambda i: (0, 0)),
        pl.BlockSpec((1, HEAD_DIM), lambda i: (0, 0)),
    ]
    args = [x, mods, n1, w_qkv, qn, kn]
    if rope:
        seq_tiles = rope_tabs[0].shape[0] // tm
        in_specs += [pl.BlockSpec((tm, HEAD_DIM), lambda i: (i % seq_tiles, 0))] * 2
        args += list(rope_tabs)
    out_shape = [jax.ShapeDtypeStruct((n_tok, D_MODEL), BF16),
                 jax.ShapeDtypeStruct((n_tok, kvw), BF16),
                 jax.ShapeDtypeStruct((n_tok, kvw), BF16)]
    out_specs = [pl.BlockSpec((tm, D_MODEL), lambda i: (i, 0)),
                 pl.BlockSpec((tm, kvw), lambda i: (i, 0)),
                 pl.BlockSpec((tm, kvw), lambda i: (i, 0))]
    if cache_out:
        out_shape += [jax.ShapeDtypeStruct((n_tok, kvw), F32)] * 2
        out_specs += [pl.BlockSpec((tm, kvw), lambda i: (i, 0))] * 2
    return pl.pallas_call(
        functools.partial(_qkv_kernel, rope=rope, cache_out=cache_out),
        out_shape=out_shape,
        grid=(n_tok // tm,),
        in_specs=in_specs,
        out_specs=out_specs,
        scratch_shapes=[pltpu.VMEM((D_MODEL, QKV_DIM + ((N_HEADS + N_KV) * HEAD_DIM if rope else 0)), BF16)],
        compiler_params=_cparams(1),
        name="qkv_rope" if rope else "qkv",
    )(*args)


def _rope_tables(seq_len):
    half = HEAD_DIM // 2
    n = half // 2
    inv_freq = ROPE_THETA ** (-np.arange(n, dtype=np.float64) / n)
    rows = seq_len // GRID_W
    row = np.repeat(np.arange(rows), GRID_W).astype(np.float64)
    col = np.tile(np.arange(GRID_W), rows).astype(np.float64)
    ang_r = row[:, None] * inv_freq[None, :]
    ang_c = col[:, None] * inv_freq[None, :]
    cos = np.concatenate([np.cos(ang_r)] * 2 + [np.cos(ang_c)] * 2, axis=-1)
    sin = np.concatenate([-np.sin(ang_r), np.sin(ang_r), -np.sin(ang_c), np.sin(ang_c)], axis=-1)
    return jnp.asarray(cos, F32), jnp.asarray(sin, F32)


def _with_ones(v):
    return jnp.concatenate([v, jnp.ones_like(v)], axis=1)


def _gqa_attention(q, segments):
    rows = q.shape[0]
    qs = jnp.concatenate([q[:, h * HEAD_DIM:(h + 1) * HEAD_DIM] for h in range(Q_PER_KV)], axis=0)
    m = jnp.full((rows * Q_PER_KV, 1), -1e30, F32)
    acc = jnp.zeros((rows * Q_PER_KV, 2 * HEAD_DIM), F32)
    for k, v in segments:
        s = _dot_nt(qs, k)
        m_new = jnp.maximum(m, jnp.max(s, axis=-1, keepdims=True))
        acc = acc * jnp.exp2(m - m_new) + _dot(jnp.exp2(s - m_new).astype(BF16), v)
        m = m_new
    o = (acc[:, :HEAD_DIM] / acc[:, HEAD_DIM:]).astype(BF16)
    return jnp.concatenate([o[h * rows:(h + 1) * rows] for h in range(Q_PER_KV)], axis=1)


def _attn_ctx_kernel(q_ref, k_ref, v_ref, o_ref):
    gw = Q_PER_KV * HEAD_DIM
    for g in range(N_KV):
        kg = k_ref[:, g * HEAD_DIM:(g + 1) * HEAD_DIM]
        vg = _with_ones(v_ref[:, g * HEAD_DIM:(g + 1) * HEAD_DIM])
        o_ref[:, g * gw:(g + 1) * gw] = _gqa_attention(q_ref[:, g * gw:(g + 1) * gw], [(kg, vg)])


def _attn_ctx(q, k, v, seq):
    n_tok = q.shape[0]
    kvw = N_KV * HEAD_DIM
    return pl.pallas_call(
        _attn_ctx_kernel,
        out_shape=jax.ShapeDtypeStruct((n_tok, D_MODEL), BF16),
        grid=(n_tok // seq,),
        in_specs=[pl.BlockSpec((seq, D_MODEL), lambda b: (b, 0)),
                  pl.BlockSpec((seq, kvw), lambda b: (b, 0)),
                  pl.BlockSpec((seq, kvw), lambda b: (b, 0))],
        out_specs=pl.BlockSpec((seq, D_MODEL), lambda b: (b, 0)),
        compiler_params=_cparams(1),
        name="attn_ctx",
    )(q, k, v)


KEY_CHUNK = 1024


def _attn_lat_kernel(q_ref, k_ref, v_ref, kc_ref, vc_ref, o_ref):
    segments = [(k_ref[c:c + KEY_CHUNK], _with_ones(v_ref[c:c + KEY_CHUNK]))
                for c in range(0, k_ref.shape[0], KEY_CHUNK)]
    segments.append((kc_ref[0], _with_ones(vc_ref[0])))
    o_ref[...] = _gqa_attention(q_ref[...], segments)


def _attn_lat(q, k, v, kc, vc, seq, tq):
    n_tok = q.shape[0]
    batch = n_tok // seq
    past = kc.shape[1]
    qt = seq // tq
    gw = Q_PER_KV * HEAD_DIM
    return pl.pallas_call(
        _attn_lat_kernel,
        out_shape=jax.ShapeDtypeStruct((n_tok, D_MODEL), BF16),
        grid=(batch, N_KV, qt),
        in_specs=[pl.BlockSpec((tq, gw), lambda b, g, i: (b * qt + i, g)),
                  pl.BlockSpec((seq, HEAD_DIM), lambda b, g, i: (b, g)),
                  pl.BlockSpec((seq, HEAD_DIM), lambda b, g, i: (b, g)),
                  pl.BlockSpec((1, past, HEAD_DIM), lambda b, g, i: (b, 0, g)),
                  pl.BlockSpec((1, past, HEAD_DIM), lambda b, g, i: (b, 0, g))],
        out_specs=pl.BlockSpec((tq, gw), lambda b, g, i: (b * qt + i, g)),
        compiler_params=_cparams(3),
        name="attn_lat",
    )(q, k, v, kc, vc)


def _router_softmax(hb, wr):
    logits = _dot(hb, wr)
    lane = lax.broadcasted_iota(I32, logits.shape, 1)
    logits = jnp.where(lane < N_EXPERTS, logits, -1e30)
    ex = jnp.exp(logits - jnp.max(logits, axis=-1, keepdims=True))
    return ex / jnp.sum(ex, axis=-1, keepdims=True)


def _moe_front(x1, m, n2, wr, x1_ref, hp_ref, aff_ref):
    tm = x1.shape[0]
    x1_ref[...] = x1
    h2 = _norm_mod(x1, n2, m[3:4], m[4:5])
    aff = _router_softmax(h2.astype(BF16), wr)
    packed = pltpu.pack_elementwise([h2[:, :HALF], h2[:, HALF:]], packed_dtype=BF16)
    for s in range(HP_ROWS):
        hp_ref[pl.ds(s, tm, stride=HP_ROWS), :] = packed[:, s * LANES:(s + 1) * LANES]
    aff_t = aff.T
    for b in range(tm // LANES):
        aff_ref[b * N_EXPERTS:(b + 1) * N_EXPERTS, :] = aff_t[0:N_EXPERTS, b * LANES:(b + 1) * LANES]


def _front_out(n_tok, tm):
    shapes = [jax.ShapeDtypeStruct((n_tok, D_MODEL), F32),
              jax.ShapeDtypeStruct((n_tok * HP_ROWS, LANES), U32),
              jax.ShapeDtypeStruct((n_tok // LANES * N_EXPERTS, LANES), F32)]
    specs = [pl.BlockSpec((tm, D_MODEL), lambda i: (i, 0)),
             pl.BlockSpec((tm * HP_ROWS, LANES), lambda i: (i, 0)),
             pl.BlockSpec((tm // LANES * N_EXPERTS, LANES), lambda i: (i, 0))]
    return shapes, specs


def _post_kernel(o_ref, x_ref, mod_ref, n2_ref, wo_ref, wr_ref, x1_ref, hp_ref, aff_ref, wb_ref):
    @pl.when(pl.program_id(0) == 0)
    def _():
        wb_ref[...] = wo_ref[...].astype(BF16)

    m = mod_ref[0, 0]
    x1 = x_ref[...] + m[2:3] * _dot(o_ref[...], wb_ref[...])
    _moe_front(x1, m, n2_ref[...], wr_ref[...], x1_ref, hp_ref, aff_ref)


def _post(o, x, mods, layer, row_of_tile, n2, w_o, wr, tm):
    n_tok = x.shape[0]
    shapes, specs = _front_out(n_tok, tm)
    return pl.pallas_call(
        _post_kernel,
        out_shape=shapes,
        grid=(n_tok // tm,),
        in_specs=[pl.BlockSpec((tm, D_MODEL), lambda i: (i, 0)),
                  pl.BlockSpec((tm, D_MODEL), lambda i: (i, 0)),
                  _mod_spec(layer, row_of_tile),
                  pl.BlockSpec((1, D_MODEL), lambda i: (0, 0)),
                  pl.BlockSpec((D_MODEL, D_MODEL), lambda i: (0, 0)),
                  pl.BlockSpec((D_MODEL, LANES), lambda i: (0, 0))],
        out_specs=specs,
        scratch_shapes=[pltpu.VMEM((D_MODEL, D_MODEL), BF16)],
        compiler_params=_cparams(1),
        name="post",
    )(o, x, mods, n2, w_o, wr)


def _pool_kernel(x_ref, xp_ref, xn_ref, mod_ref, n1_ref, n2_ref, wp_ref, ps_ref, wr_ref,
                 x1_ref, hp_ref, aff_ref, *, seq):
    tm = x_ref.shape[0]
    i = pl.program_id(0)
    m = mod_ref[0, 0]
    x = x_ref[...]

    def norm_mod(v):
        return _norm_mod(v, n1_ref[...], m[0:1], m[1:2])

    prev_ok = ((i * tm) % seq != 0).astype(F32)
    next_ok = (((i + 1) * tm) % seq != 0).astype(F32)
    h = norm_mod(x)
    hz = jnp.concatenate([norm_mod(xp_ref[...]) * prev_ok, h, norm_mod(xn_ref[...]) * next_ok], axis=0)
    rows = tm + 2 * POOL_HALO
    t = (i * tm) % seq + lax.broadcasted_iota(I32, (tm, 1), 0)
    ys = []
    for g, w in enumerate(POOL_WINDOWS):
        sl = slice(g * POOL_GROUP, (g + 1) * POOL_GROUP)
        f = hz[:, sl]
        step = 1
        while step < w:
            f = f + pltpu.roll(f, rows - step, 0)
            step *= 2
        win = pltpu.roll(f, w // 2, 0)[POOL_HALO:POOL_HALO + tm]
        cnt = (jnp.minimum(t + w // 2, seq) - jnp.maximum(t - w // 2, 0)).astype(F32)
        d = (win / cnt - h[:, sl]).astype(BF16)
        ys.append(_dot(d, wp_ref[g].astype(BF16)))
    x1 = x + (m[2:3] * ps_ref[...]) * jnp.concatenate(ys, axis=-1)
    _moe_front(x1, m, n2_ref[...], wr_ref[...], x1_ref, hp_ref, aff_ref)


def _pool(x, mods, layer, row_of_tile, n1, n2, w_pool, pool_scale, wr, seq, tm):
    n_tok = x.shape[0]
    hb = tm // POOL_HALO
    last = n_tok // POOL_HALO - 1
    shapes, specs = _front_out(n_tok, tm)
    return pl.pallas_call(
        functools.partial(_pool_kernel, seq=seq),
        out_shape=shapes,
        grid=(n_tok // tm,),
        in_specs=[pl.BlockSpec((tm, D_MODEL), lambda i: (i, 0)),
                  pl.BlockSpec((POOL_HALO, D_MODEL), lambda i: (jnp.maximum(i * hb - 1, 0), 0)),
                  pl.BlockSpec((POOL_HALO, D_MODEL), lambda i: (jnp.minimum((i + 1) * hb, last), 0)),
                  _mod_spec(layer, row_of_tile),
                  pl.BlockSpec((1, D_MODEL), lambda i: (0, 0)),
                  pl.BlockSpec((1, D_MODEL), lambda i: (0, 0)),
                  pl.BlockSpec(w_pool.shape, lambda i: (0, 0, 0)),
                  pl.BlockSpec((1, D_MODEL), lambda i: (0, 0)),
                  pl.BlockSpec((D_MODEL, LANES), lambda i: (0, 0))],
        out_specs=specs,
        compiler_params=_cparams(1),
        name="pool",
    )(x, x, x, mods, n1, n2, w_pool, pool_scale, wr)


def _select_kernel(aff_ref, src_ref, dst_ref, gate_ref, *, cap):
    nb = aff_ref.shape[0] // N_EXPERTS
    rows = nb * N_EXPERTS
    shape3 = (nb, N_EXPERTS, LANES)
    aff = aff_ref[...].reshape(shape3)

    def count(mask):
        return jnp.sum(jnp.sum(mask.astype(F32), axis=0), axis=-1, keepdims=True)

    def search(i, t):
        cand = t | (jnp.int32(1) << (29 - i))
        return jnp.where(count(aff >= pltpu.bitcast(cand, F32)[None]) >= cap, cand, t)

    thr = pltpu.bitcast(lax.fori_loop(0, 30, search, jnp.zeros((N_EXPERTS, LANES), I32)), F32)
    gt = aff > thr[None]
    eq = aff == thr[None]
    need = cap - count(gt)

    kk = lax.broadcasted_iota(I32, (LANES, LANES), 0)
    nn = lax.broadcasted_iota(I32, (LANES, LANES), 1)
    upper = (kk <= nn).astype(BF16)
    ones = jnp.ones((LANES, LANES), BF16)

    def prefix(mask):
        m2 = mask.astype(F32).astype(BF16).reshape(rows, LANES)
        p = _dot(m2, upper).reshape(shape3)
        s = _dot(m2, ones).reshape(shape3)
        offs = []
        run = jnp.zeros((N_EXPERTS, LANES), F32)
        for b in range(nb):
            offs.append(run)
            run = run + s[b]
        return p, jnp.stack(offs, axis=0), s

    pe, oe, _ = prefix(eq)
    sel = gt | (eq & ((pe + oe) <= need[None]))
    ps, os_, ss = prefix(sel)

    lane = lax.broadcasted_iota(I32, (rows, LANES), 1)
    sel2 = sel.reshape(rows, LANES)
    dist = jnp.where(sel2, lane - (ps.reshape(rows, LANES).astype(I32) - 1), 0)
    alive = sel2.astype(I32)
    val = lane
    gval = aff.reshape(rows, LANES)
    for k in range(7):
        s = 1 << k
        move = alive * ((dist >> k) & 1)
        inc = pltpu.roll(move, LANES - s, 1) * (lane < LANES - s).astype(I32) == 1
        val = jnp.where(inc, pltpu.roll(val, LANES - s, 1), val)
        gval = jnp.where(inc, pltpu.roll(gval, LANES - s, 1), gval)
        dist = jnp.where(inc, pltpu.roll(dist, LANES - s, 1), dist)
        alive = alive - move + inc.astype(I32)
    local = val.reshape(shape3)
    local_gate = gval.reshape(shape3)

    top = os_ + ss
    lane_e = lax.broadcasted_iota(I32, (N_EXPERTS, LANES), 1)
    for jc in range(cap // LANES):
        j = (lane_e + jc * LANES).astype(F32)
        acc = jnp.zeros((N_EXPERTS, LANES), I32)
        gacc = jnp.zeros((N_EXPERTS, LANES), F32)
        for b in range(nb):
            inside = (os_[b] <= j) & (j < top[b])
            jl = (j - os_[b]).astype(I32) & (LANES - 1)
            acc = jnp.where(inside, jnp.take_along_axis(local[b], jl, axis=1) + b * LANES, acc)
            gacc = jnp.where(inside, jnp.take_along_axis(local_gate[b], jl, axis=1), gacc)
        out_rows = pl.ds(jc, N_EXPERTS, stride=cap // LANES)
        src_ref[out_rows, :] = acc * HP_ROWS
        dst_ref[out_rows, :] = acc * Y_ROWS
        gate_ref[out_rows, :] = gacc


def _select(aff, cap):
    rows = aff.shape[0]
    out_rows = N_EXPERTS * cap // LANES
    out_spec = pl.BlockSpec((out_rows, LANES), lambda i: (0, 0))
    src, dst, gate = pl.pallas_call(
        functools.partial(_select_kernel, cap=cap),
        out_shape=[jax.ShapeDtypeStruct((out_rows, LANES), I32),
                   jax.ShapeDtypeStruct((out_rows, LANES), I32),
                   jax.ShapeDtypeStruct((out_rows, LANES), F32)],
        grid=(1,),
        in_specs=[pl.BlockSpec((rows, LANES), lambda i: (0, 0))],
        out_specs=[out_spec, out_spec, out_spec],
        compiler_params=_cparams(1),
        name="select",
    )(aff)
    return src.reshape(-1), dst.reshape(-1), gate


GATHER_UNROLL = 8


def _gather_kernel(src_ref, hp_ref, o_ref):
    e = pl.program_id(0)
    cap = o_ref.shape[0] // HP_ROWS

    def gather(g, carry):
        base = e * cap + g * GATHER_UNROLL
        for r in range(GATHER_UNROLL):
            src = pl.multiple_of(src_ref[base + r], HP_ROWS)
            dst = pl.multiple_of((g * GATHER_UNROLL + r) * HP_ROWS, HP_ROWS)
            o_ref[pl.ds(dst, HP_ROWS), :] = hp_ref[pl.ds(src, HP_ROWS), :]
        return carry

    lax.fori_loop(0, cap // GATHER_UNROLL, gather, 0)


def _gather(src, hp):
    cap = src.shape[0] // N_EXPERTS
    return pl.pallas_call(
        _gather_kernel,
        out_shape=jax.ShapeDtypeStruct((N_EXPERTS * cap * HP_ROWS, LANES), U32),
        grid_spec=pltpu.PrefetchScalarGridSpec(
            num_scalar_prefetch=1,
            grid=(N_EXPERTS,),
            in_specs=[pl.BlockSpec(hp.shape, lambda e, src: (0, 0), pipeline_mode=pl.Buffered(1))],
            out_specs=pl.BlockSpec((cap * HP_ROWS, LANES), lambda e, src: (e, 0))),
        compiler_params=_cparams(1),
        name="gather",
    )(src, hp)


FF_CHUNK = 256


def _ffn_kernel(*refs, n_streams):
    e = pl.program_id(0)
    ins = refs[:2 * n_streams]
    wg_ref, wu_ref, wd_ref = refs[2 * n_streams:2 * n_streams + 3]
    outs = refs[2 * n_streams + 3:]
    d_ff = wg_ref.shape[3]
    eye = lax.broadcasted_iota(I32, (LANES, LANES), 0) == lax.broadcasted_iota(I32, (LANES, LANES), 1)
    for i in range(n_streams):
        xp_ref, gate_ref, y_ref = ins[2 * i], ins[2 * i + 1], outs[i]
        cap = xp_ref.shape[0] // HP_ROWS
        halves = [[], []]
        for s in range(HP_ROWS):
            w = xp_ref[pl.ds(s, cap, stride=HP_ROWS), :]
            for half in range(2):
                v = pltpu.unpack_elementwise(w, index=half, packed_dtype=BF16, unpacked_dtype=F32)
                halves[half].append(v.astype(BF16))
        xe = jnp.concatenate(halves[0] + halves[1], axis=1)
        y = None
        for c in range(0, d_ff, FF_CHUNK):
            a = _dot(xe, wg_ref[0, 0, :, c:c + FF_CHUNK].astype(BF16))
            u = _dot(xe, wu_ref[0, 0, :, c:c + FF_CHUNK].astype(BF16))
            yc = _dot((_silu(a) * u).astype(BF16), wd_ref[0, 0, c:c + FF_CHUNK, :].astype(BF16))
            y = yc if y is None else y + yc
        for c in range(cap // LANES):
            g_row = gate_ref[pl.ds(e * (cap // LANES) + c, 1), :]
            g_col = jnp.sum(jnp.where(eye, g_row, 0.0), axis=1, keepdims=True)
            tot = y[c * LANES:(c + 1) * LANES, :] * g_col
            for s in range(Y_ROWS):
                y_ref[0, pl.ds(c * LANES * Y_ROWS + s, LANES, stride=Y_ROWS), :] = tot[:, s * LANES:(s + 1) * LANES]


def _ffn(streams, layer, w_gate, w_up, w_down):
    caps = [xp.shape[0] // (N_EXPERTS * HP_ROWS) for xp, _ in streams]
    in_specs, args = [], []
    for (xp, gate), cap in zip(streams, caps):
        in_specs += [pl.BlockSpec((cap * HP_ROWS, LANES), lambda e: (e, 0)),
                     pl.BlockSpec(gate.shape, lambda e: (0, 0))]
        args += [xp, gate]
    w_spec = pl.BlockSpec((1, 1) + w_gate.shape[2:], lambda e: (layer, e, 0, 0))
    return pl.pallas_call(
        functools.partial(_ffn_kernel, n_streams=len(streams)),
        out_shape=[jax.ShapeDtypeStruct((N_EXPERTS, cap * Y_ROWS, LANES), F32) for cap in caps],
        grid=(N_EXPERTS,),
        in_specs=in_specs + [w_spec, w_spec, w_spec],
        out_specs=[pl.BlockSpec((1, cap * Y_ROWS, LANES), lambda e: (e, 0, 0)) for cap in caps],
        compiler_params=_cparams(1),
        name="ffn",
    )(*args, w_gate, w_up, w_down)


SCATTER_UNROLL = 16


def _combine_kernel(dst_ref, y_ref, x1_ref, mod_ref, o_ref, acc_ref):
    s = pl.program_id(0)
    tf = o_ref.shape[0]
    cap = y_ref.shape[1] // Y_ROWS

    @pl.when(s == 0)
    def _():
        acc_ref[...] = jnp.zeros_like(acc_ref)

    @pl.when(s < N_EXPERTS)
    def _():
        def scatter(g, carry):
            base = s * cap + g * SCATTER_UNROLL
            rows, sums = [], []
            for r in range(SCATTER_UNROLL):
                dst = pl.ds(pl.multiple_of(dst_ref[base + r], Y_ROWS), Y_ROWS)
                src = pl.ds(pl.multiple_of((g * SCATTER_UNROLL + r) * Y_ROWS, Y_ROWS), Y_ROWS)
                rows.append(dst)
                sums.append(acc_ref[dst, :] + y_ref[0, src, :])
            for dst, v in zip(rows, sums):
                acc_ref[dst, :] = v
            return carry

        lax.fori_loop(0, cap // SCATTER_UNROLL, scatter, 0)

    @pl.when(s >= N_EXPERTS)
    def _():
        g2 = mod_ref[0, 0][5:6]
        tile = acc_ref.at[pl.ds(pl.multiple_of((s - N_EXPERTS) * tf * Y_ROWS, tf * Y_ROWS), tf * Y_ROWS), :]
        for c in range(Y_ROWS):
            sl = slice(c * LANES, (c + 1) * LANES)
            o_ref[:, sl] = x1_ref[:, sl] + g2[:, sl] * tile[pl.ds(c, tf, stride=Y_ROWS), :]


def _combine(dst, y, x1, mods, layer, row_of_tile, tf):
    n_tok = x1.shape[0]
    cap = dst.shape[0] // N_EXPERTS
    tile_of = lambda s: jnp.maximum(s - N_EXPERTS, 0)
    return pl.pallas_call(
        _combine_kernel,
        out_shape=jax.ShapeDtypeStruct((n_tok, D_MODEL), F32),
        grid_spec=pltpu.PrefetchScalarGridSpec(
            num_scalar_prefetch=1,
            grid=(N_EXPERTS + n_tok // tf,),
            in_specs=[pl.BlockSpec((1, cap * Y_ROWS, LANES), lambda s, *_: (jnp.minimum(s, N_EXPERTS - 1), 0, 0)),
                      pl.BlockSpec((tf, D_MODEL), lambda s, *_: (tile_of(s), 0)),
                      pl.BlockSpec((1, 1, N_MOD, D_MODEL), lambda s, *_: (layer, row_of_tile(tile_of(s)), 0, 0))],
            out_specs=pl.BlockSpec((tf, D_MODEL), lambda s, *_: (tile_of(s), 0)),
            scratch_shapes=[pltpu.VMEM((n_tok * Y_ROWS, LANES), F32)]),
        compiler_params=_cparams(1),
        name="combine",
    )(dst, y, x1, mods)


TOKEN_TILE = 1024
COMBINE_TILE = 512
Q_TILE = 256


def _moe(fronts, rows_of_tile, mods, layer, w_gate, w_up, w_down):
    routed, dsts = [], []
    for x1, hp, aff in fronts:
        cap = CAPACITY_FACTOR * x1.shape[0] // N_EXPERTS
        src, dst, gate = _select(aff, cap)
        routed.append((_gather(src, hp), gate))
        dsts.append(dst)
    ys = _ffn(routed, layer, w_gate, w_up, w_down)
    return [_combine(dst, y, x1, mods, layer, row_of_tile, COMBINE_TILE)
            for dst, y, (x1, _, _), row_of_tile in zip(dsts, ys, fronts, rows_of_tile)]


def kernel(x_prompt, x_sample, cache_k, cache_v, c, c_ctx, norm1, norm2, w_ada, b_ada, w_qkv, q_norm, k_norm,
           w_o, w_pool, pool_scale, w_router, w_e_gate, w_e_up, w_e_down):
    batch, seq, _ = x_prompt.shape
    dec_batch, dec_seq, _ = x_sample.shape
    depth = w_ada.shape[0]
    kvw = N_KV * HEAD_DIM

    cvec = jnp.zeros((SUBLANES, D_MODEL), F32).at[0].set(c_ctx).at[1:1 + dec_batch].set(c)
    mods = _ada(cvec, w_ada, b_ada)
    ctx = x_prompt.reshape(batch * seq, D_MODEL)
    lat = x_sample.reshape(dec_batch * dec_seq, D_MODEL)
    ctx_row = lambda i: 0
    lat_row = lambda tile: (lambda i: 1 + (i * tile) // dec_seq)
    ctx_tile = min(TOKEN_TILE, seq)
    rope_tabs = _rope_tables(dec_seq)
    new_k = new_v = None

    for layer in range(depth):
        j = layer // 2
        n1 = norm1[layer][None]
        n2 = norm2[layer][None]
        wr = jnp.pad(w_router[layer], ((0, 0), (0, LANES - N_EXPERTS))).astype(BF16)
        if layer % 2 == 0:
            qn = q_norm[j][None]
            kn = k_norm[j][None]
            qc, kc, vc, new_k, new_v = _qkv(ctx, mods, layer, ctx_row, n1, w_qkv[j], qn, kn, None, True, TOKEN_TILE)
            oc = _attn_ctx(qc, kc, vc, seq)
            ql, kl, vl = _qkv(lat, mods, layer, lat_row(TOKEN_TILE), n1, w_qkv[j], qn, kn, rope_tabs, False,
                              TOKEN_TILE)
            past_k = cache_k[:, j].reshape(dec_batch, -1, kvw).astype(BF16)
            past_v = cache_v[:, j].reshape(dec_batch, -1, kvw).astype(BF16)
            ol = _attn_lat(ql, kl, vl, past_k, past_v, dec_seq, Q_TILE)
            ctx1 = _post(oc, ctx, mods, layer, ctx_row, n2, w_o[j], wr, TOKEN_TILE)
            lat1 = _post(ol, lat, mods, layer, lat_row(TOKEN_TILE), n2, w_o[j], wr, TOKEN_TILE)
        else:
            ps = pool_scale[j][None]
            ctx1 = _pool(ctx, mods, layer, ctx_row, n1, n2, w_pool[j], ps, wr, seq, ctx_tile)
            lat1 = _pool(lat, mods, layer, lat_row(TOKEN_TILE), n1, n2, w_pool[j], ps, wr, dec_seq, TOKEN_TILE)
        ctx, lat = _moe([ctx1, lat1], [ctx_row, lat_row(COMBINE_TILE)], mods, layer, w_e_gate, w_e_up, w_e_down)

    new_cache_k = new_k.reshape(batch, 1, seq, N_KV, HEAD_DIM)
    new_cache_v = new_v.reshape(batch, 1, seq, N_KV, HEAD_DIM)
    return (ctx.reshape(batch, seq, D_MODEL), lat.reshape(dec_batch, dec_seq, D_MODEL), new_cache_k, new_cache_v)
```

```python
import functools

import jax
import jax.numpy as jnp
import numpy as np
from jax import lax
from jax.experimental import pallas as pl
from jax.experimental.pallas import tpu as pltpu

F32 = jnp.float32
BF16 = jnp.bfloat16
I32 = jnp.int32
U32 = jnp.uint32

D_MODEL = 1024
HEAD_DIM = 128
N_HEADS = 8
N_KV = 2
Q_PER_KV = N_HEADS // N_KV
QKV_DIM = (N_HEADS + 2 * N_KV) * HEAD_DIM
GRID_W = 64
ROPE_THETA = 10000.0
POOL_WINDOWS = (2, 4, 8, 16)
POOL_GROUP = D_MODEL // len(POOL_WINDOWS)
POOL_HALO = 8
N_EXPERTS = 16
CAPACITY_FACTOR = 2
N_MOD = 6
EPS = 1e-6
LOG2_E = 1.4426950408889634

LANES = 128
SUBLANES = 8
HALF = D_MODEL // 2
HP_ROWS = HALF // LANES
Y_ROWS = D_MODEL // LANES
VMEM_LIMIT = 56 * 1024 * 1024


def _cparams(n_axes, vmem=VMEM_LIMIT):
    return pltpu.CompilerParams(dimension_semantics=("arbitrary",) * n_axes, vmem_limit_bytes=vmem)


def _silu(x):
    return x / (1.0 + jnp.exp(-x))


def _rms(x, gain):
    return x * lax.rsqrt(jnp.mean(x * x, axis=-1, keepdims=True) + EPS) * gain


def _norm_mod(x, gain, shift, scale):
    return _rms(x, gain * (1.0 + scale)) + shift


def _dot(a, b):
    return jnp.dot(a, b, preferred_element_type=F32)


def _dot_nt(a, b):
    return lax.dot_general(a, b, (((1,), (1,)), ((), ())), preferred_element_type=F32)


def _ada_kernel(c_ref, w_ref, b_ref, o_ref):
    s = _silu(c_ref[...]).astype(BF16)
    o_ref[0] = _dot(s, w_ref[0].astype(BF16)) + b_ref[0]


def _ada(cvec, w_ada, b_ada):
    depth = w_ada.shape[0]
    out = pl.pallas_call(
        _ada_kernel,
        out_shape=jax.ShapeDtypeStruct((depth, SUBLANES, N_MOD * D_MODEL), F32),
        grid=(depth, N_MOD),
        in_specs=[
            pl.BlockSpec((SUBLANES, D_MODEL), lambda i, j: (0, 0)),
            pl.BlockSpec((1, D_MODEL, D_MODEL), lambda i, j: (i, 0, j)),
            pl.BlockSpec((1, 1, D_MODEL), lambda i, j: (i, 0, j)),
        ],
        out_specs=pl.BlockSpec((1, SUBLANES, D_MODEL), lambda i, j: (i, 0, j)),
        compiler_params=_cparams(2),
        name="ada",
    )(cvec, w_ada, b_ada.reshape(depth, 1, N_MOD * D_MODEL))
    return out.reshape(depth, SUBLANES, N_MOD, D_MODEL)


def _mod_spec(layer, row_of_tile):
    return pl.BlockSpec((1, 1, N_MOD, D_MODEL), lambda i, *_: (layer, row_of_tile(i), 0, 0))


def _qkv_kernel(*refs, rope, cache_out):
    x_ref, mod_ref, n1_ref, w_ref, qn_ref, kn_ref = refs[:6]
    refs = refs[6:]
    if rope:
        cos_ref, sin_ref = refs[:2]
        refs = refs[2:]
    q_ref, k_ref, v_ref = refs[:3]
    refs = refs[3:]
    if cache_out:
        kc_ref, vc_ref = refs[:2]
        refs = refs[2:]
    (wb_ref,) = refs

    qk_w = (N_HEADS + N_KV) * HEAD_DIM
    quarter = HEAD_DIM // 4

    def partner(a):
        width = a.shape[1]
        first = (lax.broadcasted_iota(I32, a.shape, 1) & quarter) == 0
        return jnp.where(first, pltpu.roll(a, width - quarter, 1), pltpu.roll(a, quarter, 1))

    @pl.when(pl.program_id(0) == 0)
    def _():
        w = w_ref[...]
        wb_ref[:, 0:QKV_DIM] = w.astype(BF16)
        if rope:
            wb_ref[:, QKV_DIM:] = partner(w[:, 0:qk_w]).astype(BF16)

    m = mod_ref[0, 0]
    h = _norm_mod(x_ref[...], n1_ref[...], m[0:1], m[1:2])
    qkv = _dot(h.astype(BF16), wb_ref[...])
    scale = HEAD_DIM ** -0.5 * LOG2_E
    if rope:
        gains = {True: qn_ref[...], False: kn_ref[...]}
        cos_g = {key: cos_ref[...] * g for key, g in gains.items()}
        sin_g = {key: sin_ref[...] * partner(g) for key, g in gains.items()}
    for hh in range(N_HEADS + N_KV):
        sl = slice(hh * HEAD_DIM, (hh + 1) * HEAD_DIM)
        is_q = hh < N_HEADS
        if rope:
            raw = qkv[:, sl]
            norm = lax.rsqrt(jnp.mean(raw * raw, axis=-1, keepdims=True) + EPS)
            xh = (raw * cos_g[is_q] + qkv[:, QKV_DIM + hh * HEAD_DIM:QKV_DIM + (hh + 1) * HEAD_DIM] * sin_g[is_q]) * norm
        else:
            xh = _rms(qkv[:, sl], qn_ref[...] if is_q else kn_ref[...])
        if cache_out and hh >= N_HEADS:
            kc_ref[:, (hh - N_HEADS) * HEAD_DIM:(hh - N_HEADS + 1) * HEAD_DIM] = xh
        if hh < N_HEADS:
            q_ref[:, sl] = (xh * scale).astype(BF16)
        else:
            k_ref[:, (hh - N_HEADS) * HEAD_DIM:(hh - N_HEADS + 1) * HEAD_DIM] = xh.astype(BF16)
    v = qkv[:, qk_w:QKV_DIM]
    v_ref[...] = v.astype(BF16)
    if cache_out:
        vc_ref[...] = v


def _qkv(x, mods, layer, row_of_tile, n1, w_qkv, qn, kn, rope_tabs, cache_out, tm):
    n_tok = x.shape[0]
    kvw = N_KV * HEAD_DIM
    rope = rope_tabs is not None
    in_specs = [
        pl.BlockSpec((tm, D_MODEL), lambda i: (i, 0)),
        _mod_spec(layer, row_of_tile),
        pl.BlockSpec((1, D_MODEL), lambda i: (0, 0)),
        pl.BlockSpec((D_MODEL, QKV_DIM), lambda i: (0, 0)),
        pl.BlockSpec((1, HEAD_DIM), lambda i: (0, 0)),
        pl.BlockSpec((1, HEAD_DIM), lambda i: (0, 0)),
    ]
    args = [x, mods, n1, w_qkv, qn, kn]
    if rope:
        seq_tiles = rope_tabs[0].shape[0] // tm
        in_specs += [pl.BlockSpec((tm, HEAD_DIM), lambda i: (i % seq_tiles, 0))] * 2
        args += list(rope_tabs)
    out_shape = [jax.ShapeDtypeStruct((n_tok, D_MODEL), BF16),
                 jax.ShapeDtypeStruct((n_tok, kvw), BF16),
                 jax.ShapeDtypeStruct((n_tok, kvw), BF16)]
    out_specs = [pl.BlockSpec((tm, D_MODEL), lambda i: (i, 0)),
                 pl.BlockSpec((tm, kvw), lambda i: (i, 0)),
                 pl.BlockSpec((tm, kvw), lambda i: (i, 0))]
    if cache_out:
        out_shape += [jax.ShapeDtypeStruct((n_tok, kvw), F32)] * 2
        out_specs += [pl.BlockSpec((tm, kvw), lambda i: (i, 0))] * 2
    return pl.pallas_call(
        functools.partial(_qkv_kernel, rope=rope, cache_out=cache_out),
        out_shape=out_shape,
        grid=(n_tok // tm,),
        in_specs=in_specs,
        out_specs=out_specs,
        scratch_shapes=[pltpu.VMEM((D_MODEL, QKV_DIM + ((N_HEADS + N_KV) * HEAD_DIM if rope else 0)), BF16)],
        compiler_params=_cparams(1),
        name="qkv_rope" if rope else "qkv",
    )(*args)


def _rope_tables(seq_len):
    half = HEAD_DIM // 2
    n = half // 2
    inv_freq = ROPE_THETA ** (-np.arange(n, dtype=np.float64) / n)
    rows = seq_len // GRID_W
    row = np.repeat(np.arange(rows), GRID_W).astype(np.float64)
    col = np.tile(np.arange(GRID_W), rows).astype(np.float64)
    ang_r = row[:, None] * inv_freq[None, :]
    ang_c = col[:, None] * inv_freq[None, :]
    cos = np.concatenate([np.cos(ang_r)] * 2 + [np.cos(ang_c)] * 2, axis=-1)
    sin = np.concatenate([-np.sin(ang_r), np.sin(ang_r), -np.sin(ang_c), np.sin(ang_c)], axis=-1)
    return jnp.asarray(cos, F32), jnp.asarray(sin, F32)


def _with_ones(v):
    return jnp.concatenate([v, jnp.ones_like(v)], axis=1)


def _gqa_attention(q, segments):
    rows = q.shape[0]
    qs = jnp.concatenate([q[:, h * HEAD_DIM:(h + 1) * HEAD_DIM] for h in range(Q_PER_KV)], axis=0)
    m = jnp.full((rows * Q_PER_KV, 1), -1e30, F32)
    acc = jnp.zeros((rows * Q_PER_KV, 2 * HEAD_DIM), F32)
    for k, v in segments:
        s = _dot_nt(qs, k)
        m_new = jnp.maximum(m, jnp.max(s, axis=-1, keepdims=True))
        acc = acc * jnp.exp2(m - m_new) + _dot(jnp.exp2(s - m_new).astype(BF16), v)
        m = m_new
    o = (acc[:, :HEAD_DIM] / acc[:, HEAD_DIM:]).astype(BF16)
    return jnp.concatenate([o[h * rows:(h + 1) * rows] for h in range(Q_PER_KV)], axis=1)


def _attn_ctx_kernel(q_ref, k_ref, v_ref, o_ref):
    gw = Q_PER_KV * HEAD_DIM
    for g in range(N_KV):
        kg = k_ref[:, g * HEAD_DIM:(g + 1) * HEAD_DIM]
        vg = _with_ones(v_ref[:, g * HEAD_DIM:(g + 1) * HEAD_DIM])
        o_ref[:, g * gw:(g + 1) * gw] = _gqa_attention(q_ref[:, g * gw:(g + 1) * gw], [(kg, vg)])


def _attn_ctx(q, k, v, seq):
    n_tok = q.shape[0]
    kvw = N_KV * HEAD_DIM
    return pl.pallas_call(
        _attn_ctx_kernel,
        out_shape=jax.ShapeDtypeStruct((n_tok, D_MODEL), BF16),
        grid=(n_tok // seq,),
        in_specs=[pl.BlockSpec((seq, D_MODEL), lambda b: (b, 0)),
                  pl.BlockSpec((seq, kvw), lambda b: (b, 0)),
                  pl.BlockSpec((seq, kvw), lambda b: (b, 0))],
        out_specs=pl.BlockSpec((seq, D_MODEL), lambda b: (b, 0)),
        compiler_params=_cparams(1),
        name="attn_ctx",
    )(q, k, v)


KEY_CHUNK = 256


def _attn_lat_kernel(q_ref, k_ref, v_ref, kc_ref, vc_ref, o_ref):
    segments = [(k_ref[c:c + KEY_CHUNK], _with_ones(v_ref[c:c + KEY_CHUNK]))
                for c in range(0, k_ref.shape[0], KEY_CHUNK)]
    segments.append((kc_ref[0], _with_ones(vc_ref[0])))
    o_ref[...] = _gqa_attention(q_ref[...], segments)


def _attn_lat(q, k, v, kc, vc, seq, tq):
    n_tok = q.shape[0]
    batch = n_tok // seq
    past = kc.shape[1]
    qt = seq // tq
    gw = Q_PER_KV * HEAD_DIM
    return pl.pallas_call(
        _attn_lat_kernel,
        out_shape=jax.ShapeDtypeStruct((n_tok, D_MODEL), BF16),
        grid=(batch, N_KV, qt),
        in_specs=[pl.BlockSpec((tq, gw), lambda b, g, i: (b * qt + i, g)),
                  pl.BlockSpec((seq, HEAD_DIM), lambda b, g, i: (b, g)),
                  pl.BlockSpec((seq, HEAD_DIM), lambda b, g, i: (b, g)),
                  pl.BlockSpec((1, past, HEAD_DIM), lambda b, g, i: (b, 0, g)),
                  pl.BlockSpec((1, past, HEAD_DIM), lambda b, g, i: (b, 0, g))],
        out_specs=pl.BlockSpec((tq, gw), lambda b, g, i: (b * qt + i, g)),
        compiler_params=_cparams(3),
        name="attn_lat",
    )(q, k, v, kc, vc)


def _router_softmax(hb, wr):
    logits = _dot(hb, wr)
    lane = lax.broadcasted_iota(I32, logits.shape, 1)
    logits = jnp.where(lane < N_EXPERTS, logits, -1e30)
    ex = jnp.exp(logits - jnp.max(logits, axis=-1, keepdims=True))
    return ex / jnp.sum(ex, axis=-1, keepdims=True)


def _moe_front(x1, m, n2, wr, x1_ref, hp_ref, aff_ref):
    tm = x1.shape[0]
    x1_ref[...] = x1
    h2 = _norm_mod(x1, n2, m[3:4], m[4:5])
    aff = _router_softmax(h2.astype(BF16), wr)
    packed = pltpu.pack_elementwise([h2[:, :HALF], h2[:, HALF:]], packed_dtype=BF16)
    for s in range(HP_ROWS):
        hp_ref[pl.ds(s, tm, stride=HP_ROWS), :] = packed[:, s * LANES:(s + 1) * LANES]
    aff_t = aff.T
    for b in range(tm // LANES):
        aff_ref[b * N_EXPERTS:(b + 1) * N_EXPERTS, :] = aff_t[0:N_EXPERTS, b * LANES:(b + 1) * LANES]


def _front_out(n_tok, tm):
    shapes = [jax.ShapeDtypeStruct((n_tok, D_MODEL), F32),
              jax.ShapeDtypeStruct((n_tok * HP_ROWS, LANES), U32),
              jax.ShapeDtypeStruct((n_tok // LANES * N_EXPERTS, LANES), F32)]
    specs = [pl.BlockSpec((tm, D_MODEL), lambda i: (i, 0)),
             pl.BlockSpec((tm * HP_ROWS, LANES), lambda i: (i, 0)),
             pl.BlockSpec((tm // LANES * N_EXPERTS, LANES), lambda i: (i, 0))]
    return shapes, specs


def _post_kernel(o_ref, x_ref, mod_ref, n2_ref, wo_ref, wr_ref, x1_ref, hp_ref, aff_ref, wb_ref):
    @pl.when(pl.program_id(0) == 0)
    def _():
        wb_ref[...] = wo_ref[...].astype(BF16)

    m = mod_ref[0, 0]
    x1 = x_ref[...] + m[2:3] * _dot(o_ref[...], wb_ref[...])
    _moe_front(x1, m, n2_ref[...], wr_ref[...], x1_ref, hp_ref, aff_ref)


def _post(o, x, mods, layer, row_of_tile, n2, w_o, wr, tm):
    n_tok = x.shape[0]
    shapes, specs = _front_out(n_tok, tm)
    return pl.pallas_call(
        _post_kernel,
        out_shape=shapes,
        grid=(n_tok // tm,),
        in_specs=[pl.BlockSpec((tm, D_MODEL), lambda i: (i, 0)),
                  pl.BlockSpec((tm, D_MODEL), lambda i: (i, 0)),
                  _mod_spec(layer, row_of_tile),
                  pl.BlockSpec((1, D_MODEL), lambda i: (0, 0)),
                  pl.BlockSpec((D_MODEL, D_MODEL), lambda i: (0, 0)),
                  pl.BlockSpec((D_MODEL, LANES), lambda i: (0, 0))],
        out_specs=specs,
        scratch_shapes=[pltpu.VMEM((D_MODEL, D_MODEL), BF16)],
        compiler_params=_cparams(1),
        name="post",
    )(o, x, mods, n2, w_o, wr)


def _pool_kernel(x_ref, xp_ref, xn_ref, mod_ref, n1_ref, n2_ref, wp_ref, ps_ref, wr_ref,
                 x1_ref, hp_ref, aff_ref, *, seq):
    tm = x_ref.shape[0]
    i = pl.program_id(0)
    m = mod_ref[0, 0]
    x = x_ref[...]

    def norm_mod(v):
        return _norm_mod(v, n1_ref[...], m[0:1], m[1:2])

    prev_ok = ((i * tm) % seq != 0).astype(F32)
    next_ok = (((i + 1) * tm) % seq != 0).astype(F32)
    h = norm_mod(x)
    hz = jnp.concatenate([norm_mod(xp_ref[...]) * prev_ok, h, norm_mod(xn_ref[...]) * next_ok], axis=0)
    rows = tm + 2 * POOL_HALO
    t = (i * tm) % seq + lax.broadcasted_iota(I32, (tm, 1), 0)
    ys = []
    for g, w in enumerate(POOL_WINDOWS):
        sl = slice(g * POOL_GROUP, (g + 1) * POOL_GROUP)
        f = hz[:, sl]
        step = 1
        while step < w:
            f = f + pltpu.roll(f, rows - step, 0)
            step *= 2
        win = pltpu.roll(f, w // 2, 0)[POOL_HALO:POOL_HALO + tm]
        cnt = (jnp.minimum(t + w // 2, seq) - jnp.maximum(t - w // 2, 0)).astype(F32)
        d = (win / cnt - h[:, sl]).astype(BF16)
        ys.append(_dot(d, wp_ref[g].astype(BF16)))
    x1 = x + (m[2:3] * ps_ref[...]) * jnp.concatenate(ys, axis=-1)
    _moe_front(x1, m, n2_ref[...], wr_ref[...], x1_ref, hp_ref, aff_ref)


def _pool(x, mods, layer, row_of_tile, n1, n2, w_pool, pool_scale, wr, seq, tm):
    n_tok = x.shape[0]
    hb = tm // POOL_HALO
    last = n_tok // POOL_HALO - 1
    shapes, specs = _front_out(n_tok, tm)
    return pl.pallas_call(
        functools.partial(_pool_kernel, seq=seq),
        out_shape=shapes,
        grid=(n_tok // tm,),
        in_specs=[pl.BlockSpec((tm, D_MODEL), lambda i: (i, 0)),
                  pl.BlockSpec((POOL_HALO, D_MODEL), lambda i: (jnp.maximum(i * hb - 1, 0), 0)),
                  pl.BlockSpec((POOL_HALO, D_MODEL), lambda i: (jnp.minimum((i + 1) * hb, last), 0)),
                  _mod_spec(layer, row_of_tile),
                  pl.BlockSpec((1, D_MODEL), lambda i: (0, 0)),
                  pl.BlockSpec((1, D_MODEL), lambda i: (0, 0)),
                  pl.BlockSpec(w_pool.shape, lambda i: (0, 0, 0)),
                  pl.BlockSpec((1, D_MODEL), lambda i: (0, 0)),
                  pl.BlockSpec((D_MODEL, LANES), lambda i: (0, 0))],
        out_specs=specs,
        compiler_params=_cparams(1),
        name="pool",
    )(x, x, x, mods, n1, n2, w_pool, pool_scale, wr)


def _select_kernel(aff_ref, src_ref, dst_ref, gate_ref, *, cap):
    nb = aff_ref.shape[0] // N_EXPERTS
    rows = nb * N_EXPERTS
    shape3 = (nb, N_EXPERTS, LANES)
    aff = aff_ref[...].reshape(shape3)

    def count(mask):
        return jnp.sum(jnp.sum(mask.astype(F32), axis=0), axis=-1, keepdims=True)

    def search(i, t):
        cand = t | (jnp.int32(1) << (29 - i))
        return jnp.where(count(aff >= pltpu.bitcast(cand, F32)[None]) >= cap, cand, t)

    thr = pltpu.bitcast(lax.fori_loop(0, 30, search, jnp.zeros((N_EXPERTS, LANES), I32)), F32)
    gt = aff > thr[None]
    eq = aff == thr[None]
    need = cap - count(gt)

    kk = lax.broadcasted_iota(I32, (LANES, LANES), 0)
    nn = lax.broadcasted_iota(I32, (LANES, LANES), 1)
    upper = (kk <= nn).astype(BF16)
    ones = jnp.ones((LANES, LANES), BF16)

    def prefix(mask):
        m2 = mask.astype(F32).astype(BF16).reshape(rows, LANES)
        p = _dot(m2, upper).reshape(shape3)
        s = _dot(m2, ones).reshape(shape3)
        offs = []
        run = jnp.zeros((N_EXPERTS, LANES), F32)
        for b in range(nb):
            offs.append(run)
            run = run + s[b]
        return p, jnp.stack(offs, axis=0), s

    pe, oe, _ = prefix(eq)
    sel = gt | (eq & ((pe + oe) <= need[None]))
    ps, os_, ss = prefix(sel)

    lane = lax.broadcasted_iota(I32, (rows, LANES), 1)
    sel2 = sel.reshape(rows, LANES)
    dist = jnp.where(sel2, lane - (ps.reshape(rows, LANES).astype(I32) - 1), 0)
    alive = sel2.astype(I32)
    val = lane
    gval = aff.reshape(rows, LANES)
    for k in range(7):
        s = 1 << k
        move = alive * ((dist >> k) & 1)
        inc = pltpu.roll(move, LANES - s, 1) * (lane < LANES - s).astype(I32) == 1
        val = jnp.where(inc, pltpu.roll(val, LANES - s, 1), val)
        gval = jnp.where(inc, pltpu.roll(gval, LANES - s, 1), gval)
        dist = jnp.where(inc, pltpu.roll(dist, LANES - s, 1), dist)
        alive = alive - move + inc.astype(I32)
    local = val.reshape(shape3)
    local_gate = gval.reshape(shape3)

    top = os_ + ss
    lane_e = lax.broadcasted_iota(I32, (N_EXPERTS, LANES), 1)
    for jc in range(cap // LANES):
        j = (lane_e + jc * LANES).astype(F32)
        acc = jnp.zeros((N_EXPERTS, LANES), I32)
        gacc = jnp.zeros((N_EXPERTS, LANES), F32)
        for b in range(nb):
            inside = (os_[b] <= j) & (j < top[b])
            jl = (j - os_[b]).astype(I32) & (LANES - 1)
            acc = jnp.where(inside, jnp.take_along_axis(local[b], jl, axis=1) + b * LANES, acc)
            gacc = jnp.where(inside, jnp.take_along_axis(local_gate[b], jl, axis=1), gacc)
        out_rows = pl.ds(jc, N_EXPERTS, stride=cap // LANES)
        src_ref[out_rows, :] = acc * HP_ROWS
        dst_ref[out_rows, :] = acc * Y_ROWS
        gate_ref[out_rows, :] = gacc


def _select(aff, cap):
    rows = aff.shape[0]
    out_rows = N_EXPERTS * cap // LANES
    out_spec = pl.BlockSpec((out_rows, LANES), lambda i: (0, 0))
    src, dst, gate = pl.pallas_call(
        functools.partial(_select_kernel, cap=cap),
        out_shape=[jax.ShapeDtypeStruct((out_rows, LANES), I32),
                   jax.ShapeDtypeStruct((out_rows, LANES), I32),
                   jax.ShapeDtypeStruct((out_rows, LANES), F32)],
        grid=(1,),
        in_specs=[pl.BlockSpec((rows, LANES), lambda i: (0, 0))],
        out_specs=[out_spec, out_spec, out_spec],
        compiler_params=_cparams(1),
        name="select",
    )(aff)
    return src.reshape(-1), dst.reshape(-1), gate


GATHER_UNROLL = 8


def _gather_kernel(src_ref, hp_ref, o_ref):
    e = pl.program_id(0)
    cap = o_ref.shape[0] // HP_ROWS

    def gather(g, carry):
        base = e * cap + g * GATHER_UNROLL
        for r in range(GATHER_UNROLL):
            src = pl.multiple_of(src_ref[base + r], HP_ROWS)
            dst = pl.multiple_of((g * GATHER_UNROLL + r) * HP_ROWS, HP_ROWS)
            o_ref[pl.ds(dst, HP_ROWS), :] = hp_ref[pl.ds(src, HP_ROWS), :]
        return carry

    lax.fori_loop(0, cap // GATHER_UNROLL, gather, 0)


def _gather(src, hp):
    cap = src.shape[0] // N_EXPERTS
    return pl.pallas_call(
        _gather_kernel,
        out_shape=jax.ShapeDtypeStruct((N_EXPERTS * cap * HP_ROWS, LANES), U32),
        grid_spec=pltpu.PrefetchScalarGridSpec(
            num_scalar_prefetch=1,
            grid=(N_EXPERTS,),
            in_specs=[pl.BlockSpec(hp.shape, lambda e, src: (0, 0), pipeline_mode=pl.Buffered(1))],
            out_specs=pl.BlockSpec((cap * HP_ROWS, LANES), lambda e, src: (e, 0))),
        compiler_params=_cparams(1),
        name="gather",
    )(src, hp)


FF_CHUNK = 256


def _ffn_kernel(*refs, n_streams):
    e = pl.program_id(0)
    ins = refs[:2 * n_streams]
    wg_ref, wu_ref, wd_ref = refs[2 * n_streams:2 * n_streams + 3]
    outs = refs[2 * n_streams + 3:]
    d_ff = wg_ref.shape[3]
    eye = lax.broadcasted_iota(I32, (LANES, LANES), 0) == lax.broadcasted_iota(I32, (LANES, LANES), 1)
    for i in range(n_streams):
        xp_ref, gate_ref, y_ref = ins[2 * i], ins[2 * i + 1], outs[i]
        cap = xp_ref.shape[0] // HP_ROWS
        halves = [[], []]
        for s in range(HP_ROWS):
            w = xp_ref[pl.ds(s, cap, stride=HP_ROWS), :]
            for half in range(2):
                v = pltpu.unpack_elementwise(w, index=half, packed_dtype=BF16, unpacked_dtype=F32)
                halves[half].append(v.astype(BF16))
        xe = jnp.concatenate(halves[0] + halves[1], axis=1)
        y = None
        for c in range(0, d_ff, FF_CHUNK):
            a = _dot(xe, wg_ref[0, 0, :, c:c + FF_CHUNK].astype(BF16))
            u = _dot(xe, wu_ref[0, 0, :, c:c + FF_CHUNK].astype(BF16))
            yc = _dot((_silu(a) * u).astype(BF16), wd_ref[0, 0, c:c + FF_CHUNK, :].astype(BF16))
            y = yc if y is None else y + yc
        for c in range(cap // LANES):
            g_row = gate_ref[pl.ds(e * (cap // LANES) + c, 1), :]
            g_col = jnp.sum(jnp.where(eye, g_row, 0.0), axis=1, keepdims=True)
            tot = y[c * LANES:(c + 1) * LANES, :] * g_col
            for s in range(Y_ROWS):
                y_ref[0, pl.ds(c * LANES * Y_ROWS + s, LANES, stride=Y_ROWS), :] = tot[:, s * LANES:(s + 1) * LANES]


def _ffn(streams, layer, w_gate, w_up, w_down):
    caps = [xp.shape[0] // (N_EXPERTS * HP_ROWS) for xp, _ in streams]
    in_specs, args = [], []
    for (xp, gate), cap in zip(streams, caps):
        in_specs += [pl.BlockSpec((cap * HP_ROWS, LANES), lambda e: (e, 0)),
                     pl.BlockSpec(gate.shape, lambda e: (0, 0))]
        args += [xp, gate]
    w_spec = pl.BlockSpec((1, 1) + w_gate.shape[2:], lambda e: (layer, e, 0, 0))
    return pl.pallas_call(
        functools.partial(_ffn_kernel, n_streams=len(streams)),
        out_shape=[jax.ShapeDtypeStruct((N_EXPERTS, cap * Y_ROWS, LANES), F32) for cap in caps],
        grid=(N_EXPERTS,),
        in_specs=in_specs + [w_spec, w_spec, w_spec],
        out_specs=[pl.BlockSpec((1, cap * Y_ROWS, LANES), lambda e: (e, 0, 0)) for cap in caps],
        compiler_params=_cparams(1),
        name="ffn",
    )(*args, w_gate, w_up, w_down)


SCATTER_UNROLL = 16


def _combine_kernel(dst_ref, y_ref, x1_ref, mod_ref, o_ref, acc_ref):
    s = pl.program_id(0)
    tf = o_ref.shape[0]
    cap = y_ref.shape[1] // Y_ROWS

    @pl.when(s == 0)
    def _():
        acc_ref[...] = jnp.zeros_like(acc_ref)

    @pl.when(s < N_EXPERTS)
    def _():
        def scatter(g, carry):
            base = s * cap + g * SCATTER_UNROLL
            rows, sums = [], []
            for r in range(SCATTER_UNROLL):
                dst = pl.ds(pl.multiple_of(dst_ref[base + r], Y_ROWS), Y_ROWS)
                src = pl.ds(pl.multiple_of((g * SCATTER_UNROLL + r) * Y_ROWS, Y_ROWS), Y_ROWS)
                rows.append(dst)
                sums.append(acc_ref[dst, :] + y_ref[0, src, :])
            for dst, v in zip(rows, sums):
                acc_ref[dst, :] = v
            return carry

        lax.fori_loop(0, cap // SCATTER_UNROLL, scatter, 0)

    @pl.when(s >= N_EXPERTS)
    def _():
        g2 = mod_ref[0, 0][5:6]
        tile = acc_ref.at[pl.ds(pl.multiple_of((s - N_EXPERTS) * tf * Y_ROWS, tf * Y_ROWS), tf * Y_ROWS), :]
        for c in range(Y_ROWS):
            sl = slice(c * LANES, (c + 1) * LANES)
            o_ref[:, sl] = x1_ref[:, sl] + g2[:, sl] * tile[pl.ds(c, tf, stride=Y_ROWS), :]


def _combine(dst, y, x1, mods, layer, row_of_tile, tf):
    n_tok = x1.shape[0]
    cap = dst.shape[0] // N_EXPERTS
    tile_of = lambda s: jnp.maximum(s - N_EXPERTS, 0)
    return pl.pallas_call(
        _combine_kernel,
        out_shape=jax.ShapeDtypeStruct((n_tok, D_MODEL), F32),
        grid_spec=pltpu.PrefetchScalarGridSpec(
            num_scalar_prefetch=1,
            grid=(N_EXPERTS + n_tok // tf,),
            in_specs=[pl.BlockSpec((1, cap * Y_ROWS, LANES), lambda s, *_: (jnp.minimum(s, N_EXPERTS - 1), 0, 0)),
                      pl.BlockSpec((tf, D_MODEL), lambda s, *_: (tile_of(s), 0)),
                      pl.BlockSpec((1, 1, N_MOD, D_MODEL), lambda s, *_: (layer, row_of_tile(tile_of(s)), 0, 0))],
            out_specs=pl.BlockSpec((tf, D_MODEL), lambda s, *_: (tile_of(s), 0)),
            scratch_shapes=[pltpu.VMEM((n_tok * Y_ROWS, LANES), F32)]),
        compiler_params=_cparams(1),
        name="combine",
    )(dst, y, x1, mods)


TOKEN_TILE = 1024
COMBINE_TILE = 512
Q_TILE = 256


def _moe(fronts, rows_of_tile, mods, layer, w_gate, w_up, w_down):
    routed, dsts = [], []
    for x1, hp, aff in fronts:
        cap = CAPACITY_FACTOR * x1.shape[0] // N_EXPERTS
        src, dst, gate = _select(aff, cap)
        routed.append((_gather(src, hp), gate))
        dsts.append(dst)
    ys = _ffn(routed, layer, w_gate, w_up, w_down)
    return [_combine(dst, y, x1, mods, layer, row_of_tile, COMBINE_TILE)
            for dst, y, (x1, _, _), row_of_tile in zip(dsts, ys, fronts, rows_of_tile)]


def kernel(x_prompt, x_sample, cache_k, cache_v, c, c_ctx, norm1, norm2, w_ada, b_ada, w_qkv, q_norm, k_norm,
           w_o, w_pool, pool_scale, w_router, w_e_gate, w_e_up, w_e_down):
    batch, seq, _ = x_prompt.shape
    dec_batch, dec_seq, _ = x_sample.shape
    depth = w_ada.shape[0]
    kvw = N_KV * HEAD_DIM

    cvec = jnp.zeros((SUBLANES, D_MODEL), F32).at[0].set(c_ctx).at[1:1 + dec_batch].set(c)
    mods = _ada(cvec, w_ada, b_ada)
    ctx = x_prompt.reshape(batch * seq, D_MODEL)
    lat = x_sample.reshape(dec_batch * dec_seq, D_MODEL)
    ctx_row = lambda i: 0
    lat_row = lambda tile: (lambda i: 1 + (i * tile) // dec_seq)
    ctx_tile = min(TOKEN_TILE, seq)
    rope_tabs = _rope_tables(dec_seq)
    new_k = new_v = None

    for layer in range(depth):
        j = layer // 2
        n1 = norm1[layer][None]
        n2 = norm2[layer][None]
        wr = jnp.pad(w_router[layer], ((0, 0), (0, LANES - N_EXPERTS))).astype(BF16)
        if layer % 2 == 0:
            qn = q_norm[j][None]
            kn = k_norm[j][None]
            qc, kc, vc, new_k, new_v = _qkv(ctx, mods, layer, ctx_row, n1, w_qkv[j], qn, kn, None, True, TOKEN_TILE)
            oc = _attn_ctx(qc, kc, vc, seq)
            ql, kl, vl = _qkv(lat, mods, layer, lat_row(TOKEN_TILE), n1, w_qkv[j], qn, kn, rope_tabs, False,
                              TOKEN_TILE)
            past_k = cache_k[:, j].reshape(dec_batch, -1, kvw).astype(BF16)
            past_v = cache_v[:, j].reshape(dec_batch, -1, kvw).astype(BF16)
            ol = _attn_lat(ql, kl, vl, past_k, past_v, dec_seq, Q_TILE)
            ctx1 = _post(oc, ctx, mods, layer, ctx_row, n2, w_o[j], wr, TOKEN_TILE)
            lat1 = _post(ol, lat, mods, layer, lat_row(TOKEN_TILE), n2, w_o[j], wr, TOKEN_TILE)
        else:
            ps = pool_scale[j][None]
            ctx1 = _pool(ctx, mods, layer, ctx_row, n1, n2, w_pool[j], ps, wr, seq, ctx_tile)
            lat1 = _pool(lat, mods, layer, lat_row(TOKEN_TILE), n1, n2, w_pool[j], ps, wr, dec_seq, TOKEN_TILE)
        ctx, lat = _moe([ctx1, lat1], [ctx_row, lat_row(COMBINE_TILE)], mods, layer, w_e_gate, w_e_up, w_e_down)

    new_cache_k = new_k.reshape(batch, 1, seq, N_KV, HEAD_DIM)
    new_cache_v = new_v.reshape(batch, 1, seq, N_KV, HEAD_DIM)
    return (ctx.reshape(batch, seq, D_MODEL), lat.reshape(dec_batch, dec_seq, D_MODEL), new_cache_k, new_cache_v)
```

```python
import functools

import jax
import jax.numpy as jnp
import numpy as np
from jax import lax
from jax.experimental import pallas as pl
from jax.experimental.pallas import tpu as pltpu

F32 = jnp.float32
BF16 = jnp.bfloat16
I32 = jnp.int32
U32 = jnp.uint32

D_MODEL = 1024
HEAD_DIM = 128
N_HEADS = 8
N_KV = 2
Q_PER_KV = N_HEADS // N_KV
QKV_DIM = (N_HEADS + 2 * N_KV) * HEAD_DIM
GRID_W = 64
ROPE_THETA = 10000.0
POOL_WINDOWS = (2, 4, 8, 16)
POOL_GROUP = D_MODEL // len(POOL_WINDOWS)
POOL_HALO = 8
N_EXPERTS = 16
CAPACITY_FACTOR = 2
N_MOD = 6
EPS = 1e-6
LOG2_E = 1.4426950408889634

LANES = 128
SUBLANES = 8
HALF = D_MODEL // 2
HP_ROWS = HALF // LANES
Y_ROWS = D_MODEL // LANES
VMEM_LIMIT = 56 * 1024 * 1024


def _cparams(n_axes, vmem=VMEM_LIMIT):
    return pltpu.CompilerParams(dimension_semantics=("arbitrary",) * n_axes, vmem_limit_bytes=vmem)


def _silu(x):
    return x / (1.0 + jnp.exp(-x))


def _rms(x, gain):
    return x * lax.rsqrt(jnp.mean(x * x, axis=-1, keepdims=True) + EPS) * gain


def _norm_mod(x, gain, shift, scale):
    return _rms(x, gain * (1.0 + scale)) + shift


def _dot(a, b):
    return jnp.dot(a, b, preferred_element_type=F32)


def _dot_nt(a, b):
    return lax.dot_general(a, b, (((1,), (1,)), ((), ())), preferred_element_type=F32)


def _ada_kernel(c_ref, w_ref, b_ref, o_ref):
    s = _silu(c_ref[...]).astype(BF16)
    o_ref[0] = _dot(s, w_ref[0].astype(BF16)) + b_ref[0]


def _ada(cvec, w_ada, b_ada):
    depth = w_ada.shape[0]
    out = pl.pallas_call(
        _ada_kernel,
        out_shape=jax.ShapeDtypeStruct((depth, SUBLANES, N_MOD * D_MODEL), F32),
        grid=(depth, N_MOD),
        in_specs=[
            pl.BlockSpec((SUBLANES, D_MODEL), lambda i, j: (0, 0)),
            pl.BlockSpec((1, D_MODEL, D_MODEL), lambda i, j: (i, 0, j)),
            pl.BlockSpec((1, 1, D_MODEL), lambda i, j: (i, 0, j)),
        ],
        out_specs=pl.BlockSpec((1, SUBLANES, D_MODEL), lambda i, j: (i, 0, j)),
        compiler_params=_cparams(2),
        name="ada",
    )(cvec, w_ada, b_ada.reshape(depth, 1, N_MOD * D_MODEL))
    return out.reshape(depth, SUBLANES, N_MOD, D_MODEL)


def _mod_spec(layer, row_of_tile):
    return pl.BlockSpec((1, 1, N_MOD, D_MODEL), lambda i, *_: (layer, row_of_tile(i), 0, 0))


def _qkv_kernel(*refs, rope, cache_out):
    x_ref, mod_ref, n1_ref, w_ref, qn_ref, kn_ref = refs[:6]
    refs = refs[6:]
    if rope:
        cos_ref, sin_ref = refs[:2]
        refs = refs[2:]
    q_ref, k_ref, v_ref = refs[:3]
    refs = refs[3:]
    if cache_out:
        kc_ref, vc_ref = refs[:2]
        refs = refs[2:]
    (wb_ref,) = refs

    qk_w = (N_HEADS + N_KV) * HEAD_DIM
    quarter = HEAD_DIM // 4

    def partner(a):
        width = a.shape[1]
        first = (lax.broadcasted_iota(I32, a.shape, 1) & quarter) == 0
        return jnp.where(first, pltpu.roll(a, width - quarter, 1), pltpu.roll(a, quarter, 1))

    @pl.when(pl.program_id(0) == 0)
    def _():
        w = w_ref[...]
        wb_ref[:, 0:QKV_DIM] = w.astype(BF16)
        if rope:
            wb_ref[:, QKV_DIM:] = partner(w[:, 0:qk_w]).astype(BF16)

    m = mod_ref[0, 0]
    h = _norm_mod(x_ref[...], n1_ref[...], m[0:1], m[1:2])
    qkv = _dot(h.astype(BF16), wb_ref[...])
    scale = HEAD_DIM ** -0.5 * LOG2_E
    if rope:
        gains = {True: qn_ref[...], False: kn_ref[...]}
        cos_g = {key: cos_ref[...] * g for key, g in gains.items()}
        sin_g = {key: sin_ref[...] * partner(g) for key, g in gains.items()}
    for hh in range(N_HEADS + N_KV):
        sl = slice(hh * HEAD_DIM, (hh + 1) * HEAD_DIM)
        is_q = hh < N_HEADS
        if rope:
            raw = qkv[:, sl]
            norm = lax.rsqrt(jnp.mean(raw * raw, axis=-1, keepdims=True) + EPS)
            xh = (raw * cos_g[is_q] + qkv[:, QKV_DIM + hh * HEAD_DIM:QKV_DIM + (hh + 1) * HEAD_DIM] * sin_g[is_q]) * norm
        else:
            xh = _rms(qkv[:, sl], qn_ref[...] if is_q else kn_ref[...])
        if cache_out and hh >= N_HEADS:
            kc_ref[:, (hh - N_HEADS) * HEAD_DIM:(hh - N_HEADS + 1) * HEAD_DIM] = xh
        if hh < N_HEADS:
            q_ref[:, sl] = (xh * scale).astype(BF16)
        else:
            k_ref[:, (hh - N_HEADS) * HEAD_DIM:(hh - N_HEADS + 1) * HEAD_DIM] = xh.astype(BF16)
    v = qkv[:, qk_w:QKV_DIM]
    v_ref[...] = v.astype(BF16)
    if cache_out:
        vc_ref[...] = v


def _qkv(x, mods, layer, row_of_tile, n1, w_qkv, qn, kn, rope_tabs, cache_out, tm):
    n_tok = x.shape[0]
    kvw = N_KV * HEAD_DIM
    rope = rope_tabs is not None
    in_specs = [
        pl.BlockSpec((tm, D_MODEL), lambda i: (i, 0)),
        _mod_spec(layer, row_of_tile),
        pl.BlockSpec((1, D_MODEL), lambda i: (0, 0)),
        pl.BlockSpec((D_MODEL, QKV_DIM), lambda i: (0, 0)),
        pl.BlockSpec((1, HEAD_DIM), lambda i: (0, 0)),
        pl.BlockSpec((1, HEAD_DIM), lambda i: (0, 0)),
    ]
    args = [x, mods, n1, w_qkv, qn, kn]
    if rope:
        seq_tiles = rope_tabs[0].shape[0] // tm
        in_specs += [pl.BlockSpec((tm, HEAD_DIM), lambda i: (i % seq_tiles, 0))] * 2
        args += list(rope_tabs)
    out_shape = [jax.ShapeDtypeStruct((n_tok, D_MODEL), BF16),
                 jax.ShapeDtypeStruct((n_tok, kvw), BF16),
                 jax.ShapeDtypeStruct((n_tok, kvw), BF16)]
    out_specs = [pl.BlockSpec((tm, D_MODEL), lambda i: (i, 0)),
                 pl.BlockSpec((tm, kvw), lambda i: (i, 0)),
                 pl.BlockSpec((tm, kvw), lambda i: (i, 0))]
    if cache_out:
        out_shape += [jax.ShapeDtypeStruct((n_tok, kvw), F32)] * 2
        out_specs += [pl.BlockSpec((tm, kvw), lambda i: (i, 0))] * 2
    return pl.pallas_call(
        functools.partial(_qkv_kernel, rope=rope, cache_out=cache_out),
        out_shape=out_shape,
        grid=(n_tok // tm,),
        in_specs=in_specs,
        out_specs=out_specs,
        scratch_shapes=[pltpu.VMEM((D_MODEL, QKV_DIM + ((N_HEADS + N_KV) * HEAD_DIM if rope else 0)), BF16)],
        compiler_params=_cparams(1),
        name="qkv_rope" if rope else "qkv",
    )(*args)


def _rope_tables(seq_len):
    half = HEAD_DIM // 2
    n = half // 2
    inv_freq = ROPE_THETA ** (-np.arange(n, dtype=np.float64) / n)
    rows = seq_len // GRID_W
    row = np.repeat(np.arange(rows), GRID_W).astype(np.float64)
    col = np.tile(np.arange(GRID_W), rows).astype(np.float64)
    ang_r = row[:, None] * inv_freq[None, :]
    ang_c = col[:, None] * inv_freq[None, :]
    cos = np.concatenate([np.cos(ang_r)] * 2 + [np.cos(ang_c)] * 2, axis=-1)
    sin = np.concatenate([-np.sin(ang_r), np.sin(ang_r), -np.sin(ang_c), np.sin(ang_c)], axis=-1)
    return jnp.asarray(cos, F32), jnp.asarray(sin, F32)


def _with_ones(v):
    return jnp.concatenate([v, jnp.ones_like(v)], axis=1)


def _gqa_attention(q, segments):
    rows = q.shape[0]
    qs = jnp.concatenate([q[:, h * HEAD_DIM:(h + 1) * HEAD_DIM] for h in range(Q_PER_KV)], axis=0)
    m = jnp.full((rows * Q_PER_KV, 1), -1e30, F32)
    acc = jnp.zeros((rows * Q_PER_KV, 2 * HEAD_DIM), F32)
    for k, v in segments:
        s = _dot_nt(qs, k)
        m_new = jnp.maximum(m, jnp.max(s, axis=-1, keepdims=True))
        acc = acc * jnp.exp2(m - m_new) + _dot(jnp.exp2(s - m_new).astype(BF16), v)
        m = m_new
    o = (acc[:, :HEAD_DIM] / acc[:, HEAD_DIM:]).astype(BF16)
    return jnp.concatenate([o[h * rows:(h + 1) * rows] for h in range(Q_PER_KV)], axis=1)


def _attn_ctx_kernel(q_ref, k_ref, v_ref, o_ref):
    gw = Q_PER_KV * HEAD_DIM
    for g in range(N_KV):
        kg = k_ref[:, g * HEAD_DIM:(g + 1) * HEAD_DIM]
        vg = _with_ones(v_ref[:, g * HEAD_DIM:(g + 1) * HEAD_DIM])
        o_ref[:, g * gw:(g + 1) * gw] = _gqa_attention(q_ref[:, g * gw:(g + 1) * gw], [(kg, vg)])


def _attn_ctx(q, k, v, seq):
    n_tok = q.shape[0]
    kvw = N_KV * HEAD_DIM
    return pl.pallas_call(
        _attn_ctx_kernel,
        out_shape=jax.ShapeDtypeStruct((n_tok, D_MODEL), BF16),
        grid=(n_tok // seq,),
        in_specs=[pl.BlockSpec((seq, D_MODEL), lambda b: (b, 0)),
                  pl.BlockSpec((seq, kvw), lambda b: (b, 0)),
                  pl.BlockSpec((seq, kvw), lambda b: (b, 0))],
        out_specs=pl.BlockSpec((seq, D_MODEL), lambda b: (b, 0)),
        compiler_params=_cparams(1),
        name="attn_ctx",
    )(q, k, v)


KEY_CHUNK = 256


def _attn_lat_kernel(q_ref, k_ref, v_ref, kc_ref, vc_ref, o_ref):
    segments = [(k_ref[c:c + KEY_CHUNK], _with_ones(v_ref[c:c + KEY_CHUNK]))
                for c in range(0, k_ref.shape[0], KEY_CHUNK)]
    segments.append((kc_ref[0], _with_ones(vc_ref[0])))
    o_ref[...] = _gqa_attention(q_ref[...], segments)


def _attn_lat(q, k, v, kc, vc, seq, tq):
    n_tok = q.shape[0]
    batch = n_tok // seq
    past = kc.shape[1]
    qt = seq // tq
    gw = Q_PER_KV * HEAD_DIM
    return pl.pallas_call(
        _attn_lat_kernel,
        out_shape=jax.ShapeDtypeStruct((n_tok, D_MODEL), BF16),
        grid=(batch, N_KV, qt),
        in_specs=[pl.BlockSpec((tq, gw), lambda b, g, i: (b * qt + i, g)),
                  pl.BlockSpec((seq, HEAD_DIM), lambda b, g, i: (b, g)),
                  pl.BlockSpec((seq, HEAD_DIM), lambda b, g, i: (b, g)),
                  pl.BlockSpec((1, past, HEAD_DIM), lambda b, g, i: (b, 0, g)),
                  pl.BlockSpec((1, past, HEAD_DIM), lambda b, g, i: (b, 0, g))],
        out_specs=pl.BlockSpec((tq, gw), lambda b, g, i: (b * qt + i, g)),
        compiler_params=_cparams(3),
        name="attn_lat",
    )(q, k, v, kc, vc)


def _router_softmax(hb, wr):
    logits = _dot(hb, wr)
    lane = lax.broadcasted_iota(I32, logits.shape, 1)
    logits = jnp.where(lane < N_EXPERTS, logits, -1e30)
    ex = jnp.exp(logits - jnp.max(logits, axis=-1, keepdims=True))
    return ex / jnp.sum(ex, axis=-1, keepdims=True)


def _moe_front(x1, m, n2, wr, x1_ref, hp_ref, aff_ref):
    tm = x1.shape[0]
    x1_ref[...] = x1
    h2 = _norm_mod(x1, n2, m[3:4], m[4:5])
    aff = _router_softmax(h2.astype(BF16), wr)
    packed = pltpu.pack_elementwise([h2[:, :HALF], h2[:, HALF:]], packed_dtype=BF16)
    for s in range(HP_ROWS):
        hp_ref[pl.ds(s, tm, stride=HP_ROWS), :] = packed[:, s * LANES:(s + 1) * LANES]
    aff_t = aff.T
    for b in range(tm // LANES):
        aff_ref[b * N_EXPERTS:(b + 1) * N_EXPERTS, :] = aff_t[0:N_EXPERTS, b * LANES:(b + 1) * LANES]


def _front_out(n_tok, tm):
    shapes = [jax.ShapeDtypeStruct((n_tok, D_MODEL), F32),
              jax.ShapeDtypeStruct((n_tok * HP_ROWS, LANES), U32),
              jax.ShapeDtypeStruct((n_tok // LANES * N_EXPERTS, LANES), F32)]
    specs = [pl.BlockSpec((tm, D_MODEL), lambda i: (i, 0)),
             pl.BlockSpec((tm * HP_ROWS, LANES), lambda i: (i, 0)),
             pl.BlockSpec((tm // LANES * N_EXPERTS, LANES), lambda i: (i, 0))]
    return shapes, specs


def _post_kernel(o_ref, x_ref, mod_ref, n2_ref, wo_ref, wr_ref, x1_ref, hp_ref, aff_ref, wb_ref):
    @pl.when(pl.program_id(0) == 0)
    def _():
        wb_ref[...] = wo_ref[...].astype(BF16)

    m = mod_ref[0, 0]
    x1 = x_ref[...] + m[2:3] * _dot(o_ref[...], wb_ref[...])
    _moe_front(x1, m, n2_ref[...], wr_ref[...], x1_ref, hp_ref, aff_ref)


def _post(o, x, mods, layer, row_of_tile, n2, w_o, wr, tm):
    n_tok = x.shape[0]
    shapes, specs = _front_out(n_tok, tm)
    return pl.pallas_call(
        _post_kernel,
        out_shape=shapes,
        grid=(n_tok // tm,),
        in_specs=[pl.BlockSpec((tm, D_MODEL), lambda i: (i, 0)),
                  pl.BlockSpec((tm, D_MODEL), lambda i: (i, 0)),
                  _mod_spec(layer, row_of_tile),
                  pl.BlockSpec((1, D_MODEL), lambda i: (0, 0)),
                  pl.BlockSpec((D_MODEL, D_MODEL), lambda i: (0, 0)),
                  pl.BlockSpec((D_MODEL, LANES), lambda i: (0, 0))],
        out_specs=specs,
        scratch_shapes=[pltpu.VMEM((D_MODEL, D_MODEL), BF16)],
        compiler_params=_cparams(1),
        name="post",
    )(o, x, mods, n2, w_o, wr)


def _pool_kernel(x_ref, xp_ref, xn_ref, mod_ref, n1_ref, n2_ref, wp_ref, ps_ref, wr_ref,
                 x1_ref, hp_ref, aff_ref, *, seq):
    tm = x_ref.shape[0]
    i = pl.program_id(0)
    m = mod_ref[0, 0]
    x = x_ref[...]

    def norm_mod(v):
        return _norm_mod(v, n1_ref[...], m[0:1], m[1:2])

    prev_ok = ((i * tm) % seq != 0).astype(F32)
    next_ok = (((i + 1) * tm) % seq != 0).astype(F32)
    h = norm_mod(x)
    hz = jnp.concatenate([norm_mod(xp_ref[...]) * prev_ok, h, norm_mod(xn_ref[...]) * next_ok], axis=0)
    rows = tm + 2 * POOL_HALO
    t = (i * tm) % seq + lax.broadcasted_iota(I32, (tm, 1), 0)
    ys = []
    for g, w in enumerate(POOL_WINDOWS):
        sl = slice(g * POOL_GROUP, (g + 1) * POOL_GROUP)
        f = hz[:, sl]
        step = 1
        while step < w:
            f = f + pltpu.roll(f, rows - step, 0)
            step *= 2
        win = pltpu.roll(f, w // 2, 0)[POOL_HALO:POOL_HALO + tm]
        cnt = (jnp.minimum(t + w // 2, seq) - jnp.maximum(t - w // 2, 0)).astype(F32)
        d = (win / cnt - h[:, sl]).astype(BF16)
        ys.append(_dot(d, wp_ref[g].astype(BF16)))
    x1 = x + (m[2:3] * ps_ref[...]) * jnp.concatenate(ys, axis=-1)
    _moe_front(x1, m, n2_ref[...], wr_ref[...], x1_ref, hp_ref, aff_ref)


def _pool(x, mods, layer, row_of_tile, n1, n2, w_pool, pool_scale, wr, seq, tm):
    n_tok = x.shape[0]
    hb = tm // POOL_HALO
    last = n_tok // POOL_HALO - 1
    shapes, specs = _front_out(n_tok, tm)
    return pl.pallas_call(
        functools.partial(_pool_kernel, seq=seq),
        out_shape=shapes,
        grid=(n_tok // tm,),
        in_specs=[pl.BlockSpec((tm, D_MODEL), lambda i: (i, 0)),
                  pl.BlockSpec((POOL_HALO, D_MODEL), lambda i: (jnp.maximum(i * hb - 1, 0), 0)),
                  pl.BlockSpec((POOL_HALO, D_MODEL), lambda i: (jnp.minimum((i + 1) * hb, last), 0)),
                  _mod_spec(layer, row_of_tile),
                  pl.BlockSpec((1, D_MODEL), lambda i: (0, 0)),
                  pl.BlockSpec((1, D_MODEL), lambda i: (0, 0)),
                  pl.BlockSpec(w_pool.shape, lambda i: (0, 0, 0)),
                  pl.BlockSpec((1, D_MODEL), lambda i: (0, 0)),
                  pl.BlockSpec((D_MODEL, LANES), lambda i: (0, 0))],
        out_specs=specs,
        compiler_params=_cparams(1),
        name="pool",
    )(x, x, x, mods, n1, n2, w_pool, pool_scale, wr)


def _select_kernel(aff_ref, src_ref, dst_ref, gate_ref, *, cap):
    nb = aff_ref.shape[0] // N_EXPERTS
    rows = nb * N_EXPERTS
    shape3 = (nb, N_EXPERTS, LANES)
    aff = aff_ref[...].reshape(shape3)

    def count(mask):
        return jnp.sum(jnp.sum(mask.astype(F32), axis=0), axis=-1, keepdims=True)

    def search(i, t):
        cand = t | (jnp.int32(1) << (29 - i))
        return jnp.where(count(aff >= pltpu.bitcast(cand, F32)[None]) >= cap, cand, t)

    thr = pltpu.bitcast(lax.fori_loop(0, 30, search, jnp.zeros((N_EXPERTS, LANES), I32)), F32)
    gt = aff > thr[None]
    eq = aff == thr[None]
    need = cap - count(gt)

    kk = lax.broadcasted_iota(I32, (LANES, LANES), 0)
    nn = lax.broadcasted_iota(I32, (LANES, LANES), 1)
    upper = (kk <= nn).astype(BF16)
    ones = jnp.ones((LANES, LANES), BF16)

    def prefix(mask):
        m2 = mask.astype(F32).astype(BF16).reshape(rows, LANES)
        p = _dot(m2, upper).reshape(shape3)
        s = _dot(m2, ones).reshape(shape3)
        offs = []
        run = jnp.zeros((N_EXPERTS, LANES), F32)
        for b in range(nb):
            offs.append(run)
            run = run + s[b]
        return p, jnp.stack(offs, axis=0), s

    pe, oe, _ = prefix(eq)
    sel = gt | (eq & ((pe + oe) <= need[None]))
    ps, os_, ss = prefix(sel)

    lane = lax.broadcasted_iota(I32, (rows, LANES), 1)
    sel2 = sel.reshape(rows, LANES)
    dist = jnp.where(sel2, lane - (ps.reshape(rows, LANES).astype(I32) - 1), 0)
    alive = sel2.astype(I32)
    val = lane
    gval = aff.reshape(rows, LANES)
    for k in range(7):
        s = 1 << k
        move = alive * ((dist >> k) & 1)
        inc = pltpu.roll(move, LANES - s, 1) * (lane < LANES - s).astype(I32) == 1
        val = jnp.where(inc, pltpu.roll(val, LANES - s, 1), val)
        gval = jnp.where(inc, pltpu.roll(gval, LANES - s, 1), gval)
        dist = jnp.where(inc, pltpu.roll(dist, LANES - s, 1), dist)
        alive = alive - move + inc.astype(I32)
    local = val.reshape(shape3)
    local_gate = gval.reshape(shape3)

    top = os_ + ss
    lane_e = lax.broadcasted_iota(I32, (N_EXPERTS, LANES), 1)
    for jc in range(cap // LANES):
        j = (lane_e + jc * LANES).astype(F32)
        acc = jnp.zeros((N_EXPERTS, LANES), I32)
        gacc = jnp.zeros((N_EXPERTS, LANES), F32)
        for b in range(nb):
            inside = (os_[b] <= j) & (j < top[b])
            jl = (j - os_[b]).astype(I32) & (LANES - 1)
            acc = jnp.where(inside, jnp.take_along_axis(local[b], jl, axis=1) + b * LANES, acc)
            gacc = jnp.where(inside, jnp.take_along_axis(local_gate[b], jl, axis=1), gacc)
        out_rows = pl.ds(jc, N_EXPERTS, stride=cap // LANES)
        src_ref[out_rows, :] = acc * HP_ROWS
        dst_ref[out_rows, :] = acc * Y_ROWS
        gate_ref[out_rows, :] = gacc


def _select(aff, cap):
    rows = aff.shape[0]
    out_rows = N_EXPERTS * cap // LANES
    out_spec = pl.BlockSpec((out_rows, LANES), lambda i: (0, 0))
    src, dst, gate = pl.pallas_call(
        functools.partial(_select_kernel, cap=cap),
        out_shape=[jax.ShapeDtypeStruct((out_rows, LANES), I32),
                   jax.ShapeDtypeStruct((out_rows, LANES), I32),
                   jax.ShapeDtypeStruct((out_rows, LANES), F32)],
        grid=(1,),
        in_specs=[pl.BlockSpec((rows, LANES), lambda i: (0, 0))],
        out_specs=[out_spec, out_spec, out_spec],
        compiler_params=_cparams(1),
        name="select",
    )(aff)
    return src.reshape(-1), dst.reshape(-1), gate


GATHER_UNROLL = 8


GATHER_STEP_ROWS = 2048


def _gather_kernel(src_ref, hp_ref, o_ref):
    i = pl.program_id(0)
    n = o_ref.shape[0] // HP_ROWS

    def gather(g, carry):
        base = i * n + g * GATHER_UNROLL
        for r in range(GATHER_UNROLL):
            src = pl.multiple_of(src_ref[base + r], HP_ROWS)
            dst = pl.multiple_of((g * GATHER_UNROLL + r) * HP_ROWS, HP_ROWS)
            o_ref[pl.ds(dst, HP_ROWS), :] = hp_ref[pl.ds(src, HP_ROWS), :]
        return carry

    lax.fori_loop(0, n // GATHER_UNROLL, gather, 0)


def _gather(src, hp):
    n = src.shape[0]
    return pl.pallas_call(
        _gather_kernel,
        out_shape=jax.ShapeDtypeStruct((n * HP_ROWS, LANES), U32),
        grid_spec=pltpu.PrefetchScalarGridSpec(
            num_scalar_prefetch=1,
            grid=(n // GATHER_STEP_ROWS,),
            in_specs=[pl.BlockSpec(hp.shape, lambda i, src: (0, 0), pipeline_mode=pl.Buffered(1))],
            out_specs=pl.BlockSpec((GATHER_STEP_ROWS * HP_ROWS, LANES), lambda i, src: (i, 0))),
        compiler_params=_cparams(1),
        name="gather",
    )(src, hp)


FF_CHUNK = 256


def _ffn_kernel(*refs, n_streams):
    e = pl.program_id(0)
    ins = refs[:2 * n_streams]
    wg_ref, wu_ref, wd_ref = refs[2 * n_streams:2 * n_streams + 3]
    outs = refs[2 * n_streams + 3:]
    d_ff = wg_ref.shape[3]
    eye = lax.broadcasted_iota(I32, (LANES, LANES), 0) == lax.broadcasted_iota(I32, (LANES, LANES), 1)
    for i in range(n_streams):
        xp_ref, gate_ref, y_ref = ins[2 * i], ins[2 * i + 1], outs[i]
        cap = xp_ref.shape[0] // HP_ROWS
        halves = [[], []]
        for s in range(HP_ROWS):
            w = xp_ref[pl.ds(s, cap, stride=HP_ROWS), :]
            for half in range(2):
                v = pltpu.unpack_elementwise(w, index=half, packed_dtype=BF16, unpacked_dtype=F32)
                halves[half].append(v.astype(BF16))
        xe = jnp.concatenate(halves[0] + halves[1], axis=1)
        y = None
        for c in range(0, d_ff, FF_CHUNK):
            a = _dot(xe, wg_ref[0, 0, :, c:c + FF_CHUNK].astype(BF16))
            u = _dot(xe, wu_ref[0, 0, :, c:c + FF_CHUNK].astype(BF16))
            yc = _dot((_silu(a) * u).astype(BF16), wd_ref[0, 0, c:c + FF_CHUNK, :].astype(BF16))
            y = yc if y is None else y + yc
        for c in range(cap // LANES):
            g_row = gate_ref[pl.ds(e * (cap // LANES) + c, 1), :]
            g_col = jnp.sum(jnp.where(eye, g_row, 0.0), axis=1, keepdims=True)
            tot = y[c * LANES:(c + 1) * LANES, :] * g_col
            for s in range(Y_ROWS):
                y_ref[0, pl.ds(c * LANES * Y_ROWS + s, LANES, stride=Y_ROWS), :] = tot[:, s * LANES:(s + 1) * LANES]


def _ffn(streams, layer, w_gate, w_up, w_down):
    caps = [xp.shape[0] // (N_EXPERTS * HP_ROWS) for xp, _ in streams]
    in_specs, args = [], []
    for (xp, gate), cap in zip(streams, caps):
        in_specs += [pl.BlockSpec((cap * HP_ROWS, LANES), lambda e: (e, 0)),
                     pl.BlockSpec(gate.shape, lambda e: (0, 0))]
        args += [xp, gate]
    w_spec = pl.BlockSpec((1, 1) + w_gate.shape[2:], lambda e: (layer, e, 0, 0))
    return pl.pallas_call(
        functools.partial(_ffn_kernel, n_streams=len(streams)),
        out_shape=[jax.ShapeDtypeStruct((N_EXPERTS, cap * Y_ROWS, LANES), F32) for cap in caps],
        grid=(N_EXPERTS,),
        in_specs=in_specs + [w_spec, w_spec, w_spec],
        out_specs=[pl.BlockSpec((1, cap * Y_ROWS, LANES), lambda e: (e, 0, 0)) for cap in caps],
        compiler_params=_cparams(1),
        name="ffn",
    )(*args, w_gate, w_up, w_down)


SCATTER_UNROLL = 16


def _combine_kernel(dst_ref, y_ref, x1_ref, mod_ref, o_ref, acc_ref, *, n_scatter):
    s = pl.program_id(0)
    tf = o_ref.shape[0]
    step_rows = y_ref.shape[0] // Y_ROWS

    @pl.when(s == 0)
    def _():
        acc_ref[...] = jnp.zeros_like(acc_ref)

    @pl.when(s < n_scatter)
    def _():
        def scatter(g, carry):
            base = s * step_rows + g * SCATTER_UNROLL
            rows, sums = [], []
            for r in range(SCATTER_UNROLL):
                dst = pl.ds(pl.multiple_of(dst_ref[base + r], Y_ROWS), Y_ROWS)
                src = pl.ds(pl.multiple_of((g * SCATTER_UNROLL + r) * Y_ROWS, Y_ROWS), Y_ROWS)
                rows.append(dst)
                sums.append(acc_ref[dst, :] + y_ref[src, :])
            for dst, v in zip(rows, sums):
                acc_ref[dst, :] = v
            return carry

        lax.fori_loop(0, step_rows // SCATTER_UNROLL, scatter, 0)

    @pl.when(s >= n_scatter)
    def _():
        g2 = mod_ref[0, 0][5:6]
        tile = acc_ref.at[pl.ds(pl.multiple_of((s - n_scatter) * tf * Y_ROWS, tf * Y_ROWS), tf * Y_ROWS), :]
        for c in range(Y_ROWS):
            sl = slice(c * LANES, (c + 1) * LANES)
            o_ref[:, sl] = x1_ref[:, sl] + g2[:, sl] * tile[pl.ds(c, tf, stride=Y_ROWS), :]


def _combine_tiles(n_tok):
    row_bytes = D_MODEL * 4
    free_rows = (VMEM_LIMIT - n_tok * row_bytes) * 4 // 5 // row_bytes
    scale = max(k for k in (1, 2, 4) if 2 * 512 * k + 4 * 256 * k <= free_rows)
    return 256 * scale, 512 * scale


def _combine(dst, y, x1, mods, layer, row_of_token):
    n_tok = x1.shape[0]
    tf, step_rows = _combine_tiles(n_tok)
    n_scatter = dst.shape[0] // step_rows
    assert (dst.shape[0] // N_EXPERTS) % SCATTER_UNROLL == 0 and dst.shape[0] % step_rows == 0
    tile_of = lambda s: jnp.maximum(s - n_scatter, 0)
    row_of_tile = lambda t: row_of_token(t * tf)
    return pl.pallas_call(
        functools.partial(_combine_kernel, n_scatter=n_scatter),
        out_shape=jax.ShapeDtypeStruct((n_tok, D_MODEL), F32),
        grid_spec=pltpu.PrefetchScalarGridSpec(
            num_scalar_prefetch=1,
            grid=(n_scatter + n_tok // tf,),
            in_specs=[pl.BlockSpec((step_rows * Y_ROWS, LANES), lambda s, *_: (jnp.minimum(s, n_scatter - 1), 0)),
                      pl.BlockSpec((tf, D_MODEL), lambda s, *_: (tile_of(s), 0)),
                      pl.BlockSpec((1, 1, N_MOD, D_MODEL), lambda s, *_: (layer, row_of_tile(tile_of(s)), 0, 0))],
            out_specs=pl.BlockSpec((tf, D_MODEL), lambda s, *_: (tile_of(s), 0)),
            scratch_shapes=[pltpu.VMEM((n_tok * Y_ROWS, LANES), F32)]),
        compiler_params=_cparams(1),
        name="combine",
    )(dst, y.reshape(-1, LANES), x1, mods)


TOKEN_TILE = 1024
Q_TILE = 512


def _moe(fronts, rows_of_token, mods, layer, w_gate, w_up, w_down):
    routed, dsts = [], []
    for x1, hp, aff in fronts:
        cap = CAPACITY_FACTOR * x1.shape[0] // N_EXPERTS
        src, dst, gate = _select(aff, cap)
        routed.append((_gather(src, hp), gate))
        dsts.append(dst)
    ys = _ffn(routed, layer, w_gate, w_up, w_down)
    return [_combine(dst, y, x1, mods, layer, row_of_token)
            for dst, y, (x1, _, _), row_of_token in zip(dsts, ys, fronts, rows_of_token)]


def kernel(x_prompt, x_sample, cache_k, cache_v, c, c_ctx, norm1, norm2, w_ada, b_ada, w_qkv, q_norm, k_norm,
           w_o, w_pool, pool_scale, w_router, w_e_gate, w_e_up, w_e_down):
    batch, seq, _ = x_prompt.shape
    dec_batch, dec_seq, _ = x_sample.shape
    depth = w_ada.shape[0]
    kvw = N_KV * HEAD_DIM

    cvec = jnp.zeros((SUBLANES, D_MODEL), F32).at[0].set(c_ctx).at[1:1 + dec_batch].set(c)
    mods = _ada(cvec, w_ada, b_ada)
    ctx = x_prompt.reshape(batch * seq, D_MODEL)
    lat = x_sample.reshape(dec_batch * dec_seq, D_MODEL)
    ctx_row = lambda i: 0
    lat_row = lambda tile: (lambda i: 1 + (i * tile) // dec_seq)
    ctx_tile = min(TOKEN_TILE, seq)
    rope_tabs = _rope_tables(dec_seq)
    new_k = new_v = None

    for layer in range(depth):
        j = layer // 2
        n1 = norm1[layer][None]
        n2 = norm2[layer][None]
        wr = jnp.pad(w_router[layer], ((0, 0), (0, LANES - N_EXPERTS))).astype(BF16)
        if layer % 2 == 0:
            qn = q_norm[j][None]
            kn = k_norm[j][None]
            qc, kc, vc, new_k, new_v = _qkv(ctx, mods, layer, ctx_row, n1, w_qkv[j], qn, kn, None, True, TOKEN_TILE)
            oc = _attn_ctx(qc, kc, vc, seq)
            ql, kl, vl = _qkv(lat, mods, layer, lat_row(TOKEN_TILE), n1, w_qkv[j], qn, kn, rope_tabs, False,
                              TOKEN_TILE)
            past_k = cache_k[:, j].reshape(dec_batch, -1, kvw).astype(BF16)
            past_v = cache_v[:, j].reshape(dec_batch, -1, kvw).astype(BF16)
            ol = _attn_lat(ql, kl, vl, past_k, past_v, dec_seq, Q_TILE)
            ctx1 = _post(oc, ctx, mods, layer, ctx_row, n2, w_o[j], wr, TOKEN_TILE)
            lat1 = _post(ol, lat, mods, layer, lat_row(TOKEN_TILE), n2, w_o[j], wr, TOKEN_TILE)
        else:
            ps = pool_scale[j][None]
            ctx1 = _pool(ctx, mods, layer, ctx_row, n1, n2, w_pool[j], ps, wr, seq, ctx_tile)
            lat1 = _pool(lat, mods, layer, lat_row(TOKEN_TILE), n1, n2, w_pool[j], ps, wr, dec_seq, TOKEN_TILE)
        ctx, lat = _moe([ctx1, lat1], [ctx_row, lat_row(1)], mods, layer, w_e_gate, w_e_up, w_e_down)

    new_cache_k = new_k.reshape(batch, 1, seq, N_KV, HEAD_DIM)
    new_cache_v = new_v.reshape(batch, 1, seq, N_KV, HEAD_DIM)
    return (ctx.reshape(batch, seq, D_MODEL), lat.reshape(dec_batch, dec_seq, D_MODEL), new_cache_k, new_cache_v)
```

```python
import functools

import jax
import jax.numpy as jnp
import numpy as np
from jax import lax
from jax.experimental import pallas as pl
from jax.experimental.pallas import tpu as pltpu

F32 = jnp.float32
BF16 = jnp.bfloat16
I32 = jnp.int32
U32 = jnp.uint32

D_MODEL = 1024
HEAD_DIM = 128
N_HEADS = 8
N_KV = 2
Q_PER_KV = N_HEADS // N_KV
QKV_DIM = (N_HEADS + 2 * N_KV) * HEAD_DIM
GRID_W = 64
ROPE_THETA = 10000.0
POOL_WINDOWS = (2, 4, 8, 16)
POOL_GROUP = D_MODEL // len(POOL_WINDOWS)
POOL_HALO = 8
N_EXPERTS = 16
CAPACITY_FACTOR = 2
N_MOD = 6
EPS = 1e-6
LOG2_E = 1.4426950408889634

LANES = 128
SUBLANES = 8
HALF = D_MODEL // 2
HP_ROWS = HALF // LANES
Y_ROWS = D_MODEL // LANES
VMEM_LIMIT = 56 * 1024 * 1024


def _cparams(n_axes, vmem=VMEM_LIMIT):
    return pltpu.CompilerParams(dimension_semantics=("arbitrary",) * n_axes, vmem_limit_bytes=vmem)


def _silu(x):
    return x / (1.0 + jnp.exp(-x))


def _rms(x, gain):
    return x * lax.rsqrt(jnp.mean(x * x, axis=-1, keepdims=True) + EPS) * gain


def _norm_mod(x, gain, shift, scale):
    return _rms(x, gain * (1.0 + scale)) + shift


def _dot(a, b):
    return jnp.dot(a, b, preferred_element_type=F32)


def _dot_nt(a, b):
    return lax.dot_general(a, b, (((1,), (1,)), ((), ())), preferred_element_type=F32)


def _ada_kernel(c_ref, w_ref, b_ref, o_ref):
    s = _silu(c_ref[...]).astype(BF16)
    o_ref[0] = _dot(s, w_ref[0].astype(BF16)) + b_ref[0]


def _ada(cvec, w_ada, b_ada):
    depth = w_ada.shape[0]
    out = pl.pallas_call(
        _ada_kernel,
        out_shape=jax.ShapeDtypeStruct((depth, SUBLANES, N_MOD * D_MODEL), F32),
        grid=(depth, N_MOD),
        in_specs=[
            pl.BlockSpec((SUBLANES, D_MODEL), lambda i, j: (0, 0)),
            pl.BlockSpec((1, D_MODEL, D_MODEL), lambda i, j: (i, 0, j)),
            pl.BlockSpec((1, 1, D_MODEL), lambda i, j: (i, 0, j)),
        ],
        out_specs=pl.BlockSpec((1, SUBLANES, D_MODEL), lambda i, j: (i, 0, j)),
        compiler_params=_cparams(2),
        name="ada",
    )(cvec, w_ada, b_ada.reshape(depth, 1, N_MOD * D_MODEL))
    return out.reshape(depth, SUBLANES, N_MOD, D_MODEL)


def _mod_spec(layer, row_of_tile):
    return pl.BlockSpec((1, 1, N_MOD, D_MODEL), lambda i, *_: (layer, row_of_tile(i), 0, 0))


def _qkv_kernel(*refs, rope, cache_out):
    x_ref, mod_ref, n1_ref, w_ref, qn_ref, kn_ref = refs[:6]
    refs = refs[6:]
    if rope:
        cos_ref, sin_ref = refs[:2]
        refs = refs[2:]
    q_ref, k_ref, v_ref = refs[:3]
    refs = refs[3:]
    if cache_out:
        kc_ref, vc_ref = refs[:2]
        refs = refs[2:]
    (wb_ref,) = refs

    qk_w = (N_HEADS + N_KV) * HEAD_DIM
    quarter = HEAD_DIM // 4

    def partner(a):
        width = a.shape[1]
        first = (lax.broadcasted_iota(I32, a.shape, 1) & quarter) == 0
        return jnp.where(first, pltpu.roll(a, width - quarter, 1), pltpu.roll(a, quarter, 1))

    @pl.when(pl.program_id(0) == 0)
    def _():
        w = w_ref[...]
        wb_ref[:, 0:QKV_DIM] = w.astype(BF16)
        if rope:
            wb_ref[:, QKV_DIM:] = partner(w[:, 0:qk_w]).astype(BF16)

    m = mod_ref[0, 0]
    h = _norm_mod(x_ref[...], n1_ref[...], m[0:1], m[1:2])
    qkv = _dot(h.astype(BF16), wb_ref[...])
    scale = HEAD_DIM ** -0.5 * LOG2_E
    if rope:
        gains = {True: qn_ref[...], False: kn_ref[...]}
        cos_g = {key: cos_ref[...] * g for key, g in gains.items()}
        sin_g = {key: sin_ref[...] * partner(g) for key, g in gains.items()}
    for hh in range(N_HEADS + N_KV):
        sl = slice(hh * HEAD_DIM, (hh + 1) * HEAD_DIM)
        is_q = hh < N_HEADS
        if rope:
            raw = qkv[:, sl]
            norm = lax.rsqrt(jnp.mean(raw * raw, axis=-1, keepdims=True) + EPS)
            xh = (raw * cos_g[is_q] + qkv[:, QKV_DIM + hh * HEAD_DIM:QKV_DIM + (hh + 1) * HEAD_DIM] * sin_g[is_q]) * norm
        else:
            xh = _rms(qkv[:, sl], qn_ref[...] if is_q else kn_ref[...])
        if cache_out and hh >= N_HEADS:
            kc_ref[:, (hh - N_HEADS) * HEAD_DIM:(hh - N_HEADS + 1) * HEAD_DIM] = xh
        if hh < N_HEADS:
            q_ref[:, sl] = (xh * scale).astype(BF16)
        else:
            k_ref[:, (hh - N_HEADS) * HEAD_DIM:(hh - N_HEADS + 1) * HEAD_DIM] = xh.astype(BF16)
    v = qkv[:, qk_w:QKV_DIM]
    v_ref[...] = v.astype(BF16)
    if cache_out:
        vc_ref[...] = v


def _qkv(x, mods, layer, row_of_tile, n1, w_qkv, qn, kn, rope_tabs, cache_out, tm):
    n_tok = x.shape[0]
    kvw = N_KV * HEAD_DIM
    rope = rope_tabs is not None
    in_specs = [
        pl.BlockSpec((tm, D_MODEL), lambda i: (i, 0)),
        _mod_spec(layer, row_of_tile),
        pl.BlockSpec((1, D_MODEL), lambda i: (0, 0)),
        pl.BlockSpec((D_MODEL, QKV_DIM), lambda i: (0, 0)),
        pl.BlockSpec((1, HEAD_DIM), lambda i: (0, 0)),
        pl.BlockSpec((1, HEAD_DIM), lambda i: (0, 0)),
    ]
    args = [x, mods, n1, w_qkv, qn, kn]
    if rope:
        seq_tiles = rope_tabs[0].shape[0] // tm
        in_specs += [pl.BlockSpec((tm, HEAD_DIM), lambda i: (i % seq_tiles, 0))] * 2
        args += list(rope_tabs)
    out_shape = [jax.ShapeDtypeStruct((n_tok, D_MODEL), BF16),
                 jax.ShapeDtypeStruct((n_tok, kvw), BF16),
                 jax.ShapeDtypeStruct((n_tok, kvw), BF16)]
    out_specs = [pl.BlockSpec((tm, D_MODEL), lambda i: (i, 0)),
                 pl.BlockSpec((tm, kvw), lambda i: (i, 0)),
                 pl.BlockSpec((tm, kvw), lambda i: (i, 0))]
    if cache_out:
        out_shape += [jax.ShapeDtypeStruct((n_tok, kvw), F32)] * 2
        out_specs += [pl.BlockSpec((tm, kvw), lambda i: (i, 0))] * 2
    return pl.pallas_call(
        functools.partial(_qkv_kernel, rope=rope, cache_out=cache_out),
        out_shape=out_shape,
        grid=(n_tok // tm,),
        in_specs=in_specs,
        out_specs=out_specs,
        scratch_shapes=[pltpu.VMEM((D_MODEL, QKV_DIM + ((N_HEADS + N_KV) * HEAD_DIM if rope else 0)), BF16)],
        compiler_params=_cparams(1),
        name="qkv_rope" if rope else "qkv",
    )(*args)


def _rope_tables(seq_len):
    half = HEAD_DIM // 2
    n = half // 2
    inv_freq = ROPE_THETA ** (-np.arange(n, dtype=np.float64) / n)
    rows = seq_len // GRID_W
    row = np.repeat(np.arange(rows), GRID_W).astype(np.float64)
    col = np.tile(np.arange(GRID_W), rows).astype(np.float64)
    ang_r = row[:, None] * inv_freq[None, :]
    ang_c = col[:, None] * inv_freq[None, :]
    cos = np.concatenate([np.cos(ang_r)] * 2 + [np.cos(ang_c)] * 2, axis=-1)
    sin = np.concatenate([-np.sin(ang_r), np.sin(ang_r), -np.sin(ang_c), np.sin(ang_c)], axis=-1)
    return jnp.asarray(cos, F32), jnp.asarray(sin, F32)


def _with_ones(v):
    return jnp.concatenate([v, jnp.ones_like(v)], axis=1)


def _gqa_attention(q, segments):
    rows = q.shape[0]
    qs = jnp.concatenate([q[:, h * HEAD_DIM:(h + 1) * HEAD_DIM] for h in range(Q_PER_KV)], axis=0)
    m = jnp.full((rows * Q_PER_KV, 1), -1e30, F32)
    acc = jnp.zeros((rows * Q_PER_KV, 2 * HEAD_DIM), F32)
    for k, v in segments:
        s = _dot_nt(qs, k)
        m_new = jnp.maximum(m, jnp.max(s, axis=-1, keepdims=True))
        acc = acc * jnp.exp2(m - m_new) + _dot(jnp.exp2(s - m_new).astype(BF16), v)
        m = m_new
    o = (acc[:, :HEAD_DIM] / acc[:, HEAD_DIM:]).astype(BF16)
    return jnp.concatenate([o[h * rows:(h + 1) * rows] for h in range(Q_PER_KV)], axis=1)


def _attn_ctx_kernel(q_ref, k_ref, v_ref, o_ref, *, seq):
    gw = Q_PER_KV * HEAD_DIM
    for r in range(0, q_ref.shape[0], seq):
        for g in range(N_KV):
            kg = k_ref[r:r + seq, g * HEAD_DIM:(g + 1) * HEAD_DIM]
            vg = _with_ones(v_ref[r:r + seq, g * HEAD_DIM:(g + 1) * HEAD_DIM])
            o_ref[r:r + seq, g * gw:(g + 1) * gw] = _gqa_attention(q_ref[r:r + seq, g * gw:(g + 1) * gw], [(kg, vg)])


def _attn_ctx(q, k, v, seq, tm):
    n_tok = q.shape[0]
    kvw = N_KV * HEAD_DIM
    return pl.pallas_call(
        functools.partial(_attn_ctx_kernel, seq=seq),
        out_shape=jax.ShapeDtypeStruct((n_tok, D_MODEL), BF16),
        grid=(n_tok // tm,),
        in_specs=[pl.BlockSpec((tm, D_MODEL), lambda b: (b, 0)),
                  pl.BlockSpec((tm, kvw), lambda b: (b, 0)),
                  pl.BlockSpec((tm, kvw), lambda b: (b, 0))],
        out_specs=pl.BlockSpec((tm, D_MODEL), lambda b: (b, 0)),
        compiler_params=_cparams(1),
        name="attn_ctx",
    )(q, k, v)


KEY_CHUNK = 256


def _attn_lat_kernel(q_ref, k_ref, v_ref, kc_ref, vc_ref, o_ref):
    segments = [(k_ref[c:c + KEY_CHUNK], _with_ones(v_ref[c:c + KEY_CHUNK]))
                for c in range(0, k_ref.shape[0], KEY_CHUNK)]
    segments.append((kc_ref[0], _with_ones(vc_ref[0])))
    o_ref[...] = _gqa_attention(q_ref[...], segments)


def _attn_lat(q, k, v, kc, vc, seq, tq):
    n_tok = q.shape[0]
    batch = n_tok // seq
    past = kc.shape[1]
    qt = seq // tq
    gw = Q_PER_KV * HEAD_DIM
    return pl.pallas_call(
        _attn_lat_kernel,
        out_shape=jax.ShapeDtypeStruct((n_tok, D_MODEL), BF16),
        grid=(batch, N_KV, qt),
        in_specs=[pl.BlockSpec((tq, gw), lambda b, g, i: (b * qt + i, g)),
                  pl.BlockSpec((seq, HEAD_DIM), lambda b, g, i: (b, g)),
                  pl.BlockSpec((seq, HEAD_DIM), lambda b, g, i: (b, g)),
                  pl.BlockSpec((1, past, HEAD_DIM), lambda b, g, i: (b, 0, g)),
                  pl.BlockSpec((1, past, HEAD_DIM), lambda b, g, i: (b, 0, g))],
        out_specs=pl.BlockSpec((tq, gw), lambda b, g, i: (b * qt + i, g)),
        compiler_params=_cparams(3),
        name="attn_lat",
    )(q, k, v, kc, vc)


def _router_softmax(hb, wr):
    logits = _dot(hb, wr)
    lane = lax.broadcasted_iota(I32, logits.shape, 1)
    logits = jnp.where(lane < N_EXPERTS, logits, -1e30)
    ex = jnp.exp(logits - jnp.max(logits, axis=-1, keepdims=True))
    return ex / jnp.sum(ex, axis=-1, keepdims=True)


def _moe_front(x1, m, n2, wr, x1_ref, hp_ref, aff_ref):
    tm = x1.shape[0]
    x1_ref[...] = x1
    h2 = _norm_mod(x1, n2, m[3:4], m[4:5])
    aff = _router_softmax(h2.astype(BF16), wr)
    packed = pltpu.pack_elementwise([h2[:, :HALF], h2[:, HALF:]], packed_dtype=BF16)
    for s in range(HP_ROWS):
        hp_ref[pl.ds(s, tm, stride=HP_ROWS), :] = packed[:, s * LANES:(s + 1) * LANES]
    aff_t = aff.T
    for b in range(tm // LANES):
        aff_ref[b * N_EXPERTS:(b + 1) * N_EXPERTS, :] = aff_t[0:N_EXPERTS, b * LANES:(b + 1) * LANES]


def _front_out(n_tok, tm):
    shapes = [jax.ShapeDtypeStruct((n_tok, D_MODEL), F32),
              jax.ShapeDtypeStruct((n_tok * HP_ROWS, LANES), U32),
              jax.ShapeDtypeStruct((n_tok // LANES * N_EXPERTS, LANES), F32)]
    specs = [pl.BlockSpec((tm, D_MODEL), lambda i: (i, 0)),
             pl.BlockSpec((tm * HP_ROWS, LANES), lambda i: (i, 0)),
             pl.BlockSpec((tm // LANES * N_EXPERTS, LANES), lambda i: (i, 0))]
    return shapes, specs


def _post_kernel(o_ref, x_ref, mod_ref, n2_ref, wo_ref, wr_ref, x1_ref, hp_ref, aff_ref, wb_ref):
    @pl.when(pl.program_id(0) == 0)
    def _():
        wb_ref[...] = wo_ref[...].astype(BF16)

    m = mod_ref[0, 0]
    x1 = x_ref[...] + m[2:3] * _dot(o_ref[...], wb_ref[...])
    _moe_front(x1, m, n2_ref[...], wr_ref[...], x1_ref, hp_ref, aff_ref)


def _post(o, x, mods, layer, row_of_tile, n2, w_o, wr, tm):
    n_tok = x.shape[0]
    shapes, specs = _front_out(n_tok, tm)
    return pl.pallas_call(
        _post_kernel,
        out_shape=shapes,
        grid=(n_tok // tm,),
        in_specs=[pl.BlockSpec((tm, D_MODEL), lambda i: (i, 0)),
                  pl.BlockSpec((tm, D_MODEL), lambda i: (i, 0)),
                  _mod_spec(layer, row_of_tile),
                  pl.BlockSpec((1, D_MODEL), lambda i: (0, 0)),
                  pl.BlockSpec((D_MODEL, D_MODEL), lambda i: (0, 0)),
                  pl.BlockSpec((D_MODEL, LANES), lambda i: (0, 0))],
        out_specs=specs,
        scratch_shapes=[pltpu.VMEM((D_MODEL, D_MODEL), BF16)],
        compiler_params=_cparams(1),
        name="post",
    )(o, x, mods, n2, w_o, wr)


def _pool_kernel(x_ref, xp_ref, xn_ref, mod_ref, n1_ref, n2_ref, wp_ref, ps_ref, wr_ref,
                 x1_ref, hp_ref, aff_ref, *, seq):
    tm = x_ref.shape[0]
    i = pl.program_id(0)
    m = mod_ref[0, 0]
    x = x_ref[...]

    def norm_mod(v):
        return _norm_mod(v, n1_ref[...], m[0:1], m[1:2])

    h = norm_mod(x)
    if tm <= seq:
        prev_ok = ((i * tm) % seq != 0).astype(F32)
        next_ok = (((i + 1) * tm) % seq != 0).astype(F32)
        pieces = [(norm_mod(xp_ref[...]) * prev_ok, h, norm_mod(xn_ref[...]) * next_ok)]
        t0 = (i * tm) % seq
    else:
        zeros = jnp.zeros((POOL_HALO, D_MODEL), F32)
        pieces = [(zeros, h[r:r + seq], zeros) for r in range(0, tm, seq)]
        t0 = 0
    n = min(tm, seq)
    rows = n + 2 * POOL_HALO
    t = t0 + lax.broadcasted_iota(I32, (n, 1), 0)
    deltas = [[] for _ in POOL_WINDOWS]
    for piece in pieces:
        hz = jnp.concatenate(piece, axis=0)
        for g, w in enumerate(POOL_WINDOWS):
            sl = slice(g * POOL_GROUP, (g + 1) * POOL_GROUP)
            f = hz[:, sl]
            step = 1
            while step < w:
                f = f + pltpu.roll(f, rows - step, 0)
                step *= 2
            win = pltpu.roll(f, w // 2, 0)[POOL_HALO:POOL_HALO + n]
            cnt = (jnp.minimum(t + w // 2, seq) - jnp.maximum(t - w // 2, 0)).astype(F32)
            deltas[g].append((win / cnt - piece[1][:, sl]).astype(BF16))
    ys = [_dot(jnp.concatenate(d, axis=0), wp_ref[g].astype(BF16)) for g, d in enumerate(deltas)]
    x1 = x + (m[2:3] * ps_ref[...]) * jnp.concatenate(ys, axis=-1)
    _moe_front(x1, m, n2_ref[...], wr_ref[...], x1_ref, hp_ref, aff_ref)


def _pool(x, mods, layer, row_of_tile, n1, n2, w_pool, pool_scale, wr, seq, tm):
    n_tok = x.shape[0]
    hb = tm // POOL_HALO
    last = n_tok // POOL_HALO - 1
    shapes, specs = _front_out(n_tok, tm)
    return pl.pallas_call(
        functools.partial(_pool_kernel, seq=seq),
        out_shape=shapes,
        grid=(n_tok // tm,),
        in_specs=[pl.BlockSpec((tm, D_MODEL), lambda i: (i, 0)),
                  pl.BlockSpec((POOL_HALO, D_MODEL), lambda i: (jnp.maximum(i * hb - 1, 0), 0)),
                  pl.BlockSpec((POOL_HALO, D_MODEL), lambda i: (jnp.minimum((i + 1) * hb, last), 0)),
                  _mod_spec(layer, row_of_tile),
                  pl.BlockSpec((1, D_MODEL), lambda i: (0, 0)),
                  pl.BlockSpec((1, D_MODEL), lambda i: (0, 0)),
                  pl.BlockSpec(w_pool.shape, lambda i: (0, 0, 0)),
                  pl.BlockSpec((1, D_MODEL), lambda i: (0, 0)),
                  pl.BlockSpec((D_MODEL, LANES), lambda i: (0, 0))],
        out_specs=specs,
        compiler_params=_cparams(1),
        name="pool",
    )(x, x, x, mods, n1, n2, w_pool, pool_scale, wr)


def _select_kernel(aff_ref, src_ref, dst_ref, gate_ref, *, cap):
    nb = aff_ref.shape[0] // N_EXPERTS
    rows = nb * N_EXPERTS
    shape3 = (nb, N_EXPERTS, LANES)
    aff = aff_ref[...].reshape(shape3)

    def count(mask):
        return jnp.sum(jnp.sum(mask.astype(F32), axis=0), axis=-1, keepdims=True)

    def search(i, t):
        cand = t | (jnp.int32(1) << (29 - i))
        return jnp.where(count(aff >= pltpu.bitcast(cand, F32)[None]) >= cap, cand, t)

    thr = pltpu.bitcast(lax.fori_loop(0, 30, search, jnp.zeros((N_EXPERTS, LANES), I32)), F32)
    gt = aff > thr[None]
    eq = aff == thr[None]
    need = cap - count(gt)

    kk = lax.broadcasted_iota(I32, (LANES, LANES), 0)
    nn = lax.broadcasted_iota(I32, (LANES, LANES), 1)
    upper = (kk <= nn).astype(BF16)
    ones = jnp.ones((LANES, LANES), BF16)

    def prefix(mask):
        m2 = mask.astype(F32).astype(BF16).reshape(rows, LANES)
        p = _dot(m2, upper).reshape(shape3)
        s = _dot(m2, ones).reshape(shape3)
        offs = []
        run = jnp.zeros((N_EXPERTS, LANES), F32)
        for b in range(nb):
            offs.append(run)
            run = run + s[b]
        return p, jnp.stack(offs, axis=0), s

    pe, oe, _ = prefix(eq)
    sel = gt | (eq & ((pe + oe) <= need[None]))
    ps, os_, ss = prefix(sel)

    lane = lax.broadcasted_iota(I32, (rows, LANES), 1)
    sel2 = sel.reshape(rows, LANES)
    dist = jnp.where(sel2, lane - (ps.reshape(rows, LANES).astype(I32) - 1), 0)
    alive = sel2.astype(I32)
    val = lane
    gval = aff.reshape(rows, LANES)
    for k in range(7):
        s = 1 << k
        move = alive * ((dist >> k) & 1)
        inc = pltpu.roll(move, LANES - s, 1) * (lane < LANES - s).astype(I32) == 1
        val = jnp.where(inc, pltpu.roll(val, LANES - s, 1), val)
        gval = jnp.where(inc, pltpu.roll(gval, LANES - s, 1), gval)
        dist = jnp.where(inc, pltpu.roll(dist, LANES - s, 1), dist)
        alive = alive - move + inc.astype(I32)
    local = val.reshape(shape3)
    local_gate = gval.reshape(shape3)

    top = os_ + ss
    lane_e = lax.broadcasted_iota(I32, (N_EXPERTS, LANES), 1)
    for jc in range(cap // LANES):
        j = (lane_e + jc * LANES).astype(F32)
        acc = jnp.zeros((N_EXPERTS, LANES), I32)
        gacc = jnp.zeros((N_EXPERTS, LANES), F32)
        for b in range(nb):
            inside = (os_[b] <= j) & (j < top[b])
            jl = (j - os_[b]).astype(I32) & (LANES - 1)
            acc = jnp.where(inside, jnp.take_along_axis(local[b], jl, axis=1) + b * LANES, acc)
            gacc = jnp.where(inside, jnp.take_along_axis(local_gate[b], jl, axis=1), gacc)
        out_rows = pl.ds(jc, N_EXPERTS, stride=cap // LANES)
        src_ref[out_rows, :] = acc * HP_ROWS
        dst_ref[out_rows, :] = acc * Y_ROWS
        gate_ref[out_rows, :] = gacc


def _select(aff, cap):
    rows = aff.shape[0]
    out_rows = N_EXPERTS * cap // LANES
    out_spec = pl.BlockSpec((out_rows, LANES), lambda i: (0, 0))
    src, dst, gate = pl.pallas_call(
        functools.partial(_select_kernel, cap=cap),
        out_shape=[jax.ShapeDtypeStruct((out_rows, LANES), I32),
                   jax.ShapeDtypeStruct((out_rows, LANES), I32),
                   jax.ShapeDtypeStruct((out_rows, LANES), F32)],
        grid=(1,),
        in_specs=[pl.BlockSpec((rows, LANES), lambda i: (0, 0))],
        out_specs=[out_spec, out_spec, out_spec],
        compiler_params=_cparams(1),
        name="select",
    )(aff)
    return src.reshape(-1), dst.reshape(-1), gate


GATHER_UNROLL = 8


GATHER_STEP_ROWS = 2048


def _gather_kernel(src_ref, hp_ref, o_ref):
    i = pl.program_id(0)
    n = o_ref.shape[0] // HP_ROWS

    def gather(g, carry):
        base = i * n + g * GATHER_UNROLL
        for r in range(GATHER_UNROLL):
            src = pl.multiple_of(src_ref[base + r], HP_ROWS)
            dst = pl.multiple_of((g * GATHER_UNROLL + r) * HP_ROWS, HP_ROWS)
            o_ref[pl.ds(dst, HP_ROWS), :] = hp_ref[pl.ds(src, HP_ROWS), :]
        return carry

    lax.fori_loop(0, n // GATHER_UNROLL, gather, 0)


def _gather(src, hp):
    n = src.shape[0]
    return pl.pallas_call(
        _gather_kernel,
        out_shape=jax.ShapeDtypeStruct((n * HP_ROWS, LANES), U32),
        grid_spec=pltpu.PrefetchScalarGridSpec(
            num_scalar_prefetch=1,
            grid=(n // GATHER_STEP_ROWS,),
            in_specs=[pl.BlockSpec(hp.shape, lambda i, src: (0, 0), pipeline_mode=pl.Buffered(1))],
            out_specs=pl.BlockSpec((GATHER_STEP_ROWS * HP_ROWS, LANES), lambda i, src: (i, 0))),
        compiler_params=_cparams(1),
        name="gather",
    )(src, hp)


FF_CHUNK = 256


def _ffn_kernel(*refs, n_streams):
    e = pl.program_id(0)
    ins = refs[:2 * n_streams]
    wg_ref, wu_ref, wd_ref = refs[2 * n_streams:2 * n_streams + 3]
    outs = refs[2 * n_streams + 3:]
    d_ff = wg_ref.shape[3]
    eye = lax.broadcasted_iota(I32, (LANES, LANES), 0) == lax.broadcasted_iota(I32, (LANES, LANES), 1)
    for i in range(n_streams):
        xp_ref, gate_ref, y_ref = ins[2 * i], ins[2 * i + 1], outs[i]
        cap = xp_ref.shape[0] // HP_ROWS
        halves = [[], []]
        for s in range(HP_ROWS):
            w = xp_ref[pl.ds(s, cap, stride=HP_ROWS), :]
            for half in range(2):
                v = pltpu.unpack_elementwise(w, index=half, packed_dtype=BF16, unpacked_dtype=F32)
                halves[half].append(v.astype(BF16))
        xe = jnp.concatenate(halves[0] + halves[1], axis=1)
        y = None
        for c in range(0, d_ff, FF_CHUNK):
            a = _dot(xe, wg_ref[0, 0, :, c:c + FF_CHUNK].astype(BF16))
            u = _dot(xe, wu_ref[0, 0, :, c:c + FF_CHUNK].astype(BF16))
            yc = _dot((_silu(a) * u).astype(BF16), wd_ref[0, 0, c:c + FF_CHUNK, :].astype(BF16))
            y = yc if y is None else y + yc
        for c in range(cap // LANES):
            g_row = gate_ref[pl.ds(e * (cap // LANES) + c, 1), :]
            g_col = jnp.sum(jnp.where(eye, g_row, 0.0), axis=1, keepdims=True)
            tot = y[c * LANES:(c + 1) * LANES, :] * g_col
            for s in range(Y_ROWS):
                y_ref[0, pl.ds(c * LANES * Y_ROWS + s, LANES, stride=Y_ROWS), :] = tot[:, s * LANES:(s + 1) * LANES]


def _ffn(streams, layer, w_gate, w_up, w_down):
    caps = [xp.shape[0] // (N_EXPERTS * HP_ROWS) for xp, _ in streams]
    in_specs, args = [], []
    for (xp, gate), cap in zip(streams, caps):
        in_specs += [pl.BlockSpec((cap * HP_ROWS, LANES), lambda e: (e, 0)),
                     pl.BlockSpec(gate.shape, lambda e: (0, 0))]
        args += [xp, gate]
    w_spec = pl.BlockSpec((1, 1) + w_gate.shape[2:], lambda e: (layer, e, 0, 0))
    return pl.pallas_call(
        functools.partial(_ffn_kernel, n_streams=len(streams)),
        out_shape=[jax.ShapeDtypeStruct((N_EXPERTS, cap * Y_ROWS, LANES), F32) for cap in caps],
        grid=(N_EXPERTS,),
        in_specs=in_specs + [w_spec, w_spec, w_spec],
        out_specs=[pl.BlockSpec((1, cap * Y_ROWS, LANES), lambda e: (e, 0, 0)) for cap in caps],
        compiler_params=_cparams(1),
        name="ffn",
    )(*args, w_gate, w_up, w_down)


SCATTER_UNROLL = 16


def _combine_kernel(dst_ref, y_ref, x1_ref, mod_ref, o_ref, acc_ref, *, n_scatter):
    s = pl.program_id(0)
    tf = o_ref.shape[0]
    step_rows = y_ref.shape[0] // Y_ROWS

    @pl.when(s == 0)
    def _():
        acc_ref[...] = jnp.zeros_like(acc_ref)

    @pl.when(s < n_scatter)
    def _():
        def scatter(g, carry):
            base = s * step_rows + g * SCATTER_UNROLL
            rows, sums = [], []
            for r in range(SCATTER_UNROLL):
                dst = pl.ds(pl.multiple_of(dst_ref[base + r], Y_ROWS), Y_ROWS)
                src = pl.ds(pl.multiple_of((g * SCATTER_UNROLL + r) * Y_ROWS, Y_ROWS), Y_ROWS)
                rows.append(dst)
                sums.append(acc_ref[dst, :] + y_ref[src, :])
            for dst, v in zip(rows, sums):
                acc_ref[dst, :] = v
            return carry

        lax.fori_loop(0, step_rows // SCATTER_UNROLL, scatter, 0)

    @pl.when(s >= n_scatter)
    def _():
        g2 = mod_ref[0, 0][5:6]
        tile = acc_ref.at[pl.ds(pl.multiple_of((s - n_scatter) * tf * Y_ROWS, tf * Y_ROWS), tf * Y_ROWS), :]
        for c in range(Y_ROWS):
            sl = slice(c * LANES, (c + 1) * LANES)
            o_ref[:, sl] = x1_ref[:, sl] + g2[:, sl] * tile[pl.ds(c, tf, stride=Y_ROWS), :]


def _combine_tiles(n_tok):
    row_bytes = D_MODEL * 4
    free_rows = (VMEM_LIMIT - n_tok * row_bytes) * 4 // 5 // row_bytes
    scale = max(k for k in (1, 2, 4) if 2 * 512 * k + 4 * 256 * k <= free_rows)
    return 256 * scale, 512 * scale


def _combine(dst, y, x1, mods, layer, row_of_token):
    n_tok = x1.shape[0]
    tf, step_rows = _combine_tiles(n_tok)
    n_scatter = dst.shape[0] // step_rows
    assert (dst.shape[0] // N_EXPERTS) % SCATTER_UNROLL == 0 and dst.shape[0] % step_rows == 0
    tile_of = lambda s: jnp.maximum(s - n_scatter, 0)
    row_of_tile = lambda t: row_of_token(t * tf)
    return pl.pallas_call(
        functools.partial(_combine_kernel, n_scatter=n_scatter),
        out_shape=jax.ShapeDtypeStruct((n_tok, D_MODEL), F32),
        grid_spec=pltpu.PrefetchScalarGridSpec(
            num_scalar_prefetch=1,
            grid=(n_scatter + n_tok // tf,),
            in_specs=[pl.BlockSpec((step_rows * Y_ROWS, LANES), lambda s, *_: (jnp.minimum(s, n_scatter - 1), 0)),
                      pl.BlockSpec((tf, D_MODEL), lambda s, *_: (tile_of(s), 0)),
                      pl.BlockSpec((1, 1, N_MOD, D_MODEL), lambda s, *_: (layer, row_of_tile(tile_of(s)), 0, 0))],
            out_specs=pl.BlockSpec((tf, D_MODEL), lambda s, *_: (tile_of(s), 0)),
            scratch_shapes=[pltpu.VMEM((n_tok * Y_ROWS, LANES), F32)]),
        compiler_params=_cparams(1),
        name="combine",
    )(dst, y.reshape(-1, LANES), x1, mods)


TOKEN_TILE = 1024
Q_TILE = 512


def _moe(fronts, rows_of_token, mods, layer, w_gate, w_up, w_down):
    routed, dsts = [], []
    for x1, hp, aff in fronts:
        cap = CAPACITY_FACTOR * x1.shape[0] // N_EXPERTS
        src, dst, gate = _select(aff, cap)
        routed.append((_gather(src, hp), gate))
        dsts.append(dst)
    ys = _ffn(routed, layer, w_gate, w_up, w_down)
    return [_combine(dst, y, x1, mods, layer, row_of_token)
            for dst, y, (x1, _, _), row_of_token in zip(dsts, ys, fronts, rows_of_token)]


def kernel(x_prompt, x_sample, cache_k, cache_v, c, c_ctx, norm1, norm2, w_ada, b_ada, w_qkv, q_norm, k_norm,
           w_o, w_pool, pool_scale, w_router, w_e_gate, w_e_up, w_e_down):
    batch, seq, _ = x_prompt.shape
    dec_batch, dec_seq, _ = x_sample.shape
    depth = w_ada.shape[0]
    kvw = N_KV * HEAD_DIM

    cvec = jnp.zeros((SUBLANES, D_MODEL), F32).at[0].set(c_ctx).at[1:1 + dec_batch].set(c)
    mods = _ada(cvec, w_ada, b_ada)
    ctx = x_prompt.reshape(batch * seq, D_MODEL)
    lat = x_sample.reshape(dec_batch * dec_seq, D_MODEL)
    ctx_row = lambda i: 0
    lat_row = lambda tile: (lambda i: 1 + (i * tile) // dec_seq)
    ctx_tile = TOKEN_TILE if TOKEN_TILE % seq == 0 else min(TOKEN_TILE, seq)
    rope_tabs = _rope_tables(dec_seq)
    new_k = new_v = None

    for layer in range(depth):
        j = layer // 2
        n1 = norm1[layer][None]
        n2 = norm2[layer][None]
        wr = jnp.pad(w_router[layer], ((0, 0), (0, LANES - N_EXPERTS))).astype(BF16)
        if layer % 2 == 0:
            qn = q_norm[j][None]
            kn = k_norm[j][None]
            qc, kc, vc, new_k, new_v = _qkv(ctx, mods, layer, ctx_row, n1, w_qkv[j], qn, kn, None, True, TOKEN_TILE)
            oc = _attn_ctx(qc, kc, vc, seq, TOKEN_TILE)
            ql, kl, vl = _qkv(lat, mods, layer, lat_row(TOKEN_TILE), n1, w_qkv[j], qn, kn, rope_tabs, False,
                              TOKEN_TILE)
            past_k = cache_k[:, j].reshape(dec_batch, -1, kvw).astype(BF16)
            past_v = cache_v[:, j].reshape(dec_batch, -1, kvw).astype(BF16)
            ol = _attn_lat(ql, kl, vl, past_k, past_v, dec_seq, Q_TILE)
            ctx1 = _post(oc, ctx, mods, layer, ctx_row, n2, w_o[j], wr, TOKEN_TILE)
            lat1 = _post(ol, lat, mods, layer, lat_row(TOKEN_TILE), n2, w_o[j], wr, TOKEN_TILE)
        else:
            ps = pool_scale[j][None]
            ctx1 = _pool(ctx, mods, layer, ctx_row, n1, n2, w_pool[j], ps, wr, seq, ctx_tile)
            lat1 = _pool(lat, mods, layer, lat_row(TOKEN_TILE), n1, n2, w_pool[j], ps, wr, dec_seq, TOKEN_TILE)
        ctx, lat = _moe([ctx1, lat1], [ctx_row, lat_row(1)], mods, layer, w_e_gate, w_e_up, w_e_down)

    new_cache_k = new_k.reshape(batch, 1, seq, N_KV, HEAD_DIM)
    new_cache_v = new_v.reshape(batch, 1, seq, N_KV, HEAD_DIM)
    return (ctx.reshape(batch, seq, D_MODEL), lat.reshape(dec_batch, dec_seq, D_MODEL), new_cache_k, new_cache_v)
```

```python
import functools

import jax
import jax.numpy as jnp
import numpy as np
from jax import lax
from jax.experimental import pallas as pl
from jax.experimental.pallas import tpu as pltpu

F32 = jnp.float32
BF16 = jnp.bfloat16
I32 = jnp.int32
U32 = jnp.uint32

D_MODEL = 1024
HEAD_DIM = 128
N_HEADS = 8
N_KV = 2
Q_PER_KV = N_HEADS // N_KV
QKV_DIM = (N_HEADS + 2 * N_KV) * HEAD_DIM
GRID_W = 64
ROPE_THETA = 10000.0
POOL_WINDOWS = (2, 4, 8, 16)
POOL_GROUP = D_MODEL // len(POOL_WINDOWS)
POOL_HALO = 8
N_EXPERTS = 16
CAPACITY_FACTOR = 2
N_MOD = 6
EPS = 1e-6
LOG2_E = 1.4426950408889634

LANES = 128
SUBLANES = 8
HALF = D_MODEL // 2
HP_ROWS = HALF // LANES
Y_ROWS = D_MODEL // LANES
VMEM_LIMIT = 56 * 1024 * 1024


def _cparams(n_axes, vmem=VMEM_LIMIT):
    return pltpu.CompilerParams(dimension_semantics=("arbitrary",) * n_axes, vmem_limit_bytes=vmem)


def _silu(x):
    return x / (1.0 + jnp.exp(-x))


def _rms(x, gain):
    return x * lax.rsqrt(jnp.mean(x * x, axis=-1, keepdims=True) + EPS) * gain


def _norm_mod(x, gain, shift, scale):
    return _rms(x, gain * (1.0 + scale)) + shift


def _dot(a, b):
    return jnp.dot(a, b, preferred_element_type=F32)


def _dot_nt(a, b):
    return lax.dot_general(a, b, (((1,), (1,)), ((), ())), preferred_element_type=F32)


def _ada_kernel(c_ref, w_ref, b_ref, o_ref):
    s = _silu(c_ref[...]).astype(BF16)
    o_ref[0] = _dot(s, w_ref[0].astype(BF16)) + b_ref[0]


def _ada(cvec, w_ada, b_ada):
    depth = w_ada.shape[0]
    out = pl.pallas_call(
        _ada_kernel,
        out_shape=jax.ShapeDtypeStruct((depth, SUBLANES, N_MOD * D_MODEL), F32),
        grid=(depth, N_MOD),
        in_specs=[
            pl.BlockSpec((SUBLANES, D_MODEL), lambda i, j: (0, 0)),
            pl.BlockSpec((1, D_MODEL, D_MODEL), lambda i, j: (i, 0, j)),
            pl.BlockSpec((1, 1, D_MODEL), lambda i, j: (i, 0, j)),
        ],
        out_specs=pl.BlockSpec((1, SUBLANES, D_MODEL), lambda i, j: (i, 0, j)),
        compiler_params=_cparams(2),
        name="ada",
    )(cvec, w_ada, b_ada.reshape(depth, 1, N_MOD * D_MODEL))
    return out.reshape(depth, SUBLANES, N_MOD, D_MODEL)


def _mod_spec(layer, row_of_tile):
    return pl.BlockSpec((1, 1, N_MOD, D_MODEL), lambda i, *_: (layer, row_of_tile(i), 0, 0))


def _qkv_kernel(*refs, rope, cache_out):
    x_ref, mod_ref, n1_ref, w_ref, qn_ref, kn_ref = refs[:6]
    refs = refs[6:]
    if rope:
        cos_ref, sin_ref = refs[:2]
        refs = refs[2:]
    q_ref, k_ref, v_ref = refs[:3]
    refs = refs[3:]
    if cache_out:
        kc_ref, vc_ref = refs[:2]
        refs = refs[2:]
    (wb_ref,) = refs

    qk_w = (N_HEADS + N_KV) * HEAD_DIM
    quarter = HEAD_DIM // 4

    def partner(a):
        width = a.shape[1]
        first = (lax.broadcasted_iota(I32, a.shape, 1) & quarter) == 0
        return jnp.where(first, pltpu.roll(a, width - quarter, 1), pltpu.roll(a, quarter, 1))

    @pl.when(pl.program_id(0) == 0)
    def _():
        w = w_ref[...]
        wb_ref[:, 0:QKV_DIM] = w.astype(BF16)
        if rope:
            wb_ref[:, QKV_DIM:] = partner(w[:, 0:qk_w]).astype(BF16)

    m = mod_ref[0, 0]
    h = _norm_mod(x_ref[...], n1_ref[...], m[0:1], m[1:2])
    qkv = _dot(h.astype(BF16), wb_ref[...])
    scale = HEAD_DIM ** -0.5 * LOG2_E
    if rope:
        gains = {True: qn_ref[...], False: kn_ref[...]}
        cos_g = {key: cos_ref[...] * g for key, g in gains.items()}
        sin_g = {key: sin_ref[...] * partner(g) for key, g in gains.items()}
    for hh in range(N_HEADS + N_KV):
        sl = slice(hh * HEAD_DIM, (hh + 1) * HEAD_DIM)
        is_q = hh < N_HEADS
        if rope:
            raw = qkv[:, sl]
            norm = lax.rsqrt(jnp.mean(raw * raw, axis=-1, keepdims=True) + EPS)
            xh = (raw * cos_g[is_q] + qkv[:, QKV_DIM + hh * HEAD_DIM:QKV_DIM + (hh + 1) * HEAD_DIM] * sin_g[is_q]) * norm
        else:
            xh = _rms(qkv[:, sl], qn_ref[...] if is_q else kn_ref[...])
        if cache_out and hh >= N_HEADS:
            kc_ref[:, (hh - N_HEADS) * HEAD_DIM:(hh - N_HEADS + 1) * HEAD_DIM] = xh
        if hh < N_HEADS:
            q_ref[:, sl] = (xh * scale).astype(BF16)
        else:
            k_ref[:, (hh - N_HEADS) * HEAD_DIM:(hh - N_HEADS + 1) * HEAD_DIM] = xh.astype(BF16)
    v = qkv[:, qk_w:QKV_DIM]
    v_ref[...] = v.astype(BF16)
    if cache_out:
        vc_ref[...] = v


def _qkv(x, mods, layer, row_of_tile, n1, w_qkv, qn, kn, rope_tabs, cache_out, tm):
    n_tok = x.shape[0]
    kvw = N_KV * HEAD_DIM
    rope = rope_tabs is not None
    in_specs = [
        pl.BlockSpec((tm, D_MODEL), lambda i: (i, 0)),
        _mod_spec(layer, row_of_tile),
        pl.BlockSpec((1, D_MODEL), lambda i: (0, 0)),
        pl.BlockSpec((D_MODEL, QKV_DIM), lambda i: (0, 0)),
        pl.BlockSpec((1, HEAD_DIM), lambda i: (0, 0)),
        pl.BlockSpec((1, HEAD_DIM), lambda i: (0, 0)),
    ]
    args = [x, mods, n1, w_qkv, qn, kn]
    if rope:
        seq_tiles = rope_tabs[0].shape[0] // tm
        in_specs += [pl.BlockSpec((tm, HEAD_DIM), lambda i: (i % seq_tiles, 0))] * 2
        args += list(rope_tabs)
    out_shape = [jax.ShapeDtypeStruct((n_tok, D_MODEL), BF16),
                 jax.ShapeDtypeStruct((n_tok, kvw), BF16),
                 jax.ShapeDtypeStruct((n_tok, kvw), BF16)]
    out_specs = [pl.BlockSpec((tm, D_MODEL), lambda i: (i, 0)),
                 pl.BlockSpec((tm, kvw), lambda i: (i, 0)),
                 pl.BlockSpec((tm, kvw), lambda i: (i, 0))]
    if cache_out:
        out_shape += [jax.ShapeDtypeStruct((n_tok, kvw), F32)] * 2
        out_specs += [pl.BlockSpec((tm, kvw), lambda i: (i, 0))] * 2
    return pl.pallas_call(
        functools.partial(_qkv_kernel, rope=rope, cache_out=cache_out),
        out_shape=out_shape,
        grid=(n_tok // tm,),
        in_specs=in_specs,
        out_specs=out_specs,
        scratch_shapes=[pltpu.VMEM((D_MODEL, QKV_DIM + ((N_HEADS + N_KV) * HEAD_DIM if rope else 0)), BF16)],
        compiler_params=_cparams(1),
        name="qkv_rope" if rope else "qkv",
    )(*args)


def _rope_tables(seq_len):
    half = HEAD_DIM // 2
    n = half // 2
    inv_freq = ROPE_THETA ** (-np.arange(n, dtype=np.float64) / n)
    rows = seq_len // GRID_W
    row = np.repeat(np.arange(rows), GRID_W).astype(np.float64)
    col = np.tile(np.arange(GRID_W), rows).astype(np.float64)
    ang_r = row[:, None] * inv_freq[None, :]
    ang_c = col[:, None] * inv_freq[None, :]
    cos = np.concatenate([np.cos(ang_r)] * 2 + [np.cos(ang_c)] * 2, axis=-1)
    sin = np.concatenate([-np.sin(ang_r), np.sin(ang_r), -np.sin(ang_c), np.sin(ang_c)], axis=-1)
    return jnp.asarray(cos, F32), jnp.asarray(sin, F32)


def _with_ones(v):
    return jnp.concatenate([v, jnp.ones_like(v)], axis=1)


def _gqa_attention(q, segments):
    rows = q.shape[0]
    qs = jnp.concatenate([q[:, h * HEAD_DIM:(h + 1) * HEAD_DIM] for h in range(Q_PER_KV)], axis=0)
    m = jnp.full((rows * Q_PER_KV, 1), -1e30, F32)
    acc = jnp.zeros((rows * Q_PER_KV, 2 * HEAD_DIM), F32)
    for k, v in segments:
        s = _dot_nt(qs, k)
        m_new = jnp.maximum(m, jnp.max(s, axis=-1, keepdims=True))
        acc = acc * jnp.exp2(m - m_new) + _dot(jnp.exp2(s - m_new).astype(BF16), v)
        m = m_new
    o = (acc[:, :HEAD_DIM] / acc[:, HEAD_DIM:]).astype(BF16)
    return jnp.concatenate([o[h * rows:(h + 1) * rows] for h in range(Q_PER_KV)], axis=1)


def _attn_ctx_kernel(q_ref, k_ref, v_ref, o_ref, *, seq):
    gw = Q_PER_KV * HEAD_DIM
    for r in range(0, q_ref.shape[0], seq):
        for g in range(N_KV):
            kg = k_ref[r:r + seq, g * HEAD_DIM:(g + 1) * HEAD_DIM]
            vg = _with_ones(v_ref[r:r + seq, g * HEAD_DIM:(g + 1) * HEAD_DIM])
            o_ref[r:r + seq, g * gw:(g + 1) * gw] = _gqa_attention(q_ref[r:r + seq, g * gw:(g + 1) * gw], [(kg, vg)])


def _attn_ctx(q, k, v, seq, tm):
    n_tok = q.shape[0]
    kvw = N_KV * HEAD_DIM
    return pl.pallas_call(
        functools.partial(_attn_ctx_kernel, seq=seq),
        out_shape=jax.ShapeDtypeStruct((n_tok, D_MODEL), BF16),
        grid=(n_tok // tm,),
        in_specs=[pl.BlockSpec((tm, D_MODEL), lambda b: (b, 0)),
                  pl.BlockSpec((tm, kvw), lambda b: (b, 0)),
                  pl.BlockSpec((tm, kvw), lambda b: (b, 0))],
        out_specs=pl.BlockSpec((tm, D_MODEL), lambda b: (b, 0)),
        compiler_params=_cparams(1),
        name="attn_ctx",
    )(q, k, v)


KEY_CHUNK = 256


def _attn_lat_kernel(q_ref, k_ref, v_ref, kc_ref, vc_ref, o_ref):
    segments = [(k_ref[c:c + KEY_CHUNK], _with_ones(v_ref[c:c + KEY_CHUNK]))
                for c in range(0, k_ref.shape[0], KEY_CHUNK)]
    segments.append((kc_ref[0], _with_ones(vc_ref[0])))
    o_ref[...] = _gqa_attention(q_ref[...], segments)


def _attn_lat(q, k, v, kc, vc, seq, tq):
    n_tok = q.shape[0]
    batch = n_tok // seq
    past = kc.shape[1]
    qt = seq // tq
    gw = Q_PER_KV * HEAD_DIM
    return pl.pallas_call(
        _attn_lat_kernel,
        out_shape=jax.ShapeDtypeStruct((n_tok, D_MODEL), BF16),
        grid=(batch, N_KV, qt),
        in_specs=[pl.BlockSpec((tq, gw), lambda b, g, i: (b * qt + i, g)),
                  pl.BlockSpec((seq, HEAD_DIM), lambda b, g, i: (b, g)),
                  pl.BlockSpec((seq, HEAD_DIM), lambda b, g, i: (b, g)),
                  pl.BlockSpec((1, past, HEAD_DIM), lambda b, g, i: (b, 0, g)),
                  pl.BlockSpec((1, past, HEAD_DIM), lambda b, g, i: (b, 0, g))],
        out_specs=pl.BlockSpec((tq, gw), lambda b, g, i: (b * qt + i, g)),
        compiler_params=_cparams(3),
        name="attn_lat",
    )(q, k, v, kc, vc)


def _router_softmax(hb, wr):
    logits = _dot(hb, wr)
    lane = lax.broadcasted_iota(I32, logits.shape, 1)
    logits = jnp.where(lane < N_EXPERTS, logits, -1e30)
    ex = jnp.exp(logits - jnp.max(logits, axis=-1, keepdims=True))
    return ex / jnp.sum(ex, axis=-1, keepdims=True)


def _moe_front(x1, m, n2, wr, x1_ref, hp_ref, aff_ref):
    tm = x1.shape[0]
    x1_ref[...] = x1
    h2 = _norm_mod(x1, n2, m[3:4], m[4:5])
    aff = _router_softmax(h2.astype(BF16), wr)
    packed = pltpu.pack_elementwise([h2[:, :HALF], h2[:, HALF:]], packed_dtype=BF16)
    for s in range(HP_ROWS):
        hp_ref[pl.ds(s, tm, stride=HP_ROWS), :] = packed[:, s * LANES:(s + 1) * LANES]
    aff_t = aff.T
    for b in range(tm // LANES):
        aff_ref[b * N_EXPERTS:(b + 1) * N_EXPERTS, :] = aff_t[0:N_EXPERTS, b * LANES:(b + 1) * LANES]


def _front_out(n_tok, tm):
    shapes = [jax.ShapeDtypeStruct((n_tok, D_MODEL), F32),
              jax.ShapeDtypeStruct((n_tok * HP_ROWS, LANES), U32),
              jax.ShapeDtypeStruct((n_tok // LANES * N_EXPERTS, LANES), F32)]
    specs = [pl.BlockSpec((tm, D_MODEL), lambda i: (i, 0)),
             pl.BlockSpec((tm * HP_ROWS, LANES), lambda i: (i, 0)),
             pl.BlockSpec((tm // LANES * N_EXPERTS, LANES), lambda i: (i, 0))]
    return shapes, specs


def _post_kernel(o_ref, x_ref, mod_ref, n2_ref, wo_ref, wr_ref, x1_ref, hp_ref, aff_ref, wb_ref):
    @pl.when(pl.program_id(0) == 0)
    def _():
        wb_ref[...] = wo_ref[...].astype(BF16)

    m = mod_ref[0, 0]
    x1 = x_ref[...] + m[2:3] * _dot(o_ref[...], wb_ref[...])
    _moe_front(x1, m, n2_ref[...], wr_ref[...], x1_ref, hp_ref, aff_ref)


def _post(o, x, mods, layer, row_of_tile, n2, w_o, wr, tm):
    n_tok = x.shape[0]
    shapes, specs = _front_out(n_tok, tm)
    return pl.pallas_call(
        _post_kernel,
        out_shape=shapes,
        grid=(n_tok // tm,),
        in_specs=[pl.BlockSpec((tm, D_MODEL), lambda i: (i, 0)),
                  pl.BlockSpec((tm, D_MODEL), lambda i: (i, 0)),
                  _mod_spec(layer, row_of_tile),
                  pl.BlockSpec((1, D_MODEL), lambda i: (0, 0)),
                  pl.BlockSpec((D_MODEL, D_MODEL), lambda i: (0, 0)),
                  pl.BlockSpec((D_MODEL, LANES), lambda i: (0, 0))],
        out_specs=specs,
        scratch_shapes=[pltpu.VMEM((D_MODEL, D_MODEL), BF16)],
        compiler_params=_cparams(1),
        name="post",
    )(o, x, mods, n2, w_o, wr)


def _pool_kernel(x_ref, xp_ref, xn_ref, mod_ref, n1_ref, n2_ref, wp_ref, ps_ref, wr_ref,
                 x1_ref, hp_ref, aff_ref, *, seq):
    tm = x_ref.shape[0]
    i = pl.program_id(0)
    m = mod_ref[0, 0]
    x = x_ref[...]

    def norm_mod(v):
        return _norm_mod(v, n1_ref[...], m[0:1], m[1:2])

    h = norm_mod(x)
    if tm <= seq:
        prev_ok = ((i * tm) % seq != 0).astype(F32)
        next_ok = (((i + 1) * tm) % seq != 0).astype(F32)
        pieces = [(norm_mod(xp_ref[...]) * prev_ok, h, norm_mod(xn_ref[...]) * next_ok)]
        t0 = (i * tm) % seq
    else:
        zeros = jnp.zeros((POOL_HALO, D_MODEL), F32)
        pieces = [(zeros, h[r:r + seq], zeros) for r in range(0, tm, seq)]
        t0 = 0
    n = min(tm, seq)
    rows = n + 2 * POOL_HALO
    t = t0 + lax.broadcasted_iota(I32, (n, 1), 0)
    deltas = [[] for _ in POOL_WINDOWS]
    for piece in pieces:
        hz = jnp.concatenate(piece, axis=0)
        for g, w in enumerate(POOL_WINDOWS):
            sl = slice(g * POOL_GROUP, (g + 1) * POOL_GROUP)
            f = hz[:, sl]
            step = 1
            while step < w:
                f = f + pltpu.roll(f, rows - step, 0)
                step *= 2
            win = pltpu.roll(f, w // 2, 0)[POOL_HALO:POOL_HALO + n]
            cnt = (jnp.minimum(t + w // 2, seq) - jnp.maximum(t - w // 2, 0)).astype(F32)
            deltas[g].append((win / cnt - piece[1][:, sl]).astype(BF16))
    ys = [_dot(jnp.concatenate(d, axis=0), wp_ref[g].astype(BF16)) for g, d in enumerate(deltas)]
    x1 = x + (m[2:3] * ps_ref[...]) * jnp.concatenate(ys, axis=-1)
    _moe_front(x1, m, n2_ref[...], wr_ref[...], x1_ref, hp_ref, aff_ref)


def _pool(x, mods, layer, row_of_tile, n1, n2, w_pool, pool_scale, wr, seq, tm):
    n_tok = x.shape[0]
    hb = tm // POOL_HALO
    last = n_tok // POOL_HALO - 1
    shapes, specs = _front_out(n_tok, tm)
    return pl.pallas_call(
        functools.partial(_pool_kernel, seq=seq),
        out_shape=shapes,
        grid=(n_tok // tm,),
        in_specs=[pl.BlockSpec((tm, D_MODEL), lambda i: (i, 0)),
                  pl.BlockSpec((POOL_HALO, D_MODEL), lambda i: (jnp.maximum(i * hb - 1, 0), 0)),
                  pl.BlockSpec((POOL_HALO, D_MODEL), lambda i: (jnp.minimum((i + 1) * hb, last), 0)),
                  _mod_spec(layer, row_of_tile),
                  pl.BlockSpec((1, D_MODEL), lambda i: (0, 0)),
                  pl.BlockSpec((1, D_MODEL), lambda i: (0, 0)),
                  pl.BlockSpec(w_pool.shape, lambda i: (0, 0, 0)),
                  pl.BlockSpec((1, D_MODEL), lambda i: (0, 0)),
                  pl.BlockSpec((D_MODEL, LANES), lambda i: (0, 0))],
        out_specs=specs,
        compiler_params=_cparams(1),
        name="pool",
    )(x, x, x, mods, n1, n2, w_pool, pool_scale, wr)


def _select_kernel(aff_ref, src_ref, dst_ref, gate_ref, *, cap):
    nb = aff_ref.shape[0] // N_EXPERTS
    rows = nb * N_EXPERTS
    shape3 = (nb, N_EXPERTS, LANES)
    aff = aff_ref[...].reshape(shape3)

    def count(mask):
        return jnp.sum(jnp.sum(mask.astype(F32), axis=0), axis=-1, keepdims=True)

    def search(i, t):
        cand = t | (jnp.int32(1) << (29 - i))
        return jnp.where(count(aff >= pltpu.bitcast(cand, F32)[None]) >= cap, cand, t)

    thr = pltpu.bitcast(lax.fori_loop(0, 30, search, jnp.zeros((N_EXPERTS, LANES), I32)), F32)
    gt = aff > thr[None]
    eq = aff == thr[None]
    need = cap - count(gt)

    kk = lax.broadcasted_iota(I32, (LANES, LANES), 0)
    nn = lax.broadcasted_iota(I32, (LANES, LANES), 1)
    upper = (kk <= nn).astype(BF16)
    ones = jnp.ones((LANES, LANES), BF16)

    def prefix(mask):
        m2 = mask.astype(F32).astype(BF16).reshape(rows, LANES)
        p = _dot(m2, upper).reshape(shape3)
        s = _dot(m2, ones).reshape(shape3)
        offs = []
        run = jnp.zeros((N_EXPERTS, LANES), F32)
        for b in range(nb):
            offs.append(run)
            run = run + s[b]
        return p, jnp.stack(offs, axis=0), s

    pe, oe, _ = prefix(eq)
    sel = gt | (eq & ((pe + oe) <= need[None]))
    ps, os_, ss = prefix(sel)

    lane = lax.broadcasted_iota(I32, (rows, LANES), 1)
    sel2 = sel.reshape(rows, LANES)
    dist = jnp.where(sel2, lane - (ps.reshape(rows, LANES).astype(I32) - 1), 0)
    alive = sel2.astype(I32)
    val = lane
    gval = aff.reshape(rows, LANES)
    for k in range(7):
        s = 1 << k
        move = alive * ((dist >> k) & 1)
        inc = pltpu.roll(move, LANES - s, 1) * (lane < LANES - s).astype(I32) == 1
        val = jnp.where(inc, pltpu.roll(val, LANES - s, 1), val)
        gval = jnp.where(inc, pltpu.roll(gval, LANES - s, 1), gval)
        dist = jnp.where(inc, pltpu.roll(dist, LANES - s, 1), dist)
        alive = alive - move + inc.astype(I32)
    local = val.reshape(shape3)
    local_gate = gval.reshape(shape3)

    top = os_ + ss
    lane_e = lax.broadcasted_iota(I32, (N_EXPERTS, LANES), 1)
    for jc in range(cap // LANES):
        j = (lane_e + jc * LANES).astype(F32)
        acc = jnp.zeros((N_EXPERTS, LANES), I32)
        gacc = jnp.zeros((N_EXPERTS, LANES), F32)
        for b in range(nb):
            inside = (os_[b] <= j) & (j < top[b])
            jl = (j - os_[b]).astype(I32) & (LANES - 1)
            acc = jnp.where(inside, jnp.take_along_axis(local[b], jl, axis=1) + b * LANES, acc)
            gacc = jnp.where(inside, jnp.take_along_axis(local_gate[b], jl, axis=1), gacc)
        out_rows = pl.ds(jc, N_EXPERTS, stride=cap // LANES)
        src_ref[out_rows, :] = acc * HP_ROWS
        dst_ref[out_rows, :] = acc * Y_ROWS
        gate_ref[out_rows, :] = gacc


def _select(aff, cap):
    rows = aff.shape[0]
    out_rows = N_EXPERTS * cap // LANES
    out_spec = pl.BlockSpec((out_rows, LANES), lambda i: (0, 0))
    src, dst, gate = pl.pallas_call(
        functools.partial(_select_kernel, cap=cap),
        out_shape=[jax.ShapeDtypeStruct((out_rows, LANES), I32),
                   jax.ShapeDtypeStruct((out_rows, LANES), I32),
                   jax.ShapeDtypeStruct((out_rows, LANES), F32)],
        grid=(1,),
        in_specs=[pl.BlockSpec((rows, LANES), lambda i: (0, 0))],
        out_specs=[out_spec, out_spec, out_spec],
        compiler_params=_cparams(1),
        name="select",
    )(aff)
    return src.reshape(-1), dst.reshape(-1), gate


GATHER_UNROLL = 32


GATHER_STEP_ROWS = 2048


def _gather_kernel(src_ref, hp_ref, o_ref):
    i = pl.program_id(0)
    n = o_ref.shape[0] // HP_ROWS

    def gather(g, carry):
        base = i * n + g * GATHER_UNROLL
        for r in range(GATHER_UNROLL):
            src = pl.multiple_of(src_ref[base + r], HP_ROWS)
            dst = pl.multiple_of((g * GATHER_UNROLL + r) * HP_ROWS, HP_ROWS)
            o_ref[pl.ds(dst, HP_ROWS), :] = hp_ref[pl.ds(src, HP_ROWS), :]
        return carry

    lax.fori_loop(0, n // GATHER_UNROLL, gather, 0)


def _gather(src, hp):
    n = src.shape[0]
    return pl.pallas_call(
        _gather_kernel,
        out_shape=jax.ShapeDtypeStruct((n * HP_ROWS, LANES), U32),
        grid_spec=pltpu.PrefetchScalarGridSpec(
            num_scalar_prefetch=1,
            grid=(n // GATHER_STEP_ROWS,),
            in_specs=[pl.BlockSpec(hp.shape, lambda i, src: (0, 0), pipeline_mode=pl.Buffered(1))],
            out_specs=pl.BlockSpec((GATHER_STEP_ROWS * HP_ROWS, LANES), lambda i, src: (i, 0))),
        compiler_params=_cparams(1),
        name="gather",
    )(src, hp)


FF_CHUNK = 256


def _ffn_kernel(*refs, n_streams):
    e = pl.program_id(0)
    ins = refs[:2 * n_streams]
    wg_ref, wu_ref, wd_ref = refs[2 * n_streams:2 * n_streams + 3]
    outs = refs[2 * n_streams + 3:]
    d_ff = wg_ref.shape[3]
    eye = lax.broadcasted_iota(I32, (LANES, LANES), 0) == lax.broadcasted_iota(I32, (LANES, LANES), 1)
    for i in range(n_streams):
        xp_ref, gate_ref, y_ref = ins[2 * i], ins[2 * i + 1], outs[i]
        cap = xp_ref.shape[0] // HP_ROWS
        halves = [[], []]
        for s in range(HP_ROWS):
            w = xp_ref[pl.ds(s, cap, stride=HP_ROWS), :]
            for half in range(2):
                v = pltpu.unpack_elementwise(w, index=half, packed_dtype=BF16, unpacked_dtype=F32)
                halves[half].append(v.astype(BF16))
        xe = jnp.concatenate(halves[0] + halves[1], axis=1)
        y = None
        for c in range(0, d_ff, FF_CHUNK):
            a = _dot(xe, wg_ref[0, 0, :, c:c + FF_CHUNK].astype(BF16))
            u = _dot(xe, wu_ref[0, 0, :, c:c + FF_CHUNK].astype(BF16))
            yc = _dot((_silu(a) * u).astype(BF16), wd_ref[0, 0, c:c + FF_CHUNK, :].astype(BF16))
            y = yc if y is None else y + yc
        for c in range(cap // LANES):
            g_row = gate_ref[pl.ds(e * (cap // LANES) + c, 1), :]
            g_col = jnp.sum(jnp.where(eye, g_row, 0.0), axis=1, keepdims=True)
            tot = y[c * LANES:(c + 1) * LANES, :] * g_col
            for s in range(Y_ROWS):
                y_ref[0, pl.ds(c * LANES * Y_ROWS + s, LANES, stride=Y_ROWS), :] = tot[:, s * LANES:(s + 1) * LANES]


def _ffn(streams, layer, w_gate, w_up, w_down):
    caps = [xp.shape[0] // (N_EXPERTS * HP_ROWS) for xp, _ in streams]
    in_specs, args = [], []
    for (xp, gate), cap in zip(streams, caps):
        in_specs += [pl.BlockSpec((cap * HP_ROWS, LANES), lambda e: (e, 0)),
                     pl.BlockSpec(gate.shape, lambda e: (0, 0))]
        args += [xp, gate]
    w_spec = pl.BlockSpec((1, 1) + w_gate.shape[2:], lambda e: (layer, e, 0, 0))
    return pl.pallas_call(
        functools.partial(_ffn_kernel, n_streams=len(streams)),
        out_shape=[jax.ShapeDtypeStruct((N_EXPERTS, cap * Y_ROWS, LANES), F32) for cap in caps],
        grid=(N_EXPERTS,),
        in_specs=in_specs + [w_spec, w_spec, w_spec],
        out_specs=[pl.BlockSpec((1, cap * Y_ROWS, LANES), lambda e: (e, 0, 0)) for cap in caps],
        compiler_params=_cparams(1),
        name="ffn",
    )(*args, w_gate, w_up, w_down)


SCATTER_UNROLL = 16
SCATTER_GROUPS = 2


def _combine_kernel(dst_ref, y_ref, x1_ref, mod_ref, o_ref, acc_ref, *, n_scatter):
    s = pl.program_id(0)
    tf = o_ref.shape[0]
    step_rows = y_ref.shape[0] // Y_ROWS

    @pl.when(s == 0)
    def _():
        acc_ref[...] = jnp.zeros_like(acc_ref)

    @pl.when(s < n_scatter)
    def _():
        def scatter(g, carry):
            for sub in range(SCATTER_GROUPS):
                first = (g * SCATTER_GROUPS + sub) * SCATTER_UNROLL
                base = s * step_rows + first
                rows, sums = [], []
                for r in range(SCATTER_UNROLL):
                    dst = pl.ds(pl.multiple_of(dst_ref[base + r], Y_ROWS), Y_ROWS)
                    src = pl.ds(pl.multiple_of((first + r) * Y_ROWS, Y_ROWS), Y_ROWS)
                    rows.append(dst)
                    sums.append(acc_ref[dst, :] + y_ref[src, :])
                for dst, v in zip(rows, sums):
                    acc_ref[dst, :] = v
            return carry

        lax.fori_loop(0, step_rows // (SCATTER_UNROLL * SCATTER_GROUPS), scatter, 0)

    @pl.when(s >= n_scatter)
    def _():
        g2 = mod_ref[0, 0][5:6]
        tile = acc_ref.at[pl.ds(pl.multiple_of((s - n_scatter) * tf * Y_ROWS, tf * Y_ROWS), tf * Y_ROWS), :]
        for c in range(Y_ROWS):
            sl = slice(c * LANES, (c + 1) * LANES)
            o_ref[:, sl] = x1_ref[:, sl] + g2[:, sl] * tile[pl.ds(c, tf, stride=Y_ROWS), :]


def _combine_tiles(n_tok):
    row_bytes = D_MODEL * 4
    free_rows = (VMEM_LIMIT - n_tok * row_bytes) * 4 // 5 // row_bytes
    scale = max(k for k in (1, 2, 4) if 2 * 512 * k + 4 * 256 * k <= free_rows)
    return 256 * scale, 512 * scale


def _combine(dst, y, x1, mods, layer, row_of_token):
    n_tok = x1.shape[0]
    tf, step_rows = _combine_tiles(n_tok)
    n_scatter = dst.shape[0] // step_rows
    assert (dst.shape[0] // N_EXPERTS) % SCATTER_UNROLL == 0 and dst.shape[0] % step_rows == 0
    tile_of = lambda s: jnp.maximum(s - n_scatter, 0)
    row_of_tile = lambda t: row_of_token(t * tf)
    return pl.pallas_call(
        functools.partial(_combine_kernel, n_scatter=n_scatter),
        out_shape=jax.ShapeDtypeStruct((n_tok, D_MODEL), F32),
        grid_spec=pltpu.PrefetchScalarGridSpec(
            num_scalar_prefetch=1,
            grid=(n_scatter + n_tok // tf,),
            in_specs=[pl.BlockSpec((step_rows * Y_ROWS, LANES), lambda s, *_: (jnp.minimum(s, n_scatter - 1), 0)),
                      pl.BlockSpec((tf, D_MODEL), lambda s, *_: (tile_of(s), 0)),
                      pl.BlockSpec((1, 1, N_MOD, D_MODEL), lambda s, *_: (layer, row_of_tile(tile_of(s)), 0, 0))],
            out_specs=pl.BlockSpec((tf, D_MODEL), lambda s, *_: (tile_of(s), 0)),
            scratch_shapes=[pltpu.VMEM((n_tok * Y_ROWS, LANES), F32)]),
        compiler_params=_cparams(1),
        name="combine",
    )(dst, y.reshape(-1, LANES), x1, mods)


TOKEN_TILE = 1024
Q_TILE = 512


def _moe(fronts, rows_of_token, mods, layer, w_gate, w_up, w_down):
    routed, dsts = [], []
    for x1, hp, aff in fronts:
        cap = CAPACITY_FACTOR * x1.shape[0] // N_EXPERTS
        src, dst, gate = _select(aff, cap)
        routed.append((_gather(src, hp), gate))
        dsts.append(dst)
    ys = _ffn(routed, layer, w_gate, w_up, w_down)
    return [_combine(dst, y, x1, mods, layer, row_of_token)
            for dst, y, (x1, _, _), row_of_token in zip(dsts, ys, fronts, rows_of_token)]


def kernel(x_prompt, x_sample, cache_k, cache_v, c, c_ctx, norm1, norm2, w_ada, b_ada, w_qkv, q_norm, k_norm,
           w_o, w_pool, pool_scale, w_router, w_e_gate, w_e_up, w_e_down):
    batch, seq, _ = x_prompt.shape
    dec_batch, dec_seq, _ = x_sample.shape
    depth = w_ada.shape[0]
    kvw = N_KV * HEAD_DIM

    cvec = jnp.zeros((SUBLANES, D_MODEL), F32).at[0].set(c_ctx).at[1:1 + dec_batch].set(c)
    mods = _ada(cvec, w_ada, b_ada)
    ctx = x_prompt.reshape(batch * seq, D_MODEL)
    lat = x_sample.reshape(dec_batch * dec_seq, D_MODEL)
    ctx_row = lambda i: 0
    lat_row = lambda tile: (lambda i: 1 + (i * tile) // dec_seq)
    ctx_tile = TOKEN_TILE if TOKEN_TILE % seq == 0 else min(TOKEN_TILE, seq)
    rope_tabs = _rope_tables(dec_seq)
    new_k = new_v = None

    for layer in range(depth):
        j = layer // 2
        n1 = norm1[layer][None]
        n2 = norm2[layer][None]
        wr = jnp.pad(w_router[layer], ((0, 0), (0, LANES - N_EXPERTS))).astype(BF16)
        if layer % 2 == 0:
            qn = q_norm[j][None]
            kn = k_norm[j][None]
            qc, kc, vc, new_k, new_v = _qkv(ctx, mods, layer, ctx_row, n1, w_qkv[j], qn, kn, None, True, TOKEN_TILE)
            oc = _attn_ctx(qc, kc, vc, seq, TOKEN_TILE)
            ql, kl, vl = _qkv(lat, mods, layer, lat_row(TOKEN_TILE), n1, w_qkv[j], qn, kn, rope_tabs, False,
                              TOKEN_TILE)
            past_k = cache_k[:, j].reshape(dec_batch, -1, kvw).astype(BF16)
            past_v = cache_v[:, j].reshape(dec_batch, -1, kvw).astype(BF16)
            ol = _attn_lat(ql, kl, vl, past_k, past_v, dec_seq, Q_TILE)
            ctx1 = _post(oc, ctx, mods, layer, ctx_row, n2, w_o[j], wr, TOKEN_TILE)
            lat1 = _post(ol, lat, mods, layer, lat_row(TOKEN_TILE), n2, w_o[j], wr, TOKEN_TILE)
        else:
            ps = pool_scale[j][None]
            ctx1 = _pool(ctx, mods, layer, ctx_row, n1, n2, w_pool[j], ps, wr, seq, ctx_tile)
            lat1 = _pool(lat, mods, layer, lat_row(TOKEN_TILE), n1, n2, w_pool[j], ps, wr, dec_seq, TOKEN_TILE)
        ctx, lat = _moe([ctx1, lat1], [ctx_row, lat_row(1)], mods, layer, w_e_gate, w_e_up, w_e_down)

    new_cache_k = new_k.reshape(batch, 1, seq, N_KV, HEAD_DIM)
    new_cache_v = new_v.reshape(batch, 1, seq, N_KV, HEAD_DIM)
    return (ctx.reshape(batch, seq, D_MODEL), lat.reshape(dec_batch, dec_seq, D_MODEL), new_cache_k, new_cache_v)
```

```python
import functools

import jax
import jax.numpy as jnp
import numpy as np
from jax import lax
from jax.experimental import pallas as pl
from jax.experimental.pallas import tpu as pltpu

F32 = jnp.float32
BF16 = jnp.bfloat16
I32 = jnp.int32
U32 = jnp.uint32

D_MODEL = 1024
HEAD_DIM = 128
N_HEADS = 8
N_KV = 2
Q_PER_KV = N_HEADS // N_KV
QKV_DIM = (N_HEADS + 2 * N_KV) * HEAD_DIM
GRID_W = 64
ROPE_THETA = 10000.0
POOL_WINDOWS = (2, 4, 8, 16)
POOL_GROUP = D_MODEL // len(POOL_WINDOWS)
POOL_HALO = 8
N_EXPERTS = 16
CAPACITY_FACTOR = 2
N_MOD = 6
EPS = 1e-6
LOG2_E = 1.4426950408889634

LANES = 128
SUBLANES = 8
HALF = D_MODEL // 2
HP_ROWS = HALF // LANES
Y_ROWS = D_MODEL // LANES
VMEM_LIMIT = 56 * 1024 * 1024


def _cparams(n_axes, vmem=VMEM_LIMIT):
    return pltpu.CompilerParams(dimension_semantics=("arbitrary",) * n_axes, vmem_limit_bytes=vmem)


def _silu(x):
    return x / (1.0 + jnp.exp(-x))


def _rms(x, gain):
    return x * lax.rsqrt(jnp.mean(x * x, axis=-1, keepdims=True) + EPS) * gain


def _norm_mod(x, gain, shift, scale):
    return _rms(x, gain * (1.0 + scale)) + shift


def _dot(a, b):
    return jnp.dot(a, b, preferred_element_type=F32)


def _dot_nt(a, b):
    return lax.dot_general(a, b, (((1,), (1,)), ((), ())), preferred_element_type=F32)


def _ada_kernel(c_ref, w_ref, b_ref, o_ref):
    s = _silu(c_ref[...]).astype(BF16)
    o_ref[0] = _dot(s, w_ref[0].astype(BF16)) + b_ref[0]


def _ada(cvec, w_ada, b_ada):
    depth = w_ada.shape[0]
    out = pl.pallas_call(
        _ada_kernel,
        out_shape=jax.ShapeDtypeStruct((depth, SUBLANES, N_MOD * D_MODEL), F32),
        grid=(depth, N_MOD),
        in_specs=[
            pl.BlockSpec((SUBLANES, D_MODEL), lambda i, j: (0, 0)),
            pl.BlockSpec((1, D_MODEL, D_MODEL), lambda i, j: (i, 0, j)),
            pl.BlockSpec((1, 1, D_MODEL), lambda i, j: (i, 0, j)),
        ],
        out_specs=pl.BlockSpec((1, SUBLANES, D_MODEL), lambda i, j: (i, 0, j)),
        compiler_params=_cparams(2),
        name="ada",
    )(cvec, w_ada, b_ada.reshape(depth, 1, N_MOD * D_MODEL))
    return out.reshape(depth, SUBLANES, N_MOD, D_MODEL)


def _mod_spec(layer, row_of_tile):
    return pl.BlockSpec((1, 1, N_MOD, D_MODEL), lambda i, *_: (layer, row_of_tile(i), 0, 0))


def _qkv_kernel(*refs, rope, cache_out):
    x_ref, mod_ref, n1_ref, w_ref, qn_ref, kn_ref = refs[:6]
    refs = refs[6:]
    if rope:
        cos_ref, sin_ref = refs[:2]
        refs = refs[2:]
    q_ref, k_ref, v_ref = refs[:3]
    refs = refs[3:]
    if cache_out:
        kc_ref, vc_ref = refs[:2]
        refs = refs[2:]
    (wb_ref,) = refs

    qk_w = (N_HEADS + N_KV) * HEAD_DIM
    quarter = HEAD_DIM // 4

    def partner(a):
        width = a.shape[1]
        first = (lax.broadcasted_iota(I32, a.shape, 1) & quarter) == 0
        return jnp.where(first, pltpu.roll(a, width - quarter, 1), pltpu.roll(a, quarter, 1))

    @pl.when(pl.program_id(0) == 0)
    def _():
        w = w_ref[...]
        wb_ref[:, 0:QKV_DIM] = w.astype(BF16)
        if rope:
            wb_ref[:, QKV_DIM:] = partner(w[:, 0:qk_w]).astype(BF16)

    m = mod_ref[0, 0]
    h = _norm_mod(x_ref[...], n1_ref[...], m[0:1], m[1:2])
    qkv = _dot(h.astype(BF16), wb_ref[...])
    scale = HEAD_DIM ** -0.5 * LOG2_E
    if rope:
        gains = {True: qn_ref[...], False: kn_ref[...]}
        cos_g = {key: cos_ref[...] * g for key, g in gains.items()}
        sin_g = {key: sin_ref[...] * partner(g) for key, g in gains.items()}
    for hh in range(N_HEADS + N_KV):
        sl = slice(hh * HEAD_DIM, (hh + 1) * HEAD_DIM)
        is_q = hh < N_HEADS
        if rope:
            raw = qkv[:, sl]
            norm = lax.rsqrt(jnp.mean(raw * raw, axis=-1, keepdims=True) + EPS)
            xh = (raw * cos_g[is_q] + qkv[:, QKV_DIM + hh * HEAD_DIM:QKV_DIM + (hh + 1) * HEAD_DIM] * sin_g[is_q]) * norm
        else:
            xh = _rms(qkv[:, sl], qn_ref[...] if is_q else kn_ref[...])
        if cache_out and hh >= N_HEADS:
            kc_ref[:, (hh - N_HEADS) * HEAD_DIM:(hh - N_HEADS + 1) * HEAD_DIM] = xh
        if hh < N_HEADS:
            q_ref[:, sl] = (xh * scale).astype(BF16)
        else:
            k_ref[:, (hh - N_HEADS) * HEAD_DIM:(hh - N_HEADS + 1) * HEAD_DIM] = xh.astype(BF16)
    v = qkv[:, qk_w:QKV_DIM]
    v_ref[...] = v.astype(BF16)
    if cache_out:
        vc_ref[...] = v


def _qkv(x, mods, layer, row_of_tile, n1, w_qkv, qn, kn, rope_tabs, cache_out, tm):
    n_tok = x.shape[0]
    kvw = N_KV * HEAD_DIM
    rope = rope_tabs is not None
    in_specs = [
        pl.BlockSpec((tm, D_MODEL), lambda i: (i, 0)),
        _mod_spec(layer, row_of_tile),
        pl.BlockSpec((1, D_MODEL), lambda i: (0, 0)),
        pl.BlockSpec((D_MODEL, QKV_DIM), lambda i: (0, 0)),
        pl.BlockSpec((1, HEAD_DIM), lambda i: (0, 0)),
        pl.BlockSpec((1, HEAD_DIM), lambda i: (0, 0)),
    ]
    args = [x, mods, n1, w_qkv, qn, kn]
    if rope:
        seq_tiles = rope_tabs[0].shape[0] // tm
        in_specs += [pl.BlockSpec((tm, HEAD_DIM), lambda i: (i % seq_tiles, 0))] * 2
        args += list(rope_tabs)
    out_shape = [jax.ShapeDtypeStruct((n_tok, D_MODEL), BF16),
                 jax.ShapeDtypeStruct((n_tok, kvw), BF16),
                 jax.ShapeDtypeStruct((n_tok, kvw), BF16)]
    out_specs = [pl.BlockSpec((tm, D_MODEL), lambda i: (i, 0)),
                 pl.BlockSpec((tm, kvw), lambda i: (i, 0)),
                 pl.BlockSpec((tm, kvw), lambda i: (i, 0))]
    if cache_out:
        out_shape += [jax.ShapeDtypeStruct((n_tok, kvw), F32)] * 2
        out_specs += [pl.BlockSpec((tm, kvw), lambda i: (i, 0))] * 2
    return pl.pallas_call(
        functools.partial(_qkv_kernel, rope=rope, cache_out=cache_out),
        out_shape=out_shape,
        grid=(n_tok // tm,),
        in_specs=in_specs,
        out_specs=out_specs,
        scratch_shapes=[pltpu.VMEM((D_MODEL, QKV_DIM + ((N_HEADS + N_KV) * HEAD_DIM if rope else 0)), BF16)],
        compiler_params=_cparams(1),
        name="qkv_rope" if rope else "qkv",
    )(*args)


def _rope_tables(seq_len):
    half = HEAD_DIM // 2
    n = half // 2
    inv_freq = ROPE_THETA ** (-np.arange(n, dtype=np.float64) / n)
    rows = seq_len // GRID_W
    row = np.repeat(np.arange(rows), GRID_W).astype(np.float64)
    col = np.tile(np.arange(GRID_W), rows).astype(np.float64)
    ang_r = row[:, None] * inv_freq[None, :]
    ang_c = col[:, None] * inv_freq[None, :]
    cos = np.concatenate([np.cos(ang_r)] * 2 + [np.cos(ang_c)] * 2, axis=-1)
    sin = np.concatenate([-np.sin(ang_r), np.sin(ang_r), -np.sin(ang_c), np.sin(ang_c)], axis=-1)
    return jnp.asarray(cos, F32), jnp.asarray(sin, F32)


def _with_ones(v):
    return jnp.concatenate([v, jnp.ones_like(v)], axis=1)


def _gqa_attention(q, segments):
    rows = q.shape[0]
    qs = jnp.concatenate([q[:, h * HEAD_DIM:(h + 1) * HEAD_DIM] for h in range(Q_PER_KV)], axis=0)
    m = acc = None
    for k, v in segments:
        s = _dot_nt(qs, k)
        seg_max = jnp.max(s, axis=-1, keepdims=True)
        m_new = seg_max if m is None else jnp.maximum(m, seg_max)
        pv = _dot(jnp.exp2(s - m_new).astype(BF16), v)
        acc = pv if m is None else acc * jnp.exp2(m - m_new) + pv
        m = m_new
    o = (acc[:, :HEAD_DIM] / acc[:, HEAD_DIM:]).astype(BF16)
    return jnp.concatenate([o[h * rows:(h + 1) * rows] for h in range(Q_PER_KV)], axis=1)


def _attn_ctx_kernel(q_ref, k_ref, v_ref, o_ref, *, seq):
    gw = Q_PER_KV * HEAD_DIM
    for r in range(0, q_ref.shape[0], seq):
        for g in range(N_KV):
            kg = k_ref[r:r + seq, g * HEAD_DIM:(g + 1) * HEAD_DIM]
            vg = _with_ones(v_ref[r:r + seq, g * HEAD_DIM:(g + 1) * HEAD_DIM])
            o_ref[r:r + seq, g * gw:(g + 1) * gw] = _gqa_attention(q_ref[r:r + seq, g * gw:(g + 1) * gw], [(kg, vg)])


def _attn_ctx(q, k, v, seq, tm):
    n_tok = q.shape[0]
    kvw = N_KV * HEAD_DIM
    return pl.pallas_call(
        functools.partial(_attn_ctx_kernel, seq=seq),
        out_shape=jax.ShapeDtypeStruct((n_tok, D_MODEL), BF16),
        grid=(n_tok // tm,),
        in_specs=[pl.BlockSpec((tm, D_MODEL), lambda b: (b, 0)),
                  pl.BlockSpec((tm, kvw), lambda b: (b, 0)),
                  pl.BlockSpec((tm, kvw), lambda b: (b, 0))],
        out_specs=pl.BlockSpec((tm, D_MODEL), lambda b: (b, 0)),
        compiler_params=_cparams(1),
        name="attn_ctx",
    )(q, k, v)


KEY_CHUNK = 256


def _attn_lat_kernel(q_ref, k_ref, v_ref, kc_ref, vc_ref, o_ref):
    segments = [(k_ref[c:c + KEY_CHUNK], _with_ones(v_ref[c:c + KEY_CHUNK]))
                for c in range(0, k_ref.shape[0], KEY_CHUNK)]
    segments.append((kc_ref[0], _with_ones(vc_ref[0])))
    o_ref[...] = _gqa_attention(q_ref[...], segments)


def _attn_lat(q, k, v, kc, vc, seq, tq):
    n_tok = q.shape[0]
    batch = n_tok // seq
    past = kc.shape[1]
    qt = seq // tq
    gw = Q_PER_KV * HEAD_DIM
    return pl.pallas_call(
        _attn_lat_kernel,
        out_shape=jax.ShapeDtypeStruct((n_tok, D_MODEL), BF16),
        grid=(batch, N_KV, qt),
        in_specs=[pl.BlockSpec((tq, gw), lambda b, g, i: (b * qt + i, g)),
                  pl.BlockSpec((seq, HEAD_DIM), lambda b, g, i: (b, g)),
                  pl.BlockSpec((seq, HEAD_DIM), lambda b, g, i: (b, g)),
                  pl.BlockSpec((1, past, HEAD_DIM), lambda b, g, i: (b, 0, g)),
                  pl.BlockSpec((1, past, HEAD_DIM), lambda b, g, i: (b, 0, g))],
        out_specs=pl.BlockSpec((tq, gw), lambda b, g, i: (b * qt + i, g)),
        compiler_params=_cparams(3),
        name="attn_lat",
    )(q, k, v, kc, vc)


def _router_softmax(hb, wr):
    logits = _dot(hb, wr)
    lane = lax.broadcasted_iota(I32, logits.shape, 1)
    logits = jnp.where(lane < N_EXPERTS, logits, -1e30)
    ex = jnp.exp(logits - jnp.max(logits, axis=-1, keepdims=True))
    return ex / jnp.sum(ex, axis=-1, keepdims=True)


def _moe_front(x1, m, n2, wr, x1_ref, hp_ref, aff_ref):
    tm = x1.shape[0]
    x1_ref[...] = x1
    h2 = _norm_mod(x1, n2, m[3:4], m[4:5])
    aff = _router_softmax(h2.astype(BF16), wr)
    packed = pltpu.pack_elementwise([h2[:, :HALF], h2[:, HALF:]], packed_dtype=BF16)
    for s in range(HP_ROWS):
        hp_ref[pl.ds(s, tm, stride=HP_ROWS), :] = packed[:, s * LANES:(s + 1) * LANES]
    aff_t = aff.T
    for b in range(tm // LANES):
        aff_ref[b * N_EXPERTS:(b + 1) * N_EXPERTS, :] = aff_t[0:N_EXPERTS, b * LANES:(b + 1) * LANES]


def _front_out(n_tok, tm):
    shapes = [jax.ShapeDtypeStruct((n_tok, D_MODEL), F32),
              jax.ShapeDtypeStruct((n_tok * HP_ROWS, LANES), U32),
              jax.ShapeDtypeStruct((n_tok // LANES * N_EXPERTS, LANES), F32)]
    specs = [pl.BlockSpec((tm, D_MODEL), lambda i: (i, 0)),
             pl.BlockSpec((tm * HP_ROWS, LANES), lambda i: (i, 0)),
             pl.BlockSpec((tm // LANES * N_EXPERTS, LANES), lambda i: (i, 0))]
    return shapes, specs


def _post_kernel(o_ref, x_ref, mod_ref, n2_ref, wo_ref, wr_ref, x1_ref, hp_ref, aff_ref, wb_ref):
    @pl.when(pl.program_id(0) == 0)
    def _():
        wb_ref[...] = wo_ref[...].astype(BF16)

    m = mod_ref[0, 0]
    x1 = x_ref[...] + m[2:3] * _dot(o_ref[...], wb_ref[...])
    _moe_front(x1, m, n2_ref[...], wr_ref[...], x1_ref, hp_ref, aff_ref)


def _post(o, x, mods, layer, row_of_tile, n2, w_o, wr, tm):
    n_tok = x.shape[0]
    shapes, specs = _front_out(n_tok, tm)
    return pl.pallas_call(
        _post_kernel,
        out_shape=shapes,
        grid=(n_tok // tm,),
        in_specs=[pl.BlockSpec((tm, D_MODEL), lambda i: (i, 0)),
                  pl.BlockSpec((tm, D_MODEL), lambda i: (i, 0)),
                  _mod_spec(layer, row_of_tile),
                  pl.BlockSpec((1, D_MODEL), lambda i: (0, 0)),
                  pl.BlockSpec((D_MODEL, D_MODEL), lambda i: (0, 0)),
                  pl.BlockSpec((D_MODEL, LANES), lambda i: (0, 0))],
        out_specs=specs,
        scratch_shapes=[pltpu.VMEM((D_MODEL, D_MODEL), BF16)],
        compiler_params=_cparams(1),
        name="post",
    )(o, x, mods, n2, w_o, wr)


def _pool_kernel(x_ref, xp_ref, xn_ref, mod_ref, n1_ref, n2_ref, wp_ref, ps_ref, wr_ref,
                 x1_ref, hp_ref, aff_ref, *, seq):
    tm = x_ref.shape[0]
    i = pl.program_id(0)
    m = mod_ref[0, 0]
    x = x_ref[...]

    def norm_mod(v):
        return _norm_mod(v, n1_ref[...], m[0:1], m[1:2])

    h = norm_mod(x)
    if tm <= seq:
        prev_ok = ((i * tm) % seq != 0).astype(F32)
        next_ok = (((i + 1) * tm) % seq != 0).astype(F32)
        pieces = [(norm_mod(xp_ref[...]) * prev_ok, h, norm_mod(xn_ref[...]) * next_ok)]
        t0 = (i * tm) % seq
    else:
        zeros = jnp.zeros((POOL_HALO, D_MODEL), F32)
        pieces = [(zeros, h[r:r + seq], zeros) for r in range(0, tm, seq)]
        t0 = 0
    n = min(tm, seq)
    rows = n + 2 * POOL_HALO
    t = t0 + lax.broadcasted_iota(I32, (n, 1), 0)
    deltas = [[] for _ in POOL_WINDOWS]
    for piece in pieces:
        hz = jnp.concatenate(piece, axis=0)
        for g, w in enumerate(POOL_WINDOWS):
            sl = slice(g * POOL_GROUP, (g + 1) * POOL_GROUP)
            f = hz[:, sl]
            step = 1
            while step < w:
                f = f + pltpu.roll(f, rows - step, 0)
                step *= 2
            win = pltpu.roll(f, w // 2, 0)[POOL_HALO:POOL_HALO + n]
            cnt = (jnp.minimum(t + w // 2, seq) - jnp.maximum(t - w // 2, 0)).astype(F32)
            deltas[g].append((win / cnt - piece[1][:, sl]).astype(BF16))
    ys = [_dot(jnp.concatenate(d, axis=0), wp_ref[g].astype(BF16)) for g, d in enumerate(deltas)]
    x1 = x + (m[2:3] * ps_ref[...]) * jnp.concatenate(ys, axis=-1)
    _moe_front(x1, m, n2_ref[...], wr_ref[...], x1_ref, hp_ref, aff_ref)


def _pool(x, mods, layer, row_of_tile, n1, n2, w_pool, pool_scale, wr, seq, tm):
    n_tok = x.shape[0]
    hb = tm // POOL_HALO
    last = n_tok // POOL_HALO - 1
    shapes, specs = _front_out(n_tok, tm)
    return pl.pallas_call(
        functools.partial(_pool_kernel, seq=seq),
        out_shape=shapes,
        grid=(n_tok // tm,),
        in_specs=[pl.BlockSpec((tm, D_MODEL), lambda i: (i, 0)),
                  pl.BlockSpec((POOL_HALO, D_MODEL), lambda i: (jnp.maximum(i * hb - 1, 0), 0)),
                  pl.BlockSpec((POOL_HALO, D_MODEL), lambda i: (jnp.minimum((i + 1) * hb, last), 0)),
                  _mod_spec(layer, row_of_tile),
                  pl.BlockSpec((1, D_MODEL), lambda i: (0, 0)),
                  pl.BlockSpec((1, D_MODEL), lambda i: (0, 0)),
                  pl.BlockSpec(w_pool.shape, lambda i: (0, 0, 0)),
                  pl.BlockSpec((1, D_MODEL), lambda i: (0, 0)),
                  pl.BlockSpec((D_MODEL, LANES), lambda i: (0, 0))],
        out_specs=specs,
        compiler_params=_cparams(1),
        name="pool",
    )(x, x, x, mods, n1, n2, w_pool, pool_scale, wr)


def _select_kernel(aff_ref, src_ref, dst_ref, gate_ref, *, cap):
    nb = aff_ref.shape[0] // N_EXPERTS
    rows = nb * N_EXPERTS
    shape3 = (nb, N_EXPERTS, LANES)
    aff = aff_ref[...].reshape(shape3)

    def count(mask):
        return jnp.sum(jnp.sum(mask.astype(F32), axis=0), axis=-1, keepdims=True)

    def search(i, t):
        cand = t | (jnp.int32(1) << (29 - i))
        return jnp.where(count(aff >= pltpu.bitcast(cand, F32)[None]) >= cap, cand, t)

    thr = pltpu.bitcast(lax.fori_loop(0, 30, search, jnp.zeros((N_EXPERTS, LANES), I32)), F32)
    gt = aff > thr[None]
    eq = aff == thr[None]
    need = cap - count(gt)

    kk = lax.broadcasted_iota(I32, (LANES, LANES), 0)
    nn = lax.broadcasted_iota(I32, (LANES, LANES), 1)
    upper = (kk <= nn).astype(BF16)
    ones = jnp.ones((LANES, LANES), BF16)

    def prefix(mask):
        m2 = mask.astype(F32).astype(BF16).reshape(rows, LANES)
        p = _dot(m2, upper).reshape(shape3)
        s = _dot(m2, ones).reshape(shape3)
        offs = []
        run = jnp.zeros((N_EXPERTS, LANES), F32)
        for b in range(nb):
            offs.append(run)
            run = run + s[b]
        return p, jnp.stack(offs, axis=0), s

    pe, oe, _ = prefix(eq)
    sel = gt | (eq & ((pe + oe) <= need[None]))
    ps, os_, ss = prefix(sel)

    lane = lax.broadcasted_iota(I32, (rows, LANES), 1)
    sel2 = sel.reshape(rows, LANES)
    dist = jnp.where(sel2, lane - (ps.reshape(rows, LANES).astype(I32) - 1), 0)
    alive = sel2.astype(I32)
    val = lane
    gval = aff.reshape(rows, LANES)
    for k in range(7):
        s = 1 << k
        move = alive * ((dist >> k) & 1)
        inc = pltpu.roll(move, LANES - s, 1) * (lane < LANES - s).astype(I32) == 1
        val = jnp.where(inc, pltpu.roll(val, LANES - s, 1), val)
        gval = jnp.where(inc, pltpu.roll(gval, LANES - s, 1), gval)
        dist = jnp.where(inc, pltpu.roll(dist, LANES - s, 1), dist)
        alive = alive - move + inc.astype(I32)
    local = val.reshape(shape3)
    local_gate = gval.reshape(shape3)

    top = os_ + ss
    lane_e = lax.broadcasted_iota(I32, (N_EXPERTS, LANES), 1)
    for jc in range(cap // LANES):
        j = (lane_e + jc * LANES).astype(F32)
        acc = jnp.zeros((N_EXPERTS, LANES), I32)
        gacc = jnp.zeros((N_EXPERTS, LANES), F32)
        for b in range(nb):
            inside = (os_[b] <= j) & (j < top[b])
            jl = (j - os_[b]).astype(I32) & (LANES - 1)
            acc = jnp.where(inside, jnp.take_along_axis(local[b], jl, axis=1) + b * LANES, acc)
            gacc = jnp.where(inside, jnp.take_along_axis(local_gate[b], jl, axis=1), gacc)
        out_rows = pl.ds(jc, N_EXPERTS, stride=cap // LANES)
        src_ref[out_rows, :] = acc * HP_ROWS
        dst_ref[out_rows, :] = acc * Y_ROWS
        gate_ref[out_rows, :] = gacc


def _select(aff, cap):
    rows = aff.shape[0]
    out_rows = N_EXPERTS * cap // LANES
    out_spec = pl.BlockSpec((out_rows, LANES), lambda i: (0, 0))
    src, dst, gate = pl.pallas_call(
        functools.partial(_select_kernel, cap=cap),
        out_shape=[jax.ShapeDtypeStruct((out_rows, LANES), I32),
                   jax.ShapeDtypeStruct((out_rows, LANES), I32),
                   jax.ShapeDtypeStruct((out_rows, LANES), F32)],
        grid=(1,),
        in_specs=[pl.BlockSpec((rows, LANES), lambda i: (0, 0))],
        out_specs=[out_spec, out_spec, out_spec],
        compiler_params=_cparams(1),
        name="select",
    )(aff)
    return src.reshape(-1), dst.reshape(-1), gate


GATHER_UNROLL = 32


GATHER_STEP_ROWS = 2048


def _gather_kernel(src_ref, hp_ref, o_ref):
    i = pl.program_id(0)
    n = o_ref.shape[0] // HP_ROWS

    def gather(g, carry):
        base = i * n + g * GATHER_UNROLL
        for r in range(GATHER_UNROLL):
            src = pl.multiple_of(src_ref[base + r], HP_ROWS)
            dst = pl.multiple_of((g * GATHER_UNROLL + r) * HP_ROWS, HP_ROWS)
            o_ref[pl.ds(dst, HP_ROWS), :] = hp_ref[pl.ds(src, HP_ROWS), :]
        return carry

    lax.fori_loop(0, n // GATHER_UNROLL, gather, 0)


def _gather(src, hp):
    n = src.shape[0]
    return pl.pallas_call(
        _gather_kernel,
        out_shape=jax.ShapeDtypeStruct((n * HP_ROWS, LANES), U32),
        grid_spec=pltpu.PrefetchScalarGridSpec(
            num_scalar_prefetch=1,
            grid=(n // GATHER_STEP_ROWS,),
            in_specs=[pl.BlockSpec(hp.shape, lambda i, src: (0, 0), pipeline_mode=pl.Buffered(1))],
            out_specs=pl.BlockSpec((GATHER_STEP_ROWS * HP_ROWS, LANES), lambda i, src: (i, 0))),
        compiler_params=_cparams(1),
        name="gather",
    )(src, hp)


FF_CHUNK = 256


def _ffn_kernel(*refs, n_streams):
    e = pl.program_id(0)
    ins = refs[:2 * n_streams]
    wg_ref, wu_ref, wd_ref = refs[2 * n_streams:2 * n_streams + 3]
    outs = refs[2 * n_streams + 3:]
    d_ff = wg_ref.shape[3]
    eye = lax.broadcasted_iota(I32, (LANES, LANES), 0) == lax.broadcasted_iota(I32, (LANES, LANES), 1)
    for i in range(n_streams):
        xp_ref, gate_ref, y_ref = ins[2 * i], ins[2 * i + 1], outs[i]
        cap = xp_ref.shape[0] // HP_ROWS
        halves = [[], []]
        for s in range(HP_ROWS):
            w = xp_ref[pl.ds(s, cap, stride=HP_ROWS), :]
            for half in range(2):
                v = pltpu.unpack_elementwise(w, index=half, packed_dtype=BF16, unpacked_dtype=F32)
                halves[half].append(v.astype(BF16))
        xe = jnp.concatenate(halves[0] + halves[1], axis=1)
        y = None
        for c in range(0, d_ff, FF_CHUNK):
            a = _dot(xe, wg_ref[0, 0, :, c:c + FF_CHUNK].astype(BF16))
            u = _dot(xe, wu_ref[0, 0, :, c:c + FF_CHUNK].astype(BF16))
            yc = _dot((_silu(a) * u).astype(BF16), wd_ref[0, 0, c:c + FF_CHUNK, :].astype(BF16))
            y = yc if y is None else y + yc
        for c in range(cap // LANES):
            g_row = gate_ref[pl.ds(e * (cap // LANES) + c, 1), :]
            g_col = jnp.sum(jnp.where(eye, g_row, 0.0), axis=1, keepdims=True)
            tot = y[c * LANES:(c + 1) * LANES, :] * g_col
            for s in range(Y_ROWS):
                y_ref[0, pl.ds(c * LANES * Y_ROWS + s, LANES, stride=Y_ROWS), :] = tot[:, s * LANES:(s + 1) * LANES]


def _ffn(streams, layer, w_gate, w_up, w_down):
    caps = [xp.shape[0] // (N_EXPERTS * HP_ROWS) for xp, _ in streams]
    in_specs, args = [], []
    for (xp, gate), cap in zip(streams, caps):
        in_specs += [pl.BlockSpec((cap * HP_ROWS, LANES), lambda e: (e, 0)),
                     pl.BlockSpec(gate.shape, lambda e: (0, 0))]
        args += [xp, gate]
    w_spec = pl.BlockSpec((1, 1) + w_gate.shape[2:], lambda e: (layer, e, 0, 0))
    return pl.pallas_call(
        functools.partial(_ffn_kernel, n_streams=len(streams)),
        out_shape=[jax.ShapeDtypeStruct((N_EXPERTS, cap * Y_ROWS, LANES), F32) for cap in caps],
        grid=(N_EXPERTS,),
        in_specs=in_specs + [w_spec, w_spec, w_spec],
        out_specs=[pl.BlockSpec((1, cap * Y_ROWS, LANES), lambda e: (e, 0, 0)) for cap in caps],
        compiler_params=_cparams(1),
        name="ffn",
    )(*args, w_gate, w_up, w_down)


SCATTER_UNROLL = 16
SCATTER_GROUPS = 2


def _combine_kernel(dst_ref, y_ref, x1_ref, mod_ref, o_ref, acc_ref, *, n_scatter):
    s = pl.program_id(0)
    tf = o_ref.shape[0]
    step_rows = y_ref.shape[0] // Y_ROWS

    @pl.when(s == 0)
    def _():
        acc_ref[...] = jnp.zeros_like(acc_ref)

    @pl.when(s < n_scatter)
    def _():
        def scatter(g, carry):
            for sub in range(SCATTER_GROUPS):
                first = (g * SCATTER_GROUPS + sub) * SCATTER_UNROLL
                base = s * step_rows + first
                rows, sums = [], []
                for r in range(SCATTER_UNROLL):
                    dst = pl.ds(pl.multiple_of(dst_ref[base + r], Y_ROWS), Y_ROWS)
                    src = pl.ds(pl.multiple_of((first + r) * Y_ROWS, Y_ROWS), Y_ROWS)
                    rows.append(dst)
                    sums.append(acc_ref[dst, :] + y_ref[src, :])
                for dst, v in zip(rows, sums):
                    acc_ref[dst, :] = v
            return carry

        lax.fori_loop(0, step_rows // (SCATTER_UNROLL * SCATTER_GROUPS), scatter, 0)

    @pl.when(s >= n_scatter)
    def _():
        g2 = mod_ref[0, 0][5:6]
        tile = acc_ref.at[pl.ds(pl.multiple_of((s - n_scatter) * tf * Y_ROWS, tf * Y_ROWS), tf * Y_ROWS), :]
        for c in range(Y_ROWS):
            sl = slice(c * LANES, (c + 1) * LANES)
            o_ref[:, sl] = x1_ref[:, sl] + g2[:, sl] * tile[pl.ds(c, tf, stride=Y_ROWS), :]


def _combine_tiles(n_tok):
    row_bytes = D_MODEL * 4
    free_rows = (VMEM_LIMIT - n_tok * row_bytes) * 4 // 5 // row_bytes
    scale = max(k for k in (1, 2, 4) if 2 * 512 * k + 4 * 256 * k <= free_rows)
    return 256 * scale, 512 * scale


def _combine(dst, y, x1, mods, layer, row_of_token):
    n_tok = x1.shape[0]
    tf, step_rows = _combine_tiles(n_tok)
    n_scatter = dst.shape[0] // step_rows
    assert (dst.shape[0] // N_EXPERTS) % SCATTER_UNROLL == 0 and dst.shape[0] % step_rows == 0
    tile_of = lambda s: jnp.maximum(s - n_scatter, 0)
    row_of_tile = lambda t: row_of_token(t * tf)
    return pl.pallas_call(
        functools.partial(_combine_kernel, n_scatter=n_scatter),
        out_shape=jax.ShapeDtypeStruct((n_tok, D_MODEL), F32),
        grid_spec=pltpu.PrefetchScalarGridSpec(
            num_scalar_prefetch=1,
            grid=(n_scatter + n_tok // tf,),
            in_specs=[pl.BlockSpec((step_rows * Y_ROWS, LANES), lambda s, *_: (jnp.minimum(s, n_scatter - 1), 0)),
                      pl.BlockSpec((tf, D_MODEL), lambda s, *_: (tile_of(s), 0)),
                      pl.BlockSpec((1, 1, N_MOD, D_MODEL), lambda s, *_: (layer, row_of_tile(tile_of(s)), 0, 0))],
            out_specs=pl.BlockSpec((tf, D_MODEL), lambda s, *_: (tile_of(s), 0)),
            scratch_shapes=[pltpu.VMEM((n_tok * Y_ROWS, LANES), F32)]),
        compiler_params=_cparams(1),
        name="combine",
    )(dst, y.reshape(-1, LANES), x1, mods)


TOKEN_TILE = 1024
Q_TILE = 512


def _moe(fronts, rows_of_token, mods, layer, w_gate, w_up, w_down):
    routed, dsts = [], []
    for x1, hp, aff in fronts:
        cap = CAPACITY_FACTOR * x1.shape[0] // N_EXPERTS
        src, dst, gate = _select(aff, cap)
        routed.append((_gather(src, hp), gate))
        dsts.append(dst)
    ys = _ffn(routed, layer, w_gate, w_up, w_down)
    return [_combine(dst, y, x1, mods, layer, row_of_token)
            for dst, y, (x1, _, _), row_of_token in zip(dsts, ys, fronts, rows_of_token)]


def kernel(x_prompt, x_sample, cache_k, cache_v, c, c_ctx, norm1, norm2, w_ada, b_ada, w_qkv, q_norm, k_norm,
           w_o, w_pool, pool_scale, w_router, w_e_gate, w_e_up, w_e_down):
    batch, seq, _ = x_prompt.shape
    dec_batch, dec_seq, _ = x_sample.shape
    depth = w_ada.shape[0]
    kvw = N_KV * HEAD_DIM

    cvec = jnp.zeros((SUBLANES, D_MODEL), F32).at[0].set(c_ctx).at[1:1 + dec_batch].set(c)
    mods = _ada(cvec, w_ada, b_ada)
    ctx = x_prompt.reshape(batch * seq, D_MODEL)
    lat = x_sample.reshape(dec_batch * dec_seq, D_MODEL)
    ctx_row = lambda i: 0
    lat_row = lambda tile: (lambda i: 1 + (i * tile) // dec_seq)
    ctx_tile = TOKEN_TILE if TOKEN_TILE % seq == 0 else min(TOKEN_TILE, seq)
    rope_tabs = _rope_tables(dec_seq)
    new_k = new_v = None

    for layer in range(depth):
        j = layer // 2
        n1 = norm1[layer][None]
        n2 = norm2[layer][None]
        wr = jnp.pad(w_router[layer], ((0, 0), (0, LANES - N_EXPERTS))).astype(BF16)
        if layer % 2 == 0:
            qn = q_norm[j][None]
            kn = k_norm[j][None]
            qc, kc, vc, new_k, new_v = _qkv(ctx, mods, layer, ctx_row, n1, w_qkv[j], qn, kn, None, True, TOKEN_TILE)
            oc = _attn_ctx(qc, kc, vc, seq, TOKEN_TILE)
            ql, kl, vl = _qkv(lat, mods, layer, lat_row(TOKEN_TILE), n1, w_qkv[j], qn, kn, rope_tabs, False,
                              TOKEN_TILE)
            past_k = cache_k[:, j].reshape(dec_batch, -1, kvw).astype(BF16)
            past_v = cache_v[:, j].reshape(dec_batch, -1, kvw).astype(BF16)
            ol = _attn_lat(ql, kl, vl, past_k, past_v, dec_seq, Q_TILE)
            ctx1 = _post(oc, ctx, mods, layer, ctx_row, n2, w_o[j], wr, TOKEN_TILE)
            lat1 = _post(ol, lat, mods, layer, lat_row(TOKEN_TILE), n2, w_o[j], wr, TOKEN_TILE)
        else:
            ps = pool_scale[j][None]
            ctx1 = _pool(ctx, mods, layer, ctx_row, n1, n2, w_pool[j], ps, wr, seq, ctx_tile)
            lat1 = _pool(lat, mods, layer, lat_row(TOKEN_TILE), n1, n2, w_pool[j], ps, wr, dec_seq, TOKEN_TILE)
        ctx, lat = _moe([ctx1, lat1], [ctx_row, lat_row(1)], mods, layer, w_e_gate, w_e_up, w_e_down)

    new_cache_k = new_k.reshape(batch, 1, seq, N_KV, HEAD_DIM)
    new_cache_v = new_v.reshape(batch, 1, seq, N_KV, HEAD_DIM)
    return (ctx.reshape(batch, seq, D_MODEL), lat.reshape(dec_batch, dec_seq, D_MODEL), new_cache_k, new_cache_v)
```

```python
import functools

import jax
import jax.numpy as jnp
import numpy as np
from jax import lax
from jax.experimental import pallas as pl
from jax.experimental.pallas import tpu as pltpu

F32 = jnp.float32
BF16 = jnp.bfloat16
I32 = jnp.int32
U32 = jnp.uint32

D_MODEL = 1024
HEAD_DIM = 128
N_HEADS = 8
N_KV = 2
Q_PER_KV = N_HEADS // N_KV
QKV_DIM = (N_HEADS + 2 * N_KV) * HEAD_DIM
GRID_W = 64
ROPE_THETA = 10000.0
POOL_WINDOWS = (2, 4, 8, 16)
POOL_GROUP = D_MODEL // len(POOL_WINDOWS)
POOL_HALO = max(POOL_WINDOWS) // 2
N_EXPERTS = 16
CAPACITY_FACTOR = 2
N_MOD = 6
EPS = 1e-6
LOG2_E = 1.4426950408889634

LANES = 128
SUBLANES = 8
HALF = D_MODEL // 2
HP_ROWS = HALF // LANES
Y_ROWS = D_MODEL // LANES
VMEM_LIMIT = 56 * 1024 * 1024


def _cparams(n_axes, vmem=VMEM_LIMIT):
    return pltpu.CompilerParams(dimension_semantics=("arbitrary",) * n_axes, vmem_limit_bytes=vmem)


def _silu(x):
    return x / (1.0 + jnp.exp(-x))


def _rms(x, gain):
    return x * lax.rsqrt(jnp.mean(x * x, axis=-1, keepdims=True) + EPS) * gain


def _norm_mod(x, gain, shift, scale):
    return _rms(x, gain * (1.0 + scale)) + shift


def _dot(a, b):
    return jnp.dot(a, b, preferred_element_type=F32)


def _dot_nt(a, b):
    return lax.dot_general(a, b, (((1,), (1,)), ((), ())), preferred_element_type=F32)


def _ada_kernel(c_ref, w_ref, b_ref, o_ref):
    s = _silu(c_ref[...]).astype(BF16)
    o_ref[0] = _dot(s, w_ref[0].astype(BF16)) + b_ref[0]


def _ada(cvec, w_ada, b_ada):
    depth = w_ada.shape[0]
    out = pl.pallas_call(
        _ada_kernel,
        out_shape=jax.ShapeDtypeStruct((depth, SUBLANES, N_MOD * D_MODEL), F32),
        grid=(depth, N_MOD),
        in_specs=[
            pl.BlockSpec((SUBLANES, D_MODEL), lambda i, j: (0, 0)),
            pl.BlockSpec((1, D_MODEL, D_MODEL), lambda i, j: (i, 0, j)),
            pl.BlockSpec((1, 1, D_MODEL), lambda i, j: (i, 0, j)),
        ],
        out_specs=pl.BlockSpec((1, SUBLANES, D_MODEL), lambda i, j: (i, 0, j)),
        compiler_params=_cparams(2),
        name="ada",
    )(cvec, w_ada, b_ada.reshape(depth, 1, N_MOD * D_MODEL))
    return out.reshape(depth, SUBLANES, N_MOD, D_MODEL)


def _mod_spec(layer, row_of_tile):
    return pl.BlockSpec((1, 1, N_MOD, D_MODEL), lambda i, *_: (layer, row_of_tile(i), 0, 0))


def _qkv_kernel(*refs, rope, cache_out):
    x_ref, mod_ref, n1_ref, w_ref, qn_ref, kn_ref = refs[:6]
    refs = refs[6:]
    if rope:
        cos_ref, sin_ref = refs[:2]
        refs = refs[2:]
    q_ref, k_ref, v_ref = refs[:3]
    refs = refs[3:]
    if cache_out:
        kc_ref, vc_ref = refs[:2]
        refs = refs[2:]
    (wb_ref,) = refs

    qk_w = (N_HEADS + N_KV) * HEAD_DIM
    quarter = HEAD_DIM // 4

    def partner(a):
        width = a.shape[1]
        first = (lax.broadcasted_iota(I32, a.shape, 1) & quarter) == 0
        return jnp.where(first, pltpu.roll(a, width - quarter, 1), pltpu.roll(a, quarter, 1))

    @pl.when(pl.program_id(0) == 0)
    def _():
        w = w_ref[...]
        wb_ref[:, 0:QKV_DIM] = w.astype(BF16)
        if rope:
            wb_ref[:, QKV_DIM:] = partner(w[:, 0:qk_w]).astype(BF16)

    m = mod_ref[0, 0]
    h = _norm_mod(x_ref[...], n1_ref[...], m[0:1], m[1:2])
    qkv = _dot(h.astype(BF16), wb_ref[...])
    scale = HEAD_DIM ** -0.5 * LOG2_E
    if rope:
        gains = {True: qn_ref[...], False: kn_ref[...]}
        cos_g = {key: cos_ref[...] * g for key, g in gains.items()}
        sin_g = {key: sin_ref[...] * partner(g) for key, g in gains.items()}
    for hh in range(N_HEADS + N_KV):
        sl = slice(hh * HEAD_DIM, (hh + 1) * HEAD_DIM)
        is_q = hh < N_HEADS
        if rope:
            raw = qkv[:, sl]
            norm = lax.rsqrt(jnp.mean(raw * raw, axis=-1, keepdims=True) + EPS)
            xh = (raw * cos_g[is_q] + qkv[:, QKV_DIM + hh * HEAD_DIM:QKV_DIM + (hh + 1) * HEAD_DIM] * sin_g[is_q]) * norm
        else:
            xh = _rms(qkv[:, sl], qn_ref[...] if is_q else kn_ref[...])
        if cache_out and hh >= N_HEADS:
            kc_ref[:, hh - N_HEADS, :] = xh
        if hh < N_HEADS:
            q_ref[:, sl] = (xh * scale).astype(BF16)
        else:
            k_ref[:, (hh - N_HEADS) * HEAD_DIM:(hh - N_HEADS + 1) * HEAD_DIM] = xh.astype(BF16)
    v = qkv[:, qk_w:QKV_DIM]
    v_ref[...] = v.astype(BF16)
    if cache_out:
        for g in range(N_KV):
            vc_ref[:, g, :] = v[:, g * HEAD_DIM:(g + 1) * HEAD_DIM]


def _qkv(x, mods, layer, row_of_tile, n1, w_qkv, qn, kn, rope_tabs, cache_out, tm):
    n_tok = x.shape[0]
    kvw = N_KV * HEAD_DIM
    rope = rope_tabs is not None
    in_specs = [
        pl.BlockSpec((tm, D_MODEL), lambda i: (i, 0)),
        _mod_spec(layer, row_of_tile),
        pl.BlockSpec((1, D_MODEL), lambda i: (0, 0)),
        pl.BlockSpec((D_MODEL, QKV_DIM), lambda i: (0, 0)),
        pl.BlockSpec((1, HEAD_DIM), lambda i: (0, 0)),
        pl.BlockSpec((1, HEAD_DIM), lambda i: (0, 0)),
    ]
    args = [x, mods, n1, w_qkv, qn, kn]
    if rope:
        seq_tiles = rope_tabs[0].shape[0] // tm
        in_specs += [pl.BlockSpec((tm, HEAD_DIM), lambda i: (i % seq_tiles, 0))] * 2
        args += list(rope_tabs)
    out_shape = [jax.ShapeDtypeStruct((n_tok, D_MODEL), BF16),
                 jax.ShapeDtypeStruct((n_tok, kvw), BF16),
                 jax.ShapeDtypeStruct((n_tok, kvw), BF16)]
    out_specs = [pl.BlockSpec((tm, D_MODEL), lambda i: (i, 0)),
                 pl.BlockSpec((tm, kvw), lambda i: (i, 0)),
                 pl.BlockSpec((tm, kvw), lambda i: (i, 0))]
    if cache_out:
        out_shape += [jax.ShapeDtypeStruct((n_tok, N_KV, HEAD_DIM), F32)] * 2
        out_specs += [pl.BlockSpec((tm, N_KV, HEAD_DIM), lambda i: (i, 0, 0))] * 2
    return pl.pallas_call(
        functools.partial(_qkv_kernel, rope=rope, cache_out=cache_out),
        out_shape=out_shape,
        grid=(n_tok // tm,),
        in_specs=in_specs,
        out_specs=out_specs,
        scratch_shapes=[pltpu.VMEM((D_MODEL, QKV_DIM + ((N_HEADS + N_KV) * HEAD_DIM if rope else 0)), BF16)],
        compiler_params=_cparams(1),
        name="qkv_rope" if rope else "qkv",
    )(*args)


def _rope_tables(seq_len):
    half = HEAD_DIM // 2
    n = half // 2
    inv_freq = ROPE_THETA ** (-np.arange(n, dtype=np.float64) / n)
    rows = seq_len // GRID_W
    row = np.repeat(np.arange(rows), GRID_W).astype(np.float64)
    col = np.tile(np.arange(GRID_W), rows).astype(np.float64)
    ang_r = row[:, None] * inv_freq[None, :]
    ang_c = col[:, None] * inv_freq[None, :]
    cos = np.concatenate([np.cos(ang_r)] * 2 + [np.cos(ang_c)] * 2, axis=-1)
    sin = np.concatenate([-np.sin(ang_r), np.sin(ang_r), -np.sin(ang_c), np.sin(ang_c)], axis=-1)
    return jnp.asarray(cos, F32), jnp.asarray(sin, F32)


def _with_ones(v):
    return jnp.concatenate([v, jnp.ones_like(v)], axis=1)


def _gqa_attention(q, segments):
    rows = q.shape[0]
    qs = jnp.concatenate([q[:, h * HEAD_DIM:(h + 1) * HEAD_DIM] for h in range(Q_PER_KV)], axis=0)
    m = acc = None
    for k, v in segments:
        s = _dot_nt(qs, k)
        seg_max = jnp.max(s, axis=-1, keepdims=True)
        m_new = seg_max if m is None else jnp.maximum(m, seg_max)
        pv = _dot(jnp.exp2(s - m_new).astype(BF16), v)
        acc = pv if m is None else acc * jnp.exp2(m - m_new) + pv
        m = m_new
    o = (acc[:, :HEAD_DIM] / acc[:, HEAD_DIM:]).astype(BF16)
    return jnp.concatenate([o[h * rows:(h + 1) * rows] for h in range(Q_PER_KV)], axis=1)


def _attn_ctx_kernel(q_ref, k_ref, v_ref, o_ref, *, seq):
    gw = Q_PER_KV * HEAD_DIM
    for r in range(0, q_ref.shape[0], seq):
        for g in range(N_KV):
            kg = k_ref[r:r + seq, g * HEAD_DIM:(g + 1) * HEAD_DIM]
            vg = _with_ones(v_ref[r:r + seq, g * HEAD_DIM:(g + 1) * HEAD_DIM])
            o_ref[r:r + seq, g * gw:(g + 1) * gw] = _gqa_attention(q_ref[r:r + seq, g * gw:(g + 1) * gw], [(kg, vg)])


def _attn_ctx(q, k, v, seq, tm):
    n_tok = q.shape[0]
    kvw = N_KV * HEAD_DIM
    return pl.pallas_call(
        functools.partial(_attn_ctx_kernel, seq=seq),
        out_shape=jax.ShapeDtypeStruct((n_tok, D_MODEL), BF16),
        grid=(n_tok // tm,),
        in_specs=[pl.BlockSpec((tm, D_MODEL), lambda b: (b, 0)),
                  pl.BlockSpec((tm, kvw), lambda b: (b, 0)),
                  pl.BlockSpec((tm, kvw), lambda b: (b, 0))],
        out_specs=pl.BlockSpec((tm, D_MODEL), lambda b: (b, 0)),
        compiler_params=_cparams(1),
        name="attn_ctx",
    )(q, k, v)


KEY_CHUNK = 256


def _attn_lat_kernel(q_ref, k_ref, v_ref, kc_ref, vc_ref, o_ref):
    segments = [(k_ref[c:c + KEY_CHUNK], _with_ones(v_ref[c:c + KEY_CHUNK]))
                for c in range(0, k_ref.shape[0], KEY_CHUNK)]
    segments.append((kc_ref[0], _with_ones(vc_ref[0])))
    o_ref[...] = _gqa_attention(q_ref[...], segments)


def _attn_lat(q, k, v, kc, vc, seq, tq):
    n_tok = q.shape[0]
    batch = n_tok // seq
    past = kc.shape[1]
    qt = seq // tq
    gw = Q_PER_KV * HEAD_DIM
    return pl.pallas_call(
        _attn_lat_kernel,
        out_shape=jax.ShapeDtypeStruct((n_tok, D_MODEL), BF16),
        grid=(batch, N_KV, qt),
        in_specs=[pl.BlockSpec((tq, gw), lambda b, g, i: (b * qt + i, g)),
                  pl.BlockSpec((seq, HEAD_DIM), lambda b, g, i: (b, g)),
                  pl.BlockSpec((seq, HEAD_DIM), lambda b, g, i: (b, g)),
                  pl.BlockSpec((1, past, HEAD_DIM), lambda b, g, i: (b, 0, g)),
                  pl.BlockSpec((1, past, HEAD_DIM), lambda b, g, i: (b, 0, g))],
        out_specs=pl.BlockSpec((tq, gw), lambda b, g, i: (b * qt + i, g)),
        compiler_params=_cparams(3),
        name="attn_lat",
    )(q, k, v, kc, vc)


def _router_softmax(hb, wr):
    wide = jnp.concatenate([wr, jnp.zeros((wr.shape[0], LANES - N_EXPERTS), wr.dtype)], axis=1)
    logits = _dot(hb, wide.astype(BF16))
    lane = lax.broadcasted_iota(I32, logits.shape, 1)
    logits = jnp.where(lane < N_EXPERTS, logits, -1e30)
    ex = jnp.exp(logits - jnp.max(logits, axis=-1, keepdims=True))
    return ex / jnp.sum(ex, axis=-1, keepdims=True)


def _moe_front(x1, m, n2, wr, x1_ref, hp_ref, aff_ref):
    tm = x1.shape[0]
    x1_ref[...] = x1
    h2 = _norm_mod(x1, n2, m[3:4], m[4:5])
    aff = _router_softmax(h2.astype(BF16), wr)
    packed = pltpu.pack_elementwise([h2[:, :HALF], h2[:, HALF:]], packed_dtype=BF16)
    for s in range(HP_ROWS):
        hp_ref[pl.ds(s, tm, stride=HP_ROWS), :] = packed[:, s * LANES:(s + 1) * LANES]
    aff_t = aff.T
    for b in range(tm // LANES):
        aff_ref[b * N_EXPERTS:(b + 1) * N_EXPERTS, :] = aff_t[0:N_EXPERTS, b * LANES:(b + 1) * LANES]


def _front_out(n_tok, tm):
    shapes = [jax.ShapeDtypeStruct((n_tok, D_MODEL), F32),
              jax.ShapeDtypeStruct((n_tok * HP_ROWS, LANES), U32),
              jax.ShapeDtypeStruct((n_tok // LANES * N_EXPERTS, LANES), F32)]
    specs = [pl.BlockSpec((tm, D_MODEL), lambda i: (i, 0)),
             pl.BlockSpec((tm * HP_ROWS, LANES), lambda i: (i, 0)),
             pl.BlockSpec((tm // LANES * N_EXPERTS, LANES), lambda i: (i, 0))]
    return shapes, specs


def _post_kernel(o_ref, x_ref, mod_ref, n2_ref, wo_ref, wr_ref, x1_ref, hp_ref, aff_ref, wb_ref):
    @pl.when(pl.program_id(0) == 0)
    def _():
        wb_ref[...] = wo_ref[...].astype(BF16)

    m = mod_ref[0, 0]
    x1 = x_ref[...] + m[2:3] * _dot(o_ref[...], wb_ref[...])
    _moe_front(x1, m, n2_ref[...], wr_ref[0], x1_ref, hp_ref, aff_ref)


def _post(o, x, mods, layer, row_of_tile, n2, w_o, wr, tm):
    n_tok = x.shape[0]
    shapes, specs = _front_out(n_tok, tm)
    return pl.pallas_call(
        _post_kernel,
        out_shape=shapes,
        grid=(n_tok // tm,),
        in_specs=[pl.BlockSpec((tm, D_MODEL), lambda i: (i, 0)),
                  pl.BlockSpec((tm, D_MODEL), lambda i: (i, 0)),
                  _mod_spec(layer, row_of_tile),
                  pl.BlockSpec((1, D_MODEL), lambda i: (0, 0)),
                  pl.BlockSpec((D_MODEL, D_MODEL), lambda i: (0, 0)),
                  pl.BlockSpec((1, D_MODEL, N_EXPERTS), lambda i: (layer, 0, 0))],
        out_specs=specs,
        scratch_shapes=[pltpu.VMEM((D_MODEL, D_MODEL), BF16)],
        compiler_params=_cparams(1),
        name="post",
    )(o, x, mods, n2, w_o, wr)


def _pool_kernel(x_ref, xp_ref, xn_ref, mod_ref, n1_ref, n2_ref, wp_ref, ps_ref, wr_ref,
                 x1_ref, hp_ref, aff_ref, *, seq):
    tm = x_ref.shape[0]
    i = pl.program_id(0)
    m = mod_ref[0, 0]
    x = x_ref[...]

    def norm_mod(v):
        return _norm_mod(v, n1_ref[...], m[0:1], m[1:2])

    h = norm_mod(x)
    if tm <= seq:
        prev_ok = ((i * tm) % seq != 0).astype(F32)
        next_ok = (((i + 1) * tm) % seq != 0).astype(F32)
        pieces = [(norm_mod(xp_ref[...]) * prev_ok, h, norm_mod(xn_ref[...]) * next_ok)]
        t0 = (i * tm) % seq
    else:
        zeros = jnp.zeros((POOL_HALO, D_MODEL), F32)
        pieces = [(zeros, h[r:r + seq], zeros) for r in range(0, tm, seq)]
        t0 = 0
    n = min(tm, seq)
    rows = n + 2 * POOL_HALO
    t = t0 + lax.broadcasted_iota(I32, (n, 1), 0)
    deltas = [[] for _ in POOL_WINDOWS]
    for piece in pieces:
        hz = jnp.concatenate(piece, axis=0)
        for g, w in enumerate(POOL_WINDOWS):
            sl = slice(g * POOL_GROUP, (g + 1) * POOL_GROUP)
            f = hz[:, sl]
            step = 1
            while step < w:
                f = f + pltpu.roll(f, rows - step, 0)
                step *= 2
            win = pltpu.roll(f, w // 2, 0)[POOL_HALO:POOL_HALO + n]
            cnt = (jnp.minimum(t + w // 2, seq) - jnp.maximum(t - w // 2, 0)).astype(F32)
            deltas[g].append((win / cnt - piece[1][:, sl]).astype(BF16))
    ys = [_dot(jnp.concatenate(d, axis=0), wp_ref[g].astype(BF16)) for g, d in enumerate(deltas)]
    x1 = x + (m[2:3] * ps_ref[...]) * jnp.concatenate(ys, axis=-1)
    _moe_front(x1, m, n2_ref[...], wr_ref[0], x1_ref, hp_ref, aff_ref)


def _pool(x, mods, layer, row_of_tile, n1, n2, w_pool, pool_scale, wr, seq, tm):
    n_tok = x.shape[0]
    hb = tm // POOL_HALO
    last = n_tok // POOL_HALO - 1
    shapes, specs = _front_out(n_tok, tm)
    return pl.pallas_call(
        functools.partial(_pool_kernel, seq=seq),
        out_shape=shapes,
        grid=(n_tok // tm,),
        in_specs=[pl.BlockSpec((tm, D_MODEL), lambda i: (i, 0)),
                  pl.BlockSpec((POOL_HALO, D_MODEL), lambda i: (jnp.maximum(i * hb - 1, 0), 0)),
                  pl.BlockSpec((POOL_HALO, D_MODEL), lambda i: (jnp.minimum((i + 1) * hb, last), 0)),
                  _mod_spec(layer, row_of_tile),
                  pl.BlockSpec((1, D_MODEL), lambda i: (0, 0)),
                  pl.BlockSpec((1, D_MODEL), lambda i: (0, 0)),
                  pl.BlockSpec(w_pool.shape, lambda i: (0, 0, 0)),
                  pl.BlockSpec((1, D_MODEL), lambda i: (0, 0)),
                  pl.BlockSpec((1, D_MODEL, N_EXPERTS), lambda i: (layer, 0, 0))],
        out_specs=specs,
        compiler_params=_cparams(1),
        name="pool",
    )(x, x, x, mods, n1, n2, w_pool, pool_scale, wr)


def _select_kernel(aff_ref, src_ref, dst_ref, gate_ref, *, cap):
    nb = aff_ref.shape[0] // N_EXPERTS
    rows = nb * N_EXPERTS
    shape3 = (nb, N_EXPERTS, LANES)
    aff = aff_ref[...].reshape(shape3)

    def count(mask):
        return jnp.sum(jnp.sum(mask.astype(F32), axis=0), axis=-1, keepdims=True)

    def search(i, t):
        cand = t | (jnp.int32(1) << (29 - i))
        return jnp.where(count(aff >= pltpu.bitcast(cand, F32)[None]) >= cap, cand, t)

    thr = pltpu.bitcast(lax.fori_loop(0, 30, search, jnp.zeros((N_EXPERTS, LANES), I32)), F32)
    gt = aff > thr[None]
    eq = aff == thr[None]
    need = cap - count(gt)

    kk = lax.broadcasted_iota(I32, (LANES, LANES), 0)
    nn = lax.broadcasted_iota(I32, (LANES, LANES), 1)
    upper = (kk <= nn).astype(BF16)
    ones = jnp.ones((LANES, LANES), BF16)

    def prefix(mask):
        m2 = mask.astype(F32).astype(BF16).reshape(rows, LANES)
        p = _dot(m2, upper).reshape(shape3)
        s = _dot(m2, ones).reshape(shape3)
        offs = []
        run = jnp.zeros((N_EXPERTS, LANES), F32)
        for b in range(nb):
            offs.append(run)
            run = run + s[b]
        return p, jnp.stack(offs, axis=0), s

    pe, oe, _ = prefix(eq)
    sel = gt | (eq & ((pe + oe) <= need[None]))
    ps, os_, ss = prefix(sel)

    lane = lax.broadcasted_iota(I32, (rows, LANES), 1)
    sel2 = sel.reshape(rows, LANES)
    dist = jnp.where(sel2, lane - (ps.reshape(rows, LANES).astype(I32) - 1), 0)
    alive = sel2.astype(I32)
    val = lane
    gval = aff.reshape(rows, LANES)
    for k in range(7):
        s = 1 << k
        move = alive * ((dist >> k) & 1)
        inc = pltpu.roll(move, LANES - s, 1) * (lane < LANES - s).astype(I32) == 1
        val = jnp.where(inc, pltpu.roll(val, LANES - s, 1), val)
        gval = jnp.where(inc, pltpu.roll(gval, LANES - s, 1), gval)
        dist = jnp.where(inc, pltpu.roll(dist, LANES - s, 1), dist)
        alive = alive - move + inc.astype(I32)
    local = val.reshape(shape3)
    local_gate = gval.reshape(shape3)

    top = os_ + ss
    lane_e = lax.broadcasted_iota(I32, (N_EXPERTS, LANES), 1)
    for jc in range(cap // LANES):
        j = (lane_e + jc * LANES).astype(F32)
        acc = jnp.zeros((N_EXPERTS, LANES), I32)
        gacc = jnp.zeros((N_EXPERTS, LANES), F32)
        for b in range(nb):
            inside = (os_[b] <= j) & (j < top[b])
            jl = (j - os_[b]).astype(I32) & (LANES - 1)
            acc = jnp.where(inside, jnp.take_along_axis(local[b], jl, axis=1) + b * LANES, acc)
            gacc = jnp.where(inside, jnp.take_along_axis(local_gate[b], jl, axis=1), gacc)
        out_rows = pl.ds(jc, N_EXPERTS, stride=cap // LANES)
        src_ref[out_rows, :] = acc * HP_ROWS
        dst_ref[out_rows, :] = acc * Y_ROWS
        gate_ref[out_rows, :] = gacc


def _select(aff, cap):
    rows = aff.shape[0]
    out_rows = N_EXPERTS * cap // LANES
    out_spec = pl.BlockSpec((out_rows, LANES), lambda i: (0, 0))
    src, dst, gate = pl.pallas_call(
        functools.partial(_select_kernel, cap=cap),
        out_shape=[jax.ShapeDtypeStruct((out_rows, LANES), I32),
                   jax.ShapeDtypeStruct((out_rows, LANES), I32),
                   jax.ShapeDtypeStruct((out_rows, LANES), F32)],
        grid=(1,),
        in_specs=[pl.BlockSpec((rows, LANES), lambda i: (0, 0))],
        out_specs=[out_spec, out_spec, out_spec],
        compiler_params=_cparams(1),
        name="select",
    )(aff)
    return src.reshape(-1), dst.reshape(-1), gate


GATHER_UNROLL = 32
GATHER_STEP_ROWS = 2048


def _gather_kernel(src_ref, hp_ref, o_ref):
    i = pl.program_id(0)
    n = o_ref.shape[0] // HP_ROWS

    def gather(g, carry):
        base = i * n + g * GATHER_UNROLL
        for r in range(GATHER_UNROLL):
            src = pl.multiple_of(src_ref[base + r], HP_ROWS)
            dst = pl.multiple_of((g * GATHER_UNROLL + r) * HP_ROWS, HP_ROWS)
            o_ref[pl.ds(dst, HP_ROWS), :] = hp_ref[pl.ds(src, HP_ROWS), :]
        return carry

    lax.fori_loop(0, n // GATHER_UNROLL, gather, 0)


def _gather(src, hp):
    n = src.shape[0]
    return pl.pallas_call(
        _gather_kernel,
        out_shape=jax.ShapeDtypeStruct((n * HP_ROWS, LANES), U32),
        grid_spec=pltpu.PrefetchScalarGridSpec(
            num_scalar_prefetch=1,
            grid=(n // GATHER_STEP_ROWS,),
            in_specs=[pl.BlockSpec(hp.shape, lambda i, src: (0, 0), pipeline_mode=pl.Buffered(1))],
            out_specs=pl.BlockSpec((GATHER_STEP_ROWS * HP_ROWS, LANES), lambda i, src: (i, 0))),
        compiler_params=_cparams(1),
        name="gather",
    )(src, hp)


FF_CHUNK = 256


def _ffn_kernel(*refs, n_streams):
    e = pl.program_id(0)
    ins = refs[:2 * n_streams]
    wg_ref, wu_ref, wd_ref = refs[2 * n_streams:2 * n_streams + 3]
    outs = refs[2 * n_streams + 3:]
    d_ff = wg_ref.shape[3]
    eye = lax.broadcasted_iota(I32, (LANES, LANES), 0) == lax.broadcasted_iota(I32, (LANES, LANES), 1)
    for i in range(n_streams):
        xp_ref, gate_ref, y_ref = ins[2 * i], ins[2 * i + 1], outs[i]
        cap = xp_ref.shape[0] // HP_ROWS
        halves = [[], []]
        for s in range(HP_ROWS):
            w = xp_ref[pl.ds(s, cap, stride=HP_ROWS), :]
            for half in range(2):
                v = pltpu.unpack_elementwise(w, index=half, packed_dtype=BF16, unpacked_dtype=F32)
                halves[half].append(v.astype(BF16))
        xe = jnp.concatenate(halves[0] + halves[1], axis=1)
        y = None
        for c in range(0, d_ff, FF_CHUNK):
            a = _dot(xe, wg_ref[0, 0, :, c:c + FF_CHUNK].astype(BF16))
            u = _dot(xe, wu_ref[0, 0, :, c:c + FF_CHUNK].astype(BF16))
            yc = _dot((_silu(a) * u).astype(BF16), wd_ref[0, 0, c:c + FF_CHUNK, :].astype(BF16))
            y = yc if y is None else y + yc
        for c in range(cap // LANES):
            g_row = gate_ref[pl.ds(e * (cap // LANES) + c, 1), :]
            g_col = jnp.sum(jnp.where(eye, g_row, 0.0), axis=1, keepdims=True)
            tot = y[c * LANES:(c + 1) * LANES, :] * g_col
            for s in range(Y_ROWS):
                y_ref[0, pl.ds(c * LANES * Y_ROWS + s, LANES, stride=Y_ROWS), :] = tot[:, s * LANES:(s + 1) * LANES]


def _ffn(streams, layer, w_gate, w_up, w_down):
    caps = [xp.shape[0] // (N_EXPERTS * HP_ROWS) for xp, _ in streams]
    in_specs, args = [], []
    for (xp, gate), cap in zip(streams, caps):
        in_specs += [pl.BlockSpec((cap * HP_ROWS, LANES), lambda e: (e, 0)),
                     pl.BlockSpec(gate.shape, lambda e: (0, 0))]
        args += [xp, gate]
    w_spec = pl.BlockSpec((1, 1) + w_gate.shape[2:], lambda e: (layer, e, 0, 0))
    return pl.pallas_call(
        functools.partial(_ffn_kernel, n_streams=len(streams)),
        out_shape=[jax.ShapeDtypeStruct((N_EXPERTS, cap * Y_ROWS, LANES), F32) for cap in caps],
        grid=(N_EXPERTS,),
        in_specs=in_specs + [w_spec, w_spec, w_spec],
        out_specs=[pl.BlockSpec((1, cap * Y_ROWS, LANES), lambda e: (e, 0, 0)) for cap in caps],
        compiler_params=_cparams(1),
        name="ffn",
    )(*args, w_gate, w_up, w_down)


SCATTER_UNROLL = 16
SCATTER_GROUPS = 2


def _combine_kernel(dst_ref, y_ref, x1_ref, mod_ref, o_ref, acc_ref, *, n_scatter):
    s = pl.program_id(0)
    tf = o_ref.shape[0]
    step_rows = y_ref.shape[0] // Y_ROWS

    @pl.when(s == 0)
    def _():
        acc_ref[...] = jnp.zeros_like(acc_ref)

    @pl.when(s < n_scatter)
    def _():
        def scatter(g, carry):
            for sub in range(SCATTER_GROUPS):
                first = (g * SCATTER_GROUPS + sub) * SCATTER_UNROLL
                base = s * step_rows + first
                rows, sums = [], []
                for r in range(SCATTER_UNROLL):
                    dst = pl.ds(pl.multiple_of(dst_ref[base + r], Y_ROWS), Y_ROWS)
                    src = pl.ds(pl.multiple_of((first + r) * Y_ROWS, Y_ROWS), Y_ROWS)
                    rows.append(dst)
                    sums.append(acc_ref[dst, :] + y_ref[src, :])
                for dst, v in zip(rows, sums):
                    acc_ref[dst, :] = v
            return carry

        lax.fori_loop(0, step_rows // (SCATTER_UNROLL * SCATTER_GROUPS), scatter, 0)

    @pl.when(s >= n_scatter)
    def _():
        g2 = mod_ref[0, 0][5:6]
        tile = acc_ref.at[pl.ds(pl.multiple_of((s - n_scatter) * tf * Y_ROWS, tf * Y_ROWS), tf * Y_ROWS), :]
        for c in range(Y_ROWS):
            sl = slice(c * LANES, (c + 1) * LANES)
            o_ref[:, sl] = x1_ref[:, sl] + g2[:, sl] * tile[pl.ds(c, tf, stride=Y_ROWS), :]


def _combine_tiles(n_tok):
    row_bytes = D_MODEL * 4
    free_rows = (VMEM_LIMIT - n_tok * row_bytes) * 4 // 5 // row_bytes
    scale = max(k for k in (1, 2, 4) if 2 * 512 * k + 4 * 256 * k <= free_rows)
    return 256 * scale, 512 * scale


def _combine(dst, y, x1, mods, layer, row_of_token):
    n_tok = x1.shape[0]
    tf, step_rows = _combine_tiles(n_tok)
    n_scatter = dst.shape[0] // step_rows
    assert (dst.shape[0] // N_EXPERTS) % SCATTER_UNROLL == 0 and dst.shape[0] % step_rows == 0
    tile_of = lambda s: jnp.maximum(s - n_scatter, 0)
    row_of_tile = lambda t: row_of_token(t * tf)
    return pl.pallas_call(
        functools.partial(_combine_kernel, n_scatter=n_scatter),
        out_shape=jax.ShapeDtypeStruct((n_tok, D_MODEL), F32),
        grid_spec=pltpu.PrefetchScalarGridSpec(
            num_scalar_prefetch=1,
            grid=(n_scatter + n_tok // tf,),
            in_specs=[pl.BlockSpec((step_rows * Y_ROWS, LANES), lambda s, *_: (jnp.minimum(s, n_scatter - 1), 0)),
                      pl.BlockSpec((tf, D_MODEL), lambda s, *_: (tile_of(s), 0)),
                      pl.BlockSpec((1, 1, N_MOD, D_MODEL), lambda s, *_: (layer, row_of_tile(tile_of(s)), 0, 0))],
            out_specs=pl.BlockSpec((tf, D_MODEL), lambda s, *_: (tile_of(s), 0)),
            scratch_shapes=[pltpu.VMEM((n_tok * Y_ROWS, LANES), F32)]),
        compiler_params=_cparams(1),
        name="combine",
    )(dst, y.reshape(-1, LANES), x1, mods)


TOKEN_TILE = 1024
Q_TILE = 512


def _moe(fronts, rows_of_token, mods, layer, w_gate, w_up, w_down):
    routed, dsts = [], []
    for x1, hp, aff in fronts:
        cap = CAPACITY_FACTOR * x1.shape[0] // N_EXPERTS
        src, dst, gate = _select(aff, cap)
        routed.append((_gather(src, hp), gate))
        dsts.append(dst)
    ys = _ffn(routed, layer, w_gate, w_up, w_down)
    return [_combine(dst, y, x1, mods, layer, row_of_token)
            for dst, y, (x1, _, _), row_of_token in zip(dsts, ys, fronts, rows_of_token)]


def kernel(x_prompt, x_sample, cache_k, cache_v, c, c_ctx, norm1, norm2, w_ada, b_ada, w_qkv, q_norm, k_norm,
           w_o, w_pool, pool_scale, w_router, w_e_gate, w_e_up, w_e_down):
    batch, seq, _ = x_prompt.shape
    dec_batch, dec_seq, _ = x_sample.shape
    depth = w_ada.shape[0]
    kvw = N_KV * HEAD_DIM

    cvec = jnp.zeros((SUBLANES, D_MODEL), F32).at[0].set(c_ctx).at[1:1 + dec_batch].set(c)
    mods = _ada(cvec, w_ada, b_ada)
    ctx = x_prompt.reshape(batch * seq, D_MODEL)
    lat = x_sample.reshape(dec_batch * dec_seq, D_MODEL)
    ctx_row = lambda i: 0
    lat_row = lambda tile: (lambda i: 1 + (i * tile) // dec_seq)
    ctx_tile = TOKEN_TILE if TOKEN_TILE % seq == 0 else min(TOKEN_TILE, seq)
    rope_tabs = _rope_tables(dec_seq)
    new_k = new_v = None

    for layer in range(depth):
        j = layer // 2
        n1 = norm1[layer][None]
        n2 = norm2[layer][None]
        wr = w_router
        if layer % 2 == 0:
            qn = q_norm[j][None]
            kn = k_norm[j][None]
            qc, kc, vc, new_k, new_v = _qkv(ctx, mods, layer, ctx_row, n1, w_qkv[j], qn, kn, None, True, TOKEN_TILE)
            oc = _attn_ctx(qc, kc, vc, seq, TOKEN_TILE)
            ql, kl, vl = _qkv(lat, mods, layer, lat_row(TOKEN_TILE), n1, w_qkv[j], qn, kn, rope_tabs, False,
                              TOKEN_TILE)
            past_k = cache_k[:, j].reshape(dec_batch, -1, kvw).astype(BF16)
            past_v = cache_v[:, j].reshape(dec_batch, -1, kvw).astype(BF16)
            ol = _attn_lat(ql, kl, vl, past_k, past_v, dec_seq, Q_TILE)
            ctx1 = _post(oc, ctx, mods, layer, ctx_row, n2, w_o[j], wr, TOKEN_TILE)
            lat1 = _post(ol, lat, mods, layer, lat_row(TOKEN_TILE), n2, w_o[j], wr, TOKEN_TILE)
        else:
            ps = pool_scale[j][None]
            ctx1 = _pool(ctx, mods, layer, ctx_row, n1, n2, w_pool[j], ps, wr, seq, ctx_tile)
            lat1 = _pool(lat, mods, layer, lat_row(TOKEN_TILE), n1, n2, w_pool[j], ps, wr, dec_seq, TOKEN_TILE)
        ctx, lat = _moe([ctx1, lat1], [ctx_row, lat_row(1)], mods, layer, w_e_gate, w_e_up, w_e_down)

    new_cache_k = new_k.reshape(batch, 1, seq, N_KV, HEAD_DIM)
    new_cache_v = new_v.reshape(batch, 1, seq, N_KV, HEAD_DIM)
    return (ctx.reshape(batch, seq, D_MODEL), lat.reshape(dec_batch, dec_seq, D_MODEL), new_cache_k, new_cache_v)
```

```python
import functools

import jax
import jax.numpy as jnp
import numpy as np
from jax import lax
from jax.experimental import pallas as pl
from jax.experimental.pallas import tpu as pltpu

F32 = jnp.float32
BF16 = jnp.bfloat16
I32 = jnp.int32
U32 = jnp.uint32

D_MODEL = 1024
HEAD_DIM = 128
N_HEADS = 8
N_KV = 2
Q_PER_KV = N_HEADS // N_KV
QKV_DIM = (N_HEADS + 2 * N_KV) * HEAD_DIM
GRID_W = 64
ROPE_THETA = 10000.0
POOL_WINDOWS = (2, 4, 8, 16)
POOL_GROUP = D_MODEL // len(POOL_WINDOWS)
POOL_HALO = max(POOL_WINDOWS) // 2
N_EXPERTS = 16
CAPACITY_FACTOR = 2
N_MOD = 6
EPS = 1e-6
LOG2_E = 1.4426950408889634

LANES = 128
SUBLANES = 8
HALF = D_MODEL // 2
HP_ROWS = HALF // LANES
Y_ROWS = D_MODEL // LANES
VMEM_LIMIT = 56 * 1024 * 1024


def _cparams(n_axes, vmem=VMEM_LIMIT):
    return pltpu.CompilerParams(dimension_semantics=("arbitrary",) * n_axes, vmem_limit_bytes=vmem)


def _silu(x):
    return x / (1.0 + jnp.exp(-x))


def _rms(x, gain):
    return x * lax.rsqrt(jnp.mean(x * x, axis=-1, keepdims=True) + EPS) * gain


def _norm_mod(x, gain, shift, scale):
    return _rms(x, gain * (1.0 + scale)) + shift


def _dot(a, b):
    return jnp.dot(a, b, preferred_element_type=F32)


def _dot_nt(a, b):
    return lax.dot_general(a, b, (((1,), (1,)), ((), ())), preferred_element_type=F32)


def _ada_kernel(c_ref, w_ref, b_ref, o_ref):
    s = _silu(c_ref[...]).astype(BF16)
    o_ref[0] = _dot(s, w_ref[0].astype(BF16)) + b_ref[0]


def _ada(cvec, w_ada, b_ada):
    depth = w_ada.shape[0]
    out = pl.pallas_call(
        _ada_kernel,
        out_shape=jax.ShapeDtypeStruct((depth, SUBLANES, N_MOD * D_MODEL), F32),
        grid=(depth, N_MOD),
        in_specs=[
            pl.BlockSpec((SUBLANES, D_MODEL), lambda i, j: (0, 0)),
            pl.BlockSpec((1, D_MODEL, D_MODEL), lambda i, j: (i, 0, j)),
            pl.BlockSpec((1, 1, D_MODEL), lambda i, j: (i, 0, j)),
        ],
        out_specs=pl.BlockSpec((1, SUBLANES, D_MODEL), lambda i, j: (i, 0, j)),
        compiler_params=_cparams(2),
        name="ada",
    )(cvec, w_ada, b_ada.reshape(depth, 1, N_MOD * D_MODEL))
    return out.reshape(depth, SUBLANES, N_MOD, D_MODEL)


def _mod_spec(layer, row_of_tile):
    return pl.BlockSpec((1, 1, N_MOD, D_MODEL), lambda i, *_: (layer, row_of_tile(i), 0, 0))


def _qkv_kernel(*refs, rope, cache_out):
    x_ref, mod_ref, n1_ref, w_ref, qn_ref, kn_ref = refs[:6]
    refs = refs[6:]
    if rope:
        cos_ref, sin_ref = refs[:2]
        refs = refs[2:]
    q_ref, k_ref, v_ref = refs[:3]
    refs = refs[3:]
    if cache_out:
        kc_ref, vc_ref = refs[:2]
        refs = refs[2:]
    (wb_ref,) = refs

    qk_w = (N_HEADS + N_KV) * HEAD_DIM
    quarter = HEAD_DIM // 4

    def partner(a):
        width = a.shape[1]
        first = (lax.broadcasted_iota(I32, a.shape, 1) & quarter) == 0
        return jnp.where(first, pltpu.roll(a, width - quarter, 1), pltpu.roll(a, quarter, 1))

    @pl.when(pl.program_id(0) == 0)
    def _():
        w = w_ref[...]
        wb_ref[:, 0:QKV_DIM] = w.astype(BF16)
        if rope:
            wb_ref[:, QKV_DIM:] = partner(w[:, 0:qk_w]).astype(BF16)

    m = mod_ref[0, 0]
    h = _norm_mod(x_ref[...], n1_ref[...], m[0:1], m[1:2])
    qkv = _dot(h.astype(BF16), wb_ref[...])
    scale = HEAD_DIM ** -0.5 * LOG2_E
    if rope:
        gains = {True: qn_ref[...], False: kn_ref[...]}
        cos_g = {key: cos_ref[...] * g for key, g in gains.items()}
        sin_g = {key: sin_ref[...] * partner(g) for key, g in gains.items()}
    for hh in range(N_HEADS + N_KV):
        sl = slice(hh * HEAD_DIM, (hh + 1) * HEAD_DIM)
        is_q = hh < N_HEADS
        if rope:
            raw = qkv[:, sl]
            norm = lax.rsqrt(jnp.mean(raw * raw, axis=-1, keepdims=True) + EPS)
            xh = (raw * cos_g[is_q] + qkv[:, QKV_DIM + hh * HEAD_DIM:QKV_DIM + (hh + 1) * HEAD_DIM] * sin_g[is_q]) * norm
        else:
            xh = _rms(qkv[:, sl], qn_ref[...] if is_q else kn_ref[...])
        if cache_out and hh >= N_HEADS:
            kc_ref[:, hh - N_HEADS, :] = xh
        if hh < N_HEADS:
            q_ref[:, sl] = (xh * scale).astype(BF16)
        else:
            k_ref[:, (hh - N_HEADS) * HEAD_DIM:(hh - N_HEADS + 1) * HEAD_DIM] = xh.astype(BF16)
    v = qkv[:, qk_w:QKV_DIM]
    v_ref[...] = v.astype(BF16)
    if cache_out:
        for g in range(N_KV):
            vc_ref[:, g, :] = v[:, g * HEAD_DIM:(g + 1) * HEAD_DIM]


def _qkv(x, mods, layer, row_of_tile, n1, w_qkv, qn, kn, rope_tabs, cache_out, tm):
    n_tok = x.shape[0]
    kvw = N_KV * HEAD_DIM
    rope = rope_tabs is not None
    in_specs = [
        pl.BlockSpec((tm, D_MODEL), lambda i: (i, 0)),
        _mod_spec(layer, row_of_tile),
        pl.BlockSpec((1, D_MODEL), lambda i: (0, 0)),
        pl.BlockSpec((D_MODEL, QKV_DIM), lambda i: (0, 0)),
        pl.BlockSpec((1, HEAD_DIM), lambda i: (0, 0)),
        pl.BlockSpec((1, HEAD_DIM), lambda i: (0, 0)),
    ]
    args = [x, mods, n1, w_qkv, qn, kn]
    if rope:
        seq_tiles = rope_tabs[0].shape[0] // tm
        in_specs += [pl.BlockSpec((tm, HEAD_DIM), lambda i: (i % seq_tiles, 0))] * 2
        args += list(rope_tabs)
    out_shape = [jax.ShapeDtypeStruct((n_tok, D_MODEL), BF16),
                 jax.ShapeDtypeStruct((n_tok, kvw), BF16),
                 jax.ShapeDtypeStruct((n_tok, kvw), BF16)]
    out_specs = [pl.BlockSpec((tm, D_MODEL), lambda i: (i, 0)),
                 pl.BlockSpec((tm, kvw), lambda i: (i, 0)),
                 pl.BlockSpec((tm, kvw), lambda i: (i, 0))]
    if cache_out:
        out_shape += [jax.ShapeDtypeStruct((n_tok, N_KV, HEAD_DIM), F32)] * 2
        out_specs += [pl.BlockSpec((tm, N_KV, HEAD_DIM), lambda i: (i, 0, 0))] * 2
    return pl.pallas_call(
        functools.partial(_qkv_kernel, rope=rope, cache_out=cache_out),
        out_shape=out_shape,
        grid=(n_tok // tm,),
        in_specs=in_specs,
        out_specs=out_specs,
        scratch_shapes=[pltpu.VMEM((D_MODEL, QKV_DIM + ((N_HEADS + N_KV) * HEAD_DIM if rope else 0)), BF16)],
        compiler_params=_cparams(1),
        name="qkv_rope" if rope else "qkv",
    )(*args)


def _rope_tables(seq_len):
    half = HEAD_DIM // 2
    n = half // 2
    inv_freq = ROPE_THETA ** (-np.arange(n, dtype=np.float64) / n)
    rows = seq_len // GRID_W
    row = np.repeat(np.arange(rows), GRID_W).astype(np.float64)
    col = np.tile(np.arange(GRID_W), rows).astype(np.float64)
    ang_r = row[:, None] * inv_freq[None, :]
    ang_c = col[:, None] * inv_freq[None, :]
    cos = np.concatenate([np.cos(ang_r)] * 2 + [np.cos(ang_c)] * 2, axis=-1)
    sin = np.concatenate([-np.sin(ang_r), np.sin(ang_r), -np.sin(ang_c), np.sin(ang_c)], axis=-1)
    return jnp.asarray(cos, F32), jnp.asarray(sin, F32)


def _with_ones(v):
    return jnp.concatenate([v, jnp.ones_like(v)], axis=1)


def _gqa_attention(q, segments):
    rows = q.shape[0]
    qs = jnp.concatenate([q[:, h * HEAD_DIM:(h + 1) * HEAD_DIM] for h in range(Q_PER_KV)], axis=0)
    m = acc = None
    for k, v in segments:
        s = _dot_nt(qs, k)
        seg_max = jnp.max(s, axis=-1, keepdims=True)
        m_new = seg_max if m is None else jnp.maximum(m, seg_max)
        pv = _dot(jnp.exp2(s - m_new).astype(BF16), v)
        acc = pv if m is None else acc * jnp.exp2(m - m_new) + pv
        m = m_new
    o = (acc[:, :HEAD_DIM] / acc[:, HEAD_DIM:]).astype(BF16)
    return jnp.concatenate([o[h * rows:(h + 1) * rows] for h in range(Q_PER_KV)], axis=1)


def _attn_ctx_kernel(q_ref, k_ref, v_ref, o_ref, *, seq):
    gw = Q_PER_KV * HEAD_DIM
    for r in range(0, q_ref.shape[0], seq):
        for g in range(N_KV):
            kg = k_ref[r:r + seq, g * HEAD_DIM:(g + 1) * HEAD_DIM]
            vg = _with_ones(v_ref[r:r + seq, g * HEAD_DIM:(g + 1) * HEAD_DIM])
            o_ref[r:r + seq, g * gw:(g + 1) * gw] = _gqa_attention(q_ref[r:r + seq, g * gw:(g + 1) * gw], [(kg, vg)])


def _attn_ctx(q, k, v, seq, tm):
    n_tok = q.shape[0]
    kvw = N_KV * HEAD_DIM
    return pl.pallas_call(
        functools.partial(_attn_ctx_kernel, seq=seq),
        out_shape=jax.ShapeDtypeStruct((n_tok, D_MODEL), BF16),
        grid=(n_tok // tm,),
        in_specs=[pl.BlockSpec((tm, D_MODEL), lambda b: (b, 0)),
                  pl.BlockSpec((tm, kvw), lambda b: (b, 0)),
                  pl.BlockSpec((tm, kvw), lambda b: (b, 0))],
        out_specs=pl.BlockSpec((tm, D_MODEL), lambda b: (b, 0)),
        compiler_params=_cparams(1),
        name="attn_ctx",
    )(q, k, v)


KEY_CHUNK = 256


def _attn_lat_kernel(q_ref, k_ref, v_ref, kc_ref, vc_ref, o_ref):
    segments = [(k_ref[c:c + KEY_CHUNK], _with_ones(v_ref[c:c + KEY_CHUNK]))
                for c in range(0, k_ref.shape[0], KEY_CHUNK)]
    segments.append((kc_ref[0], _with_ones(vc_ref[0])))
    o_ref[...] = _gqa_attention(q_ref[...], segments)


def _attn_lat(q, k, v, kc, vc, seq, tq):
    n_tok = q.shape[0]
    batch = n_tok // seq
    past = kc.shape[1]
    qt = seq // tq
    gw = Q_PER_KV * HEAD_DIM
    return pl.pallas_call(
        _attn_lat_kernel,
        out_shape=jax.ShapeDtypeStruct((n_tok, D_MODEL), BF16),
        grid=(batch, N_KV, qt),
        in_specs=[pl.BlockSpec((tq, gw), lambda b, g, i: (b * qt + i, g)),
                  pl.BlockSpec((seq, HEAD_DIM), lambda b, g, i: (b, g)),
                  pl.BlockSpec((seq, HEAD_DIM), lambda b, g, i: (b, g)),
                  pl.BlockSpec((1, past, HEAD_DIM), lambda b, g, i: (b, 0, g)),
                  pl.BlockSpec((1, past, HEAD_DIM), lambda b, g, i: (b, 0, g))],
        out_specs=pl.BlockSpec((tq, gw), lambda b, g, i: (b * qt + i, g)),
        compiler_params=_cparams(3),
        name="attn_lat",
    )(q, k, v, kc, vc)


def _router_softmax(hb, wr):
    wide = jnp.concatenate([wr, jnp.zeros((wr.shape[0], LANES - N_EXPERTS), wr.dtype)], axis=1)
    logits = _dot(hb, wide.astype(BF16))
    lane = lax.broadcasted_iota(I32, logits.shape, 1)
    logits = jnp.where(lane < N_EXPERTS, logits, -1e30)
    ex = jnp.exp(logits - jnp.max(logits, axis=-1, keepdims=True))
    return ex / jnp.sum(ex, axis=-1, keepdims=True)


def _moe_front(x1, m, n2, wr, x1_ref, hp_ref, aff_ref):
    tm = x1.shape[0]
    x1_ref[...] = x1
    h2 = _norm_mod(x1, n2, m[3:4], m[4:5])
    aff = _router_softmax(h2.astype(BF16), wr)
    packed = pltpu.pack_elementwise([h2[:, :HALF], h2[:, HALF:]], packed_dtype=BF16)
    for s in range(HP_ROWS):
        hp_ref[pl.ds(s, tm, stride=HP_ROWS), :] = packed[:, s * LANES:(s + 1) * LANES]
    aff_t = aff.T
    for b in range(tm // LANES):
        aff_ref[b * N_EXPERTS:(b + 1) * N_EXPERTS, :] = aff_t[0:N_EXPERTS, b * LANES:(b + 1) * LANES]


def _front_out(n_tok, tm):
    shapes = [jax.ShapeDtypeStruct((n_tok, D_MODEL), F32),
              jax.ShapeDtypeStruct((n_tok * HP_ROWS, LANES), U32),
              jax.ShapeDtypeStruct((n_tok // LANES * N_EXPERTS, LANES), F32)]
    specs = [pl.BlockSpec((tm, D_MODEL), lambda i: (i, 0)),
             pl.BlockSpec((tm * HP_ROWS, LANES), lambda i: (i, 0)),
             pl.BlockSpec((tm // LANES * N_EXPERTS, LANES), lambda i: (i, 0))]
    return shapes, specs


def _post_kernel(o_ref, x_ref, mod_ref, n2_ref, wo_ref, wr_ref, x1_ref, hp_ref, aff_ref, wb_ref):
    @pl.when(pl.program_id(0) == 0)
    def _():
        wb_ref[...] = wo_ref[...].astype(BF16)

    m = mod_ref[0, 0]
    x1 = x_ref[...] + m[2:3] * _dot(o_ref[...], wb_ref[...])
    _moe_front(x1, m, n2_ref[...], wr_ref[0], x1_ref, hp_ref, aff_ref)


def _post(o, x, mods, layer, row_of_tile, n2, w_o, wr, tm):
    n_tok = x.shape[0]
    shapes, specs = _front_out(n_tok, tm)
    return pl.pallas_call(
        _post_kernel,
        out_shape=shapes,
        grid=(n_tok // tm,),
        in_specs=[pl.BlockSpec((tm, D_MODEL), lambda i: (i, 0)),
                  pl.BlockSpec((tm, D_MODEL), lambda i: (i, 0)),
                  _mod_spec(layer, row_of_tile),
                  pl.BlockSpec((1, D_MODEL), lambda i: (0, 0)),
                  pl.BlockSpec((D_MODEL, D_MODEL), lambda i: (0, 0)),
                  pl.BlockSpec((1, D_MODEL, N_EXPERTS), lambda i: (layer, 0, 0))],
        out_specs=specs,
        scratch_shapes=[pltpu.VMEM((D_MODEL, D_MODEL), BF16)],
        compiler_params=_cparams(1),
        name="post",
    )(o, x, mods, n2, w_o, wr)


def _pool_kernel(x_ref, xp_ref, xn_ref, mod_ref, n1_ref, n2_ref, wp_ref, ps_ref, wr_ref,
                 x1_ref, hp_ref, aff_ref, *, seq):
    tm = x_ref.shape[0]
    i = pl.program_id(0)
    m = mod_ref[0, 0]
    x = x_ref[...]

    def norm_mod(v):
        return _norm_mod(v, n1_ref[...], m[0:1], m[1:2])

    h = norm_mod(x)
    if tm <= seq:
        prev_ok = ((i * tm) % seq != 0).astype(F32)
        next_ok = (((i + 1) * tm) % seq != 0).astype(F32)
        pieces = [(norm_mod(xp_ref[...]) * prev_ok, h, norm_mod(xn_ref[...]) * next_ok)]
        t0 = (i * tm) % seq
    else:
        zeros = jnp.zeros((POOL_HALO, D_MODEL), F32)
        pieces = [(zeros, h[r:r + seq], zeros) for r in range(0, tm, seq)]
        t0 = 0
    n = min(tm, seq)
    rows = n + 2 * POOL_HALO
    t = t0 + lax.broadcasted_iota(I32, (n, 1), 0)
    deltas = [[] for _ in POOL_WINDOWS]
    for piece in pieces:
        hz = jnp.concatenate(piece, axis=0)
        for g, w in enumerate(POOL_WINDOWS):
            sl = slice(g * POOL_GROUP, (g + 1) * POOL_GROUP)
            f = hz[:, sl]
            step = 1
            while step < w:
                f = f + pltpu.roll(f, rows - step, 0)
                step *= 2
            win = pltpu.roll(f, w // 2, 0)[POOL_HALO:POOL_HALO + n]
            cnt = (jnp.minimum(t + w // 2, seq) - jnp.maximum(t - w // 2, 0)).astype(F32)
            deltas[g].append((win / cnt - piece[1][:, sl]).astype(BF16))
    ys = [_dot(jnp.concatenate(d, axis=0), wp_ref[g].astype(BF16)) for g, d in enumerate(deltas)]
    x1 = x + (m[2:3] * ps_ref[...]) * jnp.concatenate(ys, axis=-1)
    _moe_front(x1, m, n2_ref[...], wr_ref[0], x1_ref, hp_ref, aff_ref)


def _pool(x, mods, layer, row_of_tile, n1, n2, w_pool, pool_scale, wr, seq, tm):
    n_tok = x.shape[0]
    hb = tm // POOL_HALO
    last = n_tok // POOL_HALO - 1
    shapes, specs = _front_out(n_tok, tm)
    return pl.pallas_call(
        functools.partial(_pool_kernel, seq=seq),
        out_shape=shapes,
        grid=(n_tok // tm,),
        in_specs=[pl.BlockSpec((tm, D_MODEL), lambda i: (i, 0)),
                  pl.BlockSpec((POOL_HALO, D_MODEL), lambda i: (jnp.maximum(i * hb - 1, 0), 0)),
                  pl.BlockSpec((POOL_HALO, D_MODEL), lambda i: (jnp.minimum((i + 1) * hb, last), 0)),
                  _mod_spec(layer, row_of_tile),
                  pl.BlockSpec((1, D_MODEL), lambda i: (0, 0)),
                  pl.BlockSpec((1, D_MODEL), lambda i: (0, 0)),
                  pl.BlockSpec(w_pool.shape, lambda i: (0, 0, 0)),
                  pl.BlockSpec((1, D_MODEL), lambda i: (0, 0)),
                  pl.BlockSpec((1, D_MODEL, N_EXPERTS), lambda i: (layer, 0, 0))],
        out_specs=specs,
        compiler_params=_cparams(1),
        name="pool",
    )(x, x, x, mods, n1, n2, w_pool, pool_scale, wr)


DIST_BIT = 8


def _select_kernel(aff_ref, src_ref, dst_ref, gate_ref, *, cap):
    nb = aff_ref.shape[0] // N_EXPERTS
    rows = nb * N_EXPERTS
    shape3 = (nb, N_EXPERTS, LANES)
    aff = aff_ref[...].reshape(shape3)

    def count(mask):
        return jnp.sum(jnp.sum(mask.astype(F32), axis=0), axis=-1, keepdims=True)

    def search(i, t):
        shift = 28 - 2 * i
        best = t
        for digit in (1, 2, 3):
            cand = t | (jnp.int32(digit) << shift)
            best = jnp.where(count(aff >= pltpu.bitcast(cand, F32)[None]) >= cap, cand, best)
        return best

    thr = pltpu.bitcast(lax.fori_loop(0, 15, search, jnp.zeros((N_EXPERTS, LANES), I32)), F32)
    gt = aff > thr[None]
    eq = aff == thr[None]
    need = cap - count(gt)

    kk = lax.broadcasted_iota(I32, (LANES, LANES), 0)
    nn = lax.broadcasted_iota(I32, (LANES, LANES), 1)
    upper = (kk <= nn).astype(BF16)
    ones = jnp.ones((LANES, LANES), BF16)

    def prefix(mask):
        m2 = mask.astype(F32).astype(BF16).reshape(rows, LANES)
        p = _dot(m2, upper).reshape(shape3)
        s = _dot(m2, ones).reshape(shape3)
        offs = []
        run = jnp.zeros((N_EXPERTS, LANES), F32)
        for b in range(nb):
            offs.append(run)
            run = run + s[b]
        return p, jnp.stack(offs, axis=0), s

    pe, oe, _ = prefix(eq)
    sel = gt | (eq & ((pe + oe) <= need[None]))
    ps, os_, ss = prefix(sel)

    lane = lax.broadcasted_iota(I32, (rows, LANES), 1)
    sel2 = sel.reshape(rows, LANES)
    dist = lane - (ps.reshape(rows, LANES).astype(I32) - 1)
    word = jnp.where(sel2, lane | (dist << DIST_BIT), 0)
    for k in range(7):
        bit = 1 << (DIST_BIT + k)
        arriving = pltpu.roll(word, LANES - (1 << k), 1)
        word = jnp.where((arriving & bit) != 0, arriving, jnp.where((word & bit) != 0, 0, word))
    val = word & (LANES - 1)
    local = val.reshape(shape3) + lax.broadcasted_iota(I32, shape3, 0) * LANES
    local_gate = jnp.take_along_axis(aff.reshape(rows, LANES), val, axis=1).reshape(shape3)

    first = os_.astype(I32)
    chosen = pltpu.bitcast(ss.astype(I32), U32)
    lane_e = lax.broadcasted_iota(I32, (N_EXPERTS, LANES), 1)
    n_chunks = cap // LANES
    acc = [jnp.zeros((N_EXPERTS, LANES), I32)] * n_chunks
    gacc = [jnp.zeros((N_EXPERTS, LANES), F32)] * n_chunks
    for b in range(nb):
        within = lane_e - first[b]
        phase = within & (LANES - 1)
        tokens = jnp.take_along_axis(local[b], phase, axis=1)
        gates = jnp.take_along_axis(local_gate[b], phase, axis=1)
        for jc in range(n_chunks):
            inside = pltpu.bitcast(within + jc * LANES, U32) < chosen[b]
            acc[jc] = jnp.where(inside, tokens, acc[jc])
            gacc[jc] = jnp.where(inside, gates, gacc[jc])
    for jc in range(n_chunks):
        out_rows = pl.ds(jc, N_EXPERTS, stride=n_chunks)
        src_ref[out_rows, :] = acc[jc] * HP_ROWS
        dst_ref[out_rows, :] = acc[jc] * Y_ROWS
        gate_ref[out_rows, :] = gacc[jc]


def _select(aff, cap):
    rows = aff.shape[0]
    out_rows = N_EXPERTS * cap // LANES
    out_spec = pl.BlockSpec((out_rows, LANES), lambda i: (0, 0))
    src, dst, gate = pl.pallas_call(
        functools.partial(_select_kernel, cap=cap),
        out_shape=[jax.ShapeDtypeStruct((out_rows, LANES), I32),
                   jax.ShapeDtypeStruct((out_rows, LANES), I32),
                   jax.ShapeDtypeStruct((out_rows, LANES), F32)],
        grid=(1,),
        in_specs=[pl.BlockSpec((rows, LANES), lambda i: (0, 0))],
        out_specs=[out_spec, out_spec, out_spec],
        compiler_params=_cparams(1),
        name="select",
    )(aff)
    return src.reshape(-1), dst.reshape(-1), gate


GATHER_UNROLL = 32
GATHER_STEP_ROWS = 2048


def _gather_kernel(src_ref, hp_ref, o_ref):
    i = pl.program_id(0)
    n = o_ref.shape[0] // HP_ROWS

    def gather(g, carry):
        base = i * n + g * GATHER_UNROLL
        for r in range(GATHER_UNROLL):
            src = pl.multiple_of(src_ref[base + r], HP_ROWS)
            dst = pl.multiple_of((g * GATHER_UNROLL + r) * HP_ROWS, HP_ROWS)
            o_ref[pl.ds(dst, HP_ROWS), :] = hp_ref[pl.ds(src, HP_ROWS), :]
        return carry

    lax.fori_loop(0, n // GATHER_UNROLL, gather, 0)


def _gather(src, hp):
    n = src.shape[0]
    return pl.pallas_call(
        _gather_kernel,
        out_shape=jax.ShapeDtypeStruct((n * HP_ROWS, LANES), U32),
        grid_spec=pltpu.PrefetchScalarGridSpec(
            num_scalar_prefetch=1,
            grid=(n // GATHER_STEP_ROWS,),
            in_specs=[pl.BlockSpec(hp.shape, lambda i, src: (0, 0), pipeline_mode=pl.Buffered(1))],
            out_specs=pl.BlockSpec((GATHER_STEP_ROWS * HP_ROWS, LANES), lambda i, src: (i, 0))),
        compiler_params=_cparams(1),
        name="gather",
    )(src, hp)


FF_CHUNK = 256


def _ffn_kernel(*refs, n_streams):
    e = pl.program_id(0)
    ins = refs[:2 * n_streams]
    wg_ref, wu_ref, wd_ref = refs[2 * n_streams:2 * n_streams + 3]
    outs = refs[2 * n_streams + 3:]
    d_ff = wg_ref.shape[3]
    eye = lax.broadcasted_iota(I32, (LANES, LANES), 0) == lax.broadcasted_iota(I32, (LANES, LANES), 1)
    for i in range(n_streams):
        xp_ref, gate_ref, y_ref = ins[2 * i], ins[2 * i + 1], outs[i]
        cap = xp_ref.shape[0] // HP_ROWS
        halves = [[], []]
        for s in range(HP_ROWS):
            w = xp_ref[pl.ds(s, cap, stride=HP_ROWS), :]
            for half in range(2):
                v = pltpu.unpack_elementwise(w, index=half, packed_dtype=BF16, unpacked_dtype=F32)
                halves[half].append(v.astype(BF16))
        xe = jnp.concatenate(halves[0] + halves[1], axis=1)
        y = None
        for c in range(0, d_ff, FF_CHUNK):
            a = _dot(xe, wg_ref[0, 0, :, c:c + FF_CHUNK].astype(BF16))
            u = _dot(xe, wu_ref[0, 0, :, c:c + FF_CHUNK].astype(BF16))
            yc = _dot((_silu(a) * u).astype(BF16), wd_ref[0, 0, c:c + FF_CHUNK, :].astype(BF16))
            y = yc if y is None else y + yc
        for c in range(cap // LANES):
            g_row = gate_ref[pl.ds(e * (cap // LANES) + c, 1), :]
            g_col = jnp.sum(jnp.where(eye, g_row, 0.0), axis=1, keepdims=True)
            tot = y[c * LANES:(c + 1) * LANES, :] * g_col
            for s in range(Y_ROWS):
                y_ref[0, pl.ds(c * LANES * Y_ROWS + s, LANES, stride=Y_ROWS), :] = tot[:, s * LANES:(s + 1) * LANES]


def _ffn(streams, layer, w_gate, w_up, w_down):
    caps = [xp.shape[0] // (N_EXPERTS * HP_ROWS) for xp, _ in streams]
    in_specs, args = [], []
    for (xp, gate), cap in zip(streams, caps):
        in_specs += [pl.BlockSpec((cap * HP_ROWS, LANES), lambda e: (e, 0)),
                     pl.BlockSpec(gate.shape, lambda e: (0, 0))]
        args += [xp, gate]
    w_spec = pl.BlockSpec((1, 1) + w_gate.shape[2:], lambda e: (layer, e, 0, 0))
    return pl.pallas_call(
        functools.partial(_ffn_kernel, n_streams=len(streams)),
        out_shape=[jax.ShapeDtypeStruct((N_EXPERTS, cap * Y_ROWS, LANES), F32) for cap in caps],
        grid=(N_EXPERTS,),
        in_specs=in_specs + [w_spec, w_spec, w_spec],
        out_specs=[pl.BlockSpec((1, cap * Y_ROWS, LANES), lambda e: (e, 0, 0)) for cap in caps],
        compiler_params=_cparams(1),
        name="ffn",
    )(*args, w_gate, w_up, w_down)


SCATTER_UNROLL = 16
SCATTER_GROUPS = 2


def _combine_kernel(dst_ref, y_ref, x1_ref, mod_ref, o_ref, acc_ref, *, n_scatter):
    s = pl.program_id(0)
    tf = o_ref.shape[0]
    step_rows = y_ref.shape[0] // Y_ROWS

    @pl.when(s == 0)
    def _():
        acc_ref[...] = jnp.zeros_like(acc_ref)

    @pl.when(s < n_scatter)
    def _():
        def scatter(g, carry):
            for sub in range(SCATTER_GROUPS):
                first = (g * SCATTER_GROUPS + sub) * SCATTER_UNROLL
                base = s * step_rows + first
                rows, sums = [], []
                for r in range(SCATTER_UNROLL):
                    dst = pl.ds(pl.multiple_of(dst_ref[base + r], Y_ROWS), Y_ROWS)
                    src = pl.ds(pl.multiple_of((first + r) * Y_ROWS, Y_ROWS), Y_ROWS)
                    rows.append(dst)
                    sums.append(acc_ref[dst, :] + y_ref[src, :])
                for dst, v in zip(rows, sums):
                    acc_ref[dst, :] = v
            return carry

        lax.fori_loop(0, step_rows // (SCATTER_UNROLL * SCATTER_GROUPS), scatter, 0)

    @pl.when(s >= n_scatter)
    def _():
        g2 = mod_ref[0, 0][5:6]
        tile = acc_ref.at[pl.ds(pl.multiple_of((s - n_scatter) * tf * Y_ROWS, tf * Y_ROWS), tf * Y_ROWS), :]
        for c in range(Y_ROWS):
            sl = slice(c * LANES, (c + 1) * LANES)
            o_ref[:, sl] = x1_ref[:, sl] + g2[:, sl] * tile[pl.ds(c, tf, stride=Y_ROWS), :]


def _combine_tiles(n_tok):
    row_bytes = D_MODEL * 4
    free_rows = (VMEM_LIMIT - n_tok * row_bytes) * 4 // 5 // row_bytes
    scale = max(k for k in (1, 2, 4) if 2 * 512 * k + 4 * 256 * k <= free_rows)
    return 256 * scale, 512 * scale


def _combine(dst, y, x1, mods, layer, row_of_token):
    n_tok = x1.shape[0]
    tf, step_rows = _combine_tiles(n_tok)
    n_scatter = dst.shape[0] // step_rows
    assert (dst.shape[0] // N_EXPERTS) % SCATTER_UNROLL == 0 and dst.shape[0] % step_rows == 0
    tile_of = lambda s: jnp.maximum(s - n_scatter, 0)
    row_of_tile = lambda t: row_of_token(t * tf)
    return pl.pallas_call(
        functools.partial(_combine_kernel, n_scatter=n_scatter),
        out_shape=jax.ShapeDtypeStruct((n_tok, D_MODEL), F32),
        grid_spec=pltpu.PrefetchScalarGridSpec(
            num_scalar_prefetch=1,
            grid=(n_scatter + n_tok // tf,),
            in_specs=[pl.BlockSpec((step_rows * Y_ROWS, LANES), lambda s, *_: (jnp.minimum(s, n_scatter - 1), 0)),
                      pl.BlockSpec((tf, D_MODEL), lambda s, *_: (tile_of(s), 0)),
                      pl.BlockSpec((1, 1, N_MOD, D_MODEL), lambda s, *_: (layer, row_of_tile(tile_of(s)), 0, 0))],
            out_specs=pl.BlockSpec((tf, D_MODEL), lambda s, *_: (tile_of(s), 0)),
            scratch_shapes=[pltpu.VMEM((n_tok * Y_ROWS, LANES), F32)]),
        compiler_params=_cparams(1),
        name="combine",
    )(dst, y.reshape(-1, LANES), x1, mods)


TOKEN_TILE = 1024
Q_TILE = 512


def _moe(fronts, rows_of_token, mods, layer, w_gate, w_up, w_down):
    routed, dsts = [], []
    for x1, hp, aff in fronts:
        cap = CAPACITY_FACTOR * x1.shape[0] // N_EXPERTS
        src, dst, gate = _select(aff, cap)
        routed.append((_gather(src, hp), gate))
        dsts.append(dst)
    ys = _ffn(routed, layer, w_gate, w_up, w_down)
    return [_combine(dst, y, x1, mods, layer, row_of_token)
            for dst, y, (x1, _, _), row_of_token in zip(dsts, ys, fronts, rows_of_token)]


def kernel(x_prompt, x_sample, cache_k, cache_v, c, c_ctx, norm1, norm2, w_ada, b_ada, w_qkv, q_norm, k_norm,
           w_o, w_pool, pool_scale, w_router, w_e_gate, w_e_up, w_e_down):
    batch, seq, _ = x_prompt.shape
    dec_batch, dec_seq, _ = x_sample.shape
    depth = w_ada.shape[0]
    kvw = N_KV * HEAD_DIM

    cvec = jnp.zeros((SUBLANES, D_MODEL), F32).at[0].set(c_ctx).at[1:1 + dec_batch].set(c)
    mods = _ada(cvec, w_ada, b_ada)
    ctx = x_prompt.reshape(batch * seq, D_MODEL)
    lat = x_sample.reshape(dec_batch * dec_seq, D_MODEL)
    ctx_row = lambda i: 0
    lat_row = lambda tile: (lambda i: 1 + (i * tile) // dec_seq)
    ctx_tile = TOKEN_TILE if TOKEN_TILE % seq == 0 else min(TOKEN_TILE, seq)
    rope_tabs = _rope_tables(dec_seq)
    new_k = new_v = None

    for layer in range(depth):
        j = layer // 2
        n1 = norm1[layer][None]
        n2 = norm2[layer][None]
        wr = w_router
        if layer % 2 == 0:
            qn = q_norm[j][None]
            kn = k_norm[j][None]
            qc, kc, vc, new_k, new_v = _qkv(ctx, mods, layer, ctx_row, n1, w_qkv[j], qn, kn, None, True, TOKEN_TILE)
            oc = _attn_ctx(qc, kc, vc, seq, TOKEN_TILE)
            ql, kl, vl = _qkv(lat, mods, layer, lat_row(TOKEN_TILE), n1, w_qkv[j], qn, kn, rope_tabs, False,
                              TOKEN_TILE)
            past_k = cache_k[:, j].reshape(dec_batch, -1, kvw).astype(BF16)
            past_v = cache_v[:, j].reshape(dec_batch, -1, kvw).astype(BF16)
            ol = _attn_lat(ql, kl, vl, past_k, past_v, dec_seq, Q_TILE)
            ctx1 = _post(oc, ctx, mods, layer, ctx_row, n2, w_o[j], wr, TOKEN_TILE)
            lat1 = _post(ol, lat, mods, layer, lat_row(TOKEN_TILE), n2, w_o[j], wr, TOKEN_TILE)
        else:
            ps = pool_scale[j][None]
            ctx1 = _pool(ctx, mods, layer, ctx_row, n1, n2, w_pool[j], ps, wr, seq, ctx_tile)
            lat1 = _pool(lat, mods, layer, lat_row(TOKEN_TILE), n1, n2, w_pool[j], ps, wr, dec_seq, TOKEN_TILE)
        ctx, lat = _moe([ctx1, lat1], [ctx_row, lat_row(1)], mods, layer, w_e_gate, w_e_up, w_e_down)

    new_cache_k = new_k.reshape(batch, 1, seq, N_KV, HEAD_DIM)
    new_cache_v = new_v.reshape(batch, 1, seq, N_KV, HEAD_DIM)
    return (ctx.reshape(batch, seq, D_MODEL), lat.reshape(dec_batch, dec_seq, D_MODEL), new_cache_k, new_cache_v)
```

```python
import functools

import jax
import jax.numpy as jnp
import numpy as np
from jax import lax
from jax.experimental import pallas as pl
from jax.experimental.pallas import tpu as pltpu

F32 = jnp.float32
BF16 = jnp.bfloat16
I32 = jnp.int32
U32 = jnp.uint32

D_MODEL = 1024
HEAD_DIM = 128
N_HEADS = 8
N_KV = 2
Q_PER_KV = N_HEADS // N_KV
QKV_DIM = (N_HEADS + 2 * N_KV) * HEAD_DIM
GRID_W = 64
ROPE_THETA = 10000.0
POOL_WINDOWS = (2, 4, 8, 16)
POOL_GROUP = D_MODEL // len(POOL_WINDOWS)
POOL_HALO = max(POOL_WINDOWS) // 2
N_EXPERTS = 16
CAPACITY_FACTOR = 2
N_MOD = 6
EPS = 1e-6
LOG2_E = 1.4426950408889634

LANES = 128
SUBLANES = 8
HALF = D_MODEL // 2
HP_ROWS = HALF // LANES
Y_ROWS = D_MODEL // LANES
VMEM_LIMIT = 56 * 1024 * 1024


def _cparams(n_axes, vmem=VMEM_LIMIT):
    return pltpu.CompilerParams(dimension_semantics=("arbitrary",) * n_axes, vmem_limit_bytes=vmem)


def _silu(x):
    return x / (1.0 + jnp.exp(-x))


def _rms(x, gain):
    return x * lax.rsqrt(jnp.mean(x * x, axis=-1, keepdims=True) + EPS) * gain


def _norm_mod(x, gain, shift, scale):
    return _rms(x, gain * (1.0 + scale)) + shift


def _dot(a, b):
    return jnp.dot(a, b, preferred_element_type=F32)


def _dot_nt(a, b):
    return lax.dot_general(a, b, (((1,), (1,)), ((), ())), preferred_element_type=F32)


def _ada_kernel(c_ref, w_ref, b_ref, o_ref):
    s = _silu(c_ref[...]).astype(BF16)
    o_ref[0] = _dot(s, w_ref[0].astype(BF16)) + b_ref[0]


def _ada(cvec, w_ada, b_ada):
    depth = w_ada.shape[0]
    out = pl.pallas_call(
        _ada_kernel,
        out_shape=jax.ShapeDtypeStruct((depth, SUBLANES, N_MOD * D_MODEL), F32),
        grid=(depth, N_MOD),
        in_specs=[
            pl.BlockSpec((SUBLANES, D_MODEL), lambda i, j: (0, 0)),
            pl.BlockSpec((1, D_MODEL, D_MODEL), lambda i, j: (i, 0, j)),
            pl.BlockSpec((1, 1, D_MODEL), lambda i, j: (i, 0, j)),
        ],
        out_specs=pl.BlockSpec((1, SUBLANES, D_MODEL), lambda i, j: (i, 0, j)),
        compiler_params=_cparams(2),
        name="ada",
    )(cvec, w_ada, b_ada.reshape(depth, 1, N_MOD * D_MODEL))
    return out.reshape(depth, SUBLANES, N_MOD, D_MODEL)


def _mod_spec(layer, row_of_tile):
    return pl.BlockSpec((1, 1, N_MOD, D_MODEL), lambda i, *_: (layer, row_of_tile(i), 0, 0))


def _qkv_kernel(*refs, rope, cache_out):
    x_ref, mod_ref, n1_ref, w_ref, qn_ref, kn_ref = refs[:6]
    refs = refs[6:]
    if rope:
        cos_ref, sin_ref = refs[:2]
        refs = refs[2:]
    q_ref, k_ref, v_ref = refs[:3]
    refs = refs[3:]
    if cache_out:
        kc_ref, vc_ref = refs[:2]
        refs = refs[2:]
    (wb_ref,) = refs

    qk_w = (N_HEADS + N_KV) * HEAD_DIM
    quarter = HEAD_DIM // 4

    def partner(a):
        width = a.shape[1]
        first = (lax.broadcasted_iota(I32, a.shape, 1) & quarter) == 0
        return jnp.where(first, pltpu.roll(a, width - quarter, 1), pltpu.roll(a, quarter, 1))

    @pl.when(pl.program_id(0) == 0)
    def _():
        w = w_ref[...]
        wb_ref[:, 0:QKV_DIM] = w.astype(BF16)
        if rope:
            wb_ref[:, QKV_DIM:] = partner(w[:, 0:qk_w]).astype(BF16)

    m = mod_ref[0, 0]
    h = _norm_mod(x_ref[...], n1_ref[...], m[0:1], m[1:2])
    qkv = _dot(h.astype(BF16), wb_ref[...])
    scale = HEAD_DIM ** -0.5 * LOG2_E
    if rope:
        gains = {True: qn_ref[...], False: kn_ref[...]}
        cos_g = {key: cos_ref[...] * g for key, g in gains.items()}
        sin_g = {key: sin_ref[...] * partner(g) for key, g in gains.items()}
    for hh in range(N_HEADS + N_KV):
        sl = slice(hh * HEAD_DIM, (hh + 1) * HEAD_DIM)
        is_q = hh < N_HEADS
        if rope:
            raw = qkv[:, sl]
            norm = lax.rsqrt(jnp.mean(raw * raw, axis=-1, keepdims=True) + EPS)
            xh = (raw * cos_g[is_q] + qkv[:, QKV_DIM + hh * HEAD_DIM:QKV_DIM + (hh + 1) * HEAD_DIM] * sin_g[is_q]) * norm
        else:
            xh = _rms(qkv[:, sl], qn_ref[...] if is_q else kn_ref[...])
        if cache_out and hh >= N_HEADS:
            kc_ref[:, hh - N_HEADS, :] = xh
        if hh < N_HEADS:
            q_ref[:, sl] = (xh * scale).astype(BF16)
        else:
            k_ref[:, (hh - N_HEADS) * HEAD_DIM:(hh - N_HEADS + 1) * HEAD_DIM] = xh.astype(BF16)
    v = qkv[:, qk_w:QKV_DIM]
    v_ref[...] = v.astype(BF16)
    if cache_out:
        for g in range(N_KV):
            vc_ref[:, g, :] = v[:, g * HEAD_DIM:(g + 1) * HEAD_DIM]


def _qkv(x, mods, layer, row_of_tile, n1, w_qkv, qn, kn, rope_tabs, cache_out, tm):
    n_tok = x.shape[0]
    kvw = N_KV * HEAD_DIM
    rope = rope_tabs is not None
    in_specs = [
        pl.BlockSpec((tm, D_MODEL), lambda i: (i, 0)),
        _mod_spec(layer, row_of_tile),
        pl.BlockSpec((1, D_MODEL), lambda i: (0, 0)),
        pl.BlockSpec((D_MODEL, QKV_DIM), lambda i: (0, 0)),
        pl.BlockSpec((1, HEAD_DIM), lambda i: (0, 0)),
        pl.BlockSpec((1, HEAD_DIM), lambda i: (0, 0)),
    ]
    args = [x, mods, n1, w_qkv, qn, kn]
    if rope:
        seq_tiles = rope_tabs[0].shape[0] // tm
        in_specs += [pl.BlockSpec((tm, HEAD_DIM), lambda i: (i % seq_tiles, 0))] * 2
        args += list(rope_tabs)
    out_shape = [jax.ShapeDtypeStruct((n_tok, D_MODEL), BF16),
                 jax.ShapeDtypeStruct((n_tok, kvw), BF16),
                 jax.ShapeDtypeStruct((n_tok, kvw), BF16)]
    out_specs = [pl.BlockSpec((tm, D_MODEL), lambda i: (i, 0)),
                 pl.BlockSpec((tm, kvw), lambda i: (i, 0)),
                 pl.BlockSpec((tm, kvw), lambda i: (i, 0))]
    if cache_out:
        out_shape += [jax.ShapeDtypeStruct((n_tok, N_KV, HEAD_DIM), F32)] * 2
        out_specs += [pl.BlockSpec((tm, N_KV, HEAD_DIM), lambda i: (i, 0, 0))] * 2
    return pl.pallas_call(
        functools.partial(_qkv_kernel, rope=rope, cache_out=cache_out),
        out_shape=out_shape,
        grid=(n_tok // tm,),
        in_specs=in_specs,
        out_specs=out_specs,
        scratch_shapes=[pltpu.VMEM((D_MODEL, QKV_DIM + ((N_HEADS + N_KV) * HEAD_DIM if rope else 0)), BF16)],
        compiler_params=_cparams(1),
        name="qkv_rope" if rope else "qkv",
    )(*args)


def _rope_tables(seq_len):
    half = HEAD_DIM // 2
    n = half // 2
    inv_freq = ROPE_THETA ** (-np.arange(n, dtype=np.float64) / n)
    rows = seq_len // GRID_W
    row = np.repeat(np.arange(rows), GRID_W).astype(np.float64)
    col = np.tile(np.arange(GRID_W), rows).astype(np.float64)
    ang_r = row[:, None] * inv_freq[None, :]
    ang_c = col[:, None] * inv_freq[None, :]
    cos = np.concatenate([np.cos(ang_r)] * 2 + [np.cos(ang_c)] * 2, axis=-1)
    sin = np.concatenate([-np.sin(ang_r), np.sin(ang_r), -np.sin(ang_c), np.sin(ang_c)], axis=-1)
    return jnp.asarray(cos, F32), jnp.asarray(sin, F32)


def _with_ones(v):
    return jnp.concatenate([v, jnp.ones_like(v)], axis=1)


def _gqa_attention(q, segments):
    rows = q.shape[0]
    qs = jnp.concatenate([q[:, h * HEAD_DIM:(h + 1) * HEAD_DIM] for h in range(Q_PER_KV)], axis=0)
    m = acc = None
    for k, v in segments:
        s = _dot_nt(qs, k)
        seg_max = jnp.max(s, axis=-1, keepdims=True)
        m_new = seg_max if m is None else jnp.maximum(m, seg_max)
        pv = _dot(jnp.exp2(s - m_new).astype(BF16), v)
        acc = pv if m is None else acc * jnp.exp2(m - m_new) + pv
        m = m_new
    o = (acc[:, :HEAD_DIM] / acc[:, HEAD_DIM:]).astype(BF16)
    return jnp.concatenate([o[h * rows:(h + 1) * rows] for h in range(Q_PER_KV)], axis=1)


def _attn_ctx_kernel(q_ref, k_ref, v_ref, o_ref, *, seq):
    gw = Q_PER_KV * HEAD_DIM
    for r in range(0, q_ref.shape[0], seq):
        for g in range(N_KV):
            kg = k_ref[r:r + seq, g * HEAD_DIM:(g + 1) * HEAD_DIM]
            vg = _with_ones(v_ref[r:r + seq, g * HEAD_DIM:(g + 1) * HEAD_DIM])
            o_ref[r:r + seq, g * gw:(g + 1) * gw] = _gqa_attention(q_ref[r:r + seq, g * gw:(g + 1) * gw], [(kg, vg)])


def _attn_ctx(q, k, v, seq, tm):
    n_tok = q.shape[0]
    kvw = N_KV * HEAD_DIM
    return pl.pallas_call(
        functools.partial(_attn_ctx_kernel, seq=seq),
        out_shape=jax.ShapeDtypeStruct((n_tok, D_MODEL), BF16),
        grid=(n_tok // tm,),
        in_specs=[pl.BlockSpec((tm, D_MODEL), lambda b: (b, 0)),
                  pl.BlockSpec((tm, kvw), lambda b: (b, 0)),
                  pl.BlockSpec((tm, kvw), lambda b: (b, 0))],
        out_specs=pl.BlockSpec((tm, D_MODEL), lambda b: (b, 0)),
        compiler_params=_cparams(1),
        name="attn_ctx",
    )(q, k, v)


KEY_CHUNK = 256


def _attn_lat_kernel(q_ref, k_ref, v_ref, kc_ref, vc_ref, o_ref):
    segments = [(k_ref[c:c + KEY_CHUNK], _with_ones(v_ref[c:c + KEY_CHUNK]))
                for c in range(0, k_ref.shape[0], KEY_CHUNK)]
    segments.append((kc_ref[0], _with_ones(vc_ref[0])))
    o_ref[...] = _gqa_attention(q_ref[...], segments)


def _attn_lat(q, k, v, kc, vc, seq, tq):
    n_tok = q.shape[0]
    batch = n_tok // seq
    past = kc.shape[1]
    qt = seq // tq
    gw = Q_PER_KV * HEAD_DIM
    return pl.pallas_call(
        _attn_lat_kernel,
        out_shape=jax.ShapeDtypeStruct((n_tok, D_MODEL), BF16),
        grid=(batch, N_KV, qt),
        in_specs=[pl.BlockSpec((tq, gw), lambda b, g, i: (b * qt + i, g)),
                  pl.BlockSpec((seq, HEAD_DIM), lambda b, g, i: (b, g)),
                  pl.BlockSpec((seq, HEAD_DIM), lambda b, g, i: (b, g)),
                  pl.BlockSpec((1, past, HEAD_DIM), lambda b, g, i: (b, 0, g)),
                  pl.BlockSpec((1, past, HEAD_DIM), lambda b, g, i: (b, 0, g))],
        out_specs=pl.BlockSpec((tq, gw), lambda b, g, i: (b * qt + i, g)),
        compiler_params=_cparams(3),
        name="attn_lat",
    )(q, k, v, kc, vc)


FRONT_ROWS = 256


def _router_weights(wr):
    return jnp.concatenate([wr, jnp.zeros((wr.shape[0], LANES - N_EXPERTS), wr.dtype)], axis=1).astype(BF16)


def _router_softmax(hb, wr):
    logits = _dot(hb, wr)
    lane = lax.broadcasted_iota(I32, logits.shape, 1)
    logits = jnp.where(lane < N_EXPERTS, logits, -1e30)
    ex = jnp.exp(logits - jnp.max(logits, axis=-1, keepdims=True))
    return ex / jnp.sum(ex, axis=-1, keepdims=True)


def _moe_front(x1, r0, m, n2, wr, x1_ref, hp_ref, aff_ref):
    n = x1.shape[0]
    x1_ref[r0:r0 + n, :] = x1
    h2 = _norm_mod(x1, n2, m[3:4], m[4:5])
    aff = _router_softmax(h2.astype(BF16), wr)
    packed = pltpu.pack_elementwise([h2[:, :HALF], h2[:, HALF:]], packed_dtype=BF16)
    for s in range(HP_ROWS):
        hp_ref[pl.ds(r0 * HP_ROWS + s, n, stride=HP_ROWS), :] = packed[:, s * LANES:(s + 1) * LANES]
    aff_t = aff.T
    for b in range(n // LANES):
        out_row = (r0 // LANES + b) * N_EXPERTS
        aff_ref[out_row:out_row + N_EXPERTS, :] = aff_t[0:N_EXPERTS, b * LANES:(b + 1) * LANES]


def _front_out(n_tok, tm):
    shapes = [jax.ShapeDtypeStruct((n_tok, D_MODEL), F32),
              jax.ShapeDtypeStruct((n_tok * HP_ROWS, LANES), U32),
              jax.ShapeDtypeStruct((n_tok // LANES * N_EXPERTS, LANES), F32)]
    specs = [pl.BlockSpec((tm, D_MODEL), lambda i: (i, 0)),
             pl.BlockSpec((tm * HP_ROWS, LANES), lambda i: (i, 0)),
             pl.BlockSpec((tm // LANES * N_EXPERTS, LANES), lambda i: (i, 0))]
    return shapes, specs


def _post_kernel(o_ref, x_ref, mod_ref, n2_ref, wo_ref, wr_ref, x1_ref, hp_ref, aff_ref, wb_ref):
    @pl.when(pl.program_id(0) == 0)
    def _():
        wb_ref[...] = wo_ref[...].astype(BF16)

    m = mod_ref[0, 0]
    wr = _router_weights(wr_ref[0])
    for r0 in range(0, x_ref.shape[0], FRONT_ROWS):
        rows = slice(r0, r0 + FRONT_ROWS)
        x1 = x_ref[rows, :] + m[2:3] * _dot(o_ref[rows, :], wb_ref[...])
        _moe_front(x1, r0, m, n2_ref[...], wr, x1_ref, hp_ref, aff_ref)


def _post(o, x, mods, layer, row_of_tile, n2, w_o, wr, tm):
    n_tok = x.shape[0]
    shapes, specs = _front_out(n_tok, tm)
    return pl.pallas_call(
        _post_kernel,
        out_shape=shapes,
        grid=(n_tok // tm,),
        in_specs=[pl.BlockSpec((tm, D_MODEL), lambda i: (i, 0)),
                  pl.BlockSpec((tm, D_MODEL), lambda i: (i, 0)),
                  _mod_spec(layer, row_of_tile),
                  pl.BlockSpec((1, D_MODEL), lambda i: (0, 0)),
                  pl.BlockSpec((D_MODEL, D_MODEL), lambda i: (0, 0)),
                  pl.BlockSpec((1, D_MODEL, N_EXPERTS), lambda i: (layer, 0, 0))],
        out_specs=specs,
        scratch_shapes=[pltpu.VMEM((D_MODEL, D_MODEL), BF16)],
        compiler_params=_cparams(1),
        name="post",
    )(o, x, mods, n2, w_o, wr)


def _pool_kernel(x_ref, xp_ref, xn_ref, mod_ref, n1_ref, n2_ref, wp_ref, ps_ref, wr_ref,
                 x1_ref, hp_ref, aff_ref, *, seq):
    tm = x_ref.shape[0]
    i = pl.program_id(0)
    m = mod_ref[0, 0]
    x = x_ref[...]

    def norm_mod(v):
        return _norm_mod(v, n1_ref[...], m[0:1], m[1:2])

    h = norm_mod(x)
    if tm <= seq:
        prev_ok = ((i * tm) % seq != 0).astype(F32)
        next_ok = (((i + 1) * tm) % seq != 0).astype(F32)
        pieces = [(norm_mod(xp_ref[...]) * prev_ok, h, norm_mod(xn_ref[...]) * next_ok)]
        t0 = (i * tm) % seq
    else:
        zeros = jnp.zeros((POOL_HALO, D_MODEL), F32)
        pieces = [(zeros, h[r:r + seq], zeros) for r in range(0, tm, seq)]
        t0 = 0
    n = min(tm, seq)
    rows = n + 2 * POOL_HALO
    t = t0 + lax.broadcasted_iota(I32, (n, 1), 0)
    deltas = [[] for _ in POOL_WINDOWS]
    for piece in pieces:
        hz = jnp.concatenate(piece, axis=0)
        for g, w in enumerate(POOL_WINDOWS):
            sl = slice(g * POOL_GROUP, (g + 1) * POOL_GROUP)
            f = hz[:, sl]
            step = 1
            while step < w:
                f = f + pltpu.roll(f, rows - step, 0)
                step *= 2
            win = pltpu.roll(f, w // 2, 0)[POOL_HALO:POOL_HALO + n]
            means = []
            for edge in (slice(0, POOL_HALO), slice(n - POOL_HALO, n)):
                cnt = jnp.minimum(t[edge] + w // 2, seq) - jnp.maximum(t[edge] - w // 2, 0)
                means.append(win[edge] / cnt.astype(F32))
            mean = jnp.concatenate([means[0], win[POOL_HALO:n - POOL_HALO] * (1.0 / w), means[1]], axis=0)
            deltas[g].append((mean - piece[1][:, sl]).astype(BF16))
    ys = [_dot(jnp.concatenate(d, axis=0), wp_ref[g].astype(BF16)) for g, d in enumerate(deltas)]
    x1 = x + (m[2:3] * ps_ref[...]) * jnp.concatenate(ys, axis=-1)
    wr = _router_weights(wr_ref[0])
    for r0 in range(0, tm, FRONT_ROWS):
        _moe_front(x1[r0:r0 + FRONT_ROWS], r0, m, n2_ref[...], wr, x1_ref, hp_ref, aff_ref)


def _pool(x, mods, layer, row_of_tile, n1, n2, w_pool, pool_scale, wr, seq, tm):
    n_tok = x.shape[0]
    hb = tm // POOL_HALO
    last = n_tok // POOL_HALO - 1
    shapes, specs = _front_out(n_tok, tm)
    return pl.pallas_call(
        functools.partial(_pool_kernel, seq=seq),
        out_shape=shapes,
        grid=(n_tok // tm,),
        in_specs=[pl.BlockSpec((tm, D_MODEL), lambda i: (i, 0)),
                  pl.BlockSpec((POOL_HALO, D_MODEL), lambda i: (jnp.maximum(i * hb - 1, 0), 0)),
                  pl.BlockSpec((POOL_HALO, D_MODEL), lambda i: (jnp.minimum((i + 1) * hb, last), 0)),
                  _mod_spec(layer, row_of_tile),
                  pl.BlockSpec((1, D_MODEL), lambda i: (0, 0)),
                  pl.BlockSpec((1, D_MODEL), lambda i: (0, 0)),
                  pl.BlockSpec(w_pool.shape, lambda i: (0, 0, 0)),
                  pl.BlockSpec((1, D_MODEL), lambda i: (0, 0)),
                  pl.BlockSpec((1, D_MODEL, N_EXPERTS), lambda i: (layer, 0, 0))],
        out_specs=specs,
        compiler_params=_cparams(1),
        name="pool",
    )(x, x, x, mods, n1, n2, w_pool, pool_scale, wr)


DIST_BIT = 8


def _select_kernel(aff_ref, src_ref, dst_ref, gate_ref, *, cap):
    nb = aff_ref.shape[0] // N_EXPERTS
    rows = nb * N_EXPERTS
    shape3 = (nb, N_EXPERTS, LANES)
    aff = aff_ref[...].reshape(shape3)

    def count(mask):
        return jnp.sum(jnp.sum(mask.astype(F32), axis=0), axis=-1, keepdims=True)

    def search(i, t):
        shift = 28 - 2 * i
        best = t
        for digit in (1, 2, 3):
            cand = t | (jnp.int32(digit) << shift)
            best = jnp.where(count(aff >= pltpu.bitcast(cand, F32)[None]) >= cap, cand, best)
        return best

    thr = pltpu.bitcast(lax.fori_loop(0, 15, search, jnp.zeros((N_EXPERTS, LANES), I32)), F32)
    gt = aff > thr[None]
    eq = aff == thr[None]
    need = cap - count(gt)

    kk = lax.broadcasted_iota(I32, (LANES, LANES), 0)
    nn = lax.broadcasted_iota(I32, (LANES, LANES), 1)
    upper = (kk <= nn).astype(BF16)
    ones = jnp.ones((LANES, LANES), BF16)

    def prefix(mask):
        m2 = mask.astype(F32).astype(BF16).reshape(rows, LANES)
        p = _dot(m2, upper).reshape(shape3)
        s = _dot(m2, ones).reshape(shape3)
        offs = []
        run = jnp.zeros((N_EXPERTS, LANES), F32)
        for b in range(nb):
            offs.append(run)
            run = run + s[b]
        return p, jnp.stack(offs, axis=0), s

    pe, oe, _ = prefix(eq)
    sel = gt | (eq & ((pe + oe) <= need[None]))
    ps, os_, ss = prefix(sel)

    lane = lax.broadcasted_iota(I32, (rows, LANES), 1)
    sel2 = sel.reshape(rows, LANES)
    dist = lane - (ps.reshape(rows, LANES).astype(I32) - 1)
    word = jnp.where(sel2, lane | (dist << DIST_BIT), 0)
    for k in range(7):
        bit = 1 << (DIST_BIT + k)
        arriving = pltpu.roll(word, LANES - (1 << k), 1)
        word = jnp.where((arriving & bit) != 0, arriving, jnp.where((word & bit) != 0, 0, word))
    val = word & (LANES - 1)
    local = val.reshape(shape3) + lax.broadcasted_iota(I32, shape3, 0) * LANES
    local_gate = jnp.take_along_axis(aff.reshape(rows, LANES), val, axis=1).reshape(shape3)

    first = os_.astype(I32)
    chosen = pltpu.bitcast(ss.astype(I32), U32)
    lane_e = lax.broadcasted_iota(I32, (N_EXPERTS, LANES), 1)
    n_chunks = cap // LANES
    acc = [jnp.zeros((N_EXPERTS, LANES), I32)] * n_chunks
    gacc = [jnp.zeros((N_EXPERTS, LANES), F32)] * n_chunks
    for b in range(nb):
        within = lane_e - first[b]
        phase = within & (LANES - 1)
        tokens = jnp.take_along_axis(local[b], phase, axis=1)
        gates = jnp.take_along_axis(local_gate[b], phase, axis=1)
        for jc in range(n_chunks):
            inside = pltpu.bitcast(within + jc * LANES, U32) < chosen[b]
            acc[jc] = jnp.where(inside, tokens, acc[jc])
            gacc[jc] = jnp.where(inside, gates, gacc[jc])
    for jc in range(n_chunks):
        out_rows = pl.ds(jc, N_EXPERTS, stride=n_chunks)
        src_ref[out_rows, :] = acc[jc] * HP_ROWS
        dst_ref[out_rows, :] = acc[jc] * Y_ROWS
        gate_ref[out_rows, :] = gacc[jc]


def _select(aff, cap):
    rows = aff.shape[0]
    out_rows = N_EXPERTS * cap // LANES
    out_spec = pl.BlockSpec((out_rows, LANES), lambda i: (0, 0))
    src, dst, gate = pl.pallas_call(
        functools.partial(_select_kernel, cap=cap),
        out_shape=[jax.ShapeDtypeStruct((out_rows, LANES), I32),
                   jax.ShapeDtypeStruct((out_rows, LANES), I32),
                   jax.ShapeDtypeStruct((out_rows, LANES), F32)],
        grid=(1,),
        in_specs=[pl.BlockSpec((rows, LANES), lambda i: (0, 0))],
        out_specs=[out_spec, out_spec, out_spec],
        compiler_params=_cparams(1),
        name="select",
    )(aff)
    return src.reshape(-1), dst.reshape(-1), gate


GATHER_UNROLL = 32
GATHER_STEP_ROWS = 2048


def _gather_kernel(src_ref, hp_ref, o_ref):
    i = pl.program_id(0)
    n = o_ref.shape[0] // HP_ROWS

    def gather(g, carry):
        base = i * n + g * GATHER_UNROLL
        for r in range(GATHER_UNROLL):
            src = pl.multiple_of(src_ref[base + r], HP_ROWS)
            dst = pl.multiple_of((g * GATHER_UNROLL + r) * HP_ROWS, HP_ROWS)
            o_ref[pl.ds(dst, HP_ROWS), :] = hp_ref[pl.ds(src, HP_ROWS), :]
        return carry

    lax.fori_loop(0, n // GATHER_UNROLL, gather, 0)


def _gather(src, hp):
    n = src.shape[0]
    return pl.pallas_call(
        _gather_kernel,
        out_shape=jax.ShapeDtypeStruct((n * HP_ROWS, LANES), U32),
        grid_spec=pltpu.PrefetchScalarGridSpec(
            num_scalar_prefetch=1,
            grid=(n // GATHER_STEP_ROWS,),
            in_specs=[pl.BlockSpec(hp.shape, lambda i, src: (0, 0), pipeline_mode=pl.Buffered(1))],
            out_specs=pl.BlockSpec((GATHER_STEP_ROWS * HP_ROWS, LANES), lambda i, src: (i, 0))),
        compiler_params=_cparams(1),
        name="gather",
    )(src, hp)


FF_CHUNK = 256


def _ffn_kernel(*refs, n_streams):
    e = pl.program_id(0)
    ins = refs[:2 * n_streams]
    wg_ref, wu_ref, wd_ref = refs[2 * n_streams:2 * n_streams + 3]
    outs = refs[2 * n_streams + 3:]
    d_ff = wg_ref.shape[3]
    eye = lax.broadcasted_iota(I32, (LANES, LANES), 0) == lax.broadcasted_iota(I32, (LANES, LANES), 1)
    for i in range(n_streams):
        xp_ref, gate_ref, y_ref = ins[2 * i], ins[2 * i + 1], outs[i]
        cap = xp_ref.shape[0] // HP_ROWS
        halves = [[], []]
        for s in range(HP_ROWS):
            w = xp_ref[pl.ds(s, cap, stride=HP_ROWS), :]
            for half in range(2):
                v = pltpu.unpack_elementwise(w, index=half, packed_dtype=BF16, unpacked_dtype=F32)
                halves[half].append(v.astype(BF16))
        xe = jnp.concatenate(halves[0] + halves[1], axis=1)
        y = None
        for c in range(0, d_ff, FF_CHUNK):
            a = _dot(xe, wg_ref[0, 0, :, c:c + FF_CHUNK].astype(BF16))
            u = _dot(xe, wu_ref[0, 0, :, c:c + FF_CHUNK].astype(BF16))
            yc = _dot((_silu(a) * u).astype(BF16), wd_ref[0, 0, c:c + FF_CHUNK, :].astype(BF16))
            y = yc if y is None else y + yc
        for c in range(cap // LANES):
            g_row = gate_ref[pl.ds(e * (cap // LANES) + c, 1), :]
            g_col = jnp.sum(jnp.where(eye, g_row, 0.0), axis=1, keepdims=True)
            tot = y[c * LANES:(c + 1) * LANES, :] * g_col
            for s in range(Y_ROWS):
                y_ref[0, pl.ds(c * LANES * Y_ROWS + s, LANES, stride=Y_ROWS), :] = tot[:, s * LANES:(s + 1) * LANES]


def _ffn(streams, layer, w_gate, w_up, w_down):
    caps = [xp.shape[0] // (N_EXPERTS * HP_ROWS) for xp, _ in streams]
    in_specs, args = [], []
    for (xp, gate), cap in zip(streams, caps):
        in_specs += [pl.BlockSpec((cap * HP_ROWS, LANES), lambda e: (e, 0)),
                     pl.BlockSpec(gate.shape, lambda e: (0, 0))]
        args += [xp, gate]
    w_spec = pl.BlockSpec((1, 1) + w_gate.shape[2:], lambda e: (layer, e, 0, 0))
    return pl.pallas_call(
        functools.partial(_ffn_kernel, n_streams=len(streams)),
        out_shape=[jax.ShapeDtypeStruct((N_EXPERTS, cap * Y_ROWS, LANES), F32) for cap in caps],
        grid=(N_EXPERTS,),
        in_specs=in_specs + [w_spec, w_spec, w_spec],
        out_specs=[pl.BlockSpec((1, cap * Y_ROWS, LANES), lambda e: (e, 0, 0)) for cap in caps],
        compiler_params=_cparams(1),
        name="ffn",
    )(*args, w_gate, w_up, w_down)


SCATTER_UNROLL = 16
SCATTER_GROUPS = 2


def _combine_kernel(dst_ref, y_ref, x1_ref, mod_ref, o_ref, acc_ref, *, n_scatter):
    s = pl.program_id(0)
    tf = o_ref.shape[0]
    step_rows = y_ref.shape[0] // Y_ROWS

    @pl.when(s == 0)
    def _():
        acc_ref[...] = jnp.zeros_like(acc_ref)

    @pl.when(s < n_scatter)
    def _():
        def scatter(g, carry):
            for sub in range(SCATTER_GROUPS):
                first = (g * SCATTER_GROUPS + sub) * SCATTER_UNROLL
                base = s * step_rows + first
                rows, sums = [], []
                for r in range(SCATTER_UNROLL):
                    dst = pl.ds(pl.multiple_of(dst_ref[base + r], Y_ROWS), Y_ROWS)
                    src = pl.ds(pl.multiple_of((first + r) * Y_ROWS, Y_ROWS), Y_ROWS)
                    rows.append(dst)
                    sums.append(acc_ref[dst, :] + y_ref[src, :])
                for dst, v in zip(rows, sums):
                    acc_ref[dst, :] = v
            return carry

        lax.fori_loop(0, step_rows // (SCATTER_UNROLL * SCATTER_GROUPS), scatter, 0)

    @pl.when(s >= n_scatter)
    def _():
        g2 = mod_ref[0, 0][5:6]
        tile = acc_ref.at[pl.ds(pl.multiple_of((s - n_scatter) * tf * Y_ROWS, tf * Y_ROWS), tf * Y_ROWS), :]
        for c in range(Y_ROWS):
            sl = slice(c * LANES, (c + 1) * LANES)
            o_ref[:, sl] = x1_ref[:, sl] + g2[:, sl] * tile[pl.ds(c, tf, stride=Y_ROWS), :]


def _combine_tiles(n_tok):
    row_bytes = D_MODEL * 4
    free_rows = (VMEM_LIMIT - n_tok * row_bytes) * 4 // 5 // row_bytes
    scale = max(k for k in (1, 2, 4) if 2 * 512 * k + 4 * 256 * k <= free_rows)
    return 256 * scale, 512 * scale


def _combine(dst, y, x1, mods, layer, row_of_token):
    n_tok = x1.shape[0]
    tf, step_rows = _combine_tiles(n_tok)
    n_scatter = dst.shape[0] // step_rows
    assert (dst.shape[0] // N_EXPERTS) % SCATTER_UNROLL == 0 and dst.shape[0] % step_rows == 0
    tile_of = lambda s: jnp.maximum(s - n_scatter, 0)
    row_of_tile = lambda t: row_of_token(t * tf)
    return pl.pallas_call(
        functools.partial(_combine_kernel, n_scatter=n_scatter),
        out_shape=jax.ShapeDtypeStruct((n_tok, D_MODEL), F32),
        grid_spec=pltpu.PrefetchScalarGridSpec(
            num_scalar_prefetch=1,
            grid=(n_scatter + n_tok // tf,),
            in_specs=[pl.BlockSpec((step_rows * Y_ROWS, LANES), lambda s, *_: (jnp.minimum(s, n_scatter - 1), 0)),
                      pl.BlockSpec((tf, D_MODEL), lambda s, *_: (tile_of(s), 0)),
                      pl.BlockSpec((1, 1, N_MOD, D_MODEL), lambda s, *_: (layer, row_of_tile(tile_of(s)), 0, 0))],
            out_specs=pl.BlockSpec((tf, D_MODEL), lambda s, *_: (tile_of(s), 0)),
            scratch_shapes=[pltpu.VMEM((n_tok * Y_ROWS, LANES), F32)]),
        compiler_params=_cparams(1),
        name="combine",
    )(dst, y.reshape(-1, LANES), x1, mods)


TOKEN_TILE = 1024
Q_TILE = 1024


def _moe(fronts, rows_of_token, mods, layer, w_gate, w_up, w_down):
    routed, dsts = [], []
    for x1, hp, aff in fronts:
        cap = CAPACITY_FACTOR * x1.shape[0] // N_EXPERTS
        src, dst, gate = _select(aff, cap)
        routed.append((_gather(src, hp), gate))
        dsts.append(dst)
    ys = _ffn(routed, layer, w_gate, w_up, w_down)
    return [_combine(dst, y, x1, mods, layer, row_of_token)
            for dst, y, (x1, _, _), row_of_token in zip(dsts, ys, fronts, rows_of_token)]


def kernel(x_prompt, x_sample, cache_k, cache_v, c, c_ctx, norm1, norm2, w_ada, b_ada, w_qkv, q_norm, k_norm,
           w_o, w_pool, pool_scale, w_router, w_e_gate, w_e_up, w_e_down):
    batch, seq, _ = x_prompt.shape
    dec_batch, dec_seq, _ = x_sample.shape
    depth = w_ada.shape[0]
    kvw = N_KV * HEAD_DIM

    cvec = jnp.zeros((SUBLANES, D_MODEL), F32).at[0].set(c_ctx).at[1:1 + dec_batch].set(c)
    mods = _ada(cvec, w_ada, b_ada)
    ctx = x_prompt.reshape(batch * seq, D_MODEL)
    lat = x_sample.reshape(dec_batch * dec_seq, D_MODEL)
    ctx_row = lambda i: 0
    lat_row = lambda tile: (lambda i: 1 + (i * tile) // dec_seq)
    ctx_tile = TOKEN_TILE if TOKEN_TILE % seq == 0 else min(TOKEN_TILE, seq)
    rope_tabs = _rope_tables(dec_seq)
    new_k = new_v = None

    for layer in range(depth):
        j = layer // 2
        n1 = norm1[layer][None]
        n2 = norm2[layer][None]
        wr = w_router
        if layer % 2 == 0:
            qn = q_norm[j][None]
            kn = k_norm[j][None]
            qc, kc, vc, new_k, new_v = _qkv(ctx, mods, layer, ctx_row, n1, w_qkv[j], qn, kn, None, True, TOKEN_TILE)
            oc = _attn_ctx(qc, kc, vc, seq, TOKEN_TILE)
            ql, kl, vl = _qkv(lat, mods, layer, lat_row(TOKEN_TILE), n1, w_qkv[j], qn, kn, rope_tabs, False,
                              TOKEN_TILE)
            past_k = cache_k[:, j].reshape(dec_batch, -1, kvw).astype(BF16)
            past_v = cache_v[:, j].reshape(dec_batch, -1, kvw).astype(BF16)
            ol = _attn_lat(ql, kl, vl, past_k, past_v, dec_seq, Q_TILE)
            ctx1 = _post(oc, ctx, mods, layer, ctx_row, n2, w_o[j], wr, TOKEN_TILE)
            lat1 = _post(ol, lat, mods, layer, lat_row(TOKEN_TILE), n2, w_o[j], wr, TOKEN_TILE)
        else:
            ps = pool_scale[j][None]
            ctx1 = _pool(ctx, mods, layer, ctx_row, n1, n2, w_pool[j], ps, wr, seq, ctx_tile)
            lat1 = _pool(lat, mods, layer, lat_row(TOKEN_TILE), n1, n2, w_pool[j], ps, wr, dec_seq, TOKEN_TILE)
        ctx, lat = _moe([ctx1, lat1], [ctx_row, lat_row(1)], mods, layer, w_e_gate, w_e_up, w_e_down)

    new_cache_k = new_k.reshape(batch, 1, seq, N_KV, HEAD_DIM)
    new_cache_v = new_v.reshape(batch, 1, seq, N_KV, HEAD_DIM)
    return (ctx.reshape(batch, seq, D_MODEL), lat.reshape(dec_batch, dec_seq, D_MODEL), new_cache_k, new_cache_v)
```

```python
import functools

import jax
import jax.numpy as jnp
import numpy as np
from jax import lax
from jax.experimental import pallas as pl
from jax.experimental.pallas import tpu as pltpu

F32 = jnp.float32
BF16 = jnp.bfloat16
I32 = jnp.int32
U32 = jnp.uint32

D_MODEL = 1024
HEAD_DIM = 128
N_HEADS = 8
N_KV = 2
Q_PER_KV = N_HEADS // N_KV
QKV_DIM = (N_HEADS + 2 * N_KV) * HEAD_DIM
GRID_W = 64
ROPE_THETA = 10000.0
POOL_WINDOWS = (2, 4, 8, 16)
POOL_GROUP = D_MODEL // len(POOL_WINDOWS)
POOL_HALO = max(POOL_WINDOWS) // 2
N_EXPERTS = 16
CAPACITY_FACTOR = 2
N_MOD = 6
EPS = 1e-6
LOG2_E = 1.4426950408889634

LANES = 128
SUBLANES = 8
HALF = D_MODEL // 2
HP_ROWS = HALF // LANES
Y_ROWS = D_MODEL // LANES
VMEM_LIMIT = 56 * 1024 * 1024


def _cparams(n_axes, vmem=VMEM_LIMIT):
    return pltpu.CompilerParams(dimension_semantics=("arbitrary",) * n_axes, vmem_limit_bytes=vmem)


def _silu(x):
    return x / (1.0 + jnp.exp(-x))


def _rms(x, gain):
    return x * lax.rsqrt(jnp.mean(x * x, axis=-1, keepdims=True) + EPS) * gain


def _norm_mod(x, gain, shift, scale):
    return _rms(x, gain * (1.0 + scale)) + shift


def _dot(a, b):
    return jnp.dot(a, b, preferred_element_type=F32)


def _dot_nt(a, b):
    return lax.dot_general(a, b, (((1,), (1,)), ((), ())), preferred_element_type=F32)


def _ada_kernel(c_ref, w_ref, b_ref, o_ref):
    s = _silu(c_ref[...]).astype(BF16)
    o_ref[0] = _dot(s, w_ref[0].astype(BF16)) + b_ref[0]


def _ada(cvec, w_ada, b_ada):
    depth = w_ada.shape[0]
    out = pl.pallas_call(
        _ada_kernel,
        out_shape=jax.ShapeDtypeStruct((depth, SUBLANES, N_MOD * D_MODEL), F32),
        grid=(depth, N_MOD),
        in_specs=[
            pl.BlockSpec((SUBLANES, D_MODEL), lambda i, j: (0, 0)),
            pl.BlockSpec((1, D_MODEL, D_MODEL), lambda i, j: (i, 0, j)),
            pl.BlockSpec((1, 1, D_MODEL), lambda i, j: (i, 0, j)),
        ],
        out_specs=pl.BlockSpec((1, SUBLANES, D_MODEL), lambda i, j: (i, 0, j)),
        compiler_params=_cparams(2),
        name="ada",
    )(cvec, w_ada, b_ada.reshape(depth, 1, N_MOD * D_MODEL))
    return out.reshape(depth, SUBLANES, N_MOD, D_MODEL)


def _mod_spec(layer, row_of_tile):
    return pl.BlockSpec((1, 1, N_MOD, D_MODEL), lambda i, *_: (layer, row_of_tile(i), 0, 0))


def _qkv_kernel(*refs, rope, cache_out):
    x_ref, mod_ref, n1_ref, w_ref, qn_ref, kn_ref = refs[:6]
    refs = refs[6:]
    if rope:
        cos_ref, sin_ref = refs[:2]
        refs = refs[2:]
    q_ref, k_ref, v_ref = refs[:3]
    refs = refs[3:]
    if cache_out:
        kc_ref, vc_ref = refs[:2]
        refs = refs[2:]
    (wb_ref,) = refs

    qk_w = (N_HEADS + N_KV) * HEAD_DIM
    quarter = HEAD_DIM // 4

    def partner(a):
        width = a.shape[1]
        first = (lax.broadcasted_iota(I32, a.shape, 1) & quarter) == 0
        return jnp.where(first, pltpu.roll(a, width - quarter, 1), pltpu.roll(a, quarter, 1))

    @pl.when(pl.program_id(0) == 0)
    def _():
        w = w_ref[...]
        wb_ref[:, 0:QKV_DIM] = w.astype(BF16)
        if rope:
            wb_ref[:, QKV_DIM:] = partner(w[:, 0:qk_w]).astype(BF16)

    m = mod_ref[0, 0]
    h = _norm_mod(x_ref[...], n1_ref[...], m[0:1], m[1:2])
    qkv = _dot(h.astype(BF16), wb_ref[...])
    scale = HEAD_DIM ** -0.5 * LOG2_E
    if rope:
        gains = {True: qn_ref[...], False: kn_ref[...]}
        cos_g = {key: cos_ref[...] * g for key, g in gains.items()}
        sin_g = {key: sin_ref[...] * partner(g) for key, g in gains.items()}
    for hh in range(N_HEADS + N_KV):
        sl = slice(hh * HEAD_DIM, (hh + 1) * HEAD_DIM)
        is_q = hh < N_HEADS
        if rope:
            raw = qkv[:, sl]
            norm = lax.rsqrt(jnp.mean(raw * raw, axis=-1, keepdims=True) + EPS)
            xh = (raw * cos_g[is_q] + qkv[:, QKV_DIM + hh * HEAD_DIM:QKV_DIM + (hh + 1) * HEAD_DIM] * sin_g[is_q]) * norm
        else:
            xh = _rms(qkv[:, sl], qn_ref[...] if is_q else kn_ref[...])
        if cache_out and hh >= N_HEADS:
            kc_ref[:, hh - N_HEADS, :] = xh
        if hh < N_HEADS:
            q_ref[:, sl] = (xh * scale).astype(BF16)
        else:
            k_ref[:, (hh - N_HEADS) * HEAD_DIM:(hh - N_HEADS + 1) * HEAD_DIM] = xh.astype(BF16)
    v = qkv[:, qk_w:QKV_DIM]
    v_ref[...] = v.astype(BF16)
    if cache_out:
        for g in range(N_KV):
            vc_ref[:, g, :] = v[:, g * HEAD_DIM:(g + 1) * HEAD_DIM]


def _qkv(x, mods, layer, row_of_tile, n1, w_qkv, qn, kn, rope_tabs, cache_out, tm):
    n_tok = x.shape[0]
    kvw = N_KV * HEAD_DIM
    rope = rope_tabs is not None
    in_specs = [
        pl.BlockSpec((tm, D_MODEL), lambda i: (i, 0)),
        _mod_spec(layer, row_of_tile),
        pl.BlockSpec((1, D_MODEL), lambda i: (0, 0)),
        pl.BlockSpec((D_MODEL, QKV_DIM), lambda i: (0, 0)),
        pl.BlockSpec((1, HEAD_DIM), lambda i: (0, 0)),
        pl.BlockSpec((1, HEAD_DIM), lambda i: (0, 0)),
    ]
    args = [x, mods, n1, w_qkv, qn, kn]
    if rope:
        seq_tiles = rope_tabs[0].shape[0] // tm
        in_specs += [pl.BlockSpec((tm, HEAD_DIM), lambda i: (i % seq_tiles, 0))] * 2
        args += list(rope_tabs)
    out_shape = [jax.ShapeDtypeStruct((n_tok, D_MODEL), BF16),
                 jax.ShapeDtypeStruct((n_tok, kvw), BF16),
                 jax.ShapeDtypeStruct((n_tok, kvw), BF16)]
    out_specs = [pl.BlockSpec((tm, D_MODEL), lambda i: (i, 0)),
                 pl.BlockSpec((tm, kvw), lambda i: (i, 0)),
                 pl.BlockSpec((tm, kvw), lambda i: (i, 0))]
    if cache_out:
        out_shape += [jax.ShapeDtypeStruct((n_tok, N_KV, HEAD_DIM), F32)] * 2
        out_specs += [pl.BlockSpec((tm, N_KV, HEAD_DIM), lambda i: (i, 0, 0))] * 2
    return pl.pallas_call(
        functools.partial(_qkv_kernel, rope=rope, cache_out=cache_out),
        out_shape=out_shape,
        grid=(n_tok // tm,),
        in_specs=in_specs,
        out_specs=out_specs,
        scratch_shapes=[pltpu.VMEM((D_MODEL, QKV_DIM + ((N_HEADS + N_KV) * HEAD_DIM if rope else 0)), BF16)],
        compiler_params=_cparams(1),
        name="qkv_rope" if rope else "qkv",
    )(*args)


def _rope_tables(seq_len):
    half = HEAD_DIM // 2
    n = half // 2
    inv_freq = ROPE_THETA ** (-np.arange(n, dtype=np.float64) / n)
    rows = seq_len // GRID_W
    row = np.repeat(np.arange(rows), GRID_W).astype(np.float64)
    col = np.tile(np.arange(GRID_W), rows).astype(np.float64)
    ang_r = row[:, None] * inv_freq[None, :]
    ang_c = col[:, None] * inv_freq[None, :]
    cos = np.concatenate([np.cos(ang_r)] * 2 + [np.cos(ang_c)] * 2, axis=-1)
    sin = np.concatenate([-np.sin(ang_r), np.sin(ang_r), -np.sin(ang_c), np.sin(ang_c)], axis=-1)
    return jnp.asarray(cos, F32), jnp.asarray(sin, F32)


def _with_ones(v):
    return jnp.concatenate([v, jnp.ones_like(v)], axis=1)


def _gqa_attention(q, segments):
    rows = q.shape[0]
    qs = jnp.concatenate([q[:, h * HEAD_DIM:(h + 1) * HEAD_DIM] for h in range(Q_PER_KV)], axis=0)
    m = acc = None
    for k, v in segments:
        s = _dot_nt(qs, k)
        seg_max = jnp.max(s, axis=-1, keepdims=True)
        m_new = seg_max if m is None else jnp.maximum(m, seg_max)
        pv = _dot(jnp.exp2(s - m_new).astype(BF16), v)
        acc = pv if m is None else acc * jnp.exp2(m - m_new) + pv
        m = m_new
    o = (acc[:, :HEAD_DIM] / acc[:, HEAD_DIM:]).astype(BF16)
    return jnp.concatenate([o[h * rows:(h + 1) * rows] for h in range(Q_PER_KV)], axis=1)


def _attn_ctx_kernel(q_ref, k_ref, v_ref, o_ref, *, seq):
    gw = Q_PER_KV * HEAD_DIM
    for r in range(0, q_ref.shape[0], seq):
        for g in range(N_KV):
            kg = k_ref[r:r + seq, g * HEAD_DIM:(g + 1) * HEAD_DIM]
            vg = _with_ones(v_ref[r:r + seq, g * HEAD_DIM:(g + 1) * HEAD_DIM])
            o_ref[r:r + seq, g * gw:(g + 1) * gw] = _gqa_attention(q_ref[r:r + seq, g * gw:(g + 1) * gw], [(kg, vg)])


def _attn_ctx(q, k, v, seq, tm):
    n_tok = q.shape[0]
    kvw = N_KV * HEAD_DIM
    return pl.pallas_call(
        functools.partial(_attn_ctx_kernel, seq=seq),
        out_shape=jax.ShapeDtypeStruct((n_tok, D_MODEL), BF16),
        grid=(n_tok // tm,),
        in_specs=[pl.BlockSpec((tm, D_MODEL), lambda b: (b, 0)),
                  pl.BlockSpec((tm, kvw), lambda b: (b, 0)),
                  pl.BlockSpec((tm, kvw), lambda b: (b, 0))],
        out_specs=pl.BlockSpec((tm, D_MODEL), lambda b: (b, 0)),
        compiler_params=_cparams(1),
        name="attn_ctx",
    )(q, k, v)


KEY_CHUNK = 256


def _attn_lat_kernel(q_ref, k_ref, v_ref, kc_ref, vc_ref, o_ref):
    segments = [(k_ref[c:c + KEY_CHUNK], _with_ones(v_ref[c:c + KEY_CHUNK]))
                for c in range(0, k_ref.shape[0], KEY_CHUNK)]
    segments.append((kc_ref[0], _with_ones(vc_ref[0])))
    o_ref[...] = _gqa_attention(q_ref[...], segments)


def _attn_lat(q, k, v, kc, vc, seq, tq):
    n_tok = q.shape[0]
    batch = n_tok // seq
    past = kc.shape[1]
    qt = seq // tq
    gw = Q_PER_KV * HEAD_DIM
    return pl.pallas_call(
        _attn_lat_kernel,
        out_shape=jax.ShapeDtypeStruct((n_tok, D_MODEL), BF16),
        grid=(batch, N_KV, qt),
        in_specs=[pl.BlockSpec((tq, gw), lambda b, g, i: (b * qt + i, g)),
                  pl.BlockSpec((seq, HEAD_DIM), lambda b, g, i: (b, g)),
                  pl.BlockSpec((seq, HEAD_DIM), lambda b, g, i: (b, g)),
                  pl.BlockSpec((1, past, HEAD_DIM), lambda b, g, i: (b, 0, g)),
                  pl.BlockSpec((1, past, HEAD_DIM), lambda b, g, i: (b, 0, g))],
        out_specs=pl.BlockSpec((tq, gw), lambda b, g, i: (b * qt + i, g)),
        compiler_params=_cparams(3),
        name="attn_lat",
    )(q, k, v, kc, vc)


FRONT_ROWS = 256


def _router_weights(wr):
    return jnp.concatenate([wr, jnp.zeros((wr.shape[0], LANES - N_EXPERTS), wr.dtype)], axis=1).astype(BF16)


def _router_softmax(hb, wr):
    logits = _dot(hb, wr)
    lane = lax.broadcasted_iota(I32, logits.shape, 1)
    logits = jnp.where(lane < N_EXPERTS, logits, -1e30)
    ex = jnp.exp(logits - jnp.max(logits, axis=-1, keepdims=True))
    return ex / jnp.sum(ex, axis=-1, keepdims=True)


def _moe_front(x1, r0, m, n2, wr, x1_ref, hp_ref, aff_ref):
    n = x1.shape[0]
    x1_ref[r0:r0 + n, :] = x1
    h2 = _norm_mod(x1, n2, m[3:4], m[4:5])
    aff = _router_softmax(h2.astype(BF16), wr)
    packed = pltpu.pack_elementwise([h2[:, :HALF], h2[:, HALF:]], packed_dtype=BF16)
    for s in range(HP_ROWS):
        hp_ref[pl.ds(r0 * HP_ROWS + s, n, stride=HP_ROWS), :] = packed[:, s * LANES:(s + 1) * LANES]
    aff_t = aff.T
    for b in range(n // LANES):
        out_row = (r0 // LANES + b) * N_EXPERTS
        aff_ref[out_row:out_row + N_EXPERTS, :] = aff_t[0:N_EXPERTS, b * LANES:(b + 1) * LANES]


def _front_out(n_tok, tm):
    shapes = [jax.ShapeDtypeStruct((n_tok, D_MODEL), F32),
              jax.ShapeDtypeStruct((n_tok * HP_ROWS, LANES), U32),
              jax.ShapeDtypeStruct((n_tok // LANES * N_EXPERTS, LANES), F32)]
    specs = [pl.BlockSpec((tm, D_MODEL), lambda i: (i, 0)),
             pl.BlockSpec((tm * HP_ROWS, LANES), lambda i: (i, 0)),
             pl.BlockSpec((tm // LANES * N_EXPERTS, LANES), lambda i: (i, 0))]
    return shapes, specs


def _post_kernel(o_ref, x_ref, mod_ref, n2_ref, wo_ref, wr_ref, x1_ref, hp_ref, aff_ref, wb_ref):
    @pl.when(pl.program_id(0) == 0)
    def _():
        wb_ref[...] = wo_ref[...].astype(BF16)

    m = mod_ref[0, 0]
    wr = _router_weights(wr_ref[0])
    for r0 in range(0, x_ref.shape[0], FRONT_ROWS):
        rows = slice(r0, r0 + FRONT_ROWS)
        x1 = x_ref[rows, :] + m[2:3] * _dot(o_ref[rows, :], wb_ref[...])
        _moe_front(x1, r0, m, n2_ref[...], wr, x1_ref, hp_ref, aff_ref)


def _post(o, x, mods, layer, row_of_tile, n2, w_o, wr, tm):
    n_tok = x.shape[0]
    shapes, specs = _front_out(n_tok, tm)
    return pl.pallas_call(
        _post_kernel,
        out_shape=shapes,
        grid=(n_tok // tm,),
        in_specs=[pl.BlockSpec((tm, D_MODEL), lambda i: (i, 0)),
                  pl.BlockSpec((tm, D_MODEL), lambda i: (i, 0)),
                  _mod_spec(layer, row_of_tile),
                  pl.BlockSpec((1, D_MODEL), lambda i: (0, 0)),
                  pl.BlockSpec((D_MODEL, D_MODEL), lambda i: (0, 0)),
                  pl.BlockSpec((1, D_MODEL, N_EXPERTS), lambda i: (layer, 0, 0))],
        out_specs=specs,
        scratch_shapes=[pltpu.VMEM((D_MODEL, D_MODEL), BF16)],
        compiler_params=_cparams(1),
        name="post",
    )(o, x, mods, n2, w_o, wr)


def _pool_kernel(x_ref, xp_ref, xn_ref, mod_ref, n1_ref, n2_ref, wp_ref, ps_ref, wr_ref,
                 x1_ref, hp_ref, aff_ref, *, seq):
    tm = x_ref.shape[0]
    i = pl.program_id(0)
    m = mod_ref[0, 0]
    x = x_ref[...]

    def norm_mod(v):
        return _norm_mod(v, n1_ref[...], m[0:1], m[1:2])

    h = norm_mod(x)
    if tm <= seq:
        prev_ok = ((i * tm) % seq != 0).astype(F32)
        next_ok = (((i + 1) * tm) % seq != 0).astype(F32)
        pieces = [(norm_mod(xp_ref[...]) * prev_ok, h, norm_mod(xn_ref[...]) * next_ok)]
        t0 = (i * tm) % seq
    else:
        zeros = jnp.zeros((POOL_HALO, D_MODEL), F32)
        pieces = [(zeros, h[r:r + seq], zeros) for r in range(0, tm, seq)]
        t0 = 0
    n = min(tm, seq)
    rows = n + 2 * POOL_HALO
    t = t0 + lax.broadcasted_iota(I32, (n, 1), 0)
    deltas = [[] for _ in POOL_WINDOWS]
    for piece in pieces:
        hz = jnp.concatenate(piece, axis=0)
        for g, w in enumerate(POOL_WINDOWS):
            sl = slice(g * POOL_GROUP, (g + 1) * POOL_GROUP)
            f = hz[:, sl]
            step = 1
            while step < w:
                f = f + pltpu.roll(f, rows - step, 0)
                step *= 2
            win = pltpu.roll(f, w // 2, 0)[POOL_HALO:POOL_HALO + n]
            means = []
            for edge in (slice(0, POOL_HALO), slice(n - POOL_HALO, n)):
                cnt = jnp.minimum(t[edge] + w // 2, seq) - jnp.maximum(t[edge] - w // 2, 0)
                means.append(win[edge] / cnt.astype(F32))
            mean = jnp.concatenate([means[0], win[POOL_HALO:n - POOL_HALO] * (1.0 / w), means[1]], axis=0)
            deltas[g].append((mean - piece[1][:, sl]).astype(BF16))
    ys = [_dot(jnp.concatenate(d, axis=0), wp_ref[g].astype(BF16)) for g, d in enumerate(deltas)]
    x1 = x + (m[2:3] * ps_ref[...]) * jnp.concatenate(ys, axis=-1)
    wr = _router_weights(wr_ref[0])
    for r0 in range(0, tm, FRONT_ROWS):
        _moe_front(x1[r0:r0 + FRONT_ROWS], r0, m, n2_ref[...], wr, x1_ref, hp_ref, aff_ref)


def _pool(x, mods, layer, row_of_tile, n1, n2, w_pool, pool_scale, wr, seq, tm):
    n_tok = x.shape[0]
    hb = tm // POOL_HALO
    last = n_tok // POOL_HALO - 1
    shapes, specs = _front_out(n_tok, tm)
    return pl.pallas_call(
        functools.partial(_pool_kernel, seq=seq),
        out_shape=shapes,
        grid=(n_tok // tm,),
        in_specs=[pl.BlockSpec((tm, D_MODEL), lambda i: (i, 0)),
                  pl.BlockSpec((POOL_HALO, D_MODEL), lambda i: (jnp.maximum(i * hb - 1, 0), 0)),
                  pl.BlockSpec((POOL_HALO, D_MODEL), lambda i: (jnp.minimum((i + 1) * hb, last), 0)),
                  _mod_spec(layer, row_of_tile),
                  pl.BlockSpec((1, D_MODEL), lambda i: (0, 0)),
                  pl.BlockSpec((1, D_MODEL), lambda i: (0, 0)),
                  pl.BlockSpec(w_pool.shape, lambda i: (0, 0, 0)),
                  pl.BlockSpec((1, D_MODEL), lambda i: (0, 0)),
                  pl.BlockSpec((1, D_MODEL, N_EXPERTS), lambda i: (layer, 0, 0))],
        out_specs=specs,
        compiler_params=_cparams(1),
        name="pool",
    )(x, x, x, mods, n1, n2, w_pool, pool_scale, wr)


DIST_BIT = 8


def _select_kernel(aff_ref, src_ref, dst_ref, gate_ref, *, cap):
    nb = aff_ref.shape[0] // N_EXPERTS
    rows = nb * N_EXPERTS
    shape3 = (nb, N_EXPERTS, LANES)
    aff = aff_ref[...].reshape(shape3)

    def count(mask):
        return jnp.sum(jnp.sum(mask.astype(F32), axis=0), axis=-1, keepdims=True)

    def search(i, t):
        shift = 28 - 2 * i
        best = t
        for digit in (1, 2, 3):
            cand = t | (jnp.int32(digit) << shift)
            best = jnp.where(count(aff >= pltpu.bitcast(cand, F32)[None]) >= cap, cand, best)
        return best

    thr = pltpu.bitcast(lax.fori_loop(0, 15, search, jnp.zeros((N_EXPERTS, LANES), I32)), F32)
    gt = aff > thr[None]
    eq = aff == thr[None]
    need = cap - count(gt)

    kk = lax.broadcasted_iota(I32, (LANES, LANES), 0)
    nn = lax.broadcasted_iota(I32, (LANES, LANES), 1)
    upper = (kk <= nn).astype(BF16)
    ones = jnp.ones((LANES, LANES), BF16)

    def prefix(mask):
        m2 = mask.astype(F32).astype(BF16).reshape(rows, LANES)
        p = _dot(m2, upper).reshape(shape3)
        s = _dot(m2, ones).reshape(shape3)
        offs = []
        run = jnp.zeros((N_EXPERTS, LANES), F32)
        for b in range(nb):
            offs.append(run)
            run = run + s[b]
        return p, jnp.stack(offs, axis=0), s

    pe, oe, _ = prefix(eq)
    sel = gt | (eq & ((pe + oe) <= need[None]))
    ps, os_, ss = prefix(sel)

    lane = lax.broadcasted_iota(I32, (rows, LANES), 1)
    sel2 = sel.reshape(rows, LANES)
    dist = lane - (ps.reshape(rows, LANES).astype(I32) - 1)
    word = jnp.where(sel2, lane | (dist << DIST_BIT), 0)
    for k in range(7):
        bit = 1 << (DIST_BIT + k)
        arriving = pltpu.roll(word, LANES - (1 << k), 1)
        word = jnp.where((arriving & bit) != 0, arriving, jnp.where((word & bit) != 0, 0, word))
    val = word & (LANES - 1)
    local = val.reshape(shape3) + lax.broadcasted_iota(I32, shape3, 0) * LANES
    local_gate = jnp.take_along_axis(aff.reshape(rows, LANES), val, axis=1).reshape(shape3)

    first = os_.astype(I32)
    chosen = pltpu.bitcast(ss.astype(I32), U32)
    lane_e = lax.broadcasted_iota(I32, (N_EXPERTS, LANES), 1)
    n_chunks = cap // LANES
    acc = [jnp.zeros((N_EXPERTS, LANES), I32)] * n_chunks
    gacc = [jnp.zeros((N_EXPERTS, LANES), F32)] * n_chunks
    for b in range(nb):
        within = lane_e - first[b]
        phase = within & (LANES - 1)
        tokens = jnp.take_along_axis(local[b], phase, axis=1)
        gates = jnp.take_along_axis(local_gate[b], phase, axis=1)
        for jc in range(n_chunks):
            inside = pltpu.bitcast(within + jc * LANES, U32) < chosen[b]
            acc[jc] = jnp.where(inside, tokens, acc[jc])
            gacc[jc] = jnp.where(inside, gates, gacc[jc])
    for jc in range(n_chunks):
        out_rows = pl.ds(jc, N_EXPERTS, stride=n_chunks)
        src_ref[out_rows, :] = acc[jc] * HP_ROWS
        dst_ref[out_rows, :] = acc[jc] * Y_ROWS
        gate_ref[out_rows, :] = gacc[jc]


def _select(aff, cap):
    rows = aff.shape[0]
    out_rows = N_EXPERTS * cap // LANES
    out_spec = pl.BlockSpec((out_rows, LANES), lambda i: (0, 0))
    src, dst, gate = pl.pallas_call(
        functools.partial(_select_kernel, cap=cap),
        out_shape=[jax.ShapeDtypeStruct((out_rows, LANES), I32),
                   jax.ShapeDtypeStruct((out_rows, LANES), I32),
                   jax.ShapeDtypeStruct((out_rows, LANES), F32)],
        grid=(1,),
        in_specs=[pl.BlockSpec((rows, LANES), lambda i: (0, 0))],
        out_specs=[out_spec, out_spec, out_spec],
        compiler_params=_cparams(1),
        name="select",
    )(aff)
    return src.reshape(-1), dst.reshape(-1), gate


GATHER_UNROLL = 32
GATHER_STEP_ROWS = 2048


def _gather_kernel(src_ref, hp_ref, o_ref):
    i = pl.program_id(0)
    n = o_ref.shape[0] // HP_ROWS

    def gather(g, carry):
        base = i * n + g * GATHER_UNROLL
        for r in range(GATHER_UNROLL):
            src = pl.multiple_of(src_ref[base + r], HP_ROWS)
            dst = pl.multiple_of((g * GATHER_UNROLL + r) * HP_ROWS, HP_ROWS)
            o_ref[pl.ds(dst, HP_ROWS), :] = hp_ref[pl.ds(src, HP_ROWS), :]
        return carry

    lax.fori_loop(0, n // GATHER_UNROLL, gather, 0)


def _gather(src, hp):
    n = src.shape[0]
    return pl.pallas_call(
        _gather_kernel,
        out_shape=jax.ShapeDtypeStruct((n * HP_ROWS, LANES), U32),
        grid_spec=pltpu.PrefetchScalarGridSpec(
            num_scalar_prefetch=1,
            grid=(n // GATHER_STEP_ROWS,),
            in_specs=[pl.BlockSpec(hp.shape, lambda i, src: (0, 0), pipeline_mode=pl.Buffered(1))],
            out_specs=pl.BlockSpec((GATHER_STEP_ROWS * HP_ROWS, LANES), lambda i, src: (i, 0))),
        compiler_params=_cparams(1),
        name="gather",
    )(src, hp)


FF_CHUNK = 256


def _ffn_kernel(*refs, n_streams):
    e = pl.program_id(0)
    ins = refs[:2 * n_streams]
    wg_ref, wu_ref, wd_ref = refs[2 * n_streams:2 * n_streams + 3]
    outs = refs[2 * n_streams + 3:]
    d_ff = wg_ref.shape[3]
    eye = lax.broadcasted_iota(I32, (LANES, LANES), 0) == lax.broadcasted_iota(I32, (LANES, LANES), 1)
    for i in range(n_streams):
        xp_ref, gate_ref, y_ref = ins[2 * i], ins[2 * i + 1], outs[i]
        cap = xp_ref.shape[0] // HP_ROWS
        halves = [[], []]
        for s in range(HP_ROWS):
            w = xp_ref[pl.ds(s, cap, stride=HP_ROWS), :]
            for half in range(2):
                v = pltpu.unpack_elementwise(w, index=half, packed_dtype=BF16, unpacked_dtype=F32)
                halves[half].append(v.astype(BF16))
        xe = jnp.concatenate(halves[0] + halves[1], axis=1)
        y = None
        for c in range(0, d_ff, FF_CHUNK):
            a = _dot(xe, wg_ref[0, 0, :, c:c + FF_CHUNK].astype(BF16))
            u = _dot(xe, wu_ref[0, 0, :, c:c + FF_CHUNK].astype(BF16))
            yc = _dot((_silu(a) * u).astype(BF16), wd_ref[0, 0, c:c + FF_CHUNK, :].astype(BF16))
            y = yc if y is None else y + yc
        for c in range(cap // LANES):
            g_row = gate_ref[pl.ds(e * (cap // LANES) + c, 1), :]
            g_col = jnp.sum(jnp.where(eye, g_row, 0.0), axis=1, keepdims=True)
            tot = y[c * LANES:(c + 1) * LANES, :] * g_col
            for s in range(Y_ROWS):
                y_ref[0, pl.ds(c * LANES * Y_ROWS + s, LANES, stride=Y_ROWS), :] = tot[:, s * LANES:(s + 1) * LANES]


def _ffn(streams, layer, w_gate, w_up, w_down):
    caps = [xp.shape[0] // (N_EXPERTS * HP_ROWS) for xp, _ in streams]
    in_specs, args = [], []
    for (xp, gate), cap in zip(streams, caps):
        in_specs += [pl.BlockSpec((cap * HP_ROWS, LANES), lambda e: (e, 0)),
                     pl.BlockSpec(gate.shape, lambda e: (0, 0))]
        args += [xp, gate]
    w_spec = pl.BlockSpec((1, 1) + w_gate.shape[2:], lambda e: (layer, e, 0, 0))
    return pl.pallas_call(
        functools.partial(_ffn_kernel, n_streams=len(streams)),
        out_shape=[jax.ShapeDtypeStruct((N_EXPERTS, cap * Y_ROWS, LANES), F32) for cap in caps],
        grid=(N_EXPERTS,),
        in_specs=in_specs + [w_spec, w_spec, w_spec],
        out_specs=[pl.BlockSpec((1, cap * Y_ROWS, LANES), lambda e: (e, 0, 0)) for cap in caps],
        compiler_params=_cparams(1),
        name="ffn",
    )(*args, w_gate, w_up, w_down)


SCATTER_UNROLL = 16
SCATTER_GROUPS = 2


def _combine_kernel(dst_ref, y_ref, x1_ref, mod_ref, o_ref, acc_ref, *, n_scatter):
    s = pl.program_id(0)
    tf = o_ref.shape[0]
    step_rows = y_ref.shape[0] // Y_ROWS

    @pl.when(s == 0)
    def _():
        acc_ref[...] = jnp.zeros_like(acc_ref)

    @pl.when(s < n_scatter)
    def _():
        def scatter(g, carry):
            for sub in range(SCATTER_GROUPS):
                first = (g * SCATTER_GROUPS + sub) * SCATTER_UNROLL
                base = s * step_rows + first
                rows, sums = [], []
                for r in range(SCATTER_UNROLL):
                    dst = pl.ds(pl.multiple_of(dst_ref[base + r], Y_ROWS), Y_ROWS)
                    src = pl.ds(pl.multiple_of((first + r) * Y_ROWS, Y_ROWS), Y_ROWS)
                    rows.append(dst)
                    sums.append(acc_ref[dst, :] + y_ref[src, :])
                for dst, v in zip(rows, sums):
                    acc_ref[dst, :] = v
            return carry

        lax.fori_loop(0, step_rows // (SCATTER_UNROLL * SCATTER_GROUPS), scatter, 0)

    @pl.when(s >= n_scatter)
    def _():
        g2 = mod_ref[0, 0][5:6]
        tile = acc_ref.at[pl.ds(pl.multiple_of((s - n_scatter) * tf * Y_ROWS, tf * Y_ROWS), tf * Y_ROWS), :]
        for c in range(Y_ROWS):
            sl = slice(c * LANES, (c + 1) * LANES)
            o_ref[:, sl] = x1_ref[:, sl] + g2[:, sl] * tile[pl.ds(c, tf, stride=Y_ROWS), :]


def _combine_tiles(n_tok):
    row_bytes = D_MODEL * 4
    free_rows = (VMEM_LIMIT - n_tok * row_bytes) * 4 // 5 // row_bytes
    scale = max(k for k in (1, 2, 4) if 2 * 512 * k + 4 * 256 * k <= free_rows)
    return 256 * scale, 512 * scale


def _combine(dst, y, x1, mods, layer, row_of_token):
    n_tok = x1.shape[0]
    tf, step_rows = _combine_tiles(n_tok)
    n_scatter = dst.shape[0] // step_rows
    assert (dst.shape[0] // N_EXPERTS) % SCATTER_UNROLL == 0 and dst.shape[0] % step_rows == 0
    tile_of = lambda s: jnp.maximum(s - n_scatter, 0)
    row_of_tile = lambda t: row_of_token(t * tf)
    return pl.pallas_call(
        functools.partial(_combine_kernel, n_scatter=n_scatter),
        out_shape=jax.ShapeDtypeStruct((n_tok, D_MODEL), F32),
        grid_spec=pltpu.PrefetchScalarGridSpec(
            num_scalar_prefetch=1,
            grid=(n_scatter + n_tok // tf,),
            in_specs=[pl.BlockSpec((step_rows * Y_ROWS, LANES), lambda s, *_: (jnp.minimum(s, n_scatter - 1), 0)),
                      pl.BlockSpec((tf, D_MODEL), lambda s, *_: (tile_of(s), 0)),
                      pl.BlockSpec((1, 1, N_MOD, D_MODEL), lambda s, *_: (layer, row_of_tile(tile_of(s)), 0, 0))],
            out_specs=pl.BlockSpec((tf, D_MODEL), lambda s, *_: (tile_of(s), 0)),
            scratch_shapes=[pltpu.VMEM((n_tok * Y_ROWS, LANES), F32)]),
        compiler_params=_cparams(1),
        name="combine",
    )(dst, y.reshape(-1, LANES), x1, mods)


TOKEN_TILE = 1024
Q_TILE = 2048


def _moe(fronts, rows_of_token, mods, layer, w_gate, w_up, w_down):
    routed, dsts = [], []
    for x1, hp, aff in fronts:
        cap = CAPACITY_FACTOR * x1.shape[0] // N_EXPERTS
        src, dst, gate = _select(aff, cap)
        routed.append((_gather(src, hp), gate))
        dsts.append(dst)
    ys = _ffn(routed, layer, w_gate, w_up, w_down)
    return [_combine(dst, y, x1, mods, layer, row_of_token)
            for dst, y, (x1, _, _), row_of_token in zip(dsts, ys, fronts, rows_of_token)]


def kernel(x_prompt, x_sample, cache_k, cache_v, c, c_ctx, norm1, norm2, w_ada, b_ada, w_qkv, q_norm, k_norm,
           w_o, w_pool, pool_scale, w_router, w_e_gate, w_e_up, w_e_down):
    batch, seq, _ = x_prompt.shape
    dec_batch, dec_seq, _ = x_sample.shape
    depth = w_ada.shape[0]
    kvw = N_KV * HEAD_DIM

    cvec = jnp.zeros((SUBLANES, D_MODEL), F32).at[0].set(c_ctx).at[1:1 + dec_batch].set(c)
    mods = _ada(cvec, w_ada, b_ada)
    ctx = x_prompt.reshape(batch * seq, D_MODEL)
    lat = x_sample.reshape(dec_batch * dec_seq, D_MODEL)
    ctx_row = lambda i: 0
    lat_row = lambda tile: (lambda i: 1 + (i * tile) // dec_seq)
    ctx_tile = TOKEN_TILE if TOKEN_TILE % seq == 0 else min(TOKEN_TILE, seq)
    rope_tabs = _rope_tables(dec_seq)
    new_k = new_v = None

    for layer in range(depth):
        j = layer // 2
        n1 = norm1[layer][None]
        n2 = norm2[layer][None]
        wr = w_router
        if layer % 2 == 0:
            qn = q_norm[j][None]
            kn = k_norm[j][None]
            qc, kc, vc, new_k, new_v = _qkv(ctx, mods, layer, ctx_row, n1, w_qkv[j], qn, kn, None, True, TOKEN_TILE)
            oc = _attn_ctx(qc, kc, vc, seq, TOKEN_TILE)
            ql, kl, vl = _qkv(lat, mods, layer, lat_row(TOKEN_TILE), n1, w_qkv[j], qn, kn, rope_tabs, False,
                              TOKEN_TILE)
            past_k = cache_k[:, j].reshape(dec_batch, -1, kvw).astype(BF16)
            past_v = cache_v[:, j].reshape(dec_batch, -1, kvw).astype(BF16)
            ol = _attn_lat(ql, kl, vl, past_k, past_v, dec_seq, Q_TILE)
            ctx1 = _post(oc, ctx, mods, layer, ctx_row, n2, w_o[j], wr, TOKEN_TILE)
            lat1 = _post(ol, lat, mods, layer, lat_row(TOKEN_TILE), n2, w_o[j], wr, TOKEN_TILE)
        else:
            ps = pool_scale[j][None]
            ctx1 = _pool(ctx, mods, layer, ctx_row, n1, n2, w_pool[j], ps, wr, seq, ctx_tile)
            lat1 = _pool(lat, mods, layer, lat_row(TOKEN_TILE), n1, n2, w_pool[j], ps, wr, dec_seq, TOKEN_TILE)
        ctx, lat = _moe([ctx1, lat1], [ctx_row, lat_row(1)], mods, layer, w_e_gate, w_e_up, w_e_down)

    new_cache_k = new_k.reshape(batch, 1, seq, N_KV, HEAD_DIM)
    new_cache_v = new_v.reshape(batch, 1, seq, N_KV, HEAD_DIM)
    return (ctx.reshape(batch, seq, D_MODEL), lat.reshape(dec_batch, dec_seq, D_MODEL), new_cache_k, new_cache_v)
```

```python
import functools

import jax
import jax.numpy as jnp
import numpy as np
from jax import lax
from jax.experimental import pallas as pl
from jax.experimental.pallas import tpu as pltpu

F32 = jnp.float32
BF16 = jnp.bfloat16
I32 = jnp.int32
U32 = jnp.uint32

D_MODEL = 1024
HEAD_DIM = 128
N_HEADS = 8
N_KV = 2
Q_PER_KV = N_HEADS // N_KV
QKV_DIM = (N_HEADS + 2 * N_KV) * HEAD_DIM
GRID_W = 64
ROPE_THETA = 10000.0
POOL_WINDOWS = (2, 4, 8, 16)
POOL_GROUP = D_MODEL // len(POOL_WINDOWS)
POOL_HALO = max(POOL_WINDOWS) // 2
N_EXPERTS = 16
CAPACITY_FACTOR = 2
N_MOD = 6
EPS = 1e-6
LOG2_E = 1.4426950408889634

LANES = 128
SUBLANES = 8
HALF = D_MODEL // 2
HP_ROWS = HALF // LANES
Y_ROWS = D_MODEL // LANES
VMEM_LIMIT = 56 * 1024 * 1024


def _cparams(n_axes, vmem=VMEM_LIMIT):
    return pltpu.CompilerParams(dimension_semantics=("arbitrary",) * n_axes, vmem_limit_bytes=vmem)


def _silu(x):
    return x / (1.0 + jnp.exp(-x))


def _rms(x, gain):
    return x * lax.rsqrt(jnp.mean(x * x, axis=-1, keepdims=True) + EPS) * gain


def _norm_mod(x, gain, shift, scale):
    return _rms(x, gain * (1.0 + scale)) + shift


def _dot(a, b):
    return jnp.dot(a, b, preferred_element_type=F32)


def _dot_nt(a, b):
    return lax.dot_general(a, b, (((1,), (1,)), ((), ())), preferred_element_type=F32)


def _ada_kernel(c_ref, w_ref, b_ref, o_ref):
    s = _silu(c_ref[...]).astype(BF16)
    o_ref[0] = _dot(s, w_ref[0].astype(BF16)) + b_ref[0]


def _ada(cvec, w_ada, b_ada):
    depth = w_ada.shape[0]
    out = pl.pallas_call(
        _ada_kernel,
        out_shape=jax.ShapeDtypeStruct((depth, SUBLANES, N_MOD * D_MODEL), F32),
        grid=(depth, N_MOD),
        in_specs=[
            pl.BlockSpec((SUBLANES, D_MODEL), lambda i, j: (0, 0)),
            pl.BlockSpec((1, D_MODEL, D_MODEL), lambda i, j: (i, 0, j)),
            pl.BlockSpec((1, 1, D_MODEL), lambda i, j: (i, 0, j)),
        ],
        out_specs=pl.BlockSpec((1, SUBLANES, D_MODEL), lambda i, j: (i, 0, j)),
        compiler_params=_cparams(2),
        name="ada",
    )(cvec, w_ada, b_ada.reshape(depth, 1, N_MOD * D_MODEL))
    return out.reshape(depth, SUBLANES, N_MOD, D_MODEL)


def _mod_spec(layer, row_of_tile):
    return pl.BlockSpec((1, 1, N_MOD, D_MODEL), lambda i, *_: (layer, row_of_tile(i), 0, 0))


def _qkv_kernel(*refs, rope, cache_out):
    x_ref, mod_ref, n1_ref, w_ref, qn_ref, kn_ref = refs[:6]
    refs = refs[6:]
    if rope:
        cos_ref, sin_ref = refs[:2]
        refs = refs[2:]
    q_ref, k_ref, v_ref = refs[:3]
    refs = refs[3:]
    if cache_out:
        kc_ref, vc_ref = refs[:2]
        refs = refs[2:]
    (wb_ref,) = refs

    qk_w = (N_HEADS + N_KV) * HEAD_DIM
    quarter = HEAD_DIM // 4

    def partner(a):
        width = a.shape[1]
        first = (lax.broadcasted_iota(I32, a.shape, 1) & quarter) == 0
        return jnp.where(first, pltpu.roll(a, width - quarter, 1), pltpu.roll(a, quarter, 1))

    @pl.when(pl.program_id(0) == 0)
    def _():
        w = w_ref[...]
        wb_ref[:, 0:QKV_DIM] = w.astype(BF16)
        if rope:
            wb_ref[:, QKV_DIM:] = partner(w[:, 0:qk_w]).astype(BF16)

    m = mod_ref[0, 0]
    h = _norm_mod(x_ref[...], n1_ref[...], m[0:1], m[1:2])
    qkv = _dot(h.astype(BF16), wb_ref[...])
    scale = HEAD_DIM ** -0.5 * LOG2_E
    if rope:
        gains = {True: qn_ref[...], False: kn_ref[...]}
        cos_g = {key: cos_ref[...] * g for key, g in gains.items()}
        sin_g = {key: sin_ref[...] * partner(g) for key, g in gains.items()}
    for hh in range(N_HEADS + N_KV):
        sl = slice(hh * HEAD_DIM, (hh + 1) * HEAD_DIM)
        is_q = hh < N_HEADS
        if rope:
            raw = qkv[:, sl]
            norm = lax.rsqrt(jnp.mean(raw * raw, axis=-1, keepdims=True) + EPS)
            xh = (raw * cos_g[is_q] + qkv[:, QKV_DIM + hh * HEAD_DIM:QKV_DIM + (hh + 1) * HEAD_DIM] * sin_g[is_q]) * norm
        else:
            xh = _rms(qkv[:, sl], qn_ref[...] if is_q else kn_ref[...])
        if cache_out and hh >= N_HEADS:
            kc_ref[:, hh - N_HEADS, :] = xh
        if hh < N_HEADS:
            q_ref[:, sl] = (xh * scale).astype(BF16)
        else:
            k_ref[:, (hh - N_HEADS) * HEAD_DIM:(hh - N_HEADS + 1) * HEAD_DIM] = xh.astype(BF16)
    v = qkv[:, qk_w:QKV_DIM]
    v_ref[...] = v.astype(BF16)
    if cache_out:
        for g in range(N_KV):
            vc_ref[:, g, :] = v[:, g * HEAD_DIM:(g + 1) * HEAD_DIM]


def _qkv(x, mods, layer, row_of_tile, n1, w_qkv, qn, kn, rope_tabs, cache_out, tm):
    n_tok = x.shape[0]
    kvw = N_KV * HEAD_DIM
    rope = rope_tabs is not None
    in_specs = [
        pl.BlockSpec((tm, D_MODEL), lambda i: (i, 0)),
        _mod_spec(layer, row_of_tile),
        pl.BlockSpec((1, D_MODEL), lambda i: (0, 0)),
        pl.BlockSpec((D_MODEL, QKV_DIM), lambda i: (0, 0)),
        pl.BlockSpec((1, HEAD_DIM), lambda i: (0, 0)),
        pl.BlockSpec((1, HEAD_DIM), lambda i: (0, 0)),
    ]
    args = [x, mods, n1, w_qkv, qn, kn]
    if rope:
        seq_tiles = rope_tabs[0].shape[0] // tm
        in_specs += [pl.BlockSpec((tm, HEAD_DIM), lambda i: (i % seq_tiles, 0))] * 2
        args += list(rope_tabs)
    out_shape = [jax.ShapeDtypeStruct((n_tok, D_MODEL), BF16),
                 jax.ShapeDtypeStruct((n_tok, kvw), BF16),
                 jax.ShapeDtypeStruct((n_tok, kvw), BF16)]
    out_specs = [pl.BlockSpec((tm, D_MODEL), lambda i: (i, 0)),
                 pl.BlockSpec((tm, kvw), lambda i: (i, 0)),
                 pl.BlockSpec((tm, kvw), lambda i: (i, 0))]
    if cache_out:
        out_shape += [jax.ShapeDtypeStruct((n_tok, N_KV, HEAD_DIM), F32)] * 2
        out_specs += [pl.BlockSpec((tm, N_KV, HEAD_DIM), lambda i: (i, 0, 0))] * 2
    return pl.pallas_call(
        functools.partial(_qkv_kernel, rope=rope, cache_out=cache_out),
        out_shape=out_shape,
        grid=(n_tok // tm,),
        in_specs=in_specs,
        out_specs=out_specs,
        scratch_shapes=[pltpu.VMEM((D_MODEL, QKV_DIM + ((N_HEADS + N_KV) * HEAD_DIM if rope else 0)), BF16)],
        compiler_params=_cparams(1),
        name="qkv_rope" if rope else "qkv",
    )(*args)


def _rope_tables(seq_len):
    half = HEAD_DIM // 2
    n = half // 2
    inv_freq = ROPE_THETA ** (-np.arange(n, dtype=np.float64) / n)
    rows = seq_len // GRID_W
    row = np.repeat(np.arange(rows), GRID_W).astype(np.float64)
    col = np.tile(np.arange(GRID_W), rows).astype(np.float64)
    ang_r = row[:, None] * inv_freq[None, :]
    ang_c = col[:, None] * inv_freq[None, :]
    cos = np.concatenate([np.cos(ang_r)] * 2 + [np.cos(ang_c)] * 2, axis=-1)
    sin = np.concatenate([-np.sin(ang_r), np.sin(ang_r), -np.sin(ang_c), np.sin(ang_c)], axis=-1)
    return jnp.asarray(cos, F32), jnp.asarray(sin, F32)


def _with_ones(v):
    return jnp.concatenate([v, jnp.ones_like(v)], axis=1)


def _gqa_attention(q, segments):
    rows = q.shape[0]
    qs = jnp.concatenate([q[:, h * HEAD_DIM:(h + 1) * HEAD_DIM] for h in range(Q_PER_KV)], axis=0)
    m = acc = None
    for k, v in segments:
        s = _dot_nt(qs, k)
        seg_max = jnp.max(s, axis=-1, keepdims=True)
        m_new = seg_max if m is None else jnp.maximum(m, seg_max)
        pv = _dot(jnp.exp2(s - m_new).astype(BF16), v)
        acc = pv if m is None else acc * jnp.exp2(m - m_new) + pv
        m = m_new
    o = (acc[:, :HEAD_DIM] / acc[:, HEAD_DIM:]).astype(BF16)
    return jnp.concatenate([o[h * rows:(h + 1) * rows] for h in range(Q_PER_KV)], axis=1)


def _attn_ctx_kernel(q_ref, k_ref, v_ref, o_ref, *, seq):
    gw = Q_PER_KV * HEAD_DIM
    for r in range(0, q_ref.shape[0], seq):
        for g in range(N_KV):
            kg = k_ref[r:r + seq, g * HEAD_DIM:(g + 1) * HEAD_DIM]
            vg = _with_ones(v_ref[r:r + seq, g * HEAD_DIM:(g + 1) * HEAD_DIM])
            o_ref[r:r + seq, g * gw:(g + 1) * gw] = _gqa_attention(q_ref[r:r + seq, g * gw:(g + 1) * gw], [(kg, vg)])


def _attn_ctx(q, k, v, seq, tm):
    n_tok = q.shape[0]
    kvw = N_KV * HEAD_DIM
    return pl.pallas_call(
        functools.partial(_attn_ctx_kernel, seq=seq),
        out_shape=jax.ShapeDtypeStruct((n_tok, D_MODEL), BF16),
        grid=(n_tok // tm,),
        in_specs=[pl.BlockSpec((tm, D_MODEL), lambda b: (b, 0)),
                  pl.BlockSpec((tm, kvw), lambda b: (b, 0)),
                  pl.BlockSpec((tm, kvw), lambda b: (b, 0))],
        out_specs=pl.BlockSpec((tm, D_MODEL), lambda b: (b, 0)),
        compiler_params=_cparams(1),
        name="attn_ctx",
    )(q, k, v)


KEY_CHUNK = 256


def _attn_lat_kernel(q_ref, k_ref, v_ref, kc_ref, vc_ref, o_ref):
    segments = [(k_ref[c:c + KEY_CHUNK], _with_ones(v_ref[c:c + KEY_CHUNK]))
                for c in range(0, k_ref.shape[0], KEY_CHUNK)]
    segments.append((kc_ref[0], _with_ones(vc_ref[0])))
    o_ref[...] = _gqa_attention(q_ref[...], segments)


def _attn_lat(q, k, v, kc, vc, seq, tq):
    n_tok = q.shape[0]
    batch = n_tok // seq
    past = kc.shape[1]
    qt = seq // tq
    gw = Q_PER_KV * HEAD_DIM
    return pl.pallas_call(
        _attn_lat_kernel,
        out_shape=jax.ShapeDtypeStruct((n_tok, D_MODEL), BF16),
        grid=(batch, N_KV, qt),
        in_specs=[pl.BlockSpec((tq, gw), lambda b, g, i: (b * qt + i, g)),
                  pl.BlockSpec((seq, HEAD_DIM), lambda b, g, i: (b, g)),
                  pl.BlockSpec((seq, HEAD_DIM), lambda b, g, i: (b, g)),
                  pl.BlockSpec((1, past, HEAD_DIM), lambda b, g, i: (b, 0, g)),
                  pl.BlockSpec((1, past, HEAD_DIM), lambda b, g, i: (b, 0, g))],
        out_specs=pl.BlockSpec((tq, gw), lambda b, g, i: (b * qt + i, g)),
        compiler_params=_cparams(3),
        name="attn_lat",
    )(q, k, v, kc, vc)


FRONT_ROWS = 256


def _router_weights(wr):
    return jnp.concatenate([wr, jnp.zeros((wr.shape[0], LANES - N_EXPERTS), wr.dtype)], axis=1).astype(BF16)


def _router_softmax(hb, wr):
    logits = _dot(hb, wr)
    lane = lax.broadcasted_iota(I32, logits.shape, 1)
    logits = jnp.where(lane < N_EXPERTS, logits, -1e30)
    ex = jnp.exp(logits - jnp.max(logits, axis=-1, keepdims=True))
    return ex / jnp.sum(ex, axis=-1, keepdims=True)


def _moe_front(x1, r0, m, n2, wr, x1_ref, hp_ref, aff_ref):
    n = x1.shape[0]
    x1_ref[r0:r0 + n, :] = x1
    h2 = _norm_mod(x1, n2, m[3:4], m[4:5])
    aff = _router_softmax(h2.astype(BF16), wr)
    packed = pltpu.pack_elementwise([h2[:, :HALF], h2[:, HALF:]], packed_dtype=BF16)
    for s in range(HP_ROWS):
        hp_ref[pl.ds(r0 * HP_ROWS + s, n, stride=HP_ROWS), :] = packed[:, s * LANES:(s + 1) * LANES]
    aff_t = aff.T
    for b in range(n // LANES):
        out_row = (r0 // LANES + b) * N_EXPERTS
        aff_ref[out_row:out_row + N_EXPERTS, :] = aff_t[0:N_EXPERTS, b * LANES:(b + 1) * LANES]


def _front_out(n_tok, tm):
    shapes = [jax.ShapeDtypeStruct((n_tok, D_MODEL), F32),
              jax.ShapeDtypeStruct((n_tok * HP_ROWS, LANES), U32),
              jax.ShapeDtypeStruct((n_tok // LANES * N_EXPERTS, LANES), F32)]
    specs = [pl.BlockSpec((tm, D_MODEL), lambda i: (i, 0)),
             pl.BlockSpec((tm * HP_ROWS, LANES), lambda i: (i, 0)),
             pl.BlockSpec((tm // LANES * N_EXPERTS, LANES), lambda i: (i, 0))]
    return shapes, specs


def _post_kernel(o_ref, x_ref, mod_ref, n2_ref, wo_ref, wr_ref, x1_ref, hp_ref, aff_ref, wb_ref):
    @pl.when(pl.program_id(0) == 0)
    def _():
        wb_ref[...] = wo_ref[...].astype(BF16)

    m = mod_ref[0, 0]
    wr = _router_weights(wr_ref[0])
    for r0 in range(0, x_ref.shape[0], FRONT_ROWS):
        rows = slice(r0, r0 + FRONT_ROWS)
        x1 = x_ref[rows, :] + m[2:3] * _dot(o_ref[rows, :], wb_ref[...])
        _moe_front(x1, r0, m, n2_ref[...], wr, x1_ref, hp_ref, aff_ref)


def _post(o, x, mods, layer, row_of_tile, n2, w_o, wr, tm):
    n_tok = x.shape[0]
    shapes, specs = _front_out(n_tok, tm)
    return pl.pallas_call(
        _post_kernel,
        out_shape=shapes,
        grid=(n_tok // tm,),
        in_specs=[pl.BlockSpec((tm, D_MODEL), lambda i: (i, 0)),
                  pl.BlockSpec((tm, D_MODEL), lambda i: (i, 0)),
                  _mod_spec(layer, row_of_tile),
                  pl.BlockSpec((1, D_MODEL), lambda i: (0, 0)),
                  pl.BlockSpec((D_MODEL, D_MODEL), lambda i: (0, 0)),
                  pl.BlockSpec((1, D_MODEL, N_EXPERTS), lambda i: (layer, 0, 0))],
        out_specs=specs,
        scratch_shapes=[pltpu.VMEM((D_MODEL, D_MODEL), BF16)],
        compiler_params=_cparams(1),
        name="post",
    )(o, x, mods, n2, w_o, wr)


def _pool_kernel(x_ref, xp_ref, xn_ref, mod_ref, n1_ref, n2_ref, wp_ref, ps_ref, wr_ref,
                 x1_ref, hp_ref, aff_ref, *, seq):
    tm = x_ref.shape[0]
    i = pl.program_id(0)
    m = mod_ref[0, 0]
    x = x_ref[...]

    def norm_mod(v):
        return _norm_mod(v, n1_ref[...], m[0:1], m[1:2])

    h = norm_mod(x)
    if tm <= seq:
        prev_ok = ((i * tm) % seq != 0).astype(F32)
        next_ok = (((i + 1) * tm) % seq != 0).astype(F32)
        pieces = [(norm_mod(xp_ref[...]) * prev_ok, h, norm_mod(xn_ref[...]) * next_ok)]
        t0 = (i * tm) % seq
    else:
        zeros = jnp.zeros((POOL_HALO, D_MODEL), F32)
        pieces = [(zeros, h[r:r + seq], zeros) for r in range(0, tm, seq)]
        t0 = 0
    n = min(tm, seq)
    rows = n + 2 * POOL_HALO
    t = t0 + lax.broadcasted_iota(I32, (n, 1), 0)
    deltas = [[] for _ in POOL_WINDOWS]
    for piece in pieces:
        hz = jnp.concatenate(piece, axis=0)
        for g, w in enumerate(POOL_WINDOWS):
            sl = slice(g * POOL_GROUP, (g + 1) * POOL_GROUP)
            f = hz[:, sl]
            step = 1
            while step < w:
                f = f + pltpu.roll(f, rows - step, 0)
                step *= 2
            win = pltpu.roll(f, w // 2, 0)[POOL_HALO:POOL_HALO + n]
            means = []
            for edge in (slice(0, POOL_HALO), slice(n - POOL_HALO, n)):
                cnt = jnp.minimum(t[edge] + w // 2, seq) - jnp.maximum(t[edge] - w // 2, 0)
                means.append(win[edge] / cnt.astype(F32))
            mean = jnp.concatenate([means[0], win[POOL_HALO:n - POOL_HALO] * (1.0 / w), means[1]], axis=0)
            deltas[g].append((mean - piece[1][:, sl]).astype(BF16))
    ys = [_dot(jnp.concatenate(d, axis=0), wp_ref[g].astype(BF16)) for g, d in enumerate(deltas)]
    x1 = x + (m[2:3] * ps_ref[...]) * jnp.concatenate(ys, axis=-1)
    wr = _router_weights(wr_ref[0])
    for r0 in range(0, tm, FRONT_ROWS):
        _moe_front(x1[r0:r0 + FRONT_ROWS], r0, m, n2_ref[...], wr, x1_ref, hp_ref, aff_ref)


def _pool(x, mods, layer, row_of_tile, n1, n2, w_pool, pool_scale, wr, seq, tm):
    n_tok = x.shape[0]
    hb = tm // POOL_HALO
    last = n_tok // POOL_HALO - 1
    shapes, specs = _front_out(n_tok, tm)
    return pl.pallas_call(
        functools.partial(_pool_kernel, seq=seq),
        out_shape=shapes,
        grid=(n_tok // tm,),
        in_specs=[pl.BlockSpec((tm, D_MODEL), lambda i: (i, 0)),
                  pl.BlockSpec((POOL_HALO, D_MODEL), lambda i: (jnp.maximum(i * hb - 1, 0), 0)),
                  pl.BlockSpec((POOL_HALO, D_MODEL), lambda i: (jnp.minimum((i + 1) * hb, last), 0)),
                  _mod_spec(layer, row_of_tile),
                  pl.BlockSpec((1, D_MODEL), lambda i: (0, 0)),
                  pl.BlockSpec((1, D_MODEL), lambda i: (0, 0)),
                  pl.BlockSpec(w_pool.shape, lambda i: (0, 0, 0)),
                  pl.BlockSpec((1, D_MODEL), lambda i: (0, 0)),
                  pl.BlockSpec((1, D_MODEL, N_EXPERTS), lambda i: (layer, 0, 0))],
        out_specs=specs,
        compiler_params=_cparams(1),
        name="pool",
    )(x, x, x, mods, n1, n2, w_pool, pool_scale, wr)


DIST_BIT = 8


def _select_kernel(aff_ref, src_ref, dst_ref, gate_ref, *, cap):
    nb = aff_ref.shape[0] // N_EXPERTS
    rows = nb * N_EXPERTS
    shape3 = (nb, N_EXPERTS, LANES)
    aff = aff_ref[...].reshape(shape3)

    def count(mask):
        return jnp.sum(jnp.sum(mask.astype(F32), axis=0), axis=-1, keepdims=True)

    def search(i, t):
        shift = 28 - 2 * i
        best = t
        for digit in (1, 2, 3):
            cand = t | (jnp.int32(digit) << shift)
            best = jnp.where(count(aff >= pltpu.bitcast(cand, F32)[None]) >= cap, cand, best)
        return best

    thr = pltpu.bitcast(lax.fori_loop(0, 15, search, jnp.zeros((N_EXPERTS, LANES), I32)), F32)
    gt = aff > thr[None]
    eq = aff == thr[None]
    need = cap - count(gt)

    kk = lax.broadcasted_iota(I32, (LANES, LANES), 0)
    nn = lax.broadcasted_iota(I32, (LANES, LANES), 1)
    upper = (kk <= nn).astype(BF16)
    ones = jnp.ones((LANES, LANES), BF16)

    def prefix(mask):
        m2 = mask.astype(F32).astype(BF16).reshape(rows, LANES)
        p = _dot(m2, upper).reshape(shape3)
        s = _dot(m2, ones).reshape(shape3)
        offs = []
        run = jnp.zeros((N_EXPERTS, LANES), F32)
        for b in range(nb):
            offs.append(run)
            run = run + s[b]
        return p, jnp.stack(offs, axis=0), s

    pe, oe, _ = prefix(eq)
    sel = gt | (eq & ((pe + oe) <= need[None]))
    ps, os_, ss = prefix(sel)

    lane = lax.broadcasted_iota(I32, (rows, LANES), 1)
    sel2 = sel.reshape(rows, LANES)
    dist = lane - (ps.reshape(rows, LANES).astype(I32) - 1)
    word = jnp.where(sel2, lane | (dist << DIST_BIT), 0)
    for k in range(7):
        bit = 1 << (DIST_BIT + k)
        arriving = pltpu.roll(word, LANES - (1 << k), 1)
        word = jnp.where((arriving & bit) != 0, arriving, jnp.where((word & bit) != 0, 0, word))
    val = word & (LANES - 1)
    local = val.reshape(shape3) + lax.broadcasted_iota(I32, shape3, 0) * LANES
    local_gate = jnp.take_along_axis(aff.reshape(rows, LANES), val, axis=1).reshape(shape3)

    first = os_.astype(I32)
    chosen = pltpu.bitcast(ss.astype(I32), U32)
    lane_e = lax.broadcasted_iota(I32, (N_EXPERTS, LANES), 1)
    n_chunks = cap // LANES
    acc = [jnp.zeros((N_EXPERTS, LANES), I32)] * n_chunks
    gacc = [jnp.zeros((N_EXPERTS, LANES), F32)] * n_chunks
    for b in range(nb):
        within = lane_e - first[b]
        phase = within & (LANES - 1)
        tokens = jnp.take_along_axis(local[b], phase, axis=1)
        gates = jnp.take_along_axis(local_gate[b], phase, axis=1)
        for jc in range(n_chunks):
            inside = pltpu.bitcast(within + jc * LANES, U32) < chosen[b]
            acc[jc] = jnp.where(inside, tokens, acc[jc])
            gacc[jc] = jnp.where(inside, gates, gacc[jc])
    for jc in range(n_chunks):
        out_rows = pl.ds(jc, N_EXPERTS, stride=n_chunks)
        src_ref[out_rows, :] = acc[jc] * HP_ROWS
        dst_ref[out_rows, :] = acc[jc] * Y_ROWS
        gate_ref[out_rows, :] = gacc[jc]


def _select(aff, cap):
    rows = aff.shape[0]
    out_rows = N_EXPERTS * cap // LANES
    out_spec = pl.BlockSpec((out_rows, LANES), lambda i: (0, 0))
    src, dst, gate = pl.pallas_call(
        functools.partial(_select_kernel, cap=cap),
        out_shape=[jax.ShapeDtypeStruct((out_rows, LANES), I32),
                   jax.ShapeDtypeStruct((out_rows, LANES), I32),
                   jax.ShapeDtypeStruct((out_rows, LANES), F32)],
        grid=(1,),
        in_specs=[pl.BlockSpec((rows, LANES), lambda i: (0, 0))],
        out_specs=[out_spec, out_spec, out_spec],
        compiler_params=_cparams(1),
        name="select",
    )(aff)
    return src.reshape(-1), dst.reshape(-1), gate


GATHER_UNROLL = 32
GATHER_STEP_ROWS = 2048


def _gather_kernel(src_ref, hp_ref, o_ref):
    i = pl.program_id(0)
    n = o_ref.shape[0] // HP_ROWS

    def gather(g, carry):
        base = i * n + g * GATHER_UNROLL
        for r in range(GATHER_UNROLL):
            src = pl.multiple_of(src_ref[base + r], HP_ROWS)
            dst = pl.multiple_of((g * GATHER_UNROLL + r) * HP_ROWS, HP_ROWS)
            o_ref[pl.ds(dst, HP_ROWS), :] = hp_ref[pl.ds(src, HP_ROWS), :]
        return carry

    lax.fori_loop(0, n // GATHER_UNROLL, gather, 0)


def _gather(src, hp):
    n = src.shape[0]
    return pl.pallas_call(
        _gather_kernel,
        out_shape=jax.ShapeDtypeStruct((n * HP_ROWS, LANES), U32),
        grid_spec=pltpu.PrefetchScalarGridSpec(
            num_scalar_prefetch=1,
            grid=(n // GATHER_STEP_ROWS,),
            in_specs=[pl.BlockSpec(hp.shape, lambda i, src: (0, 0), pipeline_mode=pl.Buffered(1))],
            out_specs=pl.BlockSpec((GATHER_STEP_ROWS * HP_ROWS, LANES), lambda i, src: (i, 0))),
        compiler_params=_cparams(1),
        name="gather",
    )(src, hp)


FF_CHUNK = 256
FF_ROWS = 512


def _ffn_kernel(*refs, n_streams):
    e = pl.program_id(0)
    ins = refs[:2 * n_streams]
    wg_ref, wu_ref, wd_ref = refs[2 * n_streams:2 * n_streams + 3]
    outs = refs[2 * n_streams + 3:]
    d_ff = wg_ref.shape[3]
    eye = lax.broadcasted_iota(I32, (LANES, LANES), 0) == lax.broadcasted_iota(I32, (LANES, LANES), 1)
    for i in range(n_streams):
        xp_ref, gate_ref, y_ref = ins[2 * i], ins[2 * i + 1], outs[i]
        cap = xp_ref.shape[0] // HP_ROWS
        for r0 in range(0, cap, FF_ROWS):
            halves = [[], []]
            for s in range(HP_ROWS):
                w = xp_ref[pl.ds(r0 * HP_ROWS + s, FF_ROWS, stride=HP_ROWS), :]
                for half in range(2):
                    v = pltpu.unpack_elementwise(w, index=half, packed_dtype=BF16, unpacked_dtype=F32)
                    halves[half].append(v.astype(BF16))
            xe = jnp.concatenate(halves[0] + halves[1], axis=1)
            y = None
            for c in range(0, d_ff, FF_CHUNK):
                a = _dot(xe, wg_ref[0, 0, :, c:c + FF_CHUNK].astype(BF16))
                u = _dot(xe, wu_ref[0, 0, :, c:c + FF_CHUNK].astype(BF16))
                yc = _dot((_silu(a) * u).astype(BF16), wd_ref[0, 0, c:c + FF_CHUNK, :].astype(BF16))
                y = yc if y is None else y + yc
            for c in range(FF_ROWS // LANES):
                blk = r0 // LANES + c
                g_row = gate_ref[pl.ds(e * (cap // LANES) + blk, 1), :]
                g_col = jnp.sum(jnp.where(eye, g_row, 0.0), axis=1, keepdims=True)
                tot = y[c * LANES:(c + 1) * LANES, :] * g_col
                for s in range(Y_ROWS):
                    y_ref[0, pl.ds(blk * LANES * Y_ROWS + s, LANES, stride=Y_ROWS), :] = tot[:, s * LANES:(s + 1) * LANES]


def _ffn(streams, layer, w_gate, w_up, w_down):
    caps = [xp.shape[0] // (N_EXPERTS * HP_ROWS) for xp, _ in streams]
    in_specs, args = [], []
    for (xp, gate), cap in zip(streams, caps):
        in_specs += [pl.BlockSpec((cap * HP_ROWS, LANES), lambda e: (e, 0)),
                     pl.BlockSpec(gate.shape, lambda e: (0, 0))]
        args += [xp, gate]
    w_spec = pl.BlockSpec((1, 1) + w_gate.shape[2:], lambda e: (layer, e, 0, 0))
    return pl.pallas_call(
        functools.partial(_ffn_kernel, n_streams=len(streams)),
        out_shape=[jax.ShapeDtypeStruct((N_EXPERTS, cap * Y_ROWS, LANES), F32) for cap in caps],
        grid=(N_EXPERTS,),
        in_specs=in_specs + [w_spec, w_spec, w_spec],
        out_specs=[pl.BlockSpec((1, cap * Y_ROWS, LANES), lambda e: (e, 0, 0)) for cap in caps],
        compiler_params=_cparams(1),
        name="ffn",
    )(*args, w_gate, w_up, w_down)


SCATTER_UNROLL = 16
SCATTER_GROUPS = 2


def _combine_kernel(dst_ref, y_ref, x1_ref, mod_ref, o_ref, acc_ref, *, n_scatter):
    s = pl.program_id(0)
    tf = o_ref.shape[0]
    step_rows = y_ref.shape[0] // Y_ROWS

    @pl.when(s == 0)
    def _():
        acc_ref[...] = jnp.zeros_like(acc_ref)

    @pl.when(s < n_scatter)
    def _():
        def scatter(g, carry):
            for sub in range(SCATTER_GROUPS):
                first = (g * SCATTER_GROUPS + sub) * SCATTER_UNROLL
                base = s * step_rows + first
                rows, sums = [], []
                for r in range(SCATTER_UNROLL):
                    dst = pl.ds(pl.multiple_of(dst_ref[base + r], Y_ROWS), Y_ROWS)
                    src = pl.ds(pl.multiple_of((first + r) * Y_ROWS, Y_ROWS), Y_ROWS)
                    rows.append(dst)
                    sums.append(acc_ref[dst, :] + y_ref[src, :])
                for dst, v in zip(rows, sums):
                    acc_ref[dst, :] = v
            return carry

        lax.fori_loop(0, step_rows // (SCATTER_UNROLL * SCATTER_GROUPS), scatter, 0)

    @pl.when(s >= n_scatter)
    def _():
        g2 = mod_ref[0, 0][5:6]
        tile = acc_ref.at[pl.ds(pl.multiple_of((s - n_scatter) * tf * Y_ROWS, tf * Y_ROWS), tf * Y_ROWS), :]
        for c in range(Y_ROWS):
            sl = slice(c * LANES, (c + 1) * LANES)
            o_ref[:, sl] = x1_ref[:, sl] + g2[:, sl] * tile[pl.ds(c, tf, stride=Y_ROWS), :]


def _combine_tiles(n_tok):
    row_bytes = D_MODEL * 4
    free_rows = (VMEM_LIMIT - n_tok * row_bytes) * 4 // 5 // row_bytes
    scale = max(k for k in (1, 2, 4) if 2 * 512 * k + 4 * 256 * k <= free_rows)
    return 256 * scale, 512 * scale


def _combine(dst, y, x1, mods, layer, row_of_token):
    n_tok = x1.shape[0]
    tf, step_rows = _combine_tiles(n_tok)
    n_scatter = dst.shape[0] // step_rows
    assert (dst.shape[0] // N_EXPERTS) % SCATTER_UNROLL == 0 and dst.shape[0] % step_rows == 0
    tile_of = lambda s: jnp.maximum(s - n_scatter, 0)
    row_of_tile = lambda t: row_of_token(t * tf)
    return pl.pallas_call(
        functools.partial(_combine_kernel, n_scatter=n_scatter),
        out_shape=jax.ShapeDtypeStruct((n_tok, D_MODEL), F32),
        grid_spec=pltpu.PrefetchScalarGridSpec(
            num_scalar_prefetch=1,
            grid=(n_scatter + n_tok // tf,),
            in_specs=[pl.BlockSpec((step_rows * Y_ROWS, LANES), lambda s, *_: (jnp.minimum(s, n_scatter - 1), 0)),
                      pl.BlockSpec((tf, D_MODEL), lambda s, *_: (tile_of(s), 0)),
                      pl.BlockSpec((1, 1, N_MOD, D_MODEL), lambda s, *_: (layer, row_of_tile(tile_of(s)), 0, 0))],
            out_specs=pl.BlockSpec((tf, D_MODEL), lambda s, *_: (tile_of(s), 0)),
            scratch_shapes=[pltpu.VMEM((n_tok * Y_ROWS, LANES), F32)]),
        compiler_params=_cparams(1),
        name="combine",
    )(dst, y.reshape(-1, LANES), x1, mods)


TOKEN_TILE = 1024
Q_TILE = 1024


def _moe(fronts, rows_of_token, mods, layer, w_gate, w_up, w_down):
    routed, dsts = [], []
    for x1, hp, aff in fronts:
        cap = CAPACITY_FACTOR * x1.shape[0] // N_EXPERTS
        src, dst, gate = _select(aff, cap)
        routed.append((_gather(src, hp), gate))
        dsts.append(dst)
    ys = _ffn(routed, layer, w_gate, w_up, w_down)
    return [_combine(dst, y, x1, mods, layer, row_of_token)
            for dst, y, (x1, _, _), row_of_token in zip(dsts, ys, fronts, rows_of_token)]


def kernel(x_prompt, x_sample, cache_k, cache_v, c, c_ctx, norm1, norm2, w_ada, b_ada, w_qkv, q_norm, k_norm,
           w_o, w_pool, pool_scale, w_router, w_e_gate, w_e_up, w_e_down):
    batch, seq, _ = x_prompt.shape
    dec_batch, dec_seq, _ = x_sample.shape
    depth = w_ada.shape[0]
    kvw = N_KV * HEAD_DIM

    cvec = jnp.zeros((SUBLANES, D_MODEL), F32).at[0].set(c_ctx).at[1:1 + dec_batch].set(c)
    mods = _ada(cvec, w_ada, b_ada)
    ctx = x_prompt.reshape(batch * seq, D_MODEL)
    lat = x_sample.reshape(dec_batch * dec_seq, D_MODEL)
    ctx_row = lambda i: 0
    lat_row = lambda tile: (lambda i: 1 + (i * tile) // dec_seq)
    ctx_tile = TOKEN_TILE if TOKEN_TILE % seq == 0 else min(TOKEN_TILE, seq)
    rope_tabs = _rope_tables(dec_seq)
    new_k = new_v = None

    for layer in range(depth):
        j = layer // 2
        n1 = norm1[layer][None]
        n2 = norm2[layer][None]
        wr = w_router
        if layer % 2 == 0:
            qn = q_norm[j][None]
            kn = k_norm[j][None]
            qc, kc, vc, new_k, new_v = _qkv(ctx, mods, layer, ctx_row, n1, w_qkv[j], qn, kn, None, True, TOKEN_TILE)
            oc = _attn_ctx(qc, kc, vc, seq, TOKEN_TILE)
            ql, kl, vl = _qkv(lat, mods, layer, lat_row(TOKEN_TILE), n1, w_qkv[j], qn, kn, rope_tabs, False,
                              TOKEN_TILE)
            past_k = cache_k[:, j].reshape(dec_batch, -1, kvw).astype(BF16)
            past_v = cache_v[:, j].reshape(dec_batch, -1, kvw).astype(BF16)
            ol = _attn_lat(ql, kl, vl, past_k, past_v, dec_seq, Q_TILE)
            ctx1 = _post(oc, ctx, mods, layer, ctx_row, n2, w_o[j], wr, TOKEN_TILE)
            lat1 = _post(ol, lat, mods, layer, lat_row(TOKEN_TILE), n2, w_o[j], wr, TOKEN_TILE)
        else:
            ps = pool_scale[j][None]
            ctx1 = _pool(ctx, mods, layer, ctx_row, n1, n2, w_pool[j], ps, wr, seq, ctx_tile)
            lat1 = _pool(lat, mods, layer, lat_row(TOKEN_TILE), n1, n2, w_pool[j], ps, wr, dec_seq, TOKEN_TILE)
        ctx, lat = _moe([ctx1, lat1], [ctx_row, lat_row(1)], mods, layer, w_e_gate, w_e_up, w_e_down)

    new_cache_k = new_k.reshape(batch, 1, seq, N_KV, HEAD_DIM)
    new_cache_v = new_v.reshape(batch, 1, seq, N_KV, HEAD_DIM)
    return (ctx.reshape(batch, seq, D_MODEL), lat.reshape(dec_batch, dec_seq, D_MODEL), new_cache_k, new_cache_v)
```

```python
import functools

import jax
import jax.numpy as jnp
import numpy as np
from jax import lax
from jax.experimental import pallas as pl
from jax.experimental.pallas import tpu as pltpu

F32 = jnp.float32
BF16 = jnp.bfloat16
I32 = jnp.int32
U32 = jnp.uint32

D_MODEL = 1024
HEAD_DIM = 128
N_HEADS = 8
N_KV = 2
Q_PER_KV = N_HEADS // N_KV
QKV_DIM = (N_HEADS + 2 * N_KV) * HEAD_DIM
GRID_W = 64
ROPE_THETA = 10000.0
POOL_WINDOWS = (2, 4, 8, 16)
POOL_GROUP = D_MODEL // len(POOL_WINDOWS)
POOL_HALO = max(POOL_WINDOWS) // 2
N_EXPERTS = 16
CAPACITY_FACTOR = 2
N_MOD = 6
EPS = 1e-6
LOG2_E = 1.4426950408889634

LANES = 128
SUBLANES = 8
HALF = D_MODEL // 2
HP_ROWS = HALF // LANES
Y_ROWS = D_MODEL // LANES
VMEM_LIMIT = 56 * 1024 * 1024


def _cparams(n_axes, vmem=VMEM_LIMIT):
    return pltpu.CompilerParams(dimension_semantics=("arbitrary",) * n_axes, vmem_limit_bytes=vmem)


def _silu(x):
    return x / (1.0 + jnp.exp(-x))


def _rms(x, gain):
    return x * lax.rsqrt(jnp.mean(x * x, axis=-1, keepdims=True) + EPS) * gain


def _norm_mod(x, gain, shift, scale):
    return _rms(x, gain * (1.0 + scale)) + shift


def _dot(a, b):
    return jnp.dot(a, b, preferred_element_type=F32)


def _dot_nt(a, b):
    return lax.dot_general(a, b, (((1,), (1,)), ((), ())), preferred_element_type=F32)


def _ada_kernel(c_ref, w_ref, b_ref, o_ref):
    s = _silu(c_ref[...]).astype(BF16)
    o_ref[0] = _dot(s, w_ref[0].astype(BF16)) + b_ref[0]


def _ada(cvec, w_ada, b_ada):
    depth = w_ada.shape[0]
    out = pl.pallas_call(
        _ada_kernel,
        out_shape=jax.ShapeDtypeStruct((depth, SUBLANES, N_MOD * D_MODEL), F32),
        grid=(depth, N_MOD),
        in_specs=[
            pl.BlockSpec((SUBLANES, D_MODEL), lambda i, j: (0, 0)),
            pl.BlockSpec((1, D_MODEL, D_MODEL), lambda i, j: (i, 0, j)),
            pl.BlockSpec((1, 1, D_MODEL), lambda i, j: (i, 0, j)),
        ],
        out_specs=pl.BlockSpec((1, SUBLANES, D_MODEL), lambda i, j: (i, 0, j)),
        compiler_params=_cparams(2),
        name="ada",
    )(cvec, w_ada, b_ada.reshape(depth, 1, N_MOD * D_MODEL))
    return out.reshape(depth, SUBLANES, N_MOD, D_MODEL)


def _mod_spec(layer, row_of_tile):
    return pl.BlockSpec((1, 1, N_MOD, D_MODEL), lambda i, *_: (layer, row_of_tile(i), 0, 0))


def _qkv_kernel(*refs, rope, cache_out):
    x_ref, mod_ref, n1_ref, w_ref, qn_ref, kn_ref = refs[:6]
    refs = refs[6:]
    if rope:
        cos_ref, sin_ref = refs[:2]
        refs = refs[2:]
    q_ref, k_ref, v_ref = refs[:3]
    refs = refs[3:]
    if cache_out:
        kc_ref, vc_ref = refs[:2]
        refs = refs[2:]
    (wb_ref,) = refs

    qk_w = (N_HEADS + N_KV) * HEAD_DIM
    quarter = HEAD_DIM // 4

    def partner(a):
        width = a.shape[1]
        first = (lax.broadcasted_iota(I32, a.shape, 1) & quarter) == 0
        return jnp.where(first, pltpu.roll(a, width - quarter, 1), pltpu.roll(a, quarter, 1))

    @pl.when(pl.program_id(0) == 0)
    def _():
        w = w_ref[...]
        wb_ref[:, 0:QKV_DIM] = w.astype(BF16)
        if rope:
            wb_ref[:, QKV_DIM:] = partner(w[:, 0:qk_w]).astype(BF16)

    m = mod_ref[0, 0]
    h = _norm_mod(x_ref[...], n1_ref[...], m[0:1], m[1:2])
    qkv = _dot(h.astype(BF16), wb_ref[...])
    scale = HEAD_DIM ** -0.5 * LOG2_E
    if rope:
        gains = {True: qn_ref[...], False: kn_ref[...]}
        cos_g = {key: cos_ref[...] * g for key, g in gains.items()}
        sin_g = {key: sin_ref[...] * partner(g) for key, g in gains.items()}
    for hh in range(N_HEADS + N_KV):
        sl = slice(hh * HEAD_DIM, (hh + 1) * HEAD_DIM)
        is_q = hh < N_HEADS
        if rope:
            raw = qkv[:, sl]
            norm = lax.rsqrt(jnp.mean(raw * raw, axis=-1, keepdims=True) + EPS)
            xh = (raw * cos_g[is_q] + qkv[:, QKV_DIM + hh * HEAD_DIM:QKV_DIM + (hh + 1) * HEAD_DIM] * sin_g[is_q]) * norm
        else:
            xh = _rms(qkv[:, sl], qn_ref[...] if is_q else kn_ref[...])
        if cache_out and hh >= N_HEADS:
            kc_ref[:, hh - N_HEADS, :] = xh
        if hh < N_HEADS:
            q_ref[:, sl] = (xh * scale).astype(BF16)
        else:
            k_ref[:, (hh - N_HEADS) * HEAD_DIM:(hh - N_HEADS + 1) * HEAD_DIM] = xh.astype(BF16)
    v = qkv[:, qk_w:QKV_DIM]
    v_ref[...] = v.astype(BF16)
    if cache_out:
        for g in range(N_KV):
            vc_ref[:, g, :] = v[:, g * HEAD_DIM:(g + 1) * HEAD_DIM]


def _qkv(x, mods, layer, row_of_tile, n1, w_qkv, qn, kn, rope_tabs, cache_out, tm):
    n_tok = x.shape[0]
    kvw = N_KV * HEAD_DIM
    rope = rope_tabs is not None
    in_specs = [
        pl.BlockSpec((tm, D_MODEL), lambda i: (i, 0)),
        _mod_spec(layer, row_of_tile),
        pl.BlockSpec((1, D_MODEL), lambda i: (0, 0)),
        pl.BlockSpec((D_MODEL, QKV_DIM), lambda i: (0, 0)),
        pl.BlockSpec((1, HEAD_DIM), lambda i: (0, 0)),
        pl.BlockSpec((1, HEAD_DIM), lambda i: (0, 0)),
    ]
    args = [x, mods, n1, w_qkv, qn, kn]
    if rope:
        seq_tiles = rope_tabs[0].shape[0] // tm
        in_specs += [pl.BlockSpec((tm, HEAD_DIM), lambda i: (i % seq_tiles, 0))] * 2
        args += list(rope_tabs)
    out_shape = [jax.ShapeDtypeStruct((n_tok, D_MODEL), BF16),
                 jax.ShapeDtypeStruct((n_tok, kvw), BF16),
                 jax.ShapeDtypeStruct((n_tok, kvw), BF16)]
    out_specs = [pl.BlockSpec((tm, D_MODEL), lambda i: (i, 0)),
                 pl.BlockSpec((tm, kvw), lambda i: (i, 0)),
                 pl.BlockSpec((tm, kvw), lambda i: (i, 0))]
    if cache_out:
        out_shape += [jax.ShapeDtypeStruct((n_tok, N_KV, HEAD_DIM), F32)] * 2
        out_specs += [pl.BlockSpec((tm, N_KV, HEAD_DIM), lambda i: (i, 0, 0))] * 2
    return pl.pallas_call(
        functools.partial(_qkv_kernel, rope=rope, cache_out=cache_out),
        out_shape=out_shape,
        grid=(n_tok // tm,),
        in_specs=in_specs,
        out_specs=out_specs,
        scratch_shapes=[pltpu.VMEM((D_MODEL, QKV_DIM + ((N_HEADS + N_KV) * HEAD_DIM if rope else 0)), BF16)],
        compiler_params=_cparams(1),
        name="qkv_rope" if rope else "qkv",
    )(*args)


def _rope_tables(seq_len):
    half = HEAD_DIM // 2
    n = half // 2
    inv_freq = ROPE_THETA ** (-np.arange(n, dtype=np.float64) / n)
    rows = seq_len // GRID_W
    row = np.repeat(np.arange(rows), GRID_W).astype(np.float64)
    col = np.tile(np.arange(GRID_W), rows).astype(np.float64)
    ang_r = row[:, None] * inv_freq[None, :]
    ang_c = col[:, None] * inv_freq[None, :]
    cos = np.concatenate([np.cos(ang_r)] * 2 + [np.cos(ang_c)] * 2, axis=-1)
    sin = np.concatenate([-np.sin(ang_r), np.sin(ang_r), -np.sin(ang_c), np.sin(ang_c)], axis=-1)
    return jnp.asarray(cos, F32), jnp.asarray(sin, F32)


def _with_ones(v):
    return jnp.concatenate([v, jnp.ones_like(v)], axis=1)


def _gqa_attention(q, segments):
    rows = q.shape[0]
    qs = jnp.concatenate([q[:, h * HEAD_DIM:(h + 1) * HEAD_DIM] for h in range(Q_PER_KV)], axis=0)
    m = acc = None
    for k, v in segments:
        s = _dot_nt(qs, k)
        seg_max = jnp.max(s, axis=-1, keepdims=True)
        m_new = seg_max if m is None else jnp.maximum(m, seg_max)
        pv = _dot(jnp.exp2(s - m_new).astype(BF16), v)
        acc = pv if m is None else acc * jnp.exp2(m - m_new) + pv
        m = m_new
    o = (acc[:, :HEAD_DIM] / acc[:, HEAD_DIM:]).astype(BF16)
    return jnp.concatenate([o[h * rows:(h + 1) * rows] for h in range(Q_PER_KV)], axis=1)


def _attn_ctx_kernel(q_ref, k_ref, v_ref, o_ref, *, seq):
    gw = Q_PER_KV * HEAD_DIM
    for r in range(0, q_ref.shape[0], seq):
        for g in range(N_KV):
            kg = k_ref[r:r + seq, g * HEAD_DIM:(g + 1) * HEAD_DIM]
            vg = _with_ones(v_ref[r:r + seq, g * HEAD_DIM:(g + 1) * HEAD_DIM])
            o_ref[r:r + seq, g * gw:(g + 1) * gw] = _gqa_attention(q_ref[r:r + seq, g * gw:(g + 1) * gw], [(kg, vg)])


def _attn_ctx(q, k, v, seq, tm):
    n_tok = q.shape[0]
    kvw = N_KV * HEAD_DIM
    return pl.pallas_call(
        functools.partial(_attn_ctx_kernel, seq=seq),
        out_shape=jax.ShapeDtypeStruct((n_tok, D_MODEL), BF16),
        grid=(n_tok // tm,),
        in_specs=[pl.BlockSpec((tm, D_MODEL), lambda b: (b, 0)),
                  pl.BlockSpec((tm, kvw), lambda b: (b, 0)),
                  pl.BlockSpec((tm, kvw), lambda b: (b, 0))],
        out_specs=pl.BlockSpec((tm, D_MODEL), lambda b: (b, 0)),
        compiler_params=_cparams(1),
        name="attn_ctx",
    )(q, k, v)


KEY_CHUNK = 256


def _attn_lat_kernel(q_ref, k_ref, v_ref, kc_ref, vc_ref, o_ref):
    segments = [(k_ref[c:c + KEY_CHUNK], _with_ones(v_ref[c:c + KEY_CHUNK]))
                for c in range(0, k_ref.shape[0], KEY_CHUNK)]
    segments.append((kc_ref[0], _with_ones(vc_ref[0])))
    o_ref[...] = _gqa_attention(q_ref[...], segments)


def _attn_lat(q, k, v, kc, vc, seq, tq):
    n_tok = q.shape[0]
    batch = n_tok // seq
    past = kc.shape[1]
    qt = seq // tq
    gw = Q_PER_KV * HEAD_DIM
    return pl.pallas_call(
        _attn_lat_kernel,
        out_shape=jax.ShapeDtypeStruct((n_tok, D_MODEL), BF16),
        grid=(batch, N_KV, qt),
        in_specs=[pl.BlockSpec((tq, gw), lambda b, g, i: (b * qt + i, g)),
                  pl.BlockSpec((seq, HEAD_DIM), lambda b, g, i: (b, g)),
                  pl.BlockSpec((seq, HEAD_DIM), lambda b, g, i: (b, g)),
                  pl.BlockSpec((1, past, HEAD_DIM), lambda b, g, i: (b, 0, g)),
                  pl.BlockSpec((1, past, HEAD_DIM), lambda b, g, i: (b, 0, g))],
        out_specs=pl.BlockSpec((tq, gw), lambda b, g, i: (b * qt + i, g)),
        compiler_params=_cparams(3),
        name="attn_lat",
    )(q, k, v, kc, vc)


FRONT_ROWS = 256


def _router_weights(wr):
    return jnp.concatenate([wr, jnp.zeros((wr.shape[0], LANES - N_EXPERTS), wr.dtype)], axis=1).astype(BF16)


def _router_softmax(hb, wr):
    logits = _dot(hb, wr)
    lane = lax.broadcasted_iota(I32, logits.shape, 1)
    logits = jnp.where(lane < N_EXPERTS, logits, -1e30)
    ex = jnp.exp(logits - jnp.max(logits, axis=-1, keepdims=True))
    return ex / jnp.sum(ex, axis=-1, keepdims=True)


def _moe_front(x1, r0, m, n2, wr, x1_ref, hp_ref, aff_ref):
    n = x1.shape[0]
    x1_ref[r0:r0 + n, :] = x1
    h2 = _norm_mod(x1, n2, m[3:4], m[4:5])
    aff = _router_softmax(h2.astype(BF16), wr)
    packed = pltpu.pack_elementwise([h2[:, :HALF], h2[:, HALF:]], packed_dtype=BF16)
    for s in range(HP_ROWS):
        hp_ref[pl.ds(r0 * HP_ROWS + s, n, stride=HP_ROWS), :] = packed[:, s * LANES:(s + 1) * LANES]
    aff_t = aff.T
    for b in range(n // LANES):
        out_row = (r0 // LANES + b) * N_EXPERTS
        aff_ref[out_row:out_row + N_EXPERTS, :] = aff_t[0:N_EXPERTS, b * LANES:(b + 1) * LANES]


def _front_out(n_tok, tm):
    shapes = [jax.ShapeDtypeStruct((n_tok, D_MODEL), F32),
              jax.ShapeDtypeStruct((n_tok * HP_ROWS, LANES), U32),
              jax.ShapeDtypeStruct((n_tok // LANES * N_EXPERTS, LANES), F32)]
    specs = [pl.BlockSpec((tm, D_MODEL), lambda i: (i, 0)),
             pl.BlockSpec((tm * HP_ROWS, LANES), lambda i: (i, 0)),
             pl.BlockSpec((tm // LANES * N_EXPERTS, LANES), lambda i: (i, 0))]
    return shapes, specs


RING = 3


def _post_kernel(o_hbm, x_hbm, mod_ref, n2_ref, wo_ref, wr_ref, x1_ref, hp_ref, aff_ref,
                 wb_ref, o_buf, x_buf, sem):
    i = pl.program_id(0)
    n_steps = pl.num_programs(0)
    tm = x1_ref.shape[0]

    def tile_copies(step, slot):
        rows = pl.ds(pl.multiple_of(step * tm, tm), tm)
        return (pltpu.make_async_copy(o_hbm.at[rows, :], o_buf.at[slot], sem.at[0, slot]),
                pltpu.make_async_copy(x_hbm.at[rows, :], x_buf.at[slot], sem.at[1, slot]))

    def start(step):
        for copy in tile_copies(step, step % RING):
            copy.start()

    @pl.when(i == 0)
    def _():
        for ahead in range(RING - 1):
            @pl.when(ahead < n_steps)
            def _():
                start(ahead)
        wb_ref[...] = wo_ref[...].astype(BF16)

    @pl.when(i + RING - 1 < n_steps)
    def _():
        start(i + RING - 1)

    slot = i % RING
    for copy in tile_copies(i, slot):
        copy.wait()

    m = mod_ref[0, 0]
    wr = _router_weights(wr_ref[0])
    for r0 in range(0, tm, FRONT_ROWS):
        rows = slice(r0, r0 + FRONT_ROWS)
        x1 = x_buf[slot, rows, :] + m[2:3] * _dot(o_buf[slot, rows, :], wb_ref[...])
        _moe_front(x1, r0, m, n2_ref[...], wr, x1_ref, hp_ref, aff_ref)


def _post(o, x, mods, layer, row_of_tile, n2, w_o, wr, tm):
    n_tok = x.shape[0]
    shapes, specs = _front_out(n_tok, tm)
    return pl.pallas_call(
        _post_kernel,
        out_shape=shapes,
        grid=(n_tok // tm,),
        in_specs=[pl.BlockSpec(memory_space=pl.ANY),
                  pl.BlockSpec(memory_space=pl.ANY),
                  _mod_spec(layer, row_of_tile),
                  pl.BlockSpec((1, D_MODEL), lambda i: (0, 0)),
                  pl.BlockSpec((D_MODEL, D_MODEL), lambda i: (0, 0)),
                  pl.BlockSpec((1, D_MODEL, N_EXPERTS), lambda i: (layer, 0, 0))],
        out_specs=specs,
        scratch_shapes=[pltpu.VMEM((D_MODEL, D_MODEL), BF16),
                        pltpu.VMEM((RING, tm, D_MODEL), o.dtype),
                        pltpu.VMEM((RING, tm, D_MODEL), x.dtype),
                        pltpu.SemaphoreType.DMA((2, RING))],
        compiler_params=_cparams(1),
        name="post",
    )(o, x, mods, n2, w_o, wr)


def _pool_kernel(x_ref, xp_ref, xn_ref, mod_ref, n1_ref, n2_ref, wp_ref, ps_ref, wr_ref,
                 x1_ref, hp_ref, aff_ref, *, seq):
    tm = x_ref.shape[0]
    i = pl.program_id(0)
    m = mod_ref[0, 0]
    x = x_ref[...]

    def norm_mod(v):
        return _norm_mod(v, n1_ref[...], m[0:1], m[1:2])

    h = norm_mod(x)
    if tm <= seq:
        prev_ok = ((i * tm) % seq != 0).astype(F32)
        next_ok = (((i + 1) * tm) % seq != 0).astype(F32)
        pieces = [(norm_mod(xp_ref[...]) * prev_ok, h, norm_mod(xn_ref[...]) * next_ok)]
        t0 = (i * tm) % seq
    else:
        zeros = jnp.zeros((POOL_HALO, D_MODEL), F32)
        pieces = [(zeros, h[r:r + seq], zeros) for r in range(0, tm, seq)]
        t0 = 0
    n = min(tm, seq)
    rows = n + 2 * POOL_HALO
    t = t0 + lax.broadcasted_iota(I32, (n, 1), 0)
    deltas = [[] for _ in POOL_WINDOWS]
    for piece in pieces:
        hz = jnp.concatenate(piece, axis=0)
        for g, w in enumerate(POOL_WINDOWS):
            sl = slice(g * POOL_GROUP, (g + 1) * POOL_GROUP)
            f = hz[:, sl]
            step = 1
            while step < w:
                f = f + pltpu.roll(f, rows - step, 0)
                step *= 2
            win = pltpu.roll(f, w // 2, 0)[POOL_HALO:POOL_HALO + n]
            means = []
            for edge in (slice(0, POOL_HALO), slice(n - POOL_HALO, n)):
                cnt = jnp.minimum(t[edge] + w // 2, seq) - jnp.maximum(t[edge] - w // 2, 0)
                means.append(win[edge] / cnt.astype(F32))
            mean = jnp.concatenate([means[0], win[POOL_HALO:n - POOL_HALO] * (1.0 / w), means[1]], axis=0)
            deltas[g].append((mean - piece[1][:, sl]).astype(BF16))
    ys = [_dot(jnp.concatenate(d, axis=0), wp_ref[g].astype(BF16)) for g, d in enumerate(deltas)]
    x1 = x + (m[2:3] * ps_ref[...]) * jnp.concatenate(ys, axis=-1)
    wr = _router_weights(wr_ref[0])
    for r0 in range(0, tm, FRONT_ROWS):
        _moe_front(x1[r0:r0 + FRONT_ROWS], r0, m, n2_ref[...], wr, x1_ref, hp_ref, aff_ref)


def _pool(x, mods, layer, row_of_tile, n1, n2, w_pool, pool_scale, wr, seq, tm):
    n_tok = x.shape[0]
    hb = tm // POOL_HALO
    last = n_tok // POOL_HALO - 1
    shapes, specs = _front_out(n_tok, tm)
    return pl.pallas_call(
        functools.partial(_pool_kernel, seq=seq),
        out_shape=shapes,
        grid=(n_tok // tm,),
        in_specs=[pl.BlockSpec((tm, D_MODEL), lambda i: (i, 0)),
                  pl.BlockSpec((POOL_HALO, D_MODEL), lambda i: (jnp.maximum(i * hb - 1, 0), 0)),
                  pl.BlockSpec((POOL_HALO, D_MODEL), lambda i: (jnp.minimum((i + 1) * hb, last), 0)),
                  _mod_spec(layer, row_of_tile),
                  pl.BlockSpec((1, D_MODEL), lambda i: (0, 0)),
                  pl.BlockSpec((1, D_MODEL), lambda i: (0, 0)),
                  pl.BlockSpec(w_pool.shape, lambda i: (0, 0, 0)),
                  pl.BlockSpec((1, D_MODEL), lambda i: (0, 0)),
                  pl.BlockSpec((1, D_MODEL, N_EXPERTS), lambda i: (layer, 0, 0))],
        out_specs=specs,
        compiler_params=_cparams(1),
        name="pool",
    )(x, x, x, mods, n1, n2, w_pool, pool_scale, wr)


DIST_BIT = 8


def _select_kernel(aff_ref, src_ref, dst_ref, gate_ref, *, cap):
    nb = aff_ref.shape[0] // N_EXPERTS
    rows = nb * N_EXPERTS
    shape3 = (nb, N_EXPERTS, LANES)
    aff = aff_ref[...].reshape(shape3)

    def count(mask):
        return jnp.sum(jnp.sum(mask.astype(F32), axis=0), axis=-1, keepdims=True)

    def search(i, t):
        shift = 28 - 2 * i
        best = t
        for digit in (1, 2, 3):
            cand = t | (jnp.int32(digit) << shift)
            best = jnp.where(count(aff >= pltpu.bitcast(cand, F32)[None]) >= cap, cand, best)
        return best

    thr = pltpu.bitcast(lax.fori_loop(0, 15, search, jnp.zeros((N_EXPERTS, LANES), I32)), F32)
    gt = aff > thr[None]
    eq = aff == thr[None]
    need = cap - count(gt)

    kk = lax.broadcasted_iota(I32, (LANES, LANES), 0)
    nn = lax.broadcasted_iota(I32, (LANES, LANES), 1)
    upper = (kk <= nn).astype(BF16)
    ones = jnp.ones((LANES, LANES), BF16)

    def prefix(mask):
        m2 = mask.astype(F32).astype(BF16).reshape(rows, LANES)
        p = _dot(m2, upper).reshape(shape3)
        s = _dot(m2, ones).reshape(shape3)
        offs = []
        run = jnp.zeros((N_EXPERTS, LANES), F32)
        for b in range(nb):
            offs.append(run)
            run = run + s[b]
        return p, jnp.stack(offs, axis=0), s

    pe, oe, _ = prefix(eq)
    sel = gt | (eq & ((pe + oe) <= need[None]))
    ps, os_, ss = prefix(sel)

    lane = lax.broadcasted_iota(I32, (rows, LANES), 1)
    sel2 = sel.reshape(rows, LANES)
    dist = lane - (ps.reshape(rows, LANES).astype(I32) - 1)
    word = jnp.where(sel2, lane | (dist << DIST_BIT), 0)
    for k in range(7):
        bit = 1 << (DIST_BIT + k)
        arriving = pltpu.roll(word, LANES - (1 << k), 1)
        word = jnp.where((arriving & bit) != 0, arriving, jnp.where((word & bit) != 0, 0, word))
    val = word & (LANES - 1)
    local = val.reshape(shape3) + lax.broadcasted_iota(I32, shape3, 0) * LANES
    local_gate = jnp.take_along_axis(aff.reshape(rows, LANES), val, axis=1).reshape(shape3)

    first = os_.astype(I32)
    chosen = pltpu.bitcast(ss.astype(I32), U32)
    lane_e = lax.broadcasted_iota(I32, (N_EXPERTS, LANES), 1)
    n_chunks = cap // LANES
    acc = [jnp.zeros((N_EXPERTS, LANES), I32)] * n_chunks
    gacc = [jnp.zeros((N_EXPERTS, LANES), F32)] * n_chunks
    for b in range(nb):
        within = lane_e - first[b]
        phase = within & (LANES - 1)
        tokens = jnp.take_along_axis(local[b], phase, axis=1)
        gates = jnp.take_along_axis(local_gate[b], phase, axis=1)
        for jc in range(n_chunks):
            inside = pltpu.bitcast(within + jc * LANES, U32) < chosen[b]
            acc[jc] = jnp.where(inside, tokens, acc[jc])
            gacc[jc] = jnp.where(inside, gates, gacc[jc])
    for jc in range(n_chunks):
        out_rows = pl.ds(jc, N_EXPERTS, stride=n_chunks)
        src_ref[out_rows, :] = acc[jc] * HP_ROWS
        dst_ref[out_rows, :] = acc[jc] * Y_ROWS
        gate_ref[out_rows, :] = gacc[jc]


def _select(aff, cap):
    rows = aff.shape[0]
    out_rows = N_EXPERTS * cap // LANES
    out_spec = pl.BlockSpec((out_rows, LANES), lambda i: (0, 0))
    src, dst, gate = pl.pallas_call(
        functools.partial(_select_kernel, cap=cap),
        out_shape=[jax.ShapeDtypeStruct((out_rows, LANES), I32),
                   jax.ShapeDtypeStruct((out_rows, LANES), I32),
                   jax.ShapeDtypeStruct((out_rows, LANES), F32)],
        grid=(1,),
        in_specs=[pl.BlockSpec((rows, LANES), lambda i: (0, 0))],
        out_specs=[out_spec, out_spec, out_spec],
        compiler_params=_cparams(1),
        name="select",
    )(aff)
    return src.reshape(-1), dst.reshape(-1), gate


GATHER_UNROLL = 32
GATHER_STEP_ROWS = 2048


def _gather_kernel(src_ref, hp_ref, o_ref):
    i = pl.program_id(0)
    n = o_ref.shape[0] // HP_ROWS

    def gather(g, carry):
        base = i * n + g * GATHER_UNROLL
        for r in range(GATHER_UNROLL):
            src = pl.multiple_of(src_ref[base + r], HP_ROWS)
            dst = pl.multiple_of((g * GATHER_UNROLL + r) * HP_ROWS, HP_ROWS)
            o_ref[pl.ds(dst, HP_ROWS), :] = hp_ref[pl.ds(src, HP_ROWS), :]
        return carry

    lax.fori_loop(0, n // GATHER_UNROLL, gather, 0)


def _gather(src, hp):
    n = src.shape[0]
    return pl.pallas_call(
        _gather_kernel,
        out_shape=jax.ShapeDtypeStruct((n * HP_ROWS, LANES), U32),
        grid_spec=pltpu.PrefetchScalarGridSpec(
            num_scalar_prefetch=1,
            grid=(n // GATHER_STEP_ROWS,),
            in_specs=[pl.BlockSpec(hp.shape, lambda i, src: (0, 0), pipeline_mode=pl.Buffered(1))],
            out_specs=pl.BlockSpec((GATHER_STEP_ROWS * HP_ROWS, LANES), lambda i, src: (i, 0))),
        compiler_params=_cparams(1),
        name="gather",
    )(src, hp)


FF_CHUNK = 256


def _ffn_kernel(*refs, n_streams):
    e = pl.program_id(0)
    ins = refs[:2 * n_streams]
    wg_ref, wu_ref, wd_ref = refs[2 * n_streams:2 * n_streams + 3]
    outs = refs[2 * n_streams + 3:]
    d_ff = wg_ref.shape[3]
    eye = lax.broadcasted_iota(I32, (LANES, LANES), 0) == lax.broadcasted_iota(I32, (LANES, LANES), 1)
    for i in range(n_streams):
        xp_ref, gate_ref, y_ref = ins[2 * i], ins[2 * i + 1], outs[i]
        cap = xp_ref.shape[0] // HP_ROWS
        halves = [[], []]
        for s in range(HP_ROWS):
            w = xp_ref[pl.ds(s, cap, stride=HP_ROWS), :]
            for half in range(2):
                v = pltpu.unpack_elementwise(w, index=half, packed_dtype=BF16, unpacked_dtype=F32)
                halves[half].append(v.astype(BF16))
        xe = jnp.concatenate(halves[0] + halves[1], axis=1)
        y = None
        for c in range(0, d_ff, FF_CHUNK):
            a = _dot(xe, wg_ref[0, 0, :, c:c + FF_CHUNK].astype(BF16))
            u = _dot(xe, wu_ref[0, 0, :, c:c + FF_CHUNK].astype(BF16))
            yc = _dot((_silu(a) * u).astype(BF16), wd_ref[0, 0, c:c + FF_CHUNK, :].astype(BF16))
            y = yc if y is None else y + yc
        for c in range(cap // LANES):
            g_row = gate_ref[pl.ds(e * (cap // LANES) + c, 1), :]
            g_col = jnp.sum(jnp.where(eye, g_row, 0.0), axis=1, keepdims=True)
            tot = y[c * LANES:(c + 1) * LANES, :] * g_col
            for s in range(Y_ROWS):
                y_ref[0, pl.ds(c * LANES * Y_ROWS + s, LANES, stride=Y_ROWS), :] = tot[:, s * LANES:(s + 1) * LANES]


def _ffn(streams, layer, w_gate, w_up, w_down):
    caps = [xp.shape[0] // (N_EXPERTS * HP_ROWS) for xp, _ in streams]
    in_specs, args = [], []
    for (xp, gate), cap in zip(streams, caps):
        in_specs += [pl.BlockSpec((cap * HP_ROWS, LANES), lambda e: (e, 0)),
                     pl.BlockSpec(gate.shape, lambda e: (0, 0))]
        args += [xp, gate]
    w_spec = pl.BlockSpec((1, 1) + w_gate.shape[2:], lambda e: (layer, e, 0, 0))
    return pl.pallas_call(
        functools.partial(_ffn_kernel, n_streams=len(streams)),
        out_shape=[jax.ShapeDtypeStruct((N_EXPERTS, cap * Y_ROWS, LANES), F32) for cap in caps],
        grid=(N_EXPERTS,),
        in_specs=in_specs + [w_spec, w_spec, w_spec],
        out_specs=[pl.BlockSpec((1, cap * Y_ROWS, LANES), lambda e: (e, 0, 0)) for cap in caps],
        compiler_params=_cparams(1),
        name="ffn",
    )(*args, w_gate, w_up, w_down)


SCATTER_UNROLL = 16
SCATTER_GROUPS = 2


def _combine_kernel(dst_ref, y_ref, x1_ref, mod_ref, o_ref, acc_ref, *, n_scatter):
    s = pl.program_id(0)
    tf = o_ref.shape[0]
    step_rows = y_ref.shape[0] // Y_ROWS

    @pl.when(s == 0)
    def _():
        acc_ref[...] = jnp.zeros_like(acc_ref)

    @pl.when(s < n_scatter)
    def _():
        def scatter(g, carry):
            for sub in range(SCATTER_GROUPS):
                first = (g * SCATTER_GROUPS + sub) * SCATTER_UNROLL
                base = s * step_rows + first
                rows, sums = [], []
                for r in range(SCATTER_UNROLL):
                    dst = pl.ds(pl.multiple_of(dst_ref[base + r], Y_ROWS), Y_ROWS)
                    src = pl.ds(pl.multiple_of((first + r) * Y_ROWS, Y_ROWS), Y_ROWS)
                    rows.append(dst)
                    sums.append(acc_ref[dst, :] + y_ref[src, :])
                for dst, v in zip(rows, sums):
                    acc_ref[dst, :] = v
            return carry

        lax.fori_loop(0, step_rows // (SCATTER_UNROLL * SCATTER_GROUPS), scatter, 0)

    @pl.when(s >= n_scatter)
    def _():
        g2 = mod_ref[0, 0][5:6]
        tile = acc_ref.at[pl.ds(pl.multiple_of((s - n_scatter) * tf * Y_ROWS, tf * Y_ROWS), tf * Y_ROWS), :]
        for c in range(Y_ROWS):
            sl = slice(c * LANES, (c + 1) * LANES)
            o_ref[:, sl] = x1_ref[:, sl] + g2[:, sl] * tile[pl.ds(c, tf, stride=Y_ROWS), :]


def _combine_tiles(n_tok):
    row_bytes = D_MODEL * 4
    free_rows = (VMEM_LIMIT - n_tok * row_bytes) * 4 // 5 // row_bytes
    scale = max(k for k in (1, 2, 4) if 2 * 512 * k + 4 * 256 * k <= free_rows)
    return 256 * scale, 512 * scale


def _combine(dst, y, x1, mods, layer, row_of_token):
    n_tok = x1.shape[0]
    tf, step_rows = _combine_tiles(n_tok)
    n_scatter = dst.shape[0] // step_rows
    assert (dst.shape[0] // N_EXPERTS) % SCATTER_UNROLL == 0 and dst.shape[0] % step_rows == 0
    tile_of = lambda s: jnp.maximum(s - n_scatter, 0)
    row_of_tile = lambda t: row_of_token(t * tf)
    return pl.pallas_call(
        functools.partial(_combine_kernel, n_scatter=n_scatter),
        out_shape=jax.ShapeDtypeStruct((n_tok, D_MODEL), F32),
        grid_spec=pltpu.PrefetchScalarGridSpec(
            num_scalar_prefetch=1,
            grid=(n_scatter + n_tok // tf,),
            in_specs=[pl.BlockSpec((step_rows * Y_ROWS, LANES), lambda s, *_: (jnp.minimum(s, n_scatter - 1), 0)),
                      pl.BlockSpec((tf, D_MODEL), lambda s, *_: (tile_of(s), 0)),
                      pl.BlockSpec((1, 1, N_MOD, D_MODEL), lambda s, *_: (layer, row_of_tile(tile_of(s)), 0, 0))],
            out_specs=pl.BlockSpec((tf, D_MODEL), lambda s, *_: (tile_of(s), 0)),
            scratch_shapes=[pltpu.VMEM((n_tok * Y_ROWS, LANES), F32)]),
        compiler_params=_cparams(1),
        name="combine",
    )(dst, y.reshape(-1, LANES), x1, mods)


TOKEN_TILE = 1024
Q_TILE = 1024


def _moe(fronts, rows_of_token, mods, layer, w_gate, w_up, w_down):
    routed, dsts = [], []
    for x1, hp, aff in fronts:
        cap = CAPACITY_FACTOR * x1.shape[0] // N_EXPERTS
        src, dst, gate = _select(aff, cap)
        routed.append((_gather(src, hp), gate))
        dsts.append(dst)
    ys = _ffn(routed, layer, w_gate, w_up, w_down)
    return [_combine(dst, y, x1, mods, layer, row_of_token)
            for dst, y, (x1, _, _), row_of_token in zip(dsts, ys, fronts, rows_of_token)]


def kernel(x_prompt, x_sample, cache_k, cache_v, c, c_ctx, norm1, norm2, w_ada, b_ada, w_qkv, q_norm, k_norm,
           w_o, w_pool, pool_scale, w_router, w_e_gate, w_e_up, w_e_down):
    batch, seq, _ = x_prompt.shape
    dec_batch, dec_seq, _ = x_sample.shape
    depth = w_ada.shape[0]
    kvw = N_KV * HEAD_DIM

    cvec = jnp.zeros((SUBLANES, D_MODEL), F32).at[0].set(c_ctx).at[1:1 + dec_batch].set(c)
    mods = _ada(cvec, w_ada, b_ada)
    ctx = x_prompt.reshape(batch * seq, D_MODEL)
    lat = x_sample.reshape(dec_batch * dec_seq, D_MODEL)
    ctx_row = lambda i: 0
    lat_row = lambda tile: (lambda i: 1 + (i * tile) // dec_seq)
    ctx_tile = TOKEN_TILE if TOKEN_TILE % seq == 0 else min(TOKEN_TILE, seq)
    rope_tabs = _rope_tables(dec_seq)
    new_k = new_v = None

    for layer in range(depth):
        j = layer // 2
        n1 = norm1[layer][None]
        n2 = norm2[layer][None]
        wr = w_router
        if layer % 2 == 0:
            qn = q_norm[j][None]
            kn = k_norm[j][None]
            qc, kc, vc, new_k, new_v = _qkv(ctx, mods, layer, ctx_row, n1, w_qkv[j], qn, kn, None, True, TOKEN_TILE)
            oc = _attn_ctx(qc, kc, vc, seq, TOKEN_TILE)
            ql, kl, vl = _qkv(lat, mods, layer, lat_row(TOKEN_TILE), n1, w_qkv[j], qn, kn, rope_tabs, False,
                              TOKEN_TILE)
            past_k = cache_k[:, j].reshape(dec_batch, -1, kvw).astype(BF16)
            past_v = cache_v[:, j].reshape(dec_batch, -1, kvw).astype(BF16)
            ol = _attn_lat(ql, kl, vl, past_k, past_v, dec_seq, Q_TILE)
            ctx1 = _post(oc, ctx, mods, layer, ctx_row, n2, w_o[j], wr, TOKEN_TILE)
            lat1 = _post(ol, lat, mods, layer, lat_row(TOKEN_TILE), n2, w_o[j], wr, TOKEN_TILE)
        else:
            ps = pool_scale[j][None]
            ctx1 = _pool(ctx, mods, layer, ctx_row, n1, n2, w_pool[j], ps, wr, seq, ctx_tile)
            lat1 = _pool(lat, mods, layer, lat_row(TOKEN_TILE), n1, n2, w_pool[j], ps, wr, dec_seq, TOKEN_TILE)
        ctx, lat = _moe([ctx1, lat1], [ctx_row, lat_row(1)], mods, layer, w_e_gate, w_e_up, w_e_down)

    new_cache_k = new_k.reshape(batch, 1, seq, N_KV, HEAD_DIM)
    new_cache_v = new_v.reshape(batch, 1, seq, N_KV, HEAD_DIM)
    return (ctx.reshape(batch, seq, D_MODEL), lat.reshape(dec_batch, dec_seq, D_MODEL), new_cache_k, new_cache_v)
```

```python
import functools

import jax
import jax.numpy as jnp
import numpy as np
from jax import lax
from jax.experimental import pallas as pl
from jax.experimental.pallas import tpu as pltpu

F32 = jnp.float32
BF16 = jnp.bfloat16
I32 = jnp.int32
U32 = jnp.uint32

D_MODEL = 1024
HEAD_DIM = 128
N_HEADS = 8
N_KV = 2
Q_PER_KV = N_HEADS // N_KV
QKV_DIM = (N_HEADS + 2 * N_KV) * HEAD_DIM
GRID_W = 64
ROPE_THETA = 10000.0
POOL_WINDOWS = (2, 4, 8, 16)
POOL_GROUP = D_MODEL // len(POOL_WINDOWS)
POOL_HALO = max(POOL_WINDOWS) // 2
N_EXPERTS = 16
CAPACITY_FACTOR = 2
N_MOD = 6
EPS = 1e-6
LOG2_E = 1.4426950408889634

LANES = 128
SUBLANES = 8
HALF = D_MODEL // 2
HP_ROWS = HALF // LANES
Y_ROWS = D_MODEL // LANES
VMEM_LIMIT = 56 * 1024 * 1024


def _cparams(n_axes, vmem=VMEM_LIMIT):
    return pltpu.CompilerParams(dimension_semantics=("arbitrary",) * n_axes, vmem_limit_bytes=vmem)


def _silu(x):
    return x / (1.0 + jnp.exp(-x))


def _rms(x, gain):
    return x * lax.rsqrt(jnp.mean(x * x, axis=-1, keepdims=True) + EPS) * gain


def _norm_mod(x, gain, shift, scale):
    return _rms(x, gain * (1.0 + scale)) + shift


def _dot(a, b):
    return jnp.dot(a, b, preferred_element_type=F32)


def _dot_nt(a, b):
    return lax.dot_general(a, b, (((1,), (1,)), ((), ())), preferred_element_type=F32)


def _ada_kernel(c_ref, w_ref, b_ref, o_ref):
    s = _silu(c_ref[...]).astype(BF16)
    o_ref[0] = _dot(s, w_ref[0].astype(BF16)) + b_ref[0]


def _ada(cvec, w_ada, b_ada):
    depth = w_ada.shape[0]
    out = pl.pallas_call(
        _ada_kernel,
        out_shape=jax.ShapeDtypeStruct((depth, SUBLANES, N_MOD * D_MODEL), F32),
        grid=(depth, N_MOD),
        in_specs=[
            pl.BlockSpec((SUBLANES, D_MODEL), lambda i, j: (0, 0)),
            pl.BlockSpec((1, D_MODEL, D_MODEL), lambda i, j: (i, 0, j)),
            pl.BlockSpec((1, 1, D_MODEL), lambda i, j: (i, 0, j)),
        ],
        out_specs=pl.BlockSpec((1, SUBLANES, D_MODEL), lambda i, j: (i, 0, j)),
        compiler_params=_cparams(2),
        name="ada",
    )(cvec, w_ada, b_ada.reshape(depth, 1, N_MOD * D_MODEL))
    return out.reshape(depth, SUBLANES, N_MOD, D_MODEL)


def _mod_spec(layer, row_of_tile):
    return pl.BlockSpec((1, 1, N_MOD, D_MODEL), lambda i, *_: (layer, row_of_tile(i), 0, 0))


def _qkv_kernel(*refs, rope, cache_out):
    x_ref, mod_ref, n1_ref, w_ref, qn_ref, kn_ref = refs[:6]
    refs = refs[6:]
    if rope:
        cos_ref, sin_ref = refs[:2]
        refs = refs[2:]
    q_ref, k_ref, v_ref = refs[:3]
    refs = refs[3:]
    if cache_out:
        kc_ref, vc_ref = refs[:2]
        refs = refs[2:]
    (wb_ref,) = refs

    qk_w = (N_HEADS + N_KV) * HEAD_DIM
    quarter = HEAD_DIM // 4

    def partner(a):
        width = a.shape[1]
        first = (lax.broadcasted_iota(I32, a.shape, 1) & quarter) == 0
        return jnp.where(first, pltpu.roll(a, width - quarter, 1), pltpu.roll(a, quarter, 1))

    @pl.when(pl.program_id(0) == 0)
    def _():
        w = w_ref[...]
        wb_ref[:, 0:QKV_DIM] = w.astype(BF16)
        if rope:
            wb_ref[:, QKV_DIM:] = partner(w[:, 0:qk_w]).astype(BF16)

    m = mod_ref[0, 0]
    h = _norm_mod(x_ref[...], n1_ref[...], m[0:1], m[1:2])
    qkv = _dot(h.astype(BF16), wb_ref[...])
    scale = HEAD_DIM ** -0.5 * LOG2_E
    if rope:
        gains = {True: qn_ref[...], False: kn_ref[...]}
        cos_g = {key: cos_ref[...] * g for key, g in gains.items()}
        sin_g = {key: sin_ref[...] * partner(g) for key, g in gains.items()}
    for hh in range(N_HEADS + N_KV):
        sl = slice(hh * HEAD_DIM, (hh + 1) * HEAD_DIM)
        is_q = hh < N_HEADS
        if rope:
            raw = qkv[:, sl]
            norm = lax.rsqrt(jnp.mean(raw * raw, axis=-1, keepdims=True) + EPS)
            xh = (raw * cos_g[is_q] + qkv[:, QKV_DIM + hh * HEAD_DIM:QKV_DIM + (hh + 1) * HEAD_DIM] * sin_g[is_q]) * norm
        else:
            xh = _rms(qkv[:, sl], qn_ref[...] if is_q else kn_ref[...])
        if cache_out and hh >= N_HEADS:
            kc_ref[:, hh - N_HEADS, :] = xh
        if hh < N_HEADS:
            q_ref[:, sl] = (xh * scale).astype(BF16)
        else:
            k_ref[:, (hh - N_HEADS) * HEAD_DIM:(hh - N_HEADS + 1) * HEAD_DIM] = xh.astype(BF16)
    v = qkv[:, qk_w:QKV_DIM]
    v_ref[...] = v.astype(BF16)
    if cache_out:
        for g in range(N_KV):
            vc_ref[:, g, :] = v[:, g * HEAD_DIM:(g + 1) * HEAD_DIM]


def _qkv(x, mods, layer, row_of_tile, n1, w_qkv, qn, kn, rope_tabs, cache_out, tm):
    n_tok = x.shape[0]
    kvw = N_KV * HEAD_DIM
    rope = rope_tabs is not None
    in_specs = [
        pl.BlockSpec((tm, D_MODEL), lambda i: (i, 0)),
        _mod_spec(layer, row_of_tile),
        pl.BlockSpec((1, D_MODEL), lambda i: (0, 0)),
        pl.BlockSpec((D_MODEL, QKV_DIM), lambda i: (0, 0)),
        pl.BlockSpec((1, HEAD_DIM), lambda i: (0, 0)),
        pl.BlockSpec((1, HEAD_DIM), lambda i: (0, 0)),
    ]
    args = [x, mods, n1, w_qkv, qn, kn]
    if rope:
        seq_tiles = rope_tabs[0].shape[0] // tm
        in_specs += [pl.BlockSpec((tm, HEAD_DIM), lambda i: (i % seq_tiles, 0))] * 2
        args += list(rope_tabs)
    out_shape = [jax.ShapeDtypeStruct((n_tok, D_MODEL), BF16),
                 jax.ShapeDtypeStruct((n_tok, kvw), BF16),
                 jax.ShapeDtypeStruct((n_tok, kvw), BF16)]
    out_specs = [pl.BlockSpec((tm, D_MODEL), lambda i: (i, 0)),
                 pl.BlockSpec((tm, kvw), lambda i: (i, 0)),
                 pl.BlockSpec((tm, kvw), lambda i: (i, 0))]
    if cache_out:
        out_shape += [jax.ShapeDtypeStruct((n_tok, N_KV, HEAD_DIM), F32)] * 2
        out_specs += [pl.BlockSpec((tm, N_KV, HEAD_DIM), lambda i: (i, 0, 0))] * 2
    return pl.pallas_call(
        functools.partial(_qkv_kernel, rope=rope, cache_out=cache_out),
        out_shape=out_shape,
        grid=(n_tok // tm,),
        in_specs=in_specs,
        out_specs=out_specs,
        scratch_shapes=[pltpu.VMEM((D_MODEL, QKV_DIM + ((N_HEADS + N_KV) * HEAD_DIM if rope else 0)), BF16)],
        compiler_params=_cparams(1),
        name="qkv_rope" if rope else "qkv",
    )(*args)


def _rope_tables(seq_len):
    half = HEAD_DIM // 2
    n = half // 2
    inv_freq = ROPE_THETA ** (-np.arange(n, dtype=np.float64) / n)
    rows = seq_len // GRID_W
    row = np.repeat(np.arange(rows), GRID_W).astype(np.float64)
    col = np.tile(np.arange(GRID_W), rows).astype(np.float64)
    ang_r = row[:, None] * inv_freq[None, :]
    ang_c = col[:, None] * inv_freq[None, :]
    cos = np.concatenate([np.cos(ang_r)] * 2 + [np.cos(ang_c)] * 2, axis=-1)
    sin = np.concatenate([-np.sin(ang_r), np.sin(ang_r), -np.sin(ang_c), np.sin(ang_c)], axis=-1)
    return jnp.asarray(cos, F32), jnp.asarray(sin, F32)


def _with_ones(v):
    return jnp.concatenate([v, jnp.ones_like(v)], axis=1)


def _gqa_attention(q, segments):
    rows = q.shape[0]
    qs = jnp.concatenate([q[:, h * HEAD_DIM:(h + 1) * HEAD_DIM] for h in range(Q_PER_KV)], axis=0)
    m = acc = None
    for k, v in segments:
        s = _dot_nt(qs, k)
        seg_max = jnp.max(s, axis=-1, keepdims=True)
        m_new = seg_max if m is None else jnp.maximum(m, seg_max)
        pv = _dot(jnp.exp2(s - m_new).astype(BF16), v)
        acc = pv if m is None else acc * jnp.exp2(m - m_new) + pv
        m = m_new
    o = (acc[:, :HEAD_DIM] / acc[:, HEAD_DIM:]).astype(BF16)
    return jnp.concatenate([o[h * rows:(h + 1) * rows] for h in range(Q_PER_KV)], axis=1)


def _attn_ctx_kernel(q_ref, k_ref, v_ref, o_ref, *, seq):
    gw = Q_PER_KV * HEAD_DIM
    for r in range(0, q_ref.shape[0], seq):
        for g in range(N_KV):
            kg = k_ref[r:r + seq, g * HEAD_DIM:(g + 1) * HEAD_DIM]
            vg = _with_ones(v_ref[r:r + seq, g * HEAD_DIM:(g + 1) * HEAD_DIM])
            o_ref[r:r + seq, g * gw:(g + 1) * gw] = _gqa_attention(q_ref[r:r + seq, g * gw:(g + 1) * gw], [(kg, vg)])


def _attn_ctx(q, k, v, seq, tm):
    n_tok = q.shape[0]
    kvw = N_KV * HEAD_DIM
    return pl.pallas_call(
        functools.partial(_attn_ctx_kernel, seq=seq),
        out_shape=jax.ShapeDtypeStruct((n_tok, D_MODEL), BF16),
        grid=(n_tok // tm,),
        in_specs=[pl.BlockSpec((tm, D_MODEL), lambda b: (b, 0)),
                  pl.BlockSpec((tm, kvw), lambda b: (b, 0)),
                  pl.BlockSpec((tm, kvw), lambda b: (b, 0))],
        out_specs=pl.BlockSpec((tm, D_MODEL), lambda b: (b, 0)),
        compiler_params=_cparams(1),
        name="attn_ctx",
    )(q, k, v)


KEY_CHUNK = 256


def _attn_lat_kernel(q_ref, k_ref, v_ref, kc_ref, vc_ref, o_ref):
    segments = [(k_ref[c:c + KEY_CHUNK], _with_ones(v_ref[c:c + KEY_CHUNK]))
                for c in range(0, k_ref.shape[0], KEY_CHUNK)]
    segments.append((kc_ref[0], _with_ones(vc_ref[0])))
    o_ref[...] = _gqa_attention(q_ref[...], segments)


def _attn_lat(q, k, v, kc, vc, seq, tq):
    n_tok = q.shape[0]
    batch = n_tok // seq
    past = kc.shape[1]
    qt = seq // tq
    gw = Q_PER_KV * HEAD_DIM
    return pl.pallas_call(
        _attn_lat_kernel,
        out_shape=jax.ShapeDtypeStruct((n_tok, D_MODEL), BF16),
        grid=(batch, N_KV, qt),
        in_specs=[pl.BlockSpec((tq, gw), lambda b, g, i: (b * qt + i, g)),
                  pl.BlockSpec((seq, HEAD_DIM), lambda b, g, i: (b, g)),
                  pl.BlockSpec((seq, HEAD_DIM), lambda b, g, i: (b, g)),
                  pl.BlockSpec((1, past, HEAD_DIM), lambda b, g, i: (b, 0, g)),
                  pl.BlockSpec((1, past, HEAD_DIM), lambda b, g, i: (b, 0, g))],
        out_specs=pl.BlockSpec((tq, gw), lambda b, g, i: (b * qt + i, g)),
        compiler_params=_cparams(3),
        name="attn_lat",
    )(q, k, v, kc, vc)


FRONT_ROWS = 256


def _router_weights(wr):
    return jnp.concatenate([wr, jnp.zeros((wr.shape[0], LANES - N_EXPERTS), wr.dtype)], axis=1).astype(BF16)


def _router_softmax(hb, wr):
    logits = _dot(hb, wr)
    lane = lax.broadcasted_iota(I32, logits.shape, 1)
    logits = jnp.where(lane < N_EXPERTS, logits, -1e30)
    ex = jnp.exp(logits - jnp.max(logits, axis=-1, keepdims=True))
    return ex / jnp.sum(ex, axis=-1, keepdims=True)


def _moe_front(x1, r0, m, n2, wr, x1_ref, hp_ref, aff_ref):
    n = x1.shape[0]
    x1_ref[r0:r0 + n, :] = x1
    h2 = _norm_mod(x1, n2, m[3:4], m[4:5])
    aff = _router_softmax(h2.astype(BF16), wr)
    packed = pltpu.pack_elementwise([h2[:, :HALF], h2[:, HALF:]], packed_dtype=BF16)
    for s in range(HP_ROWS):
        hp_ref[pl.ds(r0 * HP_ROWS + s, n, stride=HP_ROWS), :] = packed[:, s * LANES:(s + 1) * LANES]
    aff_t = aff.T
    for b in range(n // LANES):
        out_row = (r0 // LANES + b) * N_EXPERTS
        aff_ref[out_row:out_row + N_EXPERTS, :] = aff_t[0:N_EXPERTS, b * LANES:(b + 1) * LANES]


def _front_out(n_tok, tm):
    shapes = [jax.ShapeDtypeStruct((n_tok, D_MODEL), F32),
              jax.ShapeDtypeStruct((n_tok * HP_ROWS, LANES), U32),
              jax.ShapeDtypeStruct((n_tok // LANES * N_EXPERTS, LANES), F32)]
    specs = [pl.BlockSpec((tm, D_MODEL), lambda i: (i, 0)),
             pl.BlockSpec((tm * HP_ROWS, LANES), lambda i: (i, 0)),
             pl.BlockSpec((tm // LANES * N_EXPERTS, LANES), lambda i: (i, 0))]
    return shapes, specs


RING = 3


def _post_kernel(o_hbm, x_hbm, mod_ref, n2_ref, wo_ref, wr_ref, x1_ref, hp_ref, aff_ref,
                 wb_ref, o_buf, x_buf, sem):
    i = pl.program_id(0)
    n_steps = pl.num_programs(0)
    tm = x1_ref.shape[0]

    def tile_copies(step, slot):
        rows = pl.ds(pl.multiple_of(step * tm, tm), tm)
        return (pltpu.make_async_copy(o_hbm.at[rows, :], o_buf.at[slot], sem.at[0, slot]),
                pltpu.make_async_copy(x_hbm.at[rows, :], x_buf.at[slot], sem.at[1, slot]))

    def start(step):
        for copy in tile_copies(step, step % RING):
            copy.start()

    @pl.when(i == 0)
    def _():
        for ahead in range(RING - 1):
            @pl.when(ahead < n_steps)
            def _():
                start(ahead)
        wb_ref[...] = wo_ref[...].astype(BF16)

    @pl.when(i + RING - 1 < n_steps)
    def _():
        start(i + RING - 1)

    slot = i % RING
    for copy in tile_copies(i, slot):
        copy.wait()

    m = mod_ref[0, 0]
    wr = _router_weights(wr_ref[0])
    for r0 in range(0, tm, FRONT_ROWS):
        rows = slice(r0, r0 + FRONT_ROWS)
        x1 = x_buf[slot, rows, :] + m[2:3] * _dot(o_buf[slot, rows, :], wb_ref[...])
        _moe_front(x1, r0, m, n2_ref[...], wr, x1_ref, hp_ref, aff_ref)


def _post(o, x, mods, layer, row_of_tile, n2, w_o, wr, tm):
    n_tok = x.shape[0]
    shapes, specs = _front_out(n_tok, tm)
    return pl.pallas_call(
        _post_kernel,
        out_shape=shapes,
        grid=(n_tok // tm,),
        in_specs=[pl.BlockSpec(memory_space=pl.ANY),
                  pl.BlockSpec(memory_space=pl.ANY),
                  _mod_spec(layer, row_of_tile),
                  pl.BlockSpec((1, D_MODEL), lambda i: (0, 0)),
                  pl.BlockSpec((D_MODEL, D_MODEL), lambda i: (0, 0)),
                  pl.BlockSpec((1, D_MODEL, N_EXPERTS), lambda i: (layer, 0, 0))],
        out_specs=specs,
        scratch_shapes=[pltpu.VMEM((D_MODEL, D_MODEL), BF16),
                        pltpu.VMEM((RING, tm, D_MODEL), o.dtype),
                        pltpu.VMEM((RING, tm, D_MODEL), x.dtype),
                        pltpu.SemaphoreType.DMA((2, RING))],
        compiler_params=_cparams(1),
        name="post",
    )(o, x, mods, n2, w_o, wr)


def _pool_kernel(x_hbm, xp_ref, xn_ref, mod_ref, n1_ref, n2_ref, wp_ref, ps_ref, wr_ref,
                 x1_ref, hp_ref, aff_ref, x_buf, sem, *, seq):
    tm = x1_ref.shape[0]
    i = pl.program_id(0)
    n_steps = pl.num_programs(0)

    def tile_copy(step, slot):
        rows = pl.ds(pl.multiple_of(step * tm, tm), tm)
        return pltpu.make_async_copy(x_hbm.at[rows, :], x_buf.at[slot], sem.at[slot])

    @pl.when(i == 0)
    def _():
        for ahead in range(RING - 1):
            @pl.when(ahead < n_steps)
            def _():
                tile_copy(ahead, ahead % RING).start()

    @pl.when(i + RING - 1 < n_steps)
    def _():
        tile_copy(i + RING - 1, (i + RING - 1) % RING).start()

    slot = i % RING
    tile_copy(i, slot).wait()
    m = mod_ref[0, 0]
    x = x_buf[slot]

    def norm_mod(v):
        return _norm_mod(v, n1_ref[...], m[0:1], m[1:2])

    h = norm_mod(x)
    if tm <= seq:
        prev_ok = ((i * tm) % seq != 0).astype(F32)
        next_ok = (((i + 1) * tm) % seq != 0).astype(F32)
        pieces = [(norm_mod(xp_ref[...]) * prev_ok, h, norm_mod(xn_ref[...]) * next_ok)]
        t0 = (i * tm) % seq
    else:
        zeros = jnp.zeros((POOL_HALO, D_MODEL), F32)
        pieces = [(zeros, h[r:r + seq], zeros) for r in range(0, tm, seq)]
        t0 = 0
    n = min(tm, seq)
    rows = n + 2 * POOL_HALO
    t = t0 + lax.broadcasted_iota(I32, (n, 1), 0)
    deltas = [[] for _ in POOL_WINDOWS]
    for piece in pieces:
        hz = jnp.concatenate(piece, axis=0)
        for g, w in enumerate(POOL_WINDOWS):
            sl = slice(g * POOL_GROUP, (g + 1) * POOL_GROUP)
            f = hz[:, sl]
            step = 1
            while step < w:
                f = f + pltpu.roll(f, rows - step, 0)
                step *= 2
            win = pltpu.roll(f, w // 2, 0)[POOL_HALO:POOL_HALO + n]
            means = []
            for edge in (slice(0, POOL_HALO), slice(n - POOL_HALO, n)):
                cnt = jnp.minimum(t[edge] + w // 2, seq) - jnp.maximum(t[edge] - w // 2, 0)
                means.append(win[edge] / cnt.astype(F32))
            mean = jnp.concatenate([means[0], win[POOL_HALO:n - POOL_HALO] * (1.0 / w), means[1]], axis=0)
            deltas[g].append((mean - piece[1][:, sl]).astype(BF16))
    ys = [_dot(jnp.concatenate(d, axis=0), wp_ref[g].astype(BF16)) for g, d in enumerate(deltas)]
    x1 = x + (m[2:3] * ps_ref[...]) * jnp.concatenate(ys, axis=-1)
    wr = _router_weights(wr_ref[0])
    for r0 in range(0, tm, FRONT_ROWS):
        _moe_front(x1[r0:r0 + FRONT_ROWS], r0, m, n2_ref[...], wr, x1_ref, hp_ref, aff_ref)


def _pool(x, mods, layer, row_of_tile, n1, n2, w_pool, pool_scale, wr, seq, tm):
    n_tok = x.shape[0]
    hb = tm // POOL_HALO
    last = n_tok // POOL_HALO - 1
    shapes, specs = _front_out(n_tok, tm)
    return pl.pallas_call(
        functools.partial(_pool_kernel, seq=seq),
        out_shape=shapes,
        grid=(n_tok // tm,),
        in_specs=[pl.BlockSpec(memory_space=pl.ANY),
                  pl.BlockSpec((POOL_HALO, D_MODEL), lambda i: (jnp.maximum(i * hb - 1, 0), 0)),
                  pl.BlockSpec((POOL_HALO, D_MODEL), lambda i: (jnp.minimum((i + 1) * hb, last), 0)),
                  _mod_spec(layer, row_of_tile),
                  pl.BlockSpec((1, D_MODEL), lambda i: (0, 0)),
                  pl.BlockSpec((1, D_MODEL), lambda i: (0, 0)),
                  pl.BlockSpec(w_pool.shape, lambda i: (0, 0, 0)),
                  pl.BlockSpec((1, D_MODEL), lambda i: (0, 0)),
                  pl.BlockSpec((1, D_MODEL, N_EXPERTS), lambda i: (layer, 0, 0))],
        out_specs=specs,
        scratch_shapes=[pltpu.VMEM((RING, tm, D_MODEL), x.dtype), pltpu.SemaphoreType.DMA((RING,))],
        compiler_params=_cparams(1),
        name="pool",
    )(x, x, x, mods, n1, n2, w_pool, pool_scale, wr)


DIST_BIT = 8


def _select_kernel(aff_ref, src_ref, dst_ref, gate_ref, *, cap):
    nb = aff_ref.shape[0] // N_EXPERTS
    rows = nb * N_EXPERTS
    shape3 = (nb, N_EXPERTS, LANES)
    aff = aff_ref[...].reshape(shape3)

    def count(mask):
        return jnp.sum(jnp.sum(mask.astype(F32), axis=0), axis=-1, keepdims=True)

    def search(i, t):
        shift = 28 - 2 * i
        best = t
        for digit in (1, 2, 3):
            cand = t | (jnp.int32(digit) << shift)
            best = jnp.where(count(aff >= pltpu.bitcast(cand, F32)[None]) >= cap, cand, best)
        return best

    thr = pltpu.bitcast(lax.fori_loop(0, 15, search, jnp.zeros((N_EXPERTS, LANES), I32)), F32)
    gt = aff > thr[None]
    eq = aff == thr[None]
    need = cap - count(gt)

    kk = lax.broadcasted_iota(I32, (LANES, LANES), 0)
    nn = lax.broadcasted_iota(I32, (LANES, LANES), 1)
    upper = (kk <= nn).astype(BF16)
    ones = jnp.ones((LANES, LANES), BF16)

    def prefix(mask):
        m2 = mask.astype(F32).astype(BF16).reshape(rows, LANES)
        p = _dot(m2, upper).reshape(shape3)
        s = _dot(m2, ones).reshape(shape3)
        offs = []
        run = jnp.zeros((N_EXPERTS, LANES), F32)
        for b in range(nb):
            offs.append(run)
            run = run + s[b]
        return p, jnp.stack(offs, axis=0), s

    pe, oe, _ = prefix(eq)
    sel = gt | (eq & ((pe + oe) <= need[None]))
    ps, os_, ss = prefix(sel)

    lane = lax.broadcasted_iota(I32, (rows, LANES), 1)
    sel2 = sel.reshape(rows, LANES)
    dist = lane - (ps.reshape(rows, LANES).astype(I32) - 1)
    word = jnp.where(sel2, lane | (dist << DIST_BIT), 0)
    for k in range(7):
        bit = 1 << (DIST_BIT + k)
        arriving = pltpu.roll(word, LANES - (1 << k), 1)
        word = jnp.where((arriving & bit) != 0, arriving, jnp.where((word & bit) != 0, 0, word))
    val = word & (LANES - 1)
    local = val.reshape(shape3) + lax.broadcasted_iota(I32, shape3, 0) * LANES
    local_gate = jnp.take_along_axis(aff.reshape(rows, LANES), val, axis=1).reshape(shape3)

    first = os_.astype(I32)
    chosen = pltpu.bitcast(ss.astype(I32), U32)
    lane_e = lax.broadcasted_iota(I32, (N_EXPERTS, LANES), 1)
    n_chunks = cap // LANES
    acc = [jnp.zeros((N_EXPERTS, LANES), I32)] * n_chunks
    gacc = [jnp.zeros((N_EXPERTS, LANES), F32)] * n_chunks
    for b in range(nb):
        within = lane_e - first[b]
        phase = within & (LANES - 1)
        tokens = jnp.take_along_axis(local[b], phase, axis=1)
        gates = jnp.take_along_axis(local_gate[b], phase, axis=1)
        for jc in range(n_chunks):
            inside = pltpu.bitcast(within + jc * LANES, U32) < chosen[b]
            acc[jc] = jnp.where(inside, tokens, acc[jc])
            gacc[jc] = jnp.where(inside, gates, gacc[jc])
    for jc in range(n_chunks):
        out_rows = pl.ds(jc, N_EXPERTS, stride=n_chunks)
        src_ref[out_rows, :] = acc[jc] * HP_ROWS
        dst_ref[out_rows, :] = acc[jc] * Y_ROWS
        gate_ref[out_rows, :] = gacc[jc]


def _select(aff, cap):
    rows = aff.shape[0]
    out_rows = N_EXPERTS * cap // LANES
    out_spec = pl.BlockSpec((out_rows, LANES), lambda i: (0, 0))
    src, dst, gate = pl.pallas_call(
        functools.partial(_select_kernel, cap=cap),
        out_shape=[jax.ShapeDtypeStruct((out_rows, LANES), I32),
                   jax.ShapeDtypeStruct((out_rows, LANES), I32),
                   jax.ShapeDtypeStruct((out_rows, LANES), F32)],
        grid=(1,),
        in_specs=[pl.BlockSpec((rows, LANES), lambda i: (0, 0))],
        out_specs=[out_spec, out_spec, out_spec],
        compiler_params=_cparams(1),
        name="select",
    )(aff)
    return src.reshape(-1), dst.reshape(-1), gate


GATHER_UNROLL = 32
GATHER_STEP_ROWS = 2048


def _gather_kernel(src_ref, hp_ref, o_ref):
    i = pl.program_id(0)
    n = o_ref.shape[0] // HP_ROWS

    def gather(g, carry):
        base = i * n + g * GATHER_UNROLL
        for r in range(GATHER_UNROLL):
            src = pl.multiple_of(src_ref[base + r], HP_ROWS)
            dst = pl.multiple_of((g * GATHER_UNROLL + r) * HP_ROWS, HP_ROWS)
            o_ref[pl.ds(dst, HP_ROWS), :] = hp_ref[pl.ds(src, HP_ROWS), :]
        return carry

    lax.fori_loop(0, n // GATHER_UNROLL, gather, 0)


def _gather(src, hp):
    n = src.shape[0]
    return pl.pallas_call(
        _gather_kernel,
        out_shape=jax.ShapeDtypeStruct((n * HP_ROWS, LANES), U32),
        grid_spec=pltpu.PrefetchScalarGridSpec(
            num_scalar_prefetch=1,
            grid=(n // GATHER_STEP_ROWS,),
            in_specs=[pl.BlockSpec(hp.shape, lambda i, src: (0, 0), pipeline_mode=pl.Buffered(1))],
            out_specs=pl.BlockSpec((GATHER_STEP_ROWS * HP_ROWS, LANES), lambda i, src: (i, 0))),
        compiler_params=_cparams(1),
        name="gather",
    )(src, hp)


FF_CHUNK = 256


def _ffn_kernel(*refs, n_streams):
    e = pl.program_id(0)
    ins = refs[:2 * n_streams]
    wg_ref, wu_ref, wd_ref = refs[2 * n_streams:2 * n_streams + 3]
    outs = refs[2 * n_streams + 3:]
    d_ff = wg_ref.shape[3]
    eye = lax.broadcasted_iota(I32, (LANES, LANES), 0) == lax.broadcasted_iota(I32, (LANES, LANES), 1)
    for i in range(n_streams):
        xp_ref, gate_ref, y_ref = ins[2 * i], ins[2 * i + 1], outs[i]
        cap = xp_ref.shape[0] // HP_ROWS
        halves = [[], []]
        for s in range(HP_ROWS):
            w = xp_ref[pl.ds(s, cap, stride=HP_ROWS), :]
            for half in range(2):
                v = pltpu.unpack_elementwise(w, index=half, packed_dtype=BF16, unpacked_dtype=F32)
                halves[half].append(v.astype(BF16))
        xe = jnp.concatenate(halves[0] + halves[1], axis=1)
        y = None
        for c in range(0, d_ff, FF_CHUNK):
            a = _dot(xe, wg_ref[0, 0, :, c:c + FF_CHUNK].astype(BF16))
            u = _dot(xe, wu_ref[0, 0, :, c:c + FF_CHUNK].astype(BF16))
            yc = _dot((_silu(a) * u).astype(BF16), wd_ref[0, 0, c:c + FF_CHUNK, :].astype(BF16))
            y = yc if y is None else y + yc
        for c in range(cap // LANES):
            g_row = gate_ref[pl.ds(e * (cap // LANES) + c, 1), :]
            g_col = jnp.sum(jnp.where(eye, g_row, 0.0), axis=1, keepdims=True)
            tot = y[c * LANES:(c + 1) * LANES, :] * g_col
            for s in range(Y_ROWS):
                y_ref[0, pl.ds(c * LANES * Y_ROWS + s, LANES, stride=Y_ROWS), :] = tot[:, s * LANES:(s + 1) * LANES]


def _ffn(streams, layer, w_gate, w_up, w_down):
    caps = [xp.shape[0] // (N_EXPERTS * HP_ROWS) for xp, _ in streams]
    in_specs, args = [], []
    for (xp, gate), cap in zip(streams, caps):
        in_specs += [pl.BlockSpec((cap * HP_ROWS, LANES), lambda e: (e, 0)),
                     pl.BlockSpec(gate.shape, lambda e: (0, 0))]
        args += [xp, gate]
    w_spec = pl.BlockSpec((1, 1) + w_gate.shape[2:], lambda e: (layer, e, 0, 0))
    return pl.pallas_call(
        functools.partial(_ffn_kernel, n_streams=len(streams)),
        out_shape=[jax.ShapeDtypeStruct((N_EXPERTS, cap * Y_ROWS, LANES), F32) for cap in caps],
        grid=(N_EXPERTS,),
        in_specs=in_specs + [w_spec, w_spec, w_spec],
        out_specs=[pl.BlockSpec((1, cap * Y_ROWS, LANES), lambda e: (e, 0, 0)) for cap in caps],
        compiler_params=_cparams(1),
        name="ffn",
    )(*args, w_gate, w_up, w_down)


SCATTER_UNROLL = 16
SCATTER_GROUPS = 2


def _combine_kernel(dst_ref, y_ref, x1_ref, mod_ref, o_ref, acc_ref, *, n_scatter):
    s = pl.program_id(0)
    tf = o_ref.shape[0]
    step_rows = y_ref.shape[0] // Y_ROWS

    @pl.when(s == 0)
    def _():
        acc_ref[...] = jnp.zeros_like(acc_ref)

    @pl.when(s < n_scatter)
    def _():
        def scatter(g, carry):
            for sub in range(SCATTER_GROUPS):
                first = (g * SCATTER_GROUPS + sub) * SCATTER_UNROLL
                base = s * step_rows + first
                rows, sums = [], []
                for r in range(SCATTER_UNROLL):
                    dst = pl.ds(pl.multiple_of(dst_ref[base + r], Y_ROWS), Y_ROWS)
                    src = pl.ds(pl.multiple_of((first + r) * Y_ROWS, Y_ROWS), Y_ROWS)
                    rows.append(dst)
                    sums.append(acc_ref[dst, :] + y_ref[src, :])
                for dst, v in zip(rows, sums):
                    acc_ref[dst, :] = v
            return carry

        lax.fori_loop(0, step_rows // (SCATTER_UNROLL * SCATTER_GROUPS), scatter, 0)

    @pl.when(s >= n_scatter)
    def _():
        g2 = mod_ref[0, 0][5:6]
        tile = acc_ref.at[pl.ds(pl.multiple_of((s - n_scatter) * tf * Y_ROWS, tf * Y_ROWS), tf * Y_ROWS), :]
        for c in range(Y_ROWS):
            sl = slice(c * LANES, (c + 1) * LANES)
            o_ref[:, sl] = x1_ref[:, sl] + g2[:, sl] * tile[pl.ds(c, tf, stride=Y_ROWS), :]


def _combine_tiles(n_tok):
    row_bytes = D_MODEL * 4
    free_rows = (VMEM_LIMIT - n_tok * row_bytes) * 4 // 5 // row_bytes
    scale = max(k for k in (1, 2, 4) if 2 * 512 * k + 4 * 256 * k <= free_rows)
    return 256 * scale, 512 * scale


def _combine(dst, y, x1, mods, layer, row_of_token):
    n_tok = x1.shape[0]
    tf, step_rows = _combine_tiles(n_tok)
    n_scatter = dst.shape[0] // step_rows
    assert (dst.shape[0] // N_EXPERTS) % SCATTER_UNROLL == 0 and dst.shape[0] % step_rows == 0
    tile_of = lambda s: jnp.maximum(s - n_scatter, 0)
    row_of_tile = lambda t: row_of_token(t * tf)
    return pl.pallas_call(
        functools.partial(_combine_kernel, n_scatter=n_scatter),
        out_shape=jax.ShapeDtypeStruct((n_tok, D_MODEL), F32),
        grid_spec=pltpu.PrefetchScalarGridSpec(
            num_scalar_prefetch=1,
            grid=(n_scatter + n_tok // tf,),
            in_specs=[pl.BlockSpec((step_rows * Y_ROWS, LANES), lambda s, *_: (jnp.minimum(s, n_scatter - 1), 0)),
                      pl.BlockSpec((tf, D_MODEL), lambda s, *_: (tile_of(s), 0)),
                      pl.BlockSpec((1, 1, N_MOD, D_MODEL), lambda s, *_: (layer, row_of_tile(tile_of(s)), 0, 0))],
            out_specs=pl.BlockSpec((tf, D_MODEL), lambda s, *_: (tile_of(s), 0)),
            scratch_shapes=[pltpu.VMEM((n_tok * Y_ROWS, LANES), F32)]),
        compiler_params=_cparams(1),
        name="combine",
    )(dst, y.reshape(-1, LANES), x1, mods)


TOKEN_TILE = 1024
Q_TILE = 1024


def _moe(fronts, rows_of_token, mods, layer, w_gate, w_up, w_down):
    routed, dsts = [], []
    for x1, hp, aff in fronts:
        cap = CAPACITY_FACTOR * x1.shape[0] // N_EXPERTS
        src, dst, gate = _select(aff, cap)
        routed.append((_gather(src, hp), gate))
        dsts.append(dst)
    ys = _ffn(routed, layer, w_gate, w_up, w_down)
    return [_combine(dst, y, x1, mods, layer, row_of_token)
            for dst, y, (x1, _, _), row_of_token in zip(dsts, ys, fronts, rows_of_token)]


def kernel(x_prompt, x_sample, cache_k, cache_v, c, c_ctx, norm1, norm2, w_ada, b_ada, w_qkv, q_norm, k_norm,
           w_o, w_pool, pool_scale, w_router, w_e_gate, w_e_up, w_e_down):
    batch, seq, _ = x_prompt.shape
    dec_batch, dec_seq, _ = x_sample.shape
    depth = w_ada.shape[0]
    kvw = N_KV * HEAD_DIM

    cvec = jnp.zeros((SUBLANES, D_MODEL), F32).at[0].set(c_ctx).at[1:1 + dec_batch].set(c)
    mods = _ada(cvec, w_ada, b_ada)
    ctx = x_prompt.reshape(batch * seq, D_MODEL)
    lat = x_sample.reshape(dec_batch * dec_seq, D_MODEL)
    ctx_row = lambda i: 0
    lat_row = lambda tile: (lambda i: 1 + (i * tile) // dec_seq)
    ctx_tile = TOKEN_TILE if TOKEN_TILE % seq == 0 else min(TOKEN_TILE, seq)
    rope_tabs = _rope_tables(dec_seq)
    new_k = new_v = None

    for layer in range(depth):
        j = layer // 2
        n1 = norm1[layer][None]
        n2 = norm2[layer][None]
        wr = w_router
        if layer % 2 == 0:
            qn = q_norm[j][None]
            kn = k_norm[j][None]
            qc, kc, vc, new_k, new_v = _qkv(ctx, mods, layer, ctx_row, n1, w_qkv[j], qn, kn, None, True, TOKEN_TILE)
            oc = _attn_ctx(qc, kc, vc, seq, TOKEN_TILE)
            ql, kl, vl = _qkv(lat, mods, layer, lat_row(TOKEN_TILE), n1, w_qkv[j], qn, kn, rope_tabs, False,
                              TOKEN_TILE)
            past_k = cache_k[:, j].reshape(dec_batch, -1, kvw).astype(BF16)
            past_v = cache_v[:, j].reshape(dec_batch, -1, kvw).astype(BF16)
            ol = _attn_lat(ql, kl, vl, past_k, past_v, dec_seq, Q_TILE)
            ctx1 = _post(oc, ctx, mods, layer, ctx_row, n2, w_o[j], wr, TOKEN_TILE)
            lat1 = _post(ol, lat, mods, layer, lat_row(TOKEN_TILE), n2, w_o[j], wr, TOKEN_TILE)
        else:
            ps = pool_scale[j][None]
            ctx1 = _pool(ctx, mods, layer, ctx_row, n1, n2, w_pool[j], ps, wr, seq, ctx_tile)
            lat1 = _pool(lat, mods, layer, lat_row(TOKEN_TILE), n1, n2, w_pool[j], ps, wr, dec_seq, TOKEN_TILE)
        ctx, lat = _moe([ctx1, lat1], [ctx_row, lat_row(1)], mods, layer, w_e_gate, w_e_up, w_e_down)

    new_cache_k = new_k.reshape(batch, 1, seq, N_KV, HEAD_DIM)
    new_cache_v = new_v.reshape(batch, 1, seq, N_KV, HEAD_DIM)
    return (ctx.reshape(batch, seq, D_MODEL), lat.reshape(dec_batch, dec_seq, D_MODEL), new_cache_k, new_cache_v)
```
